```python
import jax, jax.numpy as jnp
from jax import lax
import numpy as np

D_MODEL = 1024
BATCH = 4
SEQ = 4096
DEPTH = 1

CTX_LEN = 256
GRID_W = 64
ATT_HEADS = 8
ATT_KV_HEADS = 2
ATT_GROUP = ATT_HEADS // ATT_KV_HEADS
ATT_HEAD_DIM = 64
ATT_WIDTH = ATT_HEADS * ATT_HEAD_DIM
ATT_KV_WIDTH = ATT_KV_HEADS * ATT_HEAD_DIM
AXIS_DIM = ATT_HEAD_DIM // 2
ROPE_THETA = 10000.0
Q_BLOCK = 128
HG_HEADS = 4
HG_HEAD_DIM = 128
HG_WIDTH = HG_HEADS * HG_HEAD_DIM
MIX_WIDTH = ATT_WIDTH + HG_WIDTH
CHUNK = 64
D_FF = ((8 * D_MODEL // 3 + 255) // 256) * 256
EPS = 1e-6
IN_SIZES = (ATT_WIDTH, ATT_KV_WIDTH, ATT_KV_WIDTH, HG_WIDTH, HG_WIDTH, HG_WIDTH, HG_WIDTH, HG_WIDTH)
IN_COLS = sum(IN_SIZES)
IN_SPLIT_IDX = tuple(int(v) for v in np.cumsum(IN_SIZES)[:-1])

kernel_name = "hymba_gqa_hgrn2_prefix_dit_block"


def rmsnorm(x, g):
    xf = x.astype(jnp.float32)
    y = xf * lax.rsqrt(jnp.mean(xf * xf, axis=-1, keepdims=True) + EPS)
    return (y * g.astype(jnp.float32)).astype(x.dtype)


def modulate(h, shift, scale):
    return h * (1 + scale) + shift


def axial_rope_tables(n_tokens):
    n_rows = n_tokens // GRID_W
    row = jnp.repeat(jnp.arange(n_rows, dtype=jnp.float32), GRID_W)
    col = jnp.tile(jnp.arange(GRID_W, dtype=jnp.float32), n_rows)
    inv = ROPE_THETA ** (-jnp.arange(0, AXIS_DIM, 2, dtype=jnp.float32) / AXIS_DIM)
    ar = row[:, None] * inv
    ac = col[:, None] * inv
    ang = jnp.concatenate([ar, ar, ac, ac], axis=-1)
    return jnp.cos(ang), jnp.sin(ang)


def apply_rope(x, cos, sin):
    x4 = x.reshape(*x.shape[:-1], 2, 2, ATT_HEAD_DIM // 4)
    rot = jnp.stack([-x4[..., 1, :], x4[..., 0, :]], axis=-2).reshape(x.shape)
    return (x * cos[:, None, :] + rot * sin[:, None, :]).astype(x.dtype)


def latent_attention(qx, kx, vx, kc, vc):
    B, T = qx.shape[:2]
    nblk = T // Q_BLOCK
    keys = jnp.concatenate([kc, kx], axis=1)
    vals = jnp.concatenate([vc, vx], axis=1)
    qb = qx.reshape(B, nblk, Q_BLOCK, ATT_KV_HEADS, ATT_GROUP, ATT_HEAD_DIM).transpose(1, 0, 2, 3, 4, 5)
    scale = ATT_HEAD_DIM ** -0.5

    def block(q):
        s = jnp.einsum('bqkgd,bnkd->bkgqn', q, keys).astype(jnp.float32) * scale
        p = jax.nn.softmax(s, axis=-1)
        return jnp.einsum('bkgqn,bnkd->bqkgd', p.astype(vals.dtype), vals)

    o = lax.map(block, qb)
    return o.transpose(1, 0, 2, 3, 4, 5).reshape(B, T, ATT_WIDTH)


def context_attention(qc, kc, vc):
    B, L = qc.shape[:2]
    q = qc.reshape(B, L, ATT_KV_HEADS, ATT_GROUP, ATT_HEAD_DIM)
    s = jnp.einsum('bqkgd,bnkd->bkgqn', q, kc).astype(jnp.float32) * (ATT_HEAD_DIM ** -0.5)
    p = jax.nn.softmax(s, axis=-1)
    o = jnp.einsum('bkgqn,bnkd->bqkgd', p.astype(vc.dtype), vc)
    return o.reshape(B, L, ATT_WIDTH)


def gla_chunked(q, k, v, logf, s0):
    B, T, H, DK = q.shape
    n = T // CHUNK

    def to_chunks(a):
        return a.reshape(B, n, CHUNK, H, a.shape[-1]).transpose(1, 0, 3, 2, 4)

    mask = jnp.tril(jnp.ones((CHUNK, CHUNK), dtype=bool))[:, :, None]

    def step(S, inp):
        qc, kc, vc, lf = inp
        b = jnp.cumsum(lf, axis=-2)
        diff = b[:, :, :, None, :] - b[:, :, None, :, :]
        decay = jnp.where(mask, jnp.exp(jnp.minimum(diff, 0.0)), 0.0)
        A = jnp.einsum('bhtk,bhsk,bhtsk->bhts', qc, kc, decay)
        o = jnp.einsum('bhts,bhsv->bhtv', A, vc) + jnp.einsum('bhtk,bhkv->bhtv', qc * jnp.exp(b), S)
        b_last = b[:, :, -1, :]
        S_new = jnp.exp(b_last)[..., None] * S + jnp.einsum(
            'bhsk,bhsv->bhkv', kc * jnp.exp(b_last[:, :, None, :] - b), vc)
        return S_new, o

    S, o = lax.scan(step, s0, (to_chunks(q), to_chunks(k), to_chunks(v), to_chunks(logf)))
    return o.transpose(1, 0, 3, 2, 4).reshape(B, T, H, v.shape[-1]), S


def hgrn2_prep(q, i, f, lb_d):
    B, T = q.shape[:2]
    fg = lb_d + (1.0 - lb_d) * jax.nn.sigmoid(f.astype(jnp.float32))
    sh = lambda a: a.reshape(B, T, HG_HEADS, HG_HEAD_DIM)
    qh = sh(jax.nn.silu(q.astype(jnp.float32)) * (HG_HEAD_DIM ** -0.5))
    return qh, sh(1.0 - fg), sh(i.astype(jnp.float32)), sh(jnp.log(fg))


def hgrn2_bidirectional(px, pc, lb_l, gain):
    qx, ix, ffx, fbx, gx = px
    qc, ic, ffc, fbc, gc = pc
    B = qx.shape[0]
    s0 = jnp.zeros((B, HG_HEADS, HG_HEAD_DIM, HG_HEAD_DIM), jnp.float32)
    flip = lambda a: a[:, ::-1]
    oc_f, sc_f = gla_chunked(*hgrn2_prep(qc, ic, ffc, lb_l[0]), s0)
    ox_f, _ = gla_chunked(*hgrn2_prep(qx, ix, ffx, lb_l[0]), sc_f)
    oc_b, sc_b = gla_chunked(*map(flip, hgrn2_prep(qc, ic, fbc, lb_l[1])), s0)
    ox_b, _ = gla_chunked(*map(flip, hgrn2_prep(qx, ix, fbx, lb_l[1])), sc_b)

    def readout(o_f, o_b, g):
        Bn, T = g.shape[:2]
        o = rmsnorm(o_f + flip(o_b), gain).reshape(Bn, T, HG_WIDTH)
        return (o * jax.nn.silu(g.astype(jnp.float32))).astype(g.dtype)

    return readout(ox_f, ox_b, gx), readout(oc_f, oc_b, gc)


def swiglu(h, w_gu, w_down):
    a, b = jnp.split(h @ w_gu, 2, axis=-1)
    return (jax.nn.silu(a) * b) @ w_down


def setup_inputs(seed: int = 0) -> dict:
    key = jax.random.key(seed)
    ks = jax.random.split(key, 17)
    nrm = lambda k, shape, s: jax.random.normal(k, shape, jnp.float32) * s
    return {
        "x": nrm(ks[0], (BATCH, SEQ, D_MODEL), 1.0),
        "c": nrm(ks[1], (BATCH, D_MODEL), 1.0),
        "ctx": nrm(ks[2], (BATCH, CTX_LEN, D_MODEL), 1.0),
        "c_ctx": nrm(ks[3], (D_MODEL,), 1.0),
        "w_mod": nrm(ks[4], (DEPTH, D_MODEL, 6 * D_MODEL), 0.5 * D_MODEL ** -0.5),
        "b_mod": nrm(ks[5], (DEPTH, 6 * D_MODEL), 0.02),
        "g_norm1": 1.0 + nrm(ks[6], (DEPTH, D_MODEL), 0.02),
        "w_in": nrm(ks[7], (DEPTH, D_MODEL, IN_COLS), D_MODEL ** -0.5),
        "g_q": 1.0 + nrm(ks[8], (DEPTH, ATT_HEAD_DIM), 0.02),
        "g_k": 1.0 + nrm(ks[9], (DEPTH, ATT_HEAD_DIM), 0.02),
        "lb_raw": nrm(ks[10], (DEPTH + 1, 2, HG_WIDTH), 0.5),
        "g_hg": 1.0 + nrm(ks[11], (DEPTH, HG_HEAD_DIM), 0.02),
        "w_out": nrm(ks[12], (DEPTH, MIX_WIDTH, D_MODEL), MIX_WIDTH ** -0.5),
        "g_norm2": 1.0 + nrm(ks[13], (DEPTH, D_MODEL), 0.02),
        "w_gu": nrm(ks[14], (DEPTH, D_MODEL, 2 * D_FF), D_MODEL ** -0.5),
        "w_down": nrm(ks[15], (DEPTH, D_FF, D_MODEL), D_FF ** -0.5),
        "g_final": 1.0 + nrm(ks[16], (D_MODEL,), 0.02),
    }


def reference(x, c, ctx, c_ctx, w_mod, b_mod, g_norm1, w_in, g_q, g_k, lb_raw, g_hg, w_out,
              g_norm2, w_gu, w_down, g_final):
    B, T = x.shape[:2]
    cos, sin = axial_rope_tables(T)
    lb_all = jnp.cumsum(jax.nn.softmax(lb_raw.astype(jnp.float32), axis=0), axis=0)
    heads = lambda a, h, d: a.reshape(a.shape[0], a.shape[1], h, d)
    for l in range(DEPTH):
        last = l == DEPTH - 1
        mx = [m[:, None, :] for m in jnp.split(jax.nn.silu(c) @ w_mod[l] + b_mod[l], 6, axis=-1)]
        mc = jnp.split(jax.nn.silu(c_ctx) @ w_mod[l] + b_mod[l], 6, axis=-1)
        hx = modulate(rmsnorm(x, g_norm1[l]), mx[0], mx[1])
        hc = modulate(rmsnorm(ctx, g_norm1[l]), mc[0], mc[1])
        px = jnp.split(hx @ w_in[l], IN_SPLIT_IDX, axis=-1)
        pc = jnp.split(hc @ w_in[l], IN_SPLIT_IDX, axis=-1)
        qx = apply_rope(rmsnorm(heads(px[0], ATT_HEADS, ATT_HEAD_DIM), g_q[l]), cos, sin)
        kx = apply_rope(rmsnorm(heads(px[1], ATT_KV_HEADS, ATT_HEAD_DIM), g_k[l]), cos, sin)
        vx = heads(px[2], ATT_KV_HEADS, ATT_HEAD_DIM)
        kc = rmsnorm(heads(pc[1], ATT_KV_HEADS, ATT_HEAD_DIM), g_k[l])
        vc = heads(pc[2], ATT_KV_HEADS, ATT_HEAD_DIM)
        att_x = latent_attention(qx, kx, vx, kc, vc)
        hg_x, hg_c = hgrn2_bidirectional(px[3:], pc[3:], lb_all[l], g_hg[l])
        x = x + mx[2] * (jnp.concatenate([att_x, hg_x], axis=-1) @ w_out[l])
        x = x + mx[5] * swiglu(modulate(rmsnorm(x, g_norm2[l]), mx[3], mx[4]), w_gu[l], w_down[l])
        if not last:
            qc = rmsnorm(heads(pc[0], ATT_HEADS, ATT_HEAD_DIM), g_q[l])
            att_c = context_attention(qc, kc, vc)
            ctx = ctx + mc[2] * (jnp.concatenate([att_c, hg_c], axis=-1) @ w_out[l])
            ctx = ctx + mc[5] * swiglu(modulate(rmsnorm(ctx, g_norm2[l]), mc[3], mc[4]), w_gu[l], w_down[l])
    return rmsnorm(x, g_final)
```

```python
import functools

import jax
import jax.numpy as jnp
import numpy as np
from jax import lax
from jax.experimental import pallas as pl
from jax.experimental.pallas import tpu as pltpu

F32 = jnp.float32
BF16 = jnp.bfloat16

LANES = 128
GRID_W = 64
ATT_HEADS = 8
ATT_KV_HEADS = 2
ATT_GROUP = ATT_HEADS // ATT_KV_HEADS
ATT_HEAD_DIM = 64
ATT_WIDTH = ATT_HEADS * ATT_HEAD_DIM
ATT_KV_WIDTH = ATT_KV_HEADS * ATT_HEAD_DIM
AXIS_DIM = ATT_HEAD_DIM // 2
ROPE_THETA = 10000.0
HG_HEADS = 4
HG_HEAD_DIM = 128
HG_WIDTH = HG_HEADS * HG_HEAD_DIM
EPS = 1e-6

SUB = 16
WIN = 128
ROW_TILE = 256
Q_TILE = 128
KV_TILE = 512
VMEM_LIMIT = 56 * 1024 * 1024

C_AQ = 0
C_AK = C_AQ + ATT_WIDTH
C_AV = C_AK + ATT_KV_WIDTH
C_HQ = C_AV + ATT_KV_WIDTH
C_HI = C_HQ + HG_WIDTH
C_FF = C_HI + HG_WIDTH
C_FB = C_FF + HG_WIDTH
C_HG = C_FB + HG_WIDTH
C_END = C_HG + HG_WIDTH


def _silu(v):
    return v * jax.nn.sigmoid(v)


def _rms_rows(v):
    return v * lax.rsqrt(jnp.mean(v * v, axis=-1, keepdims=True) + EPS)


def _mod_kernel(c_ref, w_ref, b_ref, o_ref):
    a = _silu(c_ref[...])
    o_ref[...] = jnp.dot(a, w_ref[...], precision=lax.Precision.HIGHEST,
                         preferred_element_type=F32) + b_ref[...]


def _modulation(cc, w_mod, b_mod):
    rows, d = cc.shape
    n = w_mod.shape[1]
    bn = n // 4
    return pl.pallas_call(
        _mod_kernel,
        out_shape=jax.ShapeDtypeStruct((rows, n), F32),
        grid=(n // bn,),
        in_specs=[pl.BlockSpec((rows, d), lambda j: (0, 0)),
                  pl.BlockSpec((d, bn), lambda j: (0, j)),
                  pl.BlockSpec((1, bn), lambda j: (0, j))],
        out_specs=pl.BlockSpec((rows, bn), lambda j: (0, j)),
        compiler_params=pltpu.CompilerParams(dimension_semantics=("arbitrary",),
                                             vmem_limit_bytes=VMEM_LIMIT),
        name="modulation",
    )(cc, w_mod, b_mod)


def _split3(v):
    hi = v.astype(BF16)
    r = v - hi.astype(F32)
    mid = r.astype(BF16)
    lo = (r - mid.astype(F32)).astype(BF16)
    return hi, mid, lo


def _head64_meansq(p, seg):
    sq = p * p
    hi = sq.astype(BF16)
    lo = (sq - hi.astype(F32)).astype(BF16)
    s = jnp.dot(hi, seg, preferred_element_type=F32) + jnp.dot(lo, seg, preferred_element_type=F32)
    return s * (1.0 / ATT_HEAD_DIM)


def _rope(v, cos, sin_signed, first_half):
    fwd = pltpu.roll(v, LANES - AXIS_DIM // 2, 1)
    bwd = pltpu.roll(v, AXIS_DIM // 2, 1)
    return v * cos + jnp.where(first_half, fwd, bwd) * sin_signed


def _gate_prep(f, lb, tri):
    fg = lb + (1.0 - lb) * jax.nn.sigmoid(f)
    lf = jnp.log(fg)
    hi, mid, lo = _split3(lf)
    cum = (jnp.dot(tri, hi, preferred_element_type=F32) + jnp.dot(tri, mid, preferred_element_type=F32)
           + jnp.dot(tri, lo, preferred_element_type=F32))
    return (1.0 - fg).astype(BF16), cum


def _inproj_kernel(x_ref, mod_ref, g1_ref, w_ref, gq_ref, gk_ref, lb_ref, cos_ref, sin_ref, trif_ref, trib_ref,
                   *out_refs, latent):
    d = x_ref.shape[-1]
    x = x_ref[0]
    shift = mod_ref[0, :, 0:d]
    scale = mod_ref[0, :, d:2 * d]
    h = (_rms_rows(x) * g1_ref[...]) * (1.0 + scale) + shift
    hb = h.astype(BF16)

    def proj(c0, width):
        return jnp.dot(hb, w_ref[:, c0:c0 + width], preferred_element_type=F32)

    lane = lax.broadcasted_iota(jnp.int32, (LANES, LANES), 0)
    lane_c = lax.broadcasted_iota(jnp.int32, (LANES, LANES), 1)
    seg = (lane // ATT_HEAD_DIM == lane_c // ATT_HEAD_DIM).astype(BF16)
    row_lane = lax.broadcasted_iota(jnp.int32, (x.shape[0], LANES), 1)
    first_half = (row_lane % AXIS_DIM) < (AXIS_DIM // 2)
    upper = row_lane >= ATT_HEAD_DIM

    r0 = lb_ref[0]
    r1 = lb_ref[1]
    rm = jnp.maximum(r0, r1)
    e0 = jnp.exp(r0 - rm)
    e1 = jnp.exp(r1 - rm)
    lb_all = e0 / (e0 + e1)

    if latent:
        (qa_ref, ka_ref, va_ref, hq_ref, hv_ref, kf_ref, bf_ref, kb_ref, bb_ref, sg_ref) = out_refs
        cos = cos_ref[...]
        sin = sin_ref[...]
    else:
        (ka_ref, va_ref, hv_ref, kf_ref, bf_ref, kb_ref, bb_ref) = out_refs

    pk = proj(C_AK, ATT_KV_WIDTH)
    kn = pk * lax.rsqrt(_head64_meansq(pk, seg) + EPS) * gk_ref[...]
    if latent:
        kn = _rope(kn, cos, sin, first_half)
    ka_ref[0] = kn.astype(BF16)
    va_ref[0] = proj(C_AV, ATT_KV_WIDTH).astype(BF16)

    if latent:
        pq = proj(C_AQ, ATT_WIDTH)
        for c in range(ATT_WIDTH // LANES):
            t = pq[:, c * LANES:(c + 1) * LANES]
            t = t * lax.rsqrt(_head64_meansq(t, seg) + EPS) * gq_ref[...]
            t = _rope(t, cos, sin, first_half) * (ATT_HEAD_DIM ** -0.5)
            t_swapped = pltpu.roll(t, ATT_HEAD_DIM, 1)
            for half in range(2):
                head = 2 * c + half
                kv = head // ATT_GROUP
                src = t if half == kv else t_swapped
                keep = upper if kv == 1 else jnp.logical_not(upper)
                qa_ref[0, :, head * LANES:(head + 1) * LANES] = jnp.where(keep, src, 0.0).astype(BF16)
        hq_ref[0] = (_silu(proj(C_HQ, HG_WIDTH)) * (HG_HEAD_DIM ** -0.5)).astype(BF16)
        sg_ref[0] = _silu(proj(C_HG, HG_WIDTH)).astype(BF16)

    hv_ref[0] = proj(C_HI, HG_WIDTH).astype(BF16)
    kf, bf = _gate_prep(proj(C_FF, HG_WIDTH), lb_all[0:1, :], trif_ref[...])
    kf_ref[0] = kf
    bf_ref[0] = bf
    kb, bb = _gate_prep(proj(C_FB, HG_WIDTH), lb_all[1:2, :], trib_ref[...])
    kb_ref[0] = kb
    bb_ref[0] = bb


def _block_tri(n, suffix):
    r = np.arange(n)[:, None]
    c = np.arange(n)[None, :]
    same = (r // SUB) == (c // SUB)
    tri = (c >= r) if suffix else (c <= r)
    return jnp.asarray((same & tri).astype(np.float32), dtype=BF16)


def _input_projection(rows, mod3, mod_row, g1, w_in, gq, gk, lb_raw, cos, sin, latent):
    b, t, d = rows.shape
    tm = min(ROW_TILE, t)
    nt = t // tm
    trif = _block_tri(tm, False)
    trib = _block_tri(tm, True)
    bspec = lambda width: pl.BlockSpec((1, tm, width), lambda i, j: (i, j, 0))
    full = lambda a: pl.BlockSpec(a.shape, lambda i, j: (0,) * a.ndim)
    sds = lambda width, dt: jax.ShapeDtypeStruct((b, t, width), dt)
    if latent:
        out_shape = [sds(ATT_HEADS * LANES, BF16), sds(ATT_KV_WIDTH, BF16), sds(ATT_KV_WIDTH, BF16),
                     sds(HG_WIDTH, BF16), sds(HG_WIDTH, BF16), sds(HG_WIDTH, BF16), sds(HG_WIDTH, F32),
                     sds(HG_WIDTH, BF16), sds(HG_WIDTH, F32), sds(HG_WIDTH, BF16)]
        out_specs = [bspec(ATT_HEADS * LANES), bspec(ATT_KV_WIDTH), bspec(ATT_KV_WIDTH), bspec(HG_WIDTH),
                     bspec(HG_WIDTH), bspec(HG_WIDTH), bspec(HG_WIDTH), bspec(HG_WIDTH), bspec(HG_WIDTH),
                     bspec(HG_WIDTH)]
        mod_spec = pl.BlockSpec((1, 1, mod3.shape[-1]), lambda i, j: (i, 0, 0))
    else:
        out_shape = [sds(ATT_KV_WIDTH, BF16), sds(ATT_KV_WIDTH, BF16), sds(HG_WIDTH, BF16), sds(HG_WIDTH, BF16),
                     sds(HG_WIDTH, F32), sds(HG_WIDTH, BF16), sds(HG_WIDTH, F32)]
        out_specs = [bspec(ATT_KV_WIDTH), bspec(ATT_KV_WIDTH), bspec(HG_WIDTH), bspec(HG_WIDTH), bspec(HG_WIDTH),
                     bspec(HG_WIDTH), bspec(HG_WIDTH)]
        mod_spec = pl.BlockSpec((1, 1, mod3.shape[-1]), lambda i, j: (mod_row, 0, 0))
    table_spec = pl.BlockSpec((tm, LANES), lambda i, j: (j, 0))
    return pl.pallas_call(
        functools.partial(_inproj_kernel, latent=latent),
        out_shape=out_shape,
        grid=(b, nt),
        in_specs=[bspec(d), mod_spec, full(g1), full(w_in), full(gq), full(gk), full(lb_raw),
                  table_spec, table_spec, full(trif), full(trib)],
        out_specs=out_specs,
        compiler_params=pltpu.CompilerParams(dimension_semantics=("parallel", "parallel"),
                                             vmem_limit_bytes=VMEM_LIMIT),
        name="inproj_latent" if latent else "inproj_context",
    )(rows, mod3, g1, w_in, gq, gk, lb_raw, cos, sin, trif, trib)


def _attention_kernel(q_ref, kc_ref, vc_ref, kx_ref, vx_ref, o_ref):
    tq = q_ref.shape[1]
    n_kv = kx_ref.shape[1] // KV_TILE
    lane = lax.broadcasted_iota(jnp.int32, (tq, LANES), 1)
    upper = lane >= ATT_HEAD_DIM
    nt_dims = (((1,), (1,)), ((), ()))

    for kv in range(ATT_KV_HEADS):
        qg = jnp.concatenate(
            [q_ref[0, :, (kv * ATT_GROUP + j) * LANES:(kv * ATT_GROUP + j + 1) * LANES] for j in range(ATT_GROUP)],
            axis=0)
        rows = qg.shape[0]

        def update(kt, vt, carry, qg=qg):
            m, l, acc = carry
            s = lax.dot_general(qg, kt, nt_dims, preferred_element_type=F32)
            m_new = jnp.maximum(m, jnp.max(s, axis=-1, keepdims=True))
            alpha = jnp.exp(m - m_new)
            p = jnp.exp(s - m_new)
            l = alpha * l + jnp.sum(p, axis=-1, keepdims=True)
            acc = alpha * acc + jnp.dot(p.astype(BF16), vt, preferred_element_type=F32)
            return m_new, l, acc

        carry = (jnp.full((rows, 1), -jnp.inf, F32), jnp.zeros((rows, 1), F32), jnp.zeros((rows, LANES), F32))
        carry = update(kc_ref[0], vc_ref[0], carry)

        def body(i, carry, update=update):
            start = pl.multiple_of(i * KV_TILE, KV_TILE)
            return update(kx_ref[0, pl.ds(start, KV_TILE), :], vx_ref[0, pl.ds(start, KV_TILE), :], carry)

        m, l, acc = lax.fori_loop(0, n_kv, body, carry)
        o = acc / l
        o_sw = pltpu.roll(o, ATT_HEAD_DIM, 1)
        for jj in range(ATT_GROUP // 2):
            left = o[(2 * jj) * tq:(2 * jj + 1) * tq]
            left_sw = o_sw[(2 * jj) * tq:(2 * jj + 1) * tq]
            right = o[(2 * jj + 1) * tq:(2 * jj + 2) * tq]
            right_sw = o_sw[(2 * jj + 1) * tq:(2 * jj + 2) * tq]
            lo_half = left if kv == 0 else left_sw
            hi_half = right_sw if kv == 0 else right
            col = kv * (ATT_GROUP // 2) + jj
            o_ref[0, :, col * LANES:(col + 1) * LANES] = jnp.where(upper, hi_half, lo_half).astype(BF16)


def _attention(qa, kc, vc, kx, vx):
    b, t, _ = qa.shape
    tq = min(Q_TILE, t)
    return pl.pallas_call(
        _attention_kernel,
        out_shape=jax.ShapeDtypeStruct((b, t, ATT_WIDTH), BF16),
        grid=(b, t // tq),
        in_specs=[pl.BlockSpec((1, tq, qa.shape[-1]), lambda i, j: (i, j, 0)),
                  pl.BlockSpec((1,) + kc.shape[1:], lambda i, j: (i, 0, 0)),
                  pl.BlockSpec((1,) + vc.shape[1:], lambda i, j: (i, 0, 0)),
                  pl.BlockSpec((1,) + kx.shape[1:], lambda i, j: (i, 0, 0)),
                  pl.BlockSpec((1,) + vx.shape[1:], lambda i, j: (i, 0, 0))],
        out_specs=pl.BlockSpec((1, tq, ATT_WIDTH), lambda i, j: (i, j, 0)),
        compiler_params=pltpu.CompilerParams(dimension_semantics=("parallel", "parallel"),
                                             vmem_limit_bytes=VMEM_LIMIT),
        name="attention",
    )(qa, kc, vc, kx, vx)


def _scan_step(st, q, k, v, vt_win, b, sub, reverse, ones, want_out):
    last = 0 if reverse else SUB - 1
    b_last = b[last:last + 1, :]
    k_dec = k * jnp.exp(b_last - b)
    before, after = sub * SUB, WIN - (sub + 1) * SUB
    k_win = jnp.concatenate(([jnp.zeros((before, LANES), F32)] if before else []) + [k_dec]
                            + ([jnp.zeros((after, LANES), F32)] if after else []), axis=0)
    st_new = jnp.exp(b_last) * st + jnp.dot(vt_win, k_win, preferred_element_type=F32)
    if not want_out:
        return st_new, None

    o = lax.dot_general(q * jnp.exp(b), st, (((1,), (1,)), ((), ())), preferred_element_type=F32)

    half = SUB // 2
    t_idx = lax.broadcasted_iota(jnp.int32, (half, LANES), 0)
    parts = []
    spans = []
    for s in range(SUB):
        if reverse:
            lo, hi = (0, SUB) if s >= half else (0, half)
        else:
            lo, hi = (0, SUB) if s < half else (half, SUB)
        w = jnp.exp(jnp.minimum(b[lo:hi] - b[s:s + 1, :], 0.0))
        p = w * q[lo:hi] * k[s:s + 1, :]
        tile0 = (s // half) * half
        pieces = []
        for r0 in range(lo, hi, half):
            piece = p[r0 - lo:r0 - lo + half]
            if r0 == tile0 and (s % half) != (half - 1 if reverse else 0):
                keep = (t_idx <= s - r0) if reverse else (t_idx >= s - r0)
                piece = jnp.where(keep, piece, 0.0)
            pieces.append(piece)
        parts.extend(pieces)
        spans.append((lo, hi))
    summed = jnp.dot(jnp.concatenate(parts, axis=0), ones, preferred_element_type=F32)
    pos = 0
    o_lo = o[0:half]
    o_hi = o[half:SUB]
    for s in range(SUB):
        lo, hi = spans[s]
        vs = v[s:s + 1, :]
        for r0 in range(lo, hi, half):
            term = summed[pos:pos + half] * vs
            pos += half
            if r0 == 0:
                o_lo = o_lo + term
            else:
                o_hi = o_hi + term
    return st_new, jnp.concatenate([o_lo, o_hi], axis=0)


def _hgrn_kernel(q_ref, v_ref, kf_ref, bf_ref, kb_ref, bb_ref, vc_ref, kfc_ref, bfc_ref, kbc_ref, bbc_ref,
                 o_ref, stf_ref, stb_ref):
    t = q_ref.shape[1]
    ctx = vc_ref.shape[1]
    nw = t // WIN
    ones = jnp.ones((LANES, LANES), BF16)
    nsub = WIN // SUB

    def window(refs, start, st, reverse, want_out):
        qr, vr, kr, br = refs
        sl = pl.ds(start, WIN)
        vw = vr[0, sl, :].astype(F32)
        kw = kr[0, sl, :].astype(F32)
        bw = br[0, sl, :]
        qw = qr[0, sl, :].astype(F32) if want_out else None
        vt_win = vw.T
        outs = [None] * nsub
        order = range(nsub - 1, -1, -1) if reverse else range(nsub)
        for sub in order:
            rs = slice(sub * SUB, (sub + 1) * SUB)
            st, o = _scan_step(st, qw[rs] if want_out else None, kw[rs], vw[rs], vt_win, bw[rs], sub, reverse,
                               ones, want_out)
            outs[sub] = o
        return st, (jnp.concatenate(outs, axis=0) if want_out else None)

    zero = jnp.zeros((LANES, LANES), F32)
    stf, stb = zero, zero
    n_cw = ctx // WIN
    for w in range(n_cw):
        stf, _ = window((None, vc_ref, kfc_ref, bfc_ref), w * WIN, stf, False, False)
        stb, _ = window((None, vc_ref, kbc_ref, bbc_ref), (n_cw - 1 - w) * WIN, stb, True, False)
    stf_ref[...] = stf
    stb_ref[...] = stb

    def body(w, accumulate):
        fs = pl.multiple_of(w * WIN, WIN)
        bs = pl.multiple_of((nw - 1 - w) * WIN, WIN)
        stf, of = window((q_ref, v_ref, kf_ref, bf_ref), fs, stf_ref[...], False, True)
        stb, ob = window((q_ref, v_ref, kb_ref, bb_ref), bs, stb_ref[...], True, True)
        stf_ref[...] = stf
        stb_ref[...] = stb
        if accumulate:
            o_ref[0, pl.ds(fs, WIN), :] += of
            o_ref[0, pl.ds(bs, WIN), :] += ob
        else:
            o_ref[0, pl.ds(fs, WIN), :] = of
            o_ref[0, pl.ds(bs, WIN), :] = ob

    def first(w, c):
        body(w, False)
        return c

    def second(w, c):
        body(w, True)
        return c

    lax.fori_loop(0, nw // 2, first, 0)
    lax.fori_loop(nw // 2, nw, second, 0)


def _hgrn_scan(hq, hv, kf, bf, kb, bb, hvc, kfc, bfc, kbc, bbc):
    b, t, _ = hq.shape
    ctx = hvc.shape[1]
    lat = pl.BlockSpec((1, t, HG_HEAD_DIM), lambda i, j: (i, 0, j))
    con = pl.BlockSpec((1, ctx, HG_HEAD_DIM), lambda i, j: (i, 0, j))
    return pl.pallas_call(
        _hgrn_kernel,
        out_shape=jax.ShapeDtypeStruct((b, t, HG_WIDTH), F32),
        grid=(b, HG_HEADS),
        in_specs=[lat] * 6 + [con] * 5,
        out_specs=lat,
        scratch_shapes=[pltpu.VMEM((LANES, LANES), F32), pltpu.VMEM((LANES, LANES), F32)],
        compiler_params=pltpu.CompilerParams(dimension_semantics=("parallel", "parallel"),
                                             vmem_limit_bytes=VMEM_LIMIT),
        name="hgrn_scan",
    )(hq, hv, kf, bf, kb, bb, hvc, kfc, bfc, kbc, bbc)


def _out_ffn_kernel(x_ref, att_ref, hg_ref, sg_ref, mod_ref, ghg_ref, wo_ref, g2_ref, wgu_ref, wd_ref, gf_ref,
                    o_ref):
    d = x_ref.shape[-1]
    dff = wd_ref.shape[0]
    x = x_ref[0]
    gate1 = mod_ref[0, :, 2 * d:3 * d]
    shift2 = mod_ref[0, :, 3 * d:4 * d]
    scale2 = mod_ref[0, :, 4 * d:5 * d]
    gate2 = mod_ref[0, :, 5 * d:6 * d]

    mix = jnp.dot(att_ref[0], wo_ref[0:ATT_WIDTH, :], preferred_element_type=F32)
    for hd in range(HG_HEADS):
        cs = slice(hd * HG_HEAD_DIM, (hd + 1) * HG_HEAD_DIM)
        o = _rms_rows(hg_ref[0, :, cs]) * ghg_ref[...]
        o = (o * sg_ref[0, :, cs].astype(F32)).astype(BF16)
        mix = mix + jnp.dot(o, wo_ref[ATT_WIDTH + hd * HG_HEAD_DIM:ATT_WIDTH + (hd + 1) * HG_HEAD_DIM, :],
                            preferred_element_type=F32)
    x1 = x + gate1 * mix
    h = ((_rms_rows(x1) * g2_ref[...]) * (1.0 + scale2) + shift2).astype(BF16)
    a = jnp.dot(h, wgu_ref[:, 0:dff], preferred_element_type=F32)
    g = jnp.dot(h, wgu_ref[:, dff:2 * dff], preferred_element_type=F32)
    act = (_silu(a) * g).astype(BF16)
    y = jnp.dot(act, wd_ref[...], preferred_element_type=F32)
    x2 = x1 + gate2 * y
    o_ref[0] = _rms_rows(x2) * gf_ref[...]


def _out_ffn(x, att, hg, sg, mod3, ghg, w_out, g2, w_gu, w_down, gf):
    b, t, d = x.shape
    tm = min(ROW_TILE, t)
    bspec = lambda width: pl.BlockSpec((1, tm, width), lambda i, j: (i, j, 0))
    full = lambda a: pl.BlockSpec(a.shape, lambda i, j: (0,) * a.ndim)
    return pl.pallas_call(
        _out_ffn_kernel,
        out_shape=jax.ShapeDtypeStruct((b, t, d), F32),
        grid=(b, t // tm),
        in_specs=[bspec(d), bspec(ATT_WIDTH), bspec(HG_WIDTH), bspec(HG_WIDTH),
                  pl.BlockSpec((1, 1, mod3.shape[-1]), lambda i, j: (i, 0, 0)),
                  full(ghg), full(w_out), full(g2), full(w_gu), full(w_down), full(gf)],
        out_specs=bspec(d),
        compiler_params=pltpu.CompilerParams(dimension_semantics=("parallel", "parallel"),
                                             vmem_limit_bytes=VMEM_LIMIT),
        name="out_ffn",
    )(x, att, hg, sg, mod3, ghg, w_out, g2, w_gu, w_down, gf)


def _rope_tables(t):
    n_rows = t // GRID_W
    row = jnp.repeat(jnp.arange(n_rows, dtype=F32), GRID_W)
    col = jnp.tile(jnp.arange(GRID_W, dtype=F32), n_rows)
    inv = ROPE_THETA ** (-jnp.arange(0, AXIS_DIM, 2, dtype=F32) / AXIS_DIM)
    ar = row[:, None] * inv
    ac = col[:, None] * inv
    ang = jnp.concatenate([ar, ar, ac, ac], axis=-1)
    sign = jnp.where((jnp.arange(ATT_HEAD_DIM) % AXIS_DIM) < AXIS_DIM // 2, -1.0, 1.0).astype(F32)
    reps = LANES // ATT_HEAD_DIM
    return jnp.tile(jnp.cos(ang), (1, reps)), jnp.tile(jnp.sin(ang) * sign, (1, reps))


def kernel(x, c, ctx, c_ctx, w_mod, b_mod, g_norm1, w_in, g_q, g_k, lb_raw, g_hg, w_out, g_norm2, w_gu, w_down,
           g_final):
    b, t, d = x.shape
    assert w_mod.shape[0] == 1 and lb_raw.shape[0] == 2, "single-layer block"
    assert t % WIN == 0 and (t // WIN) % 2 == 0 and ctx.shape[1] % WIN == 0 and t % KV_TILE == 0

    pad = (-(b + 1)) % 8
    cc = jnp.concatenate([c, c_ctx[None, :], jnp.zeros((pad, d), F32)], axis=0)
    mods = _modulation(cc, w_mod[0], b_mod)
    mod3 = mods.reshape(mods.shape[0], 1, mods.shape[1])

    cos, sin = _rope_tables(t)
    reps = LANES // ATT_HEAD_DIM
    gq = jnp.tile(g_q[0], reps)[None, :]
    gk = jnp.tile(g_k[0], reps)[None, :]
    w_in_b = w_in[0].astype(BF16)

    qa, kx, vx, hq, hv, kf, bf, kb, bb, sg = _input_projection(
        x, mod3, None, g_norm1, w_in_b, gq, gk, lb_raw, cos, sin, True)
    kc, vc, hvc, kfc, bfc, kbc, bbc = _input_projection(
        ctx, mod3, b, g_norm1, w_in_b, gq, gk, lb_raw, cos, sin, False)

    att = _attention(qa, kc, vc, kx, vx)
    hg = _hgrn_scan(hq, hv, kf, bf, kb, bb, hvc, kfc, bfc, kbc, bbc)
    return _out_ffn(x, att, hg, sg, mod3, g_hg, w_out[0].astype(BF16), g_norm2, w_gu[0].astype(BF16),
                    w_down[0].astype(BF16), g_final[None, :])
```

```python
import functools

import jax
import jax.numpy as jnp
import numpy as np
from jax import lax
from jax.experimental import pallas as pl
from jax.experimental.pallas import tpu as pltpu

F32 = jnp.float32
BF16 = jnp.bfloat16

LANES = 128
GRID_W = 64
ATT_HEADS = 8
ATT_KV_HEADS = 2
ATT_GROUP = ATT_HEADS // ATT_KV_HEADS
ATT_HEAD_DIM = 64
ATT_WIDTH = ATT_HEADS * ATT_HEAD_DIM
ATT_KV_WIDTH = ATT_KV_HEADS * ATT_HEAD_DIM
AXIS_DIM = ATT_HEAD_DIM // 2
ROPE_THETA = 10000.0
HG_HEADS = 4
HG_HEAD_DIM = 128
HG_WIDTH = HG_HEADS * HG_HEAD_DIM
EPS = 1e-6

SUB = 16
WIN = 128
ROW_TILE = 256
Q_TILE = 128
KV_TILE = 512
ONES_ROWS = 16
LOG2E = 1.4426950408889634
VMEM_LIMIT = 56 * 1024 * 1024

C_AQ = 0
C_AK = C_AQ + ATT_WIDTH
C_AV = C_AK + ATT_KV_WIDTH
C_HQ = C_AV + ATT_KV_WIDTH
C_HI = C_HQ + HG_WIDTH
C_FF = C_HI + HG_WIDTH
C_FB = C_FF + HG_WIDTH
C_HG = C_FB + HG_WIDTH
C_END = C_HG + HG_WIDTH


def _silu(v):
    return v * jax.nn.sigmoid(v)


def _rms_rows(v):
    return v * lax.rsqrt(jnp.mean(v * v, axis=-1, keepdims=True) + EPS)


def _mod_kernel(c_ref, w_ref, b_ref, o_ref):
    a = _silu(c_ref[...])
    o_ref[...] = jnp.dot(a, w_ref[...], precision=lax.Precision.HIGHEST,
                         preferred_element_type=F32) + b_ref[...]


def _modulation(cc, w_mod, b_mod):
    rows, d = cc.shape
    n = w_mod.shape[1]
    bn = n // 4
    return pl.pallas_call(
        _mod_kernel,
        out_shape=jax.ShapeDtypeStruct((rows, n), F32),
        grid=(n // bn,),
        in_specs=[pl.BlockSpec((rows, d), lambda j: (0, 0)),
                  pl.BlockSpec((d, bn), lambda j: (0, j)),
                  pl.BlockSpec((1, bn), lambda j: (0, j))],
        out_specs=pl.BlockSpec((rows, bn), lambda j: (0, j)),
        compiler_params=pltpu.CompilerParams(dimension_semantics=("arbitrary",),
                                             vmem_limit_bytes=VMEM_LIMIT),
        name="modulation",
    )(cc, w_mod, b_mod)


def _split3(v):
    hi = v.astype(BF16)
    r = v - hi.astype(F32)
    mid = r.astype(BF16)
    lo = (r - mid.astype(F32)).astype(BF16)
    return hi, mid, lo


def _head64_meansq(p, seg):
    sq = p * p
    hi = sq.astype(BF16)
    lo = (sq - hi.astype(F32)).astype(BF16)
    s = jnp.dot(hi, seg, preferred_element_type=F32) + jnp.dot(lo, seg, preferred_element_type=F32)
    return s * (1.0 / ATT_HEAD_DIM)


def _rope(v, cos, sin_signed, first_half):
    fwd = pltpu.roll(v, LANES - AXIS_DIM // 2, 1)
    bwd = pltpu.roll(v, AXIS_DIM // 2, 1)
    return v * cos + jnp.where(first_half, fwd, bwd) * sin_signed


def _gate_prep(f, lb, tri):
    fg = lb + (1.0 - lb) * jax.nn.sigmoid(f)
    lf = jnp.log(fg)
    hi, mid, lo = _split3(lf)
    cum = (jnp.dot(tri, hi, preferred_element_type=F32) + jnp.dot(tri, mid, preferred_element_type=F32)
           + jnp.dot(tri, lo, preferred_element_type=F32))
    return (1.0 - fg).astype(BF16), cum


def _inproj_kernel(x_ref, mod_ref, g1_ref, w_ref, gq_ref, gk_ref, lb_ref, cos_ref, sin_ref, trif_ref, trib_ref,
                   *out_refs, latent):
    d = x_ref.shape[-1]
    x = x_ref[0]
    shift = mod_ref[0, :, 0:d]
    scale = mod_ref[0, :, d:2 * d]
    h = (_rms_rows(x) * g1_ref[...]) * (1.0 + scale) + shift
    hb = h.astype(BF16)

    def proj(c0, width):
        return jnp.dot(hb, w_ref[:, c0:c0 + width], preferred_element_type=F32)

    lane = lax.broadcasted_iota(jnp.int32, (LANES, LANES), 0)
    lane_c = lax.broadcasted_iota(jnp.int32, (LANES, LANES), 1)
    seg = (lane // ATT_HEAD_DIM == lane_c // ATT_HEAD_DIM).astype(BF16)
    row_lane = lax.broadcasted_iota(jnp.int32, (x.shape[0], LANES), 1)
    first_half = (row_lane % AXIS_DIM) < (AXIS_DIM // 2)
    upper = row_lane >= ATT_HEAD_DIM

    r0 = lb_ref[0]
    r1 = lb_ref[1]
    rm = jnp.maximum(r0, r1)
    e0 = jnp.exp(r0 - rm)
    e1 = jnp.exp(r1 - rm)
    lb_all = e0 / (e0 + e1)

    if latent:
        (qa_ref, ka_ref, vat_ref, hq_ref, hv_ref, kf_ref, bf_ref, kb_ref, bb_ref, sg_ref) = out_refs
        cos = cos_ref[...]
        sin = sin_ref[...]
    else:
        (ka_ref, vat_ref, hv_ref, kf_ref, bf_ref, kb_ref, bb_ref) = out_refs

    pk = proj(C_AK, ATT_KV_WIDTH)
    kn = pk * lax.rsqrt(_head64_meansq(pk, seg) + EPS) * gk_ref[...]
    if latent:
        kn = _rope(kn, cos, sin, first_half)
    ka_ref[0] = kn.astype(BF16)
    vat_ref[0] = proj(C_AV, ATT_KV_WIDTH).T.astype(BF16)

    if latent:
        pq = proj(C_AQ, ATT_WIDTH)
        for c in range(ATT_WIDTH // LANES):
            t = pq[:, c * LANES:(c + 1) * LANES]
            t = t * lax.rsqrt(_head64_meansq(t, seg) + EPS) * gq_ref[...]
            t = _rope(t, cos, sin, first_half) * (ATT_HEAD_DIM ** -0.5 * LOG2E)
            t_swapped = pltpu.roll(t, ATT_HEAD_DIM, 1)
            for half in range(2):
                head = 2 * c + half
                kv = head // ATT_GROUP
                src = t if half == kv else t_swapped
                keep = upper if kv == 1 else jnp.logical_not(upper)
                qa_ref[0, :, head * LANES:(head + 1) * LANES] = jnp.where(keep, src, 0.0).astype(BF16)
        hq_ref[0] = (_silu(proj(C_HQ, HG_WIDTH)) * (HG_HEAD_DIM ** -0.5)).astype(BF16)
        sg_ref[0] = _silu(proj(C_HG, HG_WIDTH)).astype(BF16)

    hv_ref[0] = proj(C_HI, HG_WIDTH).astype(BF16)
    kf, bf = _gate_prep(proj(C_FF, HG_WIDTH), lb_all[0:1, :], trif_ref[...])
    kf_ref[0] = kf
    bf_ref[0] = bf
    kb, bb = _gate_prep(proj(C_FB, HG_WIDTH), lb_all[1:2, :], trib_ref[...])
    kb_ref[0] = kb
    bb_ref[0] = bb


def _block_tri(n, suffix):
    r = np.arange(n)[:, None]
    c = np.arange(n)[None, :]
    same = (r // SUB) == (c // SUB)
    tri = (c >= r) if suffix else (c <= r)
    return jnp.asarray((same & tri).astype(np.float32), dtype=BF16)


def _input_projection(rows, mod3, mod_row, g1, w_in, gq, gk, lb_raw, cos, sin, latent):
    b, t, d = rows.shape
    tm = min(ROW_TILE, t)
    nt = t // tm
    trif = _block_tri(tm, False)
    trib = _block_tri(tm, True)
    bspec = lambda width: pl.BlockSpec((1, tm, width), lambda i, j: (i, j, 0))
    full = lambda a: pl.BlockSpec(a.shape, lambda i, j: (0,) * a.ndim)
    sds = lambda width, dt: jax.ShapeDtypeStruct((b, t, width), dt)
    vt_sds = jax.ShapeDtypeStruct((b, ATT_KV_WIDTH, t), BF16)
    vt_spec = pl.BlockSpec((1, ATT_KV_WIDTH, tm), lambda i, j: (i, 0, j))
    if latent:
        out_shape = [sds(ATT_HEADS * LANES, BF16), sds(ATT_KV_WIDTH, BF16), vt_sds,
                     sds(HG_WIDTH, BF16), sds(HG_WIDTH, BF16), sds(HG_WIDTH, BF16), sds(HG_WIDTH, F32),
                     sds(HG_WIDTH, BF16), sds(HG_WIDTH, F32), sds(HG_WIDTH, BF16)]
        out_specs = [bspec(ATT_HEADS * LANES), bspec(ATT_KV_WIDTH), vt_spec, bspec(HG_WIDTH),
                     bspec(HG_WIDTH), bspec(HG_WIDTH), bspec(HG_WIDTH), bspec(HG_WIDTH), bspec(HG_WIDTH),
                     bspec(HG_WIDTH)]
        mod_spec = pl.BlockSpec((1, 1, mod3.shape[-1]), lambda i, j: (i, 0, 0))
    else:
        out_shape = [sds(ATT_KV_WIDTH, BF16), vt_sds, sds(HG_WIDTH, BF16), sds(HG_WIDTH, BF16),
                     sds(HG_WIDTH, F32), sds(HG_WIDTH, BF16), sds(HG_WIDTH, F32)]
        out_specs = [bspec(ATT_KV_WIDTH), vt_spec, bspec(HG_WIDTH), bspec(HG_WIDTH), bspec(HG_WIDTH),
                     bspec(HG_WIDTH), bspec(HG_WIDTH)]
        mod_spec = pl.BlockSpec((1, 1, mod3.shape[-1]), lambda i, j: (mod_row, 0, 0))
    table_spec = pl.BlockSpec((tm, LANES), lambda i, j: (j, 0))
    return pl.pallas_call(
        functools.partial(_inproj_kernel, latent=latent),
        out_shape=out_shape,
        grid=(b, nt),
        in_specs=[bspec(d), mod_spec, full(g1), full(w_in), full(gq), full(gk), full(lb_raw),
                  table_spec, table_spec, full(trif), full(trib)],
        out_specs=out_specs,
        compiler_params=pltpu.CompilerParams(dimension_semantics=("parallel", "parallel"),
                                             vmem_limit_bytes=VMEM_LIMIT),
        name="inproj_latent" if latent else "inproj_context",
    )(rows, mod3, g1, w_in, gq, gk, lb_raw, cos, sin, trif, trib)


def _attention_kernel(q_ref, kc_ref, vtc_ref, kx_ref, vtx_ref, o_ref, sc_ref, s_ref):
    tq = q_ref.shape[1]
    n_kv = kx_ref.shape[1] // KV_TILE
    nt_dims = (((1,), (1,)), ((), ()))
    cols = ATT_GROUP * tq
    acc_rows = ATT_HEAD_DIM + ONES_ROWS

    qgs = [jnp.concatenate(
        [q_ref[0, :, (kv * ATT_GROUP + j) * LANES:(kv * ATT_GROUP + j + 1) * LANES] for j in range(ATT_GROUP)],
        axis=0) for kv in range(ATT_KV_HEADS)]

    def scores(kt, dst):
        for kv in range(ATT_KV_HEADS):
            dst(kv, lax.dot_general(kt, qgs[kv], nt_dims, preferred_element_type=F32))

    def consume(src, vt_all, carry):
        ones = jnp.ones((ONES_ROWS, vt_all.shape[1]), BF16)
        out = []
        for kv in range(ATT_KV_HEADS):
            m, acc = carry[kv]
            m_new = jnp.maximum(m, jnp.max(src(kv), axis=0, keepdims=True))
            alpha = jnp.exp2(m - m_new)
            p = jnp.exp2(src(kv) - m_new).astype(BF16)
            vt = jnp.concatenate([vt_all[kv * ATT_HEAD_DIM:(kv + 1) * ATT_HEAD_DIM], ones], axis=0)
            acc = alpha * acc + jnp.dot(vt, p, preferred_element_type=F32)
            out.append((m_new, acc))
        return tuple(out)

    def put_ctx(kv, val):
        sc_ref[kv] = val

    def put(slot):
        def dst(kv, val):
            s_ref[slot, kv] = val
        return dst

    def k_tile(j):
        return kx_ref[0, pl.ds(pl.multiple_of(j * KV_TILE, KV_TILE), KV_TILE), :]

    def vt_tile(j):
        return vtx_ref[0, :, pl.ds(pl.multiple_of(j * KV_TILE, KV_TILE), KV_TILE)]

    init = (jnp.full((1, cols), -jnp.inf, F32), jnp.zeros((acc_rows, cols), F32))
    scores(kc_ref[0], put_ctx)
    scores(k_tile(0), put(0))
    carry = consume(lambda kv: sc_ref[kv], vtc_ref[0], (init,) * ATT_KV_HEADS)

    def body(i, carry):
        j = 2 * i
        scores(k_tile(j + 1), put(1))
        carry = consume(lambda kv: s_ref[0, kv], vt_tile(j), carry)
        scores(k_tile(j + 2), put(0))
        return consume(lambda kv: s_ref[1, kv], vt_tile(j + 1), carry)

    carry = lax.fori_loop(0, n_kv // 2 - 1, body, carry)
    scores(k_tile(n_kv - 1), put(1))
    carry = consume(lambda kv: s_ref[0, kv], vt_tile(n_kv - 2), carry)
    carry = consume(lambda kv: s_ref[1, kv], vt_tile(n_kv - 1), carry)

    for kv in range(ATT_KV_HEADS):
        acc = carry[kv][1]
        o = acc[0:ATT_HEAD_DIM] / acc[ATT_HEAD_DIM:ATT_HEAD_DIM + 1]
        for jj in range(ATT_GROUP // 2):
            pair = jnp.concatenate([o[:, (2 * jj) * tq:(2 * jj + 1) * tq],
                                    o[:, (2 * jj + 1) * tq:(2 * jj + 2) * tq]], axis=0)
            col = kv * (ATT_GROUP // 2) + jj
            o_ref[0, :, col * LANES:(col + 1) * LANES] = pair.T.astype(BF16)


def _attention(qa, kc, vtc, kx, vtx):
    b, t, _ = qa.shape
    tq = min(Q_TILE, t)
    n_kv = t // KV_TILE
    assert tq == LANES and t % KV_TILE == 0 and n_kv >= 2 and n_kv % 2 == 0
    cols = ATT_GROUP * tq
    return pl.pallas_call(
        _attention_kernel,
        out_shape=jax.ShapeDtypeStruct((b, t, ATT_WIDTH), BF16),
        grid=(b, t // tq),
        scratch_shapes=[pltpu.VMEM((ATT_KV_HEADS, kc.shape[1], cols), F32),
                        pltpu.VMEM((2, ATT_KV_HEADS, KV_TILE, cols), F32)],
        in_specs=[pl.BlockSpec((1, tq, qa.shape[-1]), lambda i, j: (i, j, 0)),
                  pl.BlockSpec((1,) + kc.shape[1:], lambda i, j: (i, 0, 0)),
                  pl.BlockSpec((1,) + vtc.shape[1:], lambda i, j: (i, 0, 0)),
                  pl.BlockSpec((1,) + kx.shape[1:], lambda i, j: (i, 0, 0)),
                  pl.BlockSpec((1,) + vtx.shape[1:], lambda i, j: (i, 0, 0))],
        out_specs=pl.BlockSpec((1, tq, ATT_WIDTH), lambda i, j: (i, j, 0)),
        compiler_params=pltpu.CompilerParams(dimension_semantics=("parallel", "parallel"),
                                             vmem_limit_bytes=VMEM_LIMIT),
        name="attention",
    )(qa, kc, vtc, kx, vtx)


def _scan_step(st, q, k, v, vt_win, b, sub, reverse, ones, want_out):
    last = 0 if reverse else SUB - 1
    b_last = b[last:last + 1, :]
    k_dec = k * jnp.exp(b_last - b)
    before, after = sub * SUB, WIN - (sub + 1) * SUB
    k_win = jnp.concatenate(([jnp.zeros((before, LANES), F32)] if before else []) + [k_dec]
                            + ([jnp.zeros((after, LANES), F32)] if after else []), axis=0)
    st_new = jnp.exp(b_last) * st + jnp.dot(vt_win, k_win, preferred_element_type=F32)
    if not want_out:
        return st_new, None

    o = lax.dot_general(q * jnp.exp(b), st, (((1,), (1,)), ((), ())), preferred_element_type=F32)

    half = SUB // 2
    t_idx = lax.broadcasted_iota(jnp.int32, (half, LANES), 0)
    parts = []
    spans = []
    for s in range(SUB):
        if reverse:
            lo, hi = (0, SUB) if s >= half else (0, half)
        else:
            lo, hi = (0, SUB) if s < half else (half, SUB)
        w = jnp.exp(jnp.minimum(b[lo:hi] - b[s:s + 1, :], 0.0))
        p = w * q[lo:hi] * k[s:s + 1, :]
        tile0 = (s // half) * half
        pieces = []
        for r0 in range(lo, hi, half):
            piece = p[r0 - lo:r0 - lo + half]
            if r0 == tile0 and (s % half) != (half - 1 if reverse else 0):
                keep = (t_idx <= s - r0) if reverse else (t_idx >= s - r0)
                piece = jnp.where(keep, piece, 0.0)
            pieces.append(piece)
        parts.extend(pieces)
        spans.append((lo, hi))
    summed = jnp.dot(jnp.concatenate(parts, axis=0), ones, preferred_element_type=F32)
    pos = 0
    o_lo = o[0:half]
    o_hi = o[half:SUB]
    for s in range(SUB):
        lo, hi = spans[s]
        vs = v[s:s + 1, :]
        for r0 in range(lo, hi, half):
            term = summed[pos:pos + half] * vs
            pos += half
            if r0 == 0:
                o_lo = o_lo + term
            else:
                o_hi = o_hi + term
    return st_new, jnp.concatenate([o_lo, o_hi], axis=0)


def _hgrn_kernel(q_ref, v_ref, kf_ref, bf_ref, kb_ref, bb_ref, vc_ref, kfc_ref, bfc_ref, kbc_ref, bbc_ref,
                 o_ref, stf_ref, stb_ref):
    t = q_ref.shape[1]
    ctx = vc_ref.shape[1]
    nw = t // WIN
    ones = jnp.ones((LANES, LANES), BF16)
    nsub = WIN // SUB

    def window(refs, start, st, reverse, want_out):
        qr, vr, kr, br = refs
        sl = pl.ds(start, WIN)
        vw = vr[0, sl, :].astype(F32)
        kw = kr[0, sl, :].astype(F32)
        bw = br[0, sl, :]
        qw = qr[0, sl, :].astype(F32) if want_out else None
        vt_win = vw.T
        outs = [None] * nsub
        order = range(nsub - 1, -1, -1) if reverse else range(nsub)
        for sub in order:
            rs = slice(sub * SUB, (sub + 1) * SUB)
            st, o = _scan_step(st, qw[rs] if want_out else None, kw[rs], vw[rs], vt_win, bw[rs], sub, reverse,
                               ones, want_out)
            outs[sub] = o
        return st, (jnp.concatenate(outs, axis=0) if want_out else None)

    zero = jnp.zeros((LANES, LANES), F32)
    stf, stb = zero, zero
    n_cw = ctx // WIN
    for w in range(n_cw):
        stf, _ = window((None, vc_ref, kfc_ref, bfc_ref), w * WIN, stf, False, False)
        stb, _ = window((None, vc_ref, kbc_ref, bbc_ref), (n_cw - 1 - w) * WIN, stb, True, False)
    stf_ref[...] = stf
    stb_ref[...] = stb

    def body(w, accumulate):
        fs = pl.multiple_of(w * WIN, WIN)
        bs = pl.multiple_of((nw - 1 - w) * WIN, WIN)
        stf, of = window((q_ref, v_ref, kf_ref, bf_ref), fs, stf_ref[...], False, True)
        stb, ob = window((q_ref, v_ref, kb_ref, bb_ref), bs, stb_ref[...], True, True)
        stf_ref[...] = stf
        stb_ref[...] = stb
        if accumulate:
            o_ref[0, pl.ds(fs, WIN), :] += of
            o_ref[0, pl.ds(bs, WIN), :] += ob
        else:
            o_ref[0, pl.ds(fs, WIN), :] = of
            o_ref[0, pl.ds(bs, WIN), :] = ob

    def first(w, c):
        body(w, False)
        return c

    def second(w, c):
        body(w, True)
        return c

    lax.fori_loop(0, nw // 2, first, 0)
    lax.fori_loop(nw // 2, nw, second, 0)


def _hgrn_scan(hq, hv, kf, bf, kb, bb, hvc, kfc, bfc, kbc, bbc):
    b, t, _ = hq.shape
    ctx = hvc.shape[1]
    lat = pl.BlockSpec((1, t, HG_HEAD_DIM), lambda i, j: (i, 0, j))
    con = pl.BlockSpec((1, ctx, HG_HEAD_DIM), lambda i, j: (i, 0, j))
    return pl.pallas_call(
        _hgrn_kernel,
        out_shape=jax.ShapeDtypeStruct((b, t, HG_WIDTH), F32),
        grid=(b, HG_HEADS),
        in_specs=[lat] * 6 + [con] * 5,
        out_specs=lat,
        scratch_shapes=[pltpu.VMEM((LANES, LANES), F32), pltpu.VMEM((LANES, LANES), F32)],
        compiler_params=pltpu.CompilerParams(dimension_semantics=("parallel", "parallel"),
                                             vmem_limit_bytes=VMEM_LIMIT),
        name="hgrn_scan",
    )(hq, hv, kf, bf, kb, bb, hvc, kfc, bfc, kbc, bbc)


def _out_ffn_kernel(x_ref, att_ref, hg_ref, sg_ref, mod_ref, ghg_ref, wo_ref, g2_ref, wgu_ref, wd_ref, gf_ref,
                    o_ref):
    d = x_ref.shape[-1]
    dff = wd_ref.shape[0]
    x = x_ref[0]
    gate1 = mod_ref[0, :, 2 * d:3 * d]
    shift2 = mod_ref[0, :, 3 * d:4 * d]
    scale2 = mod_ref[0, :, 4 * d:5 * d]
    gate2 = mod_ref[0, :, 5 * d:6 * d]

    mix = jnp.dot(att_ref[0], wo_ref[0:ATT_WIDTH, :], preferred_element_type=F32)
    for hd in range(HG_HEADS):
        cs = slice(hd * HG_HEAD_DIM, (hd + 1) * HG_HEAD_DIM)
        o = _rms_rows(hg_ref[0, :, cs]) * ghg_ref[...]
        o = (o * sg_ref[0, :, cs].astype(F32)).astype(BF16)
        mix = mix + jnp.dot(o, wo_ref[ATT_WIDTH + hd * HG_HEAD_DIM:ATT_WIDTH + (hd + 1) * HG_HEAD_DIM, :],
                            preferred_element_type=F32)
    x1 = x + gate1 * mix
    h = ((_rms_rows(x1) * g2_ref[...]) * (1.0 + scale2) + shift2).astype(BF16)
    a = jnp.dot(h, wgu_ref[:, 0:dff], preferred_element_type=F32)
    g = jnp.dot(h, wgu_ref[:, dff:2 * dff], preferred_element_type=F32)
    act = (_silu(a) * g).astype(BF16)
    y = jnp.dot(act, wd_ref[...], preferred_element_type=F32)
    x2 = x1 + gate2 * y
    o_ref[0] = _rms_rows(x2) * gf_ref[...]


def _out_ffn(x, att, hg, sg, mod3, ghg, w_out, g2, w_gu, w_down, gf):
    b, t, d = x.shape
    tm = min(ROW_TILE, t)
    bspec = lambda width: pl.BlockSpec((1, tm, width), lambda i, j: (i, j, 0))
    full = lambda a: pl.BlockSpec(a.shape, lambda i, j: (0,) * a.ndim)
    return pl.pallas_call(
        _out_ffn_kernel,
        out_shape=jax.ShapeDtypeStruct((b, t, d), F32),
        grid=(b, t // tm),
        in_specs=[bspec(d), bspec(ATT_WIDTH), bspec(HG_WIDTH), bspec(HG_WIDTH),
                  pl.BlockSpec((1, 1, mod3.shape[-1]), lambda i, j: (i, 0, 0)),
                  full(ghg), full(w_out), full(g2), full(w_gu), full(w_down), full(gf)],
        out_specs=bspec(d),
        compiler_params=pltpu.CompilerParams(dimension_semantics=("parallel", "parallel"),
                                             vmem_limit_bytes=VMEM_LIMIT),
        name="out_ffn",
    )(x, att, hg, sg, mod3, ghg, w_out, g2, w_gu, w_down, gf)


def _rope_tables(t):
    n_rows = t // GRID_W
    row = jnp.repeat(jnp.arange(n_rows, dtype=F32), GRID_W)
    col = jnp.tile(jnp.arange(GRID_W, dtype=F32), n_rows)
    inv = ROPE_THETA ** (-jnp.arange(0, AXIS_DIM, 2, dtype=F32) / AXIS_DIM)
    ar = row[:, None] * inv
    ac = col[:, None] * inv
    ang = jnp.concatenate([ar, ar, ac, ac], axis=-1)
    sign = jnp.where((jnp.arange(ATT_HEAD_DIM) % AXIS_DIM) < AXIS_DIM // 2, -1.0, 1.0).astype(F32)
    reps = LANES // ATT_HEAD_DIM
    return jnp.tile(jnp.cos(ang), (1, reps)), jnp.tile(jnp.sin(ang) * sign, (1, reps))


def kernel(x, c, ctx, c_ctx, w_mod, b_mod, g_norm1, w_in, g_q, g_k, lb_raw, g_hg, w_out, g_norm2, w_gu, w_down,
           g_final):
    b, t, d = x.shape
    assert w_mod.shape[0] == 1 and lb_raw.shape[0] == 2, "single-layer block"
    assert t % WIN == 0 and (t // WIN) % 2 == 0 and ctx.shape[1] % WIN == 0 and t % KV_TILE == 0

    pad = (-(b + 1)) % 8
    cc = jnp.concatenate([c, c_ctx[None, :], jnp.zeros((pad, d), F32)], axis=0)
    mods = _modulation(cc, w_mod[0], b_mod)
    mod3 = mods.reshape(mods.shape[0], 1, mods.shape[1])

    cos, sin = _rope_tables(t)
    reps = LANES // ATT_HEAD_DIM
    gq = jnp.tile(g_q[0], reps)[None, :]
    gk = jnp.tile(g_k[0], reps)[None, :]
    w_in_b = w_in[0].astype(BF16)

    qa, kx, vtx, hq, hv, kf, bf, kb, bb, sg = _input_projection(
        x, mod3, None, g_norm1, w_in_b, gq, gk, lb_raw, cos, sin, True)
    kc, vtc, hvc, kfc, bfc, kbc, bbc = _input_projection(
        ctx, mod3, b, g_norm1, w_in_b, gq, gk, lb_raw, cos, sin, False)

    att = _attention(qa, kc, vtc, kx, vtx)
    hg = _hgrn_scan(hq, hv, kf, bf, kb, bb, hvc, kfc, bfc, kbc, bbc)
    return _out_ffn(x, att, hg, sg, mod3, g_hg, w_out[0].astype(BF16), g_norm2, w_gu[0].astype(BF16),
                    w_down[0].astype(BF16), g_final[None, :])
```

```python
import functools

import jax
import jax.numpy as jnp
import numpy as np
from jax import lax
from jax.experimental import pallas as pl
from jax.experimental.pallas import tpu as pltpu

F32 = jnp.float32
BF16 = jnp.bfloat16

LANES = 128
GRID_W = 64
ATT_HEADS = 8
ATT_KV_HEADS = 2
ATT_GROUP = ATT_HEADS // ATT_KV_HEADS
ATT_HEAD_DIM = 64
ATT_WIDTH = ATT_HEADS * ATT_HEAD_DIM
ATT_KV_WIDTH = ATT_KV_HEADS * ATT_HEAD_DIM
AXIS_DIM = ATT_HEAD_DIM // 2
ROPE_THETA = 10000.0
HG_HEADS = 4
HG_HEAD_DIM = 128
HG_WIDTH = HG_HEADS * HG_HEAD_DIM
EPS = 1e-6

SUB = 16
WIN = 128
ROW_TILE = 256
Q_TILE = 128
KV_TILE = 512
ONES_ROWS = 16
LOG2E = 1.4426950408889634
VMEM_LIMIT = 56 * 1024 * 1024

C_AQ = 0
C_AK = C_AQ + ATT_WIDTH
C_AV = C_AK + ATT_KV_WIDTH
C_HQ = C_AV + ATT_KV_WIDTH
C_HI = C_HQ + HG_WIDTH
C_FF = C_HI + HG_WIDTH
C_FB = C_FF + HG_WIDTH
C_HG = C_FB + HG_WIDTH
C_END = C_HG + HG_WIDTH


def _silu(v):
    return v * jax.nn.sigmoid(v)


def _rms_rows(v):
    return v * lax.rsqrt(jnp.mean(v * v, axis=-1, keepdims=True) + EPS)


def _mod_kernel(c_ref, w_ref, b_ref, o_ref):
    a = _silu(c_ref[...])
    o_ref[...] = jnp.dot(a, w_ref[...], precision=lax.Precision.HIGHEST,
                         preferred_element_type=F32) + b_ref[...]


def _modulation(cc, w_mod, b_mod):
    rows, d = cc.shape
    n = w_mod.shape[1]
    bn = n // 4
    return pl.pallas_call(
        _mod_kernel,
        out_shape=jax.ShapeDtypeStruct((rows, n), F32),
        grid=(n // bn,),
        in_specs=[pl.BlockSpec((rows, d), lambda j: (0, 0)),
                  pl.BlockSpec((d, bn), lambda j: (0, j)),
                  pl.BlockSpec((1, bn), lambda j: (0, j))],
        out_specs=pl.BlockSpec((rows, bn), lambda j: (0, j)),
        compiler_params=pltpu.CompilerParams(dimension_semantics=("arbitrary",),
                                             vmem_limit_bytes=VMEM_LIMIT),
        name="modulation",
    )(cc, w_mod, b_mod)


def _split3(v):
    hi = v.astype(BF16)
    r = v - hi.astype(F32)
    mid = r.astype(BF16)
    lo = (r - mid.astype(F32)).astype(BF16)
    return hi, mid, lo


def _head64_meansq(p, seg):
    sq = p * p
    hi = sq.astype(BF16)
    lo = (sq - hi.astype(F32)).astype(BF16)
    s = jnp.dot(hi, seg, preferred_element_type=F32) + jnp.dot(lo, seg, preferred_element_type=F32)
    return s * (1.0 / ATT_HEAD_DIM)


def _rope(v, cos, sin_signed, first_half):
    fwd = pltpu.roll(v, LANES - AXIS_DIM // 2, 1)
    bwd = pltpu.roll(v, AXIS_DIM // 2, 1)
    return v * cos + jnp.where(first_half, fwd, bwd) * sin_signed


def _gate_prep(f, lb, tri):
    fg = lb + (1.0 - lb) * jax.nn.sigmoid(f)
    lf = jnp.log2(fg)
    hi, mid, lo = _split3(lf)
    cum = (jnp.dot(tri, hi, preferred_element_type=F32) + jnp.dot(tri, mid, preferred_element_type=F32)
           + jnp.dot(tri, lo, preferred_element_type=F32))
    return (1.0 - fg).astype(BF16), cum


def _inproj_kernel(x_ref, mod_ref, g1_ref, w_ref, gq_ref, gk_ref, lb_ref, cos_ref, sin_ref, trif_ref, trib_ref,
                   *out_refs, latent):
    d = x_ref.shape[-1]
    x = x_ref[0]
    shift = mod_ref[0, :, 0:d]
    scale = mod_ref[0, :, d:2 * d]
    h = (_rms_rows(x) * g1_ref[...]) * (1.0 + scale) + shift
    hb = h.astype(BF16)

    def proj(c0, width):
        return jnp.dot(hb, w_ref[:, c0:c0 + width], preferred_element_type=F32)

    lane = lax.broadcasted_iota(jnp.int32, (LANES, LANES), 0)
    lane_c = lax.broadcasted_iota(jnp.int32, (LANES, LANES), 1)
    seg = (lane // ATT_HEAD_DIM == lane_c // ATT_HEAD_DIM).astype(BF16)
    row_lane = lax.broadcasted_iota(jnp.int32, (x.shape[0], LANES), 1)
    first_half = (row_lane % AXIS_DIM) < (AXIS_DIM // 2)
    upper = row_lane >= ATT_HEAD_DIM

    r0 = lb_ref[0]
    r1 = lb_ref[1]
    rm = jnp.maximum(r0, r1)
    e0 = jnp.exp(r0 - rm)
    e1 = jnp.exp(r1 - rm)
    lb_all = e0 / (e0 + e1)

    if latent:
        (qa_ref, ka_ref, vat_ref, hq_ref, hv_ref, kf_ref, bf_ref, kb_ref, bb_ref, sg_ref) = out_refs
        cos = cos_ref[...]
        sin = sin_ref[...]
    else:
        (ka_ref, vat_ref, hv_ref, kf_ref, bf_ref, kb_ref, bb_ref) = out_refs

    pk = proj(C_AK, ATT_KV_WIDTH)
    kn = pk * lax.rsqrt(_head64_meansq(pk, seg) + EPS) * gk_ref[...]
    if latent:
        kn = _rope(kn, cos, sin, first_half)
    ka_ref[0] = kn.astype(BF16)
    vat_ref[0] = proj(C_AV, ATT_KV_WIDTH).T.astype(BF16)

    if latent:
        pq = proj(C_AQ, ATT_WIDTH)
        for c in range(ATT_WIDTH // LANES):
            t = pq[:, c * LANES:(c + 1) * LANES]
            t = t * lax.rsqrt(_head64_meansq(t, seg) + EPS) * gq_ref[...]
            t = _rope(t, cos, sin, first_half) * (ATT_HEAD_DIM ** -0.5 * LOG2E)
            t_swapped = pltpu.roll(t, ATT_HEAD_DIM, 1)
            for half in range(2):
                head = 2 * c + half
                kv = head // ATT_GROUP
                src = t if half == kv else t_swapped
                keep = upper if kv == 1 else jnp.logical_not(upper)
                qa_ref[0, :, head * LANES:(head + 1) * LANES] = jnp.where(keep, src, 0.0).astype(BF16)
        hq_ref[0] = (_silu(proj(C_HQ, HG_WIDTH)) * (HG_HEAD_DIM ** -0.5)).astype(BF16)
        sg_ref[0] = _silu(proj(C_HG, HG_WIDTH)).astype(BF16)

    hv_ref[0] = proj(C_HI, HG_WIDTH).astype(BF16)
    kf, bf = _gate_prep(proj(C_FF, HG_WIDTH), lb_all[0:1, :], trif_ref[...])
    kf_ref[0] = kf
    bf_ref[0] = bf
    kb, bb = _gate_prep(proj(C_FB, HG_WIDTH), lb_all[1:2, :], trib_ref[...])
    kb_ref[0] = kb
    bb_ref[0] = bb


def _block_tri(n, suffix):
    r = np.arange(n)[:, None]
    c = np.arange(n)[None, :]
    same = (r // SUB) == (c // SUB)
    tri = (c >= r) if suffix else (c <= r)
    return jnp.asarray((same & tri).astype(np.float32), dtype=BF16)


def _input_projection(rows, mod3, mod_row, g1, w_in, gq, gk, lb_raw, cos, sin, latent):
    b, t, d = rows.shape
    tm = min(ROW_TILE, t)
    nt = t // tm
    trif = _block_tri(tm, False)
    trib = _block_tri(tm, True)
    bspec = lambda width: pl.BlockSpec((1, tm, width), lambda i, j: (i, j, 0))
    full = lambda a: pl.BlockSpec(a.shape, lambda i, j: (0,) * a.ndim)
    sds = lambda width, dt: jax.ShapeDtypeStruct((b, t, width), dt)
    vt_sds = jax.ShapeDtypeStruct((b, ATT_KV_WIDTH, t), BF16)
    vt_spec = pl.BlockSpec((1, ATT_KV_WIDTH, tm), lambda i, j: (i, 0, j))
    if latent:
        out_shape = [sds(ATT_HEADS * LANES, BF16), sds(ATT_KV_WIDTH, BF16), vt_sds,
                     sds(HG_WIDTH, BF16), sds(HG_WIDTH, BF16), sds(HG_WIDTH, BF16), sds(HG_WIDTH, F32),
                     sds(HG_WIDTH, BF16), sds(HG_WIDTH, F32), sds(HG_WIDTH, BF16)]
        out_specs = [bspec(ATT_HEADS * LANES), bspec(ATT_KV_WIDTH), vt_spec, bspec(HG_WIDTH),
                     bspec(HG_WIDTH), bspec(HG_WIDTH), bspec(HG_WIDTH), bspec(HG_WIDTH), bspec(HG_WIDTH),
                     bspec(HG_WIDTH)]
        mod_spec = pl.BlockSpec((1, 1, mod3.shape[-1]), lambda i, j: (i, 0, 0))
    else:
        out_shape = [sds(ATT_KV_WIDTH, BF16), vt_sds, sds(HG_WIDTH, BF16), sds(HG_WIDTH, BF16),
                     sds(HG_WIDTH, F32), sds(HG_WIDTH, BF16), sds(HG_WIDTH, F32)]
        out_specs = [bspec(ATT_KV_WIDTH), vt_spec, bspec(HG_WIDTH), bspec(HG_WIDTH), bspec(HG_WIDTH),
                     bspec(HG_WIDTH), bspec(HG_WIDTH)]
        mod_spec = pl.BlockSpec((1, 1, mod3.shape[-1]), lambda i, j: (mod_row, 0, 0))
    table_spec = pl.BlockSpec((tm, LANES), lambda i, j: (j, 0))
    return pl.pallas_call(
        functools.partial(_inproj_kernel, latent=latent),
        out_shape=out_shape,
        grid=(b, nt),
        in_specs=[bspec(d), mod_spec, full(g1), full(w_in), full(gq), full(gk), full(lb_raw),
                  table_spec, table_spec, full(trif), full(trib)],
        out_specs=out_specs,
        compiler_params=pltpu.CompilerParams(dimension_semantics=("parallel", "parallel"),
                                             vmem_limit_bytes=VMEM_LIMIT),
        name="inproj_latent" if latent else "inproj_context",
    )(rows, mod3, g1, w_in, gq, gk, lb_raw, cos, sin, trif, trib)


def _attention_kernel(q_ref, kc_ref, vtc_ref, kx_ref, vtx_ref, o_ref, sc_ref, s_ref):
    tq = q_ref.shape[1]
    n_kv = kx_ref.shape[1] // KV_TILE
    nt_dims = (((1,), (1,)), ((), ()))
    cols = ATT_GROUP * tq
    acc_rows = ATT_HEAD_DIM + ONES_ROWS

    qgs = [jnp.concatenate(
        [q_ref[0, :, (kv * ATT_GROUP + j) * LANES:(kv * ATT_GROUP + j + 1) * LANES] for j in range(ATT_GROUP)],
        axis=0) for kv in range(ATT_KV_HEADS)]

    def scores(kt, dst):
        for kv in range(ATT_KV_HEADS):
            dst(kv, lax.dot_general(kt, qgs[kv], nt_dims, preferred_element_type=F32))

    def consume(src, vt_all, carry):
        ones = jnp.ones((ONES_ROWS, vt_all.shape[1]), BF16)
        out = []
        for kv in range(ATT_KV_HEADS):
            m, acc = carry[kv]
            m_new = jnp.maximum(m, jnp.max(src(kv), axis=0, keepdims=True))
            alpha = jnp.exp2(m - m_new)
            p = jnp.exp2(src(kv) - m_new).astype(BF16)
            vt = jnp.concatenate([vt_all[kv * ATT_HEAD_DIM:(kv + 1) * ATT_HEAD_DIM], ones], axis=0)
            acc = alpha * acc + jnp.dot(vt, p, preferred_element_type=F32)
            out.append((m_new, acc))
        return tuple(out)

    def put_ctx(kv, val):
        sc_ref[kv] = val

    def put(slot):
        def dst(kv, val):
            s_ref[slot, kv] = val
        return dst

    def k_tile(j):
        return kx_ref[0, pl.ds(pl.multiple_of(j * KV_TILE, KV_TILE), KV_TILE), :]

    def vt_tile(j):
        return vtx_ref[0, :, pl.ds(pl.multiple_of(j * KV_TILE, KV_TILE), KV_TILE)]

    init = (jnp.full((1, cols), -jnp.inf, F32), jnp.zeros((acc_rows, cols), F32))
    scores(kc_ref[0], put_ctx)
    scores(k_tile(0), put(0))
    carry = consume(lambda kv: sc_ref[kv], vtc_ref[0], (init,) * ATT_KV_HEADS)

    def body(i, carry):
        j = 2 * i
        scores(k_tile(j + 1), put(1))
        carry = consume(lambda kv: s_ref[0, kv], vt_tile(j), carry)
        scores(k_tile(j + 2), put(0))
        return consume(lambda kv: s_ref[1, kv], vt_tile(j + 1), carry)

    carry = lax.fori_loop(0, n_kv // 2 - 1, body, carry)
    scores(k_tile(n_kv - 1), put(1))
    carry = consume(lambda kv: s_ref[0, kv], vt_tile(n_kv - 2), carry)
    carry = consume(lambda kv: s_ref[1, kv], vt_tile(n_kv - 1), carry)

    for kv in range(ATT_KV_HEADS):
        acc = carry[kv][1]
        o = acc[0:ATT_HEAD_DIM] / acc[ATT_HEAD_DIM:ATT_HEAD_DIM + 1]
        for jj in range(ATT_GROUP // 2):
            pair = jnp.concatenate([o[:, (2 * jj) * tq:(2 * jj + 1) * tq],
                                    o[:, (2 * jj + 1) * tq:(2 * jj + 2) * tq]], axis=0)
            col = kv * (ATT_GROUP // 2) + jj
            o_ref[0, :, col * LANES:(col + 1) * LANES] = pair.T.astype(BF16)


def _attention(qa, kc, vtc, kx, vtx):
    b, t, _ = qa.shape
    tq = min(Q_TILE, t)
    n_kv = t // KV_TILE
    assert tq == LANES and t % KV_TILE == 0 and n_kv >= 2 and n_kv % 2 == 0
    cols = ATT_GROUP * tq
    return pl.pallas_call(
        _attention_kernel,
        out_shape=jax.ShapeDtypeStruct((b, t, ATT_WIDTH), BF16),
        grid=(b, t // tq),
        scratch_shapes=[pltpu.VMEM((ATT_KV_HEADS, kc.shape[1], cols), F32),
                        pltpu.VMEM((2, ATT_KV_HEADS, KV_TILE, cols), F32)],
        in_specs=[pl.BlockSpec((1, tq, qa.shape[-1]), lambda i, j: (i, j, 0)),
                  pl.BlockSpec((1,) + kc.shape[1:], lambda i, j: (i, 0, 0)),
                  pl.BlockSpec((1,) + vtc.shape[1:], lambda i, j: (i, 0, 0)),
                  pl.BlockSpec((1,) + kx.shape[1:], lambda i, j: (i, 0, 0)),
                  pl.BlockSpec((1,) + vtx.shape[1:], lambda i, j: (i, 0, 0))],
        out_specs=pl.BlockSpec((1, tq, ATT_WIDTH), lambda i, j: (i, j, 0)),
        compiler_params=pltpu.CompilerParams(dimension_semantics=("parallel", "parallel"),
                                             vmem_limit_bytes=VMEM_LIMIT),
        name="attention",
    )(qa, kc, vtc, kx, vtx)


NT_DIMS = (((1,), (1,)), ((), ()))


def _zero_rows(n):
    return [jnp.zeros((n, LANES), F32)] if n else []


def _scan_window(st, q, k, v, b, b_row, k_row, reverse, want_out):
    nsub = WIN // SUB
    half = SUB // 2
    last = 0 if reverse else SUB - 1
    sub_of = (lambda p: nsub - 1 - p) if reverse else (lambda p: p)
    blk = lambda a, i: a[i * SUB:(i + 1) * SUB]

    b_last = [b_row(i * SUB + last) for i in range(nsub)]
    cum = []
    for p in range(nsub):
        cum.append(b_last[sub_of(p)] if p == 0 else cum[-1] + b_last[sub_of(p)])
    ks = [blk(k, i) * jnp.exp2(b_last[i] - blk(b, i)) for i in range(nsub)]

    k_end = jnp.concatenate(
        [ks[i] if sub_of(i) == nsub - 1 else ks[i] * jnp.exp2(cum[nsub - 1] - cum[sub_of(i)]) for i in range(nsub)],
        axis=0)
    st_new = jnp.exp2(cum[nsub - 1]) * st + jnp.dot(v.T, k_end, preferred_element_type=F32)
    if not want_out:
        return st_new, None

    qs = [blk(q, i) * jnp.exp2(blk(b, i)) for i in range(nsub)]
    q_start = jnp.concatenate(
        [qs[i] if sub_of(i) == 0 else qs[i] * jnp.exp2(cum[sub_of(i) - 1]) for i in range(nsub)], axis=0)
    o_carried = lax.dot_general(q_start, st, NT_DIMS, preferred_element_type=F32)

    rows = [[] for _ in range(nsub)]
    for h in (nsub // 2, nsub // 4, nsub // 8):
        for ub in range(1, nsub // h, 2):
            ref = ub * h - 1
            upper = {sub_of(p): p for p in range(ub * h, (ub + 1) * h)}
            lower = {sub_of(p): p for p in range((ub - 1) * h, ub * h)}
            iu, il = sorted(upper), sorted(lower)
            lhs = jnp.concatenate(
                [qs[i] if upper[i] - 1 == ref else qs[i] * jnp.exp2(cum[upper[i] - 1] - cum[ref]) for i in iu], axis=0)
            rhs = jnp.concatenate(
                _zero_rows(il[0] * SUB)
                + [ks[i] if lower[i] == ref else ks[i] * jnp.exp2(cum[ref] - cum[lower[i]]) for i in il]
                + _zero_rows((nsub - 1 - il[-1]) * SUB), axis=0)
            c = lax.dot_general(lhs, rhs, NT_DIMS, preferred_element_type=F32)
            for n, i in enumerate(iu):
                rows[i].append(c[n * SUB:(n + 1) * SUB])

    ones = jnp.ones((LANES, LANES), BF16)
    lane =lax.broadcasted_iota(jnp.int32, (half, LANES), 1)
    t_loc = lax.broadcasted_iota(jnp.int32, (half, LANES), 0)
    a_rows = []
    for i in range(nsub):
        bi, qi = blk(b, i), blk(q, i)
        parts, meta = [], []
        for s in range(SUB):
            if reverse:
                lo, hi = (0, SUB) if s >= half else (0, half)
            else:
                lo, hi = (0, SUB) if s < half else (half, SUB)
            w = jnp.exp2(jnp.minimum(bi[lo:hi] - b_row(i * SUB + s), 0.0))
            p = w * qi[lo:hi] * k_row(i * SUB + s)
            for r0 in range(lo, hi, half):
                parts.append(p[r0 - lo:r0 - lo + half])
                meta.append((s, r0 // half))
        summed = jnp.dot(jnp.concatenate(parts[0::2], axis=0), ones, preferred_element_type=F32)
        diag = [jnp.zeros((half, LANES), F32), jnp.zeros((half, LANES), F32)]
        for n, (s, hh) in enumerate(meta):
            if n % 2 == 0:
                r = summed[(n // 2) * half:(n // 2 + 1) * half]
            else:
                r = jnp.sum(parts[n], axis=-1, keepdims=True)
            diag[hh] = jnp.where(lane == i * SUB + s, r, diag[hh])
        tiles = []
        for hh in range(2):
            s_loc = lane - (i * SUB + hh * half)
            keep = (s_loc >= t_loc) if reverse else (s_loc <= t_loc)
            tiles.append(jnp.where(keep, diag[hh], 0.0))
        a = jnp.concatenate(tiles, axis=0)
        for c in rows[i]:
            a = a + c
        a_rows.append(a)
    scores = jnp.concatenate(a_rows, axis=0)
    return st_new, (scores.astype(BF16), o_carried)


def _hgrn_kernel(q_ref, v_ref, kf_ref, bf_ref, kb_ref, bb_ref, vc_ref, kfc_ref, bfc_ref, kbc_ref, bbc_ref,
                 o_ref, stf_ref, stb_ref, kwf_ref, kwb_ref, scf_ref, scb_ref, ocf_ref, ocb_ref):
    t = q_ref.shape[1]
    ctx = vc_ref.shape[1]
    nw = t // WIN

    def window(refs, start, st, kw_ref, reverse, want_out):
        qr, vr, kr, br = refs
        sl = pl.ds(start, WIN)
        vw = vr[0, sl, :].astype(F32)
        kw = kr[0, sl, :].astype(F32)
        bw = br[0, sl, :]
        qw = qr[0, sl, :].astype(F32) if want_out else None
        kw_ref[...] = kw
        b_row = lambda r: br[0, pl.ds(start + r, 1), :]
        k_row = lambda r: kw_ref[pl.ds(r, 1), :]
        return _scan_window(st, qw, kw, vw, bw, b_row, k_row, reverse, want_out)

    zero = jnp.zeros((LANES, LANES), F32)
    stf, stb = zero, zero
    n_cw = ctx // WIN
    for w in range(n_cw):
        stf, _ = window((None, vc_ref, kfc_ref, bfc_ref), w * WIN, stf, kwf_ref, False, False)
        stb, _ = window((None, vc_ref, kbc_ref, bbc_ref), (n_cw - 1 - w) * WIN, stb, kwb_ref, True, False)
    stf_ref[...] = stf
    stb_ref[...] = stb

    def starts(w):
        return pl.multiple_of(w * WIN, WIN), pl.multiple_of((nw - 1 - w) * WIN, WIN)

    def scores_stage(w):
        fs, bs = starts(w)
        stf, (sf, cf) = window((q_ref, v_ref, kf_ref, bf_ref), fs, stf_ref[...], kwf_ref, False, True)
        stb, (sb, cb) = window((q_ref, v_ref, kb_ref, bb_ref), bs, stb_ref[...], kwb_ref, True, True)
        stf_ref[...] = stf
        stb_ref[...] = stb
        scf_ref[...] = sf
        scb_ref[...] = sb
        ocf_ref[...] = cf
        ocb_ref[...] = cb

    def output_stage(w, accumulate):
        fs, bs = starts(w)
        of = jnp.dot(scf_ref[...], v_ref[0, pl.ds(fs, WIN), :], preferred_element_type=F32) + ocf_ref[...]
        ob = jnp.dot(scb_ref[...], v_ref[0, pl.ds(bs, WIN), :], preferred_element_type=F32) + ocb_ref[...]
        if accumulate:
            o_ref[0, pl.ds(fs, WIN), :] += of
            o_ref[0, pl.ds(bs, WIN), :] += ob
        else:
            o_ref[0, pl.ds(fs, WIN), :] = of
            o_ref[0, pl.ds(bs, WIN), :] = ob

    def step(w, accumulate):
        output_stage(w - 1, accumulate)
        scores_stage(w)

    def first(w, c):
        step(w, False)
        return c

    def second(w, c):
        step(w, True)
        return c

    scores_stage(0)
    lax.fori_loop(1, nw // 2 + 1, first, 0)
    lax.fori_loop(nw // 2 + 1, nw, second, 0)
    output_stage(nw - 1, True)


def _hgrn_scan(hq, hv, kf, bf, kb, bb, hvc, kfc, bfc, kbc, bbc):
    b, t, _ = hq.shape
    ctx = hvc.shape[1]
    lat = pl.BlockSpec((1, t, HG_HEAD_DIM), lambda i, j: (i, 0, j))
    con = pl.BlockSpec((1, ctx, HG_HEAD_DIM), lambda i, j: (i, 0, j))
    return pl.pallas_call(
        _hgrn_kernel,
        out_shape=jax.ShapeDtypeStruct((b, t, HG_WIDTH), F32),
        grid=(b, HG_HEADS),
        in_specs=[lat] * 6 + [con] * 5,
        out_specs=lat,
        scratch_shapes=[pltpu.VMEM((LANES, LANES), F32), pltpu.VMEM((LANES, LANES), F32),
                        pltpu.VMEM((WIN, LANES), F32), pltpu.VMEM((WIN, LANES), F32),
                        pltpu.VMEM((WIN, WIN), BF16), pltpu.VMEM((WIN, WIN), BF16),
                        pltpu.VMEM((WIN, LANES), F32), pltpu.VMEM((WIN, LANES), F32)],
        compiler_params=pltpu.CompilerParams(dimension_semantics=("parallel", "parallel"),
                                             vmem_limit_bytes=VMEM_LIMIT),
        name="hgrn_scan",
    )(hq, hv, kf, bf, kb, bb, hvc, kfc, bfc, kbc, bbc)


def _out_ffn_kernel(x_ref, att_ref, hg_ref, sg_ref, mod_ref, ghg_ref, wo_ref, g2_ref, wgu_ref, wd_ref, gf_ref,
                    o_ref):
    d = x_ref.shape[-1]
    dff = wd_ref.shape[0]
    x = x_ref[0]
    gate1 = mod_ref[0, :, 2 * d:3 * d]
    shift2 = mod_ref[0, :, 3 * d:4 * d]
    scale2 = mod_ref[0, :, 4 * d:5 * d]
    gate2 = mod_ref[0, :, 5 * d:6 * d]

    mix = jnp.dot(att_ref[0], wo_ref[0:ATT_WIDTH, :], preferred_element_type=F32)
    for hd in range(HG_HEADS):
        cs = slice(hd * HG_HEAD_DIM, (hd + 1) * HG_HEAD_DIM)
        o = _rms_rows(hg_ref[0, :, cs]) * ghg_ref[...]
        o = (o * sg_ref[0, :, cs].astype(F32)).astype(BF16)
        mix = mix + jnp.dot(o, wo_ref[ATT_WIDTH + hd * HG_HEAD_DIM:ATT_WIDTH + (hd + 1) * HG_HEAD_DIM, :],
                            preferred_element_type=F32)
    x1 = x + gate1 * mix
    h = ((_rms_rows(x1) * g2_ref[...]) * (1.0 + scale2) + shift2).astype(BF16)
    a = jnp.dot(h, wgu_ref[:, 0:dff], preferred_element_type=F32)
    g = jnp.dot(h, wgu_ref[:, dff:2 * dff], preferred_element_type=F32)
    act = (_silu(a) * g).astype(BF16)
    y = jnp.dot(act, wd_ref[...], preferred_element_type=F32)
    x2 = x1 + gate2 * y
    o_ref[0] = _rms_rows(x2) * gf_ref[...]


def _out_ffn(x, att, hg, sg, mod3, ghg, w_out, g2, w_gu, w_down, gf):
    b, t, d = x.shape
    tm = min(ROW_TILE, t)
    bspec = lambda width: pl.BlockSpec((1, tm, width), lambda i, j: (i, j, 0))
    full = lambda a: pl.BlockSpec(a.shape, lambda i, j: (0,) * a.ndim)
    return pl.pallas_call(
        _out_ffn_kernel,
        out_shape=jax.ShapeDtypeStruct((b, t, d), F32),
        grid=(b, t // tm),
        in_specs=[bspec(d), bspec(ATT_WIDTH), bspec(HG_WIDTH), bspec(HG_WIDTH),
                  pl.BlockSpec((1, 1, mod3.shape[-1]), lambda i, j: (i, 0, 0)),
                  full(ghg), full(w_out), full(g2), full(w_gu), full(w_down), full(gf)],
        out_specs=bspec(d),
        compiler_params=pltpu.CompilerParams(dimension_semantics=("parallel", "parallel"),
                                             vmem_limit_bytes=VMEM_LIMIT),
        name="out_ffn",
    )(x, att, hg, sg, mod3, ghg, w_out, g2, w_gu, w_down, gf)


def _rope_tables(t):
    n_rows = t // GRID_W
    row = jnp.repeat(jnp.arange(n_rows, dtype=F32), GRID_W)
    col = jnp.tile(jnp.arange(GRID_W, dtype=F32), n_rows)
    inv = ROPE_THETA ** (-jnp.arange(0, AXIS_DIM, 2, dtype=F32) / AXIS_DIM)
    ar = row[:, None] * inv
    ac = col[:, None] * inv
    ang = jnp.concatenate([ar, ar, ac, ac], axis=-1)
    sign = jnp.where((jnp.arange(ATT_HEAD_DIM) % AXIS_DIM) < AXIS_DIM // 2, -1.0, 1.0).astype(F32)
    reps = LANES // ATT_HEAD_DIM
    return jnp.tile(jnp.cos(ang), (1, reps)), jnp.tile(jnp.sin(ang) * sign, (1, reps))


def kernel(x, c, ctx, c_ctx, w_mod, b_mod, g_norm1, w_in, g_q, g_k, lb_raw, g_hg, w_out, g_norm2, w_gu, w_down,
           g_final):
    b, t, d = x.shape
    assert w_mod.shape[0] == 1 and lb_raw.shape[0] == 2, "single-layer block"
    assert t % WIN == 0 and (t // WIN) % 2 == 0 and ctx.shape[1] % WIN == 0 and t % KV_TILE == 0

    pad = (-(b + 1)) % 8
    cc = jnp.concatenate([c, c_ctx[None, :], jnp.zeros((pad, d), F32)], axis=0)
    mods = _modulation(cc, w_mod[0], b_mod)
    mod3 = mods.reshape(mods.shape[0], 1, mods.shape[1])

    cos, sin = _rope_tables(t)
    reps = LANES // ATT_HEAD_DIM
    gq = jnp.tile(g_q[0], reps)[None, :]
    gk = jnp.tile(g_k[0], reps)[None, :]
    w_in_b = w_in[0].astype(BF16)

    qa, kx, vtx, hq, hv, kf, bf, kb, bb, sg = _input_projection(
        x, mod3, None, g_norm1, w_in_b, gq, gk, lb_raw, cos, sin, True)
    kc, vtc, hvc, kfc, bfc, kbc, bbc = _input_projection(
        ctx, mod3, b, g_norm1, w_in_b, gq, gk, lb_raw, cos, sin, False)

    att = _attention(qa, kc, vtc, kx, vtx)
    hg = _hgrn_scan(hq, hv, kf, bf, kb, bb, hvc, kfc, bfc, kbc, bbc)
    return _out_ffn(x, att, hg, sg, mod3, g_hg, w_out[0].astype(BF16), g_norm2, w_gu[0].astype(BF16),
                    w_down[0].astype(BF16), g_final[None, :])
```

```python
import functools

import jax
import jax.numpy as jnp
import numpy as np
from jax import lax
from jax.experimental import pallas as pl
from jax.experimental.pallas import tpu as pltpu

F32 = jnp.float32
BF16 = jnp.bfloat16

LANES = 128
GRID_W = 64
ATT_HEADS = 8
ATT_KV_HEADS = 2
ATT_GROUP = ATT_HEADS // ATT_KV_HEADS
ATT_HEAD_DIM = 64
ATT_WIDTH = ATT_HEADS * ATT_HEAD_DIM
ATT_KV_WIDTH = ATT_KV_HEADS * ATT_HEAD_DIM
AXIS_DIM = ATT_HEAD_DIM // 2
ROPE_THETA = 10000.0
HG_HEADS = 4
HG_HEAD_DIM = 128
HG_WIDTH = HG_HEADS * HG_HEAD_DIM
EPS = 1e-6

SUB = 16
WIN = 128
ROW_TILE = 256
Q_TILE = 256
KV_TILE = 512
KV_SLOTS = 4
KV_AHEAD = 2
ONES_ROWS = 16
LOG2E = 1.4426950408889634
VMEM_LIMIT = 56 * 1024 * 1024

C_AQ = 0
C_AK = C_AQ + ATT_WIDTH
C_AV = C_AK + ATT_KV_WIDTH
C_HQ = C_AV + ATT_KV_WIDTH
C_HI = C_HQ + HG_WIDTH
C_FF = C_HI + HG_WIDTH
C_FB = C_FF + HG_WIDTH
C_HG = C_FB + HG_WIDTH
C_END = C_HG + HG_WIDTH


def _silu(v):
    return v * jax.nn.sigmoid(v)


def _rms_rows(v):
    return v * lax.rsqrt(jnp.mean(v * v, axis=-1, keepdims=True) + EPS)


def _mod_kernel(c_ref, w_ref, b_ref, o_ref):
    a = _silu(c_ref[...])
    o_ref[...] = jnp.dot(a, w_ref[...], precision=lax.Precision.HIGHEST,
                         preferred_element_type=F32) + b_ref[...]


def _modulation(cc, w_mod, b_mod):
    rows, d = cc.shape
    n = w_mod.shape[1]
    bn = n // 4
    return pl.pallas_call(
        _mod_kernel,
        out_shape=jax.ShapeDtypeStruct((rows, n), F32),
        grid=(n // bn,),
        in_specs=[pl.BlockSpec((rows, d), lambda j: (0, 0)),
                  pl.BlockSpec((d, bn), lambda j: (0, j)),
                  pl.BlockSpec((1, bn), lambda j: (0, j))],
        out_specs=pl.BlockSpec((rows, bn), lambda j: (0, j)),
        compiler_params=pltpu.CompilerParams(dimension_semantics=("arbitrary",),
                                             vmem_limit_bytes=VMEM_LIMIT),
        name="modulation",
    )(cc, w_mod, b_mod)


def _split3(v):
    hi = v.astype(BF16)
    r = v - hi.astype(F32)
    mid = r.astype(BF16)
    lo = (r - mid.astype(F32)).astype(BF16)
    return hi, mid, lo


def _head64_meansq(p, seg):
    sq = p * p
    hi = sq.astype(BF16)
    lo = (sq - hi.astype(F32)).astype(BF16)
    s = jnp.dot(hi, seg, preferred_element_type=F32) + jnp.dot(lo, seg, preferred_element_type=F32)
    return s * (1.0 / ATT_HEAD_DIM)


def _rope(v, cos, sin_signed, first_half):
    fwd = pltpu.roll(v, LANES - AXIS_DIM // 2, 1)
    bwd = pltpu.roll(v, AXIS_DIM // 2, 1)
    return v * cos + jnp.where(first_half, fwd, bwd) * sin_signed


def _gate_prep(f, lb, tri):
    fg = lb + (1.0 - lb) * jax.nn.sigmoid(f)
    lf = jnp.log2(fg)
    hi, mid, lo = _split3(lf)
    cum = (jnp.dot(tri, hi, preferred_element_type=F32) + jnp.dot(tri, mid, preferred_element_type=F32)
           + jnp.dot(tri, lo, preferred_element_type=F32))
    return (1.0 - fg).astype(BF16), cum


def _inproj_kernel(x_ref, mod_ref, g1_ref, w_ref, gq_ref, gk_ref, lb_ref, cos_ref, sin_ref, trif_ref, trib_ref,
                   *out_refs, latent):
    d = x_ref.shape[-1]
    x = x_ref[0]
    shift = mod_ref[0, :, 0:d]
    scale = mod_ref[0, :, d:2 * d]
    h = (_rms_rows(x) * g1_ref[...]) * (1.0 + scale) + shift
    hb = h.astype(BF16)

    def proj(c0, width):
        return jnp.dot(hb, w_ref[:, c0:c0 + width], preferred_element_type=F32)

    lane = lax.broadcasted_iota(jnp.int32, (LANES, LANES), 0)
    lane_c = lax.broadcasted_iota(jnp.int32, (LANES, LANES), 1)
    seg = (lane // ATT_HEAD_DIM == lane_c // ATT_HEAD_DIM).astype(BF16)
    row_lane = lax.broadcasted_iota(jnp.int32, (x.shape[0], LANES), 1)
    first_half = (row_lane % AXIS_DIM) < (AXIS_DIM // 2)
    upper = row_lane >= ATT_HEAD_DIM

    r0 = lb_ref[0]
    r1 = lb_ref[1]
    rm = jnp.maximum(r0, r1)
    e0 = jnp.exp(r0 - rm)
    e1 = jnp.exp(r1 - rm)
    lb_all = e0 / (e0 + e1)

    if latent:
        (qa_ref, ka_ref, vat_ref, hq_ref, hv_ref, kf_ref, bf_ref, kb_ref, bb_ref, sg_ref) = out_refs
        cos = cos_ref[...]
        sin = sin_ref[...]
    else:
        (ka_ref, vat_ref, hv_ref, kf_ref, bf_ref, kb_ref, bb_ref) = out_refs

    pk = proj(C_AK, ATT_KV_WIDTH)
    kn = pk * lax.rsqrt(_head64_meansq(pk, seg) + EPS) * gk_ref[...]
    if latent:
        kn = _rope(kn, cos, sin, first_half)
    ka_ref[0] = kn.astype(BF16)
    vat_ref[0] = proj(C_AV, ATT_KV_WIDTH).T.astype(BF16)

    if latent:
        pq = proj(C_AQ, ATT_WIDTH)
        for c in range(ATT_WIDTH // LANES):
            t = pq[:, c * LANES:(c + 1) * LANES]
            t = t * lax.rsqrt(_head64_meansq(t, seg) + EPS) * gq_ref[...]
            t = _rope(t, cos, sin, first_half) * (ATT_HEAD_DIM ** -0.5 * LOG2E)
            t_swapped = pltpu.roll(t, ATT_HEAD_DIM, 1)
            for half in range(2):
                head = 2 * c + half
                kv = head // ATT_GROUP
                src = t if half == kv else t_swapped
                keep = upper if kv == 1 else jnp.logical_not(upper)
                qa_ref[0, :, head * LANES:(head + 1) * LANES] = jnp.where(keep, src, 0.0).astype(BF16)
        hq_ref[0] = (_silu(proj(C_HQ, HG_WIDTH)) * (HG_HEAD_DIM ** -0.5)).astype(BF16)
        sg_ref[0] = _silu(proj(C_HG, HG_WIDTH)).astype(BF16)

    hv_ref[0] = proj(C_HI, HG_WIDTH).astype(BF16)
    kf, bf = _gate_prep(proj(C_FF, HG_WIDTH), lb_all[0:1, :], trif_ref[...])
    kf_ref[0] = kf
    bf_ref[0] = bf
    kb, bb = _gate_prep(proj(C_FB, HG_WIDTH), lb_all[1:2, :], trib_ref[...])
    kb_ref[0] = kb
    bb_ref[0] = bb


def _block_tri(n, suffix):
    r = np.arange(n)[:, None]
    c = np.arange(n)[None, :]
    same = (r // SUB) == (c // SUB)
    tri = (c >= r) if suffix else (c <= r)
    return jnp.asarray((same & tri).astype(np.float32), dtype=BF16)


def _input_projection(rows, mod3, mod_row, g1, w_in, gq, gk, lb_raw, cos, sin, latent):
    b, t, d = rows.shape
    tm = min(ROW_TILE, t)
    nt = t // tm
    trif = _block_tri(tm, False)
    trib = _block_tri(tm, True)
    bspec = lambda width: pl.BlockSpec((1, tm, width), lambda i, j: (i, j, 0))
    full = lambda a: pl.BlockSpec(a.shape, lambda i, j: (0,) * a.ndim)
    sds = lambda width, dt: jax.ShapeDtypeStruct((b, t, width), dt)
    vt_sds = jax.ShapeDtypeStruct((b, ATT_KV_WIDTH, t), BF16)
    vt_spec = pl.BlockSpec((1, ATT_KV_WIDTH, tm), lambda i, j: (i, 0, j))
    if latent:
        out_shape = [sds(ATT_HEADS * LANES, BF16), sds(ATT_KV_WIDTH, BF16), vt_sds,
                     sds(HG_WIDTH, BF16), sds(HG_WIDTH, BF16), sds(HG_WIDTH, BF16), sds(HG_WIDTH, F32),
                     sds(HG_WIDTH, BF16), sds(HG_WIDTH, F32), sds(HG_WIDTH, BF16)]
        out_specs = [bspec(ATT_HEADS * LANES), bspec(ATT_KV_WIDTH), vt_spec, bspec(HG_WIDTH),
                     bspec(HG_WIDTH), bspec(HG_WIDTH), bspec(HG_WIDTH), bspec(HG_WIDTH), bspec(HG_WIDTH),
                     bspec(HG_WIDTH)]
        mod_spec = pl.BlockSpec((1, 1, mod3.shape[-1]), lambda i, j: (i, 0, 0))
    else:
        out_shape = [sds(ATT_KV_WIDTH, BF16), vt_sds, sds(HG_WIDTH, BF16), sds(HG_WIDTH, BF16),
                     sds(HG_WIDTH, F32), sds(HG_WIDTH, BF16), sds(HG_WIDTH, F32)]
        out_specs = [bspec(ATT_KV_WIDTH), vt_spec, bspec(HG_WIDTH), bspec(HG_WIDTH), bspec(HG_WIDTH),
                     bspec(HG_WIDTH), bspec(HG_WIDTH)]
        mod_spec = pl.BlockSpec((1, 1, mod3.shape[-1]), lambda i, j: (mod_row, 0, 0))
    table_spec = pl.BlockSpec((tm, LANES), lambda i, j: (j, 0))
    return pl.pallas_call(
        functools.partial(_inproj_kernel, latent=latent),
        out_shape=out_shape,
        grid=(b, nt),
        in_specs=[bspec(d), mod_spec, full(g1), full(w_in), full(gq), full(gk), full(lb_raw),
                  table_spec, table_spec, full(trif), full(trib)],
        out_specs=out_specs,
        compiler_params=pltpu.CompilerParams(dimension_semantics=("parallel", "parallel"),
                                             vmem_limit_bytes=VMEM_LIMIT),
        name="inproj_latent" if latent else "inproj_context",
    )(rows, mod3, g1, w_in, gq, gk, lb_raw, cos, sin, trif, trib)


def _attention_kernel(q_ref, kc_ref, vtc_ref, kx_ref, vtx_ref, o_ref, sc_ref, s_ref):
    tq = q_ref.shape[1]
    n_kv = kx_ref.shape[1] // KV_TILE
    nt_dims = (((1,), (1,)), ((), ()))
    cols = ATT_GROUP * tq
    acc_rows = ATT_HEAD_DIM + ONES_ROWS

    qgs = [jnp.concatenate(
        [q_ref[0, :, (kv * ATT_GROUP + j) * LANES:(kv * ATT_GROUP + j + 1) * LANES] for j in range(ATT_GROUP)],
        axis=0) for kv in range(ATT_KV_HEADS)]

    def scores(kt, dst):
        for kv in range(ATT_KV_HEADS):
            dst(kv, lax.dot_general(kt, qgs[kv], nt_dims, preferred_element_type=F32))

    def consume(src, vt_all, carry):
        ones = jnp.ones((ONES_ROWS, vt_all.shape[1]), BF16)
        out = []
        for kv in range(ATT_KV_HEADS):
            m, acc = carry[kv]
            m_new = jnp.maximum(m, jnp.max(src(kv), axis=0, keepdims=True))
            alpha = jnp.exp2(m - m_new)
            p = jnp.exp2(src(kv) - m_new).astype(BF16)
            vt = jnp.concatenate([vt_all[kv * ATT_HEAD_DIM:(kv + 1) * ATT_HEAD_DIM], ones], axis=0)
            acc = alpha * acc + jnp.dot(vt, p, preferred_element_type=F32)
            out.append((m_new, acc))
        return tuple(out)

    def put_ctx(kv, val):
        sc_ref[kv] = val

    def put(slot):
        def dst(kv, val):
            s_ref[slot, kv] = val
        return dst

    def k_tile(j):
        return kx_ref[0, j * KV_TILE:(j + 1) * KV_TILE, :]

    def vt_tile(j):
        return vtx_ref[0, :, j * KV_TILE:(j + 1) * KV_TILE]

    init = (jnp.full((1, cols), -jnp.inf, F32), jnp.zeros((acc_rows, cols), F32))
    scores(kc_ref[0], put_ctx)
    for j in range(min(KV_AHEAD, n_kv)):
        scores(k_tile(j), put(j % KV_SLOTS))
    carry = consume(lambda kv: sc_ref[kv], vtc_ref[0], (init,) * ATT_KV_HEADS)
    for j in range(n_kv):
        if j + KV_AHEAD < n_kv:
            scores(k_tile(j + KV_AHEAD), put((j + KV_AHEAD) % KV_SLOTS))
        carry = consume(lambda kv, slot=j % KV_SLOTS: s_ref[slot, kv], vt_tile(j), carry)

    for kv in range(ATT_KV_HEADS):
        acc = carry[kv][1]
        o = acc[0:ATT_HEAD_DIM] / acc[ATT_HEAD_DIM:ATT_HEAD_DIM + 1]
        for jj in range(ATT_GROUP // 2):
            pair = jnp.concatenate([o[:, (2 * jj) * tq:(2 * jj + 1) * tq],
                                    o[:, (2 * jj + 1) * tq:(2 * jj + 2) * tq]], axis=0)
            col = kv * (ATT_GROUP // 2) + jj
            o_ref[0, :, col * LANES:(col + 1) * LANES] = pair.T.astype(BF16)


def _attention(qa, kc, vtc, kx, vtx):
    b, t, _ = qa.shape
    tq = min(Q_TILE, t)
    n_kv = t // KV_TILE
    assert tq % LANES == 0 and t % tq == 0 and t % KV_TILE == 0 and KV_AHEAD < KV_SLOTS
    cols = ATT_GROUP * tq
    return pl.pallas_call(
        _attention_kernel,
        out_shape=jax.ShapeDtypeStruct((b, t, ATT_WIDTH), BF16),
        grid=(b, t // tq),
        scratch_shapes=[pltpu.VMEM((ATT_KV_HEADS, kc.shape[1], cols), F32),
                        pltpu.VMEM((KV_SLOTS, ATT_KV_HEADS, KV_TILE, cols), F32)],
        in_specs=[pl.BlockSpec((1, tq, qa.shape[-1]), lambda i, j: (i, j, 0)),
                  pl.BlockSpec((1,) + kc.shape[1:], lambda i, j: (i, 0, 0)),
                  pl.BlockSpec((1,) + vtc.shape[1:], lambda i, j: (i, 0, 0)),
                  pl.BlockSpec((1,) + kx.shape[1:], lambda i, j: (i, 0, 0)),
                  pl.BlockSpec((1,) + vtx.shape[1:], lambda i, j: (i, 0, 0))],
        out_specs=pl.BlockSpec((1, tq, ATT_WIDTH), lambda i, j: (i, j, 0)),
        compiler_params=pltpu.CompilerParams(dimension_semantics=("parallel", "parallel"),
                                             vmem_limit_bytes=VMEM_LIMIT),
        name="attention",
    )(qa, kc, vtc, kx, vtx)


NT_DIMS = (((1,), (1,)), ((), ()))


def _zero_rows(n):
    return [jnp.zeros((n, LANES), F32)] if n else []


def _scan_window(st, q, k, v, b, b_row, k_row, reverse, want_out):
    nsub = WIN // SUB
    half = SUB // 2
    last = 0 if reverse else SUB - 1
    sub_of = (lambda p: nsub - 1 - p) if reverse else (lambda p: p)
    blk = lambda a, i: a[i * SUB:(i + 1) * SUB]

    b_last = [b_row(i * SUB + last) for i in range(nsub)]
    cum = []
    for p in range(nsub):
        cum.append(b_last[sub_of(p)] if p == 0 else cum[-1] + b_last[sub_of(p)])
    ks = [blk(k, i) * jnp.exp2(b_last[i] - blk(b, i)) for i in range(nsub)]

    k_end = jnp.concatenate(
        [ks[i] if sub_of(i) == nsub - 1 else ks[i] * jnp.exp2(cum[nsub - 1] - cum[sub_of(i)]) for i in range(nsub)],
        axis=0)
    st_new = jnp.exp2(cum[nsub - 1]) * st + jnp.dot(v.T, k_end, preferred_element_type=F32)
    if not want_out:
        return st_new, None

    qs = [blk(q, i) * jnp.exp2(blk(b, i)) for i in range(nsub)]
    q_start = jnp.concatenate(
        [qs[i] if sub_of(i) == 0 else qs[i] * jnp.exp2(cum[sub_of(i) - 1]) for i in range(nsub)], axis=0)
    o_carried = lax.dot_general(q_start, st, NT_DIMS, preferred_element_type=F32)

    rows = [[] for _ in range(nsub)]
    for h in (nsub // 2, nsub // 4, nsub // 8):
        for ub in range(1, nsub // h, 2):
            ref = ub * h - 1
            upper = {sub_of(p): p for p in range(ub * h, (ub + 1) * h)}
            lower = {sub_of(p): p for p in range((ub - 1) * h, ub * h)}
            iu, il = sorted(upper), sorted(lower)
            lhs = jnp.concatenate(
                [qs[i] if upper[i] - 1 == ref else qs[i] * jnp.exp2(cum[upper[i] - 1] - cum[ref]) for i in iu], axis=0)
            rhs = jnp.concatenate(
                _zero_rows(il[0] * SUB)
                + [ks[i] if lower[i] == ref else ks[i] * jnp.exp2(cum[ref] - cum[lower[i]]) for i in il]
                + _zero_rows((nsub - 1 - il[-1]) * SUB), axis=0)
            c = lax.dot_general(lhs, rhs, NT_DIMS, preferred_element_type=F32)
            for n, i in enumerate(iu):
                rows[i].append(c[n * SUB:(n + 1) * SUB])

    ones = jnp.ones((LANES, LANES), BF16)
    lane =lax.broadcasted_iota(jnp.int32, (half, LANES), 1)
    t_loc = lax.broadcasted_iota(jnp.int32, (half, LANES), 0)
    a_rows = []
    for i in range(nsub):
        bi, qi = blk(b, i), blk(q, i)
        parts, meta = [], []
        for s in range(SUB):
            if reverse:
                lo, hi = (0, SUB) if s >= half else (0, half)
            else:
                lo, hi = (0, SUB) if s < half else (half, SUB)
            w = jnp.exp2(jnp.minimum(bi[lo:hi] - b_row(i * SUB + s), 0.0))
            p = w * qi[lo:hi] * k_row(i * SUB + s)
            for r0 in range(lo, hi, half):
                parts.append(p[r0 - lo:r0 - lo + half])
                meta.append((s, r0 // half))
        summed = jnp.dot(jnp.concatenate(parts[0::2], axis=0), ones, preferred_element_type=F32)
        diag = [jnp.zeros((half, LANES), F32), jnp.zeros((half, LANES), F32)]
        for n, (s, hh) in enumerate(meta):
            if n % 2 == 0:
                r = summed[(n // 2) * half:(n // 2 + 1) * half]
            else:
                r = jnp.sum(parts[n], axis=-1, keepdims=True)
            diag[hh] = jnp.where(lane == i * SUB + s, r, diag[hh])
        tiles = []
        for hh in range(2):
            s_loc = lane - (i * SUB + hh * half)
            keep = (s_loc >= t_loc) if reverse else (s_loc <= t_loc)
            tiles.append(jnp.where(keep, diag[hh], 0.0))
        a = jnp.concatenate(tiles, axis=0)
        for c in rows[i]:
            a = a + c
        a_rows.append(a)
    scores = jnp.concatenate(a_rows, axis=0)
    return st_new, (scores.astype(BF16), o_carried)


def _hgrn_kernel(q_ref, v_ref, kf_ref, bf_ref, kb_ref, bb_ref, vc_ref, kfc_ref, bfc_ref, kbc_ref, bbc_ref,
                 o_ref, stf_ref, stb_ref, kwf_ref, kwb_ref, scf_ref, scb_ref, ocf_ref, ocb_ref):
    t = q_ref.shape[1]
    ctx = vc_ref.shape[1]
    nw = t // WIN

    def window(refs, start, st, kw_ref, reverse, want_out):
        qr, vr, kr, br = refs
        sl = pl.ds(start, WIN)
        vw = vr[0, sl, :].astype(F32)
        kw = kr[0, sl, :].astype(F32)
        bw = br[0, sl, :]
        qw = qr[0, sl, :].astype(F32) if want_out else None
        kw_ref[...] = kw
        b_row = lambda r: br[0, pl.ds(start + r, 1), :]
        k_row = lambda r: kw_ref[pl.ds(r, 1), :]
        return _scan_window(st, qw, kw, vw, bw, b_row, k_row, reverse, want_out)

    zero = jnp.zeros((LANES, LANES), F32)
    stf, stb = zero, zero
    n_cw = ctx // WIN
    for w in range(n_cw):
        stf, _ = window((None, vc_ref, kfc_ref, bfc_ref), w * WIN, stf, kwf_ref, False, False)
        stb, _ = window((None, vc_ref, kbc_ref, bbc_ref), (n_cw - 1 - w) * WIN, stb, kwb_ref, True, False)
    stf_ref[...] = stf
    stb_ref[...] = stb

    def starts(w):
        return pl.multiple_of(w * WIN, WIN), pl.multiple_of((nw - 1 - w) * WIN, WIN)

    def scores_stage(w):
        fs, bs = starts(w)
        stf, (sf, cf) = window((q_ref, v_ref, kf_ref, bf_ref), fs, stf_ref[...], kwf_ref, False, True)
        stb, (sb, cb) = window((q_ref, v_ref, kb_ref, bb_ref), bs, stb_ref[...], kwb_ref, True, True)
        stf_ref[...] = stf
        stb_ref[...] = stb
        scf_ref[...] = sf
        scb_ref[...] = sb
        ocf_ref[...] = cf
        ocb_ref[...] = cb

    def output_stage(w, accumulate):
        fs, bs = starts(w)
        of = jnp.dot(scf_ref[...], v_ref[0, pl.ds(fs, WIN), :], preferred_element_type=F32) + ocf_ref[...]
        ob = jnp.dot(scb_ref[...], v_ref[0, pl.ds(bs, WIN), :], preferred_element_type=F32) + ocb_ref[...]
        if accumulate:
            o_ref[0, pl.ds(fs, WIN), :] += of
            o_ref[0, pl.ds(bs, WIN), :] += ob
        else:
            o_ref[0, pl.ds(fs, WIN), :] = of
            o_ref[0, pl.ds(bs, WIN), :] = ob

    def step(w, accumulate):
        output_stage(w - 1, accumulate)
        scores_stage(w)

    def first(w, c):
        step(w, False)
        return c

    def second(w, c):
        step(w, True)
        return c

    scores_stage(0)
    lax.fori_loop(1, nw // 2 + 1, first, 0)
    lax.fori_loop(nw // 2 + 1, nw, second, 0)
    output_stage(nw - 1, True)


def _hgrn_scan(hq, hv, kf, bf, kb, bb, hvc, kfc, bfc, kbc, bbc):
    b, t, _ = hq.shape
    ctx = hvc.shape[1]
    lat = pl.BlockSpec((1, t, HG_HEAD_DIM), lambda i, j: (i, 0, j))
    con = pl.BlockSpec((1, ctx, HG_HEAD_DIM), lambda i, j: (i, 0, j))
    return pl.pallas_call(
        _hgrn_kernel,
        out_shape=jax.ShapeDtypeStruct((b, t, HG_WIDTH), F32),
        grid=(b, HG_HEADS),
        in_specs=[lat] * 6 + [con] * 5,
        out_specs=lat,
        scratch_shapes=[pltpu.VMEM((LANES, LANES), F32), pltpu.VMEM((LANES, LANES), F32),
                        pltpu.VMEM((WIN, LANES), F32), pltpu.VMEM((WIN, LANES), F32),
                        pltpu.VMEM((WIN, WIN), BF16), pltpu.VMEM((WIN, WIN), BF16),
                        pltpu.VMEM((WIN, LANES), F32), pltpu.VMEM((WIN, LANES), F32)],
        compiler_params=pltpu.CompilerParams(dimension_semantics=("parallel", "parallel"),
                                             vmem_limit_bytes=VMEM_LIMIT),
        name="hgrn_scan",
    )(hq, hv, kf, bf, kb, bb, hvc, kfc, bfc, kbc, bbc)


def _out_ffn_kernel(x_ref, att_ref, hg_ref, sg_ref, mod_ref, ghg_ref, wo_ref, g2_ref, wgu_ref, wd_ref, gf_ref,
                    o_ref):
    d = x_ref.shape[-1]
    dff = wd_ref.shape[0]
    x = x_ref[0]
    gate1 = mod_ref[0, :, 2 * d:3 * d]
    shift2 = mod_ref[0, :, 3 * d:4 * d]
    scale2 = mod_ref[0, :, 4 * d:5 * d]
    gate2 = mod_ref[0, :, 5 * d:6 * d]

    mix = jnp.dot(att_ref[0], wo_ref[0:ATT_WIDTH, :], preferred_element_type=F32)
    for hd in range(HG_HEADS):
        cs = slice(hd * HG_HEAD_DIM, (hd + 1) * HG_HEAD_DIM)
        o = _rms_rows(hg_ref[0, :, cs]) * ghg_ref[...]
        o = (o * sg_ref[0, :, cs].astype(F32)).astype(BF16)
        mix = mix + jnp.dot(o, wo_ref[ATT_WIDTH + hd * HG_HEAD_DIM:ATT_WIDTH + (hd + 1) * HG_HEAD_DIM, :],
                            preferred_element_type=F32)
    x1 = x + gate1 * mix
    h = ((_rms_rows(x1) * g2_ref[...]) * (1.0 + scale2) + shift2).astype(BF16)
    a = jnp.dot(h, wgu_ref[:, 0:dff], preferred_element_type=F32)
    g = jnp.dot(h, wgu_ref[:, dff:2 * dff], preferred_element_type=F32)
    act = (_silu(a) * g).astype(BF16)
    y = jnp.dot(act, wd_ref[...], preferred_element_type=F32)
    x2 = x1 + gate2 * y
    o_ref[0] = _rms_rows(x2) * gf_ref[...]


def _out_ffn(x, att, hg, sg, mod3, ghg, w_out, g2, w_gu, w_down, gf):
    b, t, d = x.shape
    tm = min(ROW_TILE, t)
    bspec = lambda width: pl.BlockSpec((1, tm, width), lambda i, j: (i, j, 0))
    full = lambda a: pl.BlockSpec(a.shape, lambda i, j: (0,) * a.ndim)
    return pl.pallas_call(
        _out_ffn_kernel,
        out_shape=jax.ShapeDtypeStruct((b, t, d), F32),
        grid=(b, t // tm),
        in_specs=[bspec(d), bspec(ATT_WIDTH), bspec(HG_WIDTH), bspec(HG_WIDTH),
                  pl.BlockSpec((1, 1, mod3.shape[-1]), lambda i, j: (i, 0, 0)),
                  full(ghg), full(w_out), full(g2), full(w_gu), full(w_down), full(gf)],
        out_specs=bspec(d),
        compiler_params=pltpu.CompilerParams(dimension_semantics=("parallel", "parallel"),
                                             vmem_limit_bytes=VMEM_LIMIT),
        name="out_ffn",
    )(x, att, hg, sg, mod3, ghg, w_out, g2, w_gu, w_down, gf)


def _rope_tables(t):
    n_rows = t // GRID_W
    row = jnp.repeat(jnp.arange(n_rows, dtype=F32), GRID_W)
    col = jnp.tile(jnp.arange(GRID_W, dtype=F32), n_rows)
    inv = ROPE_THETA ** (-jnp.arange(0, AXIS_DIM, 2, dtype=F32) / AXIS_DIM)
    ar = row[:, None] * inv
    ac = col[:, None] * inv
    ang = jnp.concatenate([ar, ar, ac, ac], axis=-1)
    sign = jnp.where((jnp.arange(ATT_HEAD_DIM) % AXIS_DIM) < AXIS_DIM // 2, -1.0, 1.0).astype(F32)
    reps = LANES // ATT_HEAD_DIM
    return jnp.tile(jnp.cos(ang), (1, reps)), jnp.tile(jnp.sin(ang) * sign, (1, reps))


def kernel(x, c, ctx, c_ctx, w_mod, b_mod, g_norm1, w_in, g_q, g_k, lb_raw, g_hg, w_out, g_norm2, w_gu, w_down,
           g_final):
    b, t, d = x.shape
    assert w_mod.shape[0] == 1 and lb_raw.shape[0] == 2, "single-layer block"
    assert t % WIN == 0 and (t // WIN) % 2 == 0 and ctx.shape[1] % WIN == 0 and t % KV_TILE == 0

    pad = (-(b + 1)) % 8
    cc = jnp.concatenate([c, c_ctx[None, :], jnp.zeros((pad, d), F32)], axis=0)
    mods = _modulation(cc, w_mod[0], b_mod)
    mod3 = mods.reshape(mods.shape[0], 1, mods.shape[1])

    cos, sin = _rope_tables(t)
    reps = LANES // ATT_HEAD_DIM
    gq = jnp.tile(g_q[0], reps)[None, :]
    gk = jnp.tile(g_k[0], reps)[None, :]
    w_in_b = w_in[0].astype(BF16)

    qa, kx, vtx, hq, hv, kf, bf, kb, bb, sg = _input_projection(
        x, mod3, None, g_norm1, w_in_b, gq, gk, lb_raw, cos, sin, True)
    kc, vtc, hvc, kfc, bfc, kbc, bbc = _input_projection(
        ctx, mod3, b, g_norm1, w_in_b, gq, gk, lb_raw, cos, sin, False)

    att = _attention(qa, kc, vtc, kx, vtx)
    hg = _hgrn_scan(hq, hv, kf, bf, kb, bb, hvc, kfc, bfc, kbc, bbc)
    return _out_ffn(x, att, hg, sg, mod3, g_hg, w_out[0].astype(BF16), g_norm2, w_gu[0].astype(BF16),
                    w_down[0].astype(BF16), g_final[None, :])
```

```python
import functools

import jax
import jax.numpy as jnp
import numpy as np
from jax import lax
from jax.experimental import pallas as pl
from jax.experimental.pallas import tpu as pltpu

F32 = jnp.float32
BF16 = jnp.bfloat16

LANES = 128
GRID_W = 64
ATT_HEADS = 8
ATT_KV_HEADS = 2
ATT_GROUP = ATT_HEADS // ATT_KV_HEADS
ATT_HEAD_DIM = 64
ATT_WIDTH = ATT_HEADS * ATT_HEAD_DIM
ATT_KV_WIDTH = ATT_KV_HEADS * ATT_HEAD_DIM
AXIS_DIM = ATT_HEAD_DIM // 2
ROPE_THETA = 10000.0
HG_HEADS = 4
HG_HEAD_DIM = 128
HG_WIDTH = HG_HEADS * HG_HEAD_DIM
EPS = 1e-6

SUB = 16
WIN = 128
ROW_TILE = 256
Q_TILE = 256
KV_TILE = 512
KV_SLOTS = 4
KV_AHEAD = 2
ONES_ROWS = 16
LOG2E = 1.4426950408889634
VMEM_LIMIT = 56 * 1024 * 1024

C_AQ = 0
C_AK = C_AQ + ATT_WIDTH
C_AV = C_AK + ATT_KV_WIDTH
C_HQ = C_AV + ATT_KV_WIDTH
C_HI = C_HQ + HG_WIDTH
C_FF = C_HI + HG_WIDTH
C_FB = C_FF + HG_WIDTH
C_HG = C_FB + HG_WIDTH
C_END = C_HG + HG_WIDTH


def _silu(v):
    return v * jax.nn.sigmoid(v)


def _rms_rows(v):
    return v * lax.rsqrt(jnp.mean(v * v, axis=-1, keepdims=True) + EPS)


def _mod_kernel(c_ref, w_ref, b_ref, o_ref):
    a = _silu(c_ref[...])
    o_ref[...] = jnp.dot(a, w_ref[...], precision=lax.Precision.HIGHEST,
                         preferred_element_type=F32) + b_ref[...]


def _modulation(cc, w_mod, b_mod):
    rows, d = cc.shape
    n = w_mod.shape[1]
    bn = n // 4
    return pl.pallas_call(
        _mod_kernel,
        out_shape=jax.ShapeDtypeStruct((rows, n), F32),
        grid=(n // bn,),
        in_specs=[pl.BlockSpec((rows, d), lambda j: (0, 0)),
                  pl.BlockSpec((d, bn), lambda j: (0, j)),
                  pl.BlockSpec((1, bn), lambda j: (0, j))],
        out_specs=pl.BlockSpec((rows, bn), lambda j: (0, j)),
        compiler_params=pltpu.CompilerParams(dimension_semantics=("arbitrary",),
                                             vmem_limit_bytes=VMEM_LIMIT),
        name="modulation",
    )(cc, w_mod, b_mod)


def _head64_meansq(p, seg2):
    sq = p * p
    hi = sq.astype(BF16)
    lo = (sq - hi.astype(F32)).astype(BF16)
    s = jnp.dot(jnp.concatenate([hi, lo], axis=1), seg2, preferred_element_type=F32)
    return s * (1.0 / ATT_HEAD_DIM)


def _rope(v, cos, sin_signed, first_half):
    fwd = pltpu.roll(v, LANES - AXIS_DIM // 2, 1)
    bwd = pltpu.roll(v, AXIS_DIM // 2, 1)
    return v * cos + jnp.where(first_half, fwd, bwd) * sin_signed


def _gate_prep(f, lb, suffix):
    fg = lb + (1.0 - lb) * jax.nn.sigmoid(f)
    cum = jnp.log2(fg)
    rows = cum.shape[0]
    pos = lax.broadcasted_iota(jnp.int32, cum.shape, 0) % SUB
    shift = 1
    while shift < SUB:
        if suffix:
            moved = pltpu.roll(cum, rows - shift, 0)
            cum = cum + jnp.where(pos < SUB - shift, moved, 0.0)
        else:
            moved = pltpu.roll(cum, shift, 0)
            cum = cum + jnp.where(pos >= shift, moved, 0.0)
        shift *= 2
    return (1.0 - fg).astype(BF16), cum


def _inproj_kernel(x_ref, mod_ref, g1_ref, w_ref, gq_ref, gk_ref, lb_ref, cos_ref, sin_ref, *out_refs, latent):
    d = x_ref.shape[-1]
    x = x_ref[0]
    shift = mod_ref[0, :, 0:d]
    scale = mod_ref[0, :, d:2 * d]
    h = (_rms_rows(x) * g1_ref[...]) * (1.0 + scale) + shift
    hb = h.astype(BF16)

    def proj(c0, width):
        return jnp.dot(hb, w_ref[:, c0:c0 + width], preferred_element_type=F32)

    lane = lax.broadcasted_iota(jnp.int32, (2 * LANES, LANES), 0) % LANES
    lane_c = lax.broadcasted_iota(jnp.int32, (2 * LANES, LANES), 1)
    seg = (lane // ATT_HEAD_DIM == lane_c // ATT_HEAD_DIM).astype(BF16)
    row_lane = lax.broadcasted_iota(jnp.int32, (x.shape[0], LANES), 1)
    first_half = (row_lane % AXIS_DIM) < (AXIS_DIM // 2)
    upper = row_lane >= ATT_HEAD_DIM

    r0 = lb_ref[0]
    r1 = lb_ref[1]
    rm = jnp.maximum(r0, r1)
    e0 = jnp.exp(r0 - rm)
    e1 = jnp.exp(r1 - rm)
    lb_all = e0 / (e0 + e1)

    if latent:
        (qa_ref, ka_ref, vat_ref, hq_ref, hv_ref, kf_ref, bf_ref, kb_ref, bb_ref, sg_ref) = out_refs
        cos = cos_ref[...]
        sin = sin_ref[...]
    else:
        (ka_ref, vat_ref, hv_ref, kf_ref, bf_ref, kb_ref, bb_ref) = out_refs

    p_ff = proj(C_FF, HG_WIDTH)
    p_fb = proj(C_FB, HG_WIDTH)
    pk = proj(C_AK, ATT_KV_WIDTH)
    pv = proj(C_AV, ATT_KV_WIDTH)
    if latent:
        pq = proj(C_AQ, ATT_WIDTH)
    else:
        p_hv = proj(C_HI, HG_WIDTH)
    ms_k = _head64_meansq(pk, seg)
    if latent:
        p_hq = proj(C_HQ, HG_WIDTH)
        ms_q = [_head64_meansq(pq[:, c * LANES:(c + 1) * LANES], seg) for c in range(ATT_WIDTH // LANES)]
        p_hv = proj(C_HI, HG_WIDTH)

    kn = pk * lax.rsqrt(ms_k + EPS) * gk_ref[...]
    if latent:
        kn = _rope(kn, cos, sin, first_half)
    ka_ref[0] = kn.astype(BF16)
    vat_ref[0] = pv.T.astype(BF16)

    if latent:
        for c in range(ATT_WIDTH // LANES):
            t = pq[:, c * LANES:(c + 1) * LANES]
            t = t * lax.rsqrt(ms_q[c] + EPS) * gq_ref[...]
            t = _rope(t, cos, sin, first_half) * (ATT_HEAD_DIM ** -0.5 * LOG2E)
            t_swapped = pltpu.roll(t, ATT_HEAD_DIM, 1)
            for half in range(2):
                head = 2 * c + half
                kv = head // ATT_GROUP
                src = t if half == kv else t_swapped
                keep = upper if kv == 1 else jnp.logical_not(upper)
                qa_ref[0, :, head * LANES:(head + 1) * LANES] = jnp.where(keep, src, 0.0).astype(BF16)
        hq_ref[0] = (_silu(p_hq) * (HG_HEAD_DIM ** -0.5)).astype(BF16)
        sg_ref[0] = _silu(proj(C_HG, HG_WIDTH)).astype(BF16)

    hv_ref[0] = p_hv.astype(BF16)
    kf, bf = _gate_prep(p_ff, lb_all[0:1, :], False)
    kf_ref[0] = kf
    bf_ref[0] = bf
    kb, bb = _gate_prep(p_fb, lb_all[1:2, :], True)
    kb_ref[0] = kb
    bb_ref[0] = bb


def _input_projection(rows, mod3, mod_row, g1, w_in, gq, gk, lb_raw, cos, sin, latent):
    b, t, d = rows.shape
    tm = min(ROW_TILE, t)
    nt = t // tm
    bspec = lambda width: pl.BlockSpec((1, tm, width), lambda i, j: (i, j, 0))
    full = lambda a: pl.BlockSpec(a.shape, lambda i, j: (0,) * a.ndim)
    sds = lambda width, dt: jax.ShapeDtypeStruct((b, t, width), dt)
    vt_sds = jax.ShapeDtypeStruct((b, ATT_KV_WIDTH, t), BF16)
    vt_spec = pl.BlockSpec((1, ATT_KV_WIDTH, tm), lambda i, j: (i, 0, j))
    if latent:
        out_shape = [sds(ATT_HEADS * LANES, BF16), sds(ATT_KV_WIDTH, BF16), vt_sds,
                     sds(HG_WIDTH, BF16), sds(HG_WIDTH, BF16), sds(HG_WIDTH, BF16), sds(HG_WIDTH, F32),
                     sds(HG_WIDTH, BF16), sds(HG_WIDTH, F32), sds(HG_WIDTH, BF16)]
        out_specs = [bspec(ATT_HEADS * LANES), bspec(ATT_KV_WIDTH), vt_spec, bspec(HG_WIDTH),
                     bspec(HG_WIDTH), bspec(HG_WIDTH), bspec(HG_WIDTH), bspec(HG_WIDTH), bspec(HG_WIDTH),
                     bspec(HG_WIDTH)]
        mod_spec = pl.BlockSpec((1, 1, mod3.shape[-1]), lambda i, j: (i, 0, 0))
    else:
        out_shape = [sds(ATT_KV_WIDTH, BF16), vt_sds, sds(HG_WIDTH, BF16), sds(HG_WIDTH, BF16),
                     sds(HG_WIDTH, F32), sds(HG_WIDTH, BF16), sds(HG_WIDTH, F32)]
        out_specs = [bspec(ATT_KV_WIDTH), vt_spec, bspec(HG_WIDTH), bspec(HG_WIDTH), bspec(HG_WIDTH),
                     bspec(HG_WIDTH), bspec(HG_WIDTH)]
        mod_spec = pl.BlockSpec((1, 1, mod3.shape[-1]), lambda i, j: (mod_row, 0, 0))
    table_spec = pl.BlockSpec((tm, LANES), lambda i, j: (j, 0))
    return pl.pallas_call(
        functools.partial(_inproj_kernel, latent=latent),
        out_shape=out_shape,
        grid=(b, nt),
        in_specs=[bspec(d), mod_spec, full(g1), full(w_in), full(gq), full(gk), full(lb_raw),
                  table_spec, table_spec],
        out_specs=out_specs,
        compiler_params=pltpu.CompilerParams(dimension_semantics=("parallel", "parallel"),
                                             vmem_limit_bytes=VMEM_LIMIT),
        name="inproj_latent" if latent else "inproj_context",
    )(rows, mod3, g1, w_in, gq, gk, lb_raw, cos, sin)


def _attention_kernel(q_ref, kc_ref, vtc_ref, kx_ref, vtx_ref, o_ref, sc_ref, s_ref):
    tq = q_ref.shape[1]
    n_kv = kx_ref.shape[1] // KV_TILE
    nt_dims = (((1,), (1,)), ((), ()))
    cols = ATT_GROUP * tq
    acc_rows = ATT_HEAD_DIM + ONES_ROWS

    qgs = [jnp.concatenate(
        [q_ref[0, :, (kv * ATT_GROUP + j) * LANES:(kv * ATT_GROUP + j + 1) * LANES] for j in range(ATT_GROUP)],
        axis=0) for kv in range(ATT_KV_HEADS)]

    def scores(kt, dst):
        for kv in range(ATT_KV_HEADS):
            dst(kv, lax.dot_general(kt, qgs[kv], nt_dims, preferred_element_type=F32))

    def consume(src, vt_all, carry):
        ones = jnp.ones((ONES_ROWS, vt_all.shape[1]), BF16)
        out = []
        for kv in range(ATT_KV_HEADS):
            m, acc = carry[kv]
            m_new = jnp.maximum(m, jnp.max(src(kv), axis=0, keepdims=True))
            alpha = jnp.exp2(m - m_new)
            p = jnp.exp2(src(kv) - m_new).astype(BF16)
            vt = jnp.concatenate([vt_all[kv * ATT_HEAD_DIM:(kv + 1) * ATT_HEAD_DIM], ones], axis=0)
            acc = alpha * acc + jnp.dot(vt, p, preferred_element_type=F32)
            out.append((m_new, acc))
        return tuple(out)

    def put_ctx(kv, val):
        sc_ref[kv] = val

    def put(slot):
        def dst(kv, val):
            s_ref[slot, kv] = val
        return dst

    def k_tile(j):
        return kx_ref[0, j * KV_TILE:(j + 1) * KV_TILE, :]

    def vt_tile(j):
        return vtx_ref[0, :, j * KV_TILE:(j + 1) * KV_TILE]

    init = (jnp.full((1, cols), -jnp.inf, F32), jnp.zeros((acc_rows, cols), F32))
    scores(kc_ref[0], put_ctx)
    for j in range(min(KV_AHEAD, n_kv)):
        scores(k_tile(j), put(j % KV_SLOTS))
    carry = consume(lambda kv: sc_ref[kv], vtc_ref[0], (init,) * ATT_KV_HEADS)
    for j in range(n_kv):
        if j + KV_AHEAD < n_kv:
            scores(k_tile(j + KV_AHEAD), put((j + KV_AHEAD) % KV_SLOTS))
        carry = consume(lambda kv, slot=j % KV_SLOTS: s_ref[slot, kv], vt_tile(j), carry)

    for kv in range(ATT_KV_HEADS):
        acc = carry[kv][1]
        o = acc[0:ATT_HEAD_DIM] / acc[ATT_HEAD_DIM:ATT_HEAD_DIM + 1]
        for jj in range(ATT_GROUP // 2):
            pair = jnp.concatenate([o[:, (2 * jj) * tq:(2 * jj + 1) * tq],
                                    o[:, (2 * jj + 1) * tq:(2 * jj + 2) * tq]], axis=0)
            col = kv * (ATT_GROUP // 2) + jj
            o_ref[0, :, col * LANES:(col + 1) * LANES] = pair.T.astype(BF16)


def _attention(qa, kc, vtc, kx, vtx):
    b, t, _ = qa.shape
    tq = min(Q_TILE, t)
    n_kv = t // KV_TILE
    assert tq % LANES == 0 and t % tq == 0 and t % KV_TILE == 0 and KV_AHEAD < KV_SLOTS
    cols = ATT_GROUP * tq
    return pl.pallas_call(
        _attention_kernel,
        out_shape=jax.ShapeDtypeStruct((b, t, ATT_WIDTH), BF16),
        grid=(b, t // tq),
        scratch_shapes=[pltpu.VMEM((ATT_KV_HEADS, kc.shape[1], cols), F32),
                        pltpu.VMEM((KV_SLOTS, ATT_KV_HEADS, KV_TILE, cols), F32)],
        in_specs=[pl.BlockSpec((1, tq, qa.shape[-1]), lambda i, j: (i, j, 0)),
                  pl.BlockSpec((1,) + kc.shape[1:], lambda i, j: (i, 0, 0)),
                  pl.BlockSpec((1,) + vtc.shape[1:], lambda i, j: (i, 0, 0)),
                  pl.BlockSpec((1,) + kx.shape[1:], lambda i, j: (i, 0, 0)),
                  pl.BlockSpec((1,) + vtx.shape[1:], lambda i, j: (i, 0, 0))],
        out_specs=pl.BlockSpec((1, tq, ATT_WIDTH), lambda i, j: (i, j, 0)),
        compiler_params=pltpu.CompilerParams(dimension_semantics=("parallel", "parallel"),
                                             vmem_limit_bytes=VMEM_LIMIT),
        name="attention",
    )(qa, kc, vtc, kx, vtx)


NT_DIMS = (((1,), (1,)), ((), ()))


def _zero_rows(n):
    return [jnp.zeros((n, LANES), F32)] if n else []


def _scan_window(st, q, k, v, b, b_row, k_row, reverse, want_out):
    nsub = WIN // SUB
    half = SUB // 2
    last = 0 if reverse else SUB - 1
    sub_of = (lambda p: nsub - 1 - p) if reverse else (lambda p: p)
    blk = lambda a, i: a[i * SUB:(i + 1) * SUB]

    b_last = [b_row(i * SUB + last) for i in range(nsub)]
    cum = []
    for p in range(nsub):
        cum.append(b_last[sub_of(p)] if p == 0 else cum[-1] + b_last[sub_of(p)])
    ks = [blk(k, i) * jnp.exp2(b_last[i] - blk(b, i)) for i in range(nsub)]

    k_end = jnp.concatenate(
        [ks[i] if sub_of(i) == nsub - 1 else ks[i] * jnp.exp2(cum[nsub - 1] - cum[sub_of(i)]) for i in range(nsub)],
        axis=0)
    st_new = jnp.exp2(cum[nsub - 1]) * st + jnp.dot(v.T, k_end, preferred_element_type=F32)
    if not want_out:
        return st_new, None

    qs = [blk(q, i) * jnp.exp2(blk(b, i)) for i in range(nsub)]
    q_start = jnp.concatenate(
        [qs[i] if sub_of(i) == 0 else qs[i] * jnp.exp2(cum[sub_of(i) - 1]) for i in range(nsub)], axis=0)
    o_carried = lax.dot_general(q_start, st, NT_DIMS, preferred_element_type=F32)

    rows = [[] for _ in range(nsub)]
    for h in (nsub // 2, nsub // 4, nsub // 8):
        for ub in range(1, nsub // h, 2):
            ref = ub * h - 1
            upper = {sub_of(p): p for p in range(ub * h, (ub + 1) * h)}
            lower = {sub_of(p): p for p in range((ub - 1) * h, ub * h)}
            iu, il = sorted(upper), sorted(lower)
            lhs = jnp.concatenate(
                [qs[i] if upper[i] - 1 == ref else qs[i] * jnp.exp2(cum[upper[i] - 1] - cum[ref]) for i in iu], axis=0)
            rhs = jnp.concatenate(
                _zero_rows(il[0] * SUB)
                + [ks[i] if lower[i] == ref else ks[i] * jnp.exp2(cum[ref] - cum[lower[i]]) for i in il]
                + _zero_rows((nsub - 1 - il[-1]) * SUB), axis=0)
            c = lax.dot_general(lhs, rhs, NT_DIMS, preferred_element_type=F32)
            for n, i in enumerate(iu):
                rows[i].append(c[n * SUB:(n + 1) * SUB])

    ones = jnp.ones((LANES, LANES), BF16)
    lane =lax.broadcasted_iota(jnp.int32, (half, LANES), 1)
    t_loc = lax.broadcasted_iota(jnp.int32, (half, LANES), 0)
    a_rows = []
    for i in range(nsub):
        bi, qi = blk(b, i), blk(q, i)
        parts, meta = [], []
        for s in range(SUB):
            if reverse:
                lo, hi = (0, SUB) if s >= half else (0, half)
            else:
                lo, hi = (0, SUB) if s < half else (half, SUB)
            w = jnp.exp2(jnp.minimum(bi[lo:hi] - b_row(i * SUB + s), 0.0))
            p = w * qi[lo:hi] * k_row(i * SUB + s)
            for r0 in range(lo, hi, half):
                parts.append(p[r0 - lo:r0 - lo + half])
                meta.append((s, r0 // half))
        summed = jnp.dot(jnp.concatenate(parts[0::2], axis=0), ones, preferred_element_type=F32)
        diag = [jnp.zeros((half, LANES), F32), jnp.zeros((half, LANES), F32)]
        for n, (s, hh) in enumerate(meta):
            if n % 2 == 0:
                r = summed[(n // 2) * half:(n // 2 + 1) * half]
            else:
                r = jnp.sum(parts[n], axis=-1, keepdims=True)
            diag[hh] = jnp.where(lane == i * SUB + s, r, diag[hh])
        tiles = []
        for hh in range(2):
            s_loc = lane - (i * SUB + hh * half)
            keep = (s_loc >= t_loc) if reverse else (s_loc <= t_loc)
            tiles.append(jnp.where(keep, diag[hh], 0.0))
        a = jnp.concatenate(tiles, axis=0)
        for c in rows[i]:
            a = a + c
        a_rows.append(a)
    scores = jnp.concatenate(a_rows, axis=0)
    return st_new, (scores.astype(BF16), o_carried)


def _hgrn_kernel(q_ref, v_ref, kf_ref, bf_ref, kb_ref, bb_ref, vc_ref, kfc_ref, bfc_ref, kbc_ref, bbc_ref,
                 o_ref, stf_ref, stb_ref, kwf_ref, kwb_ref, scf_ref, scb_ref, ocf_ref, ocb_ref):
    t = q_ref.shape[1]
    ctx = vc_ref.shape[1]
    nw = t // WIN

    def window(refs, start, st, kw_ref, reverse, want_out):
        qr, vr, kr, br = refs
        sl = pl.ds(start, WIN)
        vw = vr[0, sl, :].astype(F32)
        kw = kr[0, sl, :].astype(F32)
        bw = br[0, sl, :]
        qw = qr[0, sl, :].astype(F32) if want_out else None
        kw_ref[...] = kw
        b_row = lambda r: br[0, pl.ds(start + r, 1), :]
        k_row = lambda r: kw_ref[pl.ds(r, 1), :]
        return _scan_window(st, qw, kw, vw, bw, b_row, k_row, reverse, want_out)

    zero = jnp.zeros((LANES, LANES), F32)
    stf, stb = zero, zero
    n_cw = ctx // WIN
    for w in range(n_cw):
        stf, _ = window((None, vc_ref, kfc_ref, bfc_ref), w * WIN, stf, kwf_ref, False, False)
        stb, _ = window((None, vc_ref, kbc_ref, bbc_ref), (n_cw - 1 - w) * WIN, stb, kwb_ref, True, False)
    stf_ref[...] = stf
    stb_ref[...] = stb

    def starts(w):
        return pl.multiple_of(w * WIN, WIN), pl.multiple_of((nw - 1 - w) * WIN, WIN)

    def scores_stage(w):
        fs, bs = starts(w)
        stf, (sf, cf) = window((q_ref, v_ref, kf_ref, bf_ref), fs, stf_ref[...], kwf_ref, False, True)
        stb, (sb, cb) = window((q_ref, v_ref, kb_ref, bb_ref), bs, stb_ref[...], kwb_ref, True, True)
        stf_ref[...] = stf
        stb_ref[...] = stb
        scf_ref[...] = sf
        scb_ref[...] = sb
        ocf_ref[...] = cf
        ocb_ref[...] = cb

    def output_stage(w, accumulate):
        fs, bs = starts(w)
        of = jnp.dot(scf_ref[...], v_ref[0, pl.ds(fs, WIN), :], preferred_element_type=F32) + ocf_ref[...]
        ob = jnp.dot(scb_ref[...], v_ref[0, pl.ds(bs, WIN), :], preferred_element_type=F32) + ocb_ref[...]
        if accumulate:
            o_ref[0, pl.ds(fs, WIN), :] += of
            o_ref[0, pl.ds(bs, WIN), :] += ob
        else:
            o_ref[0, pl.ds(fs, WIN), :] = of
            o_ref[0, pl.ds(bs, WIN), :] = ob

    def step(w, accumulate):
        output_stage(w - 1, accumulate)
        scores_stage(w)

    def first(w, c):
        step(w, False)
        return c

    def second(w, c):
        step(w, True)
        return c

    scores_stage(0)
    lax.fori_loop(1, nw // 2 + 1, first, 0)
    lax.fori_loop(nw // 2 + 1, nw, second, 0)
    output_stage(nw - 1, True)


def _hgrn_scan(hq, hv, kf, bf, kb, bb, hvc, kfc, bfc, kbc, bbc):
    b, t, _ = hq.shape
    ctx = hvc.shape[1]
    lat = pl.BlockSpec((1, t, HG_HEAD_DIM), lambda i, j: (i, 0, j))
    con = pl.BlockSpec((1, ctx, HG_HEAD_DIM), lambda i, j: (i, 0, j))
    return pl.pallas_call(
        _hgrn_kernel,
        out_shape=jax.ShapeDtypeStruct((b, t, HG_WIDTH), F32),
        grid=(b, HG_HEADS),
        in_specs=[lat] * 6 + [con] * 5,
        out_specs=lat,
        scratch_shapes=[pltpu.VMEM((LANES, LANES), F32), pltpu.VMEM((LANES, LANES), F32),
                        pltpu.VMEM((WIN, LANES), F32), pltpu.VMEM((WIN, LANES), F32),
                        pltpu.VMEM((WIN, WIN), BF16), pltpu.VMEM((WIN, WIN), BF16),
                        pltpu.VMEM((WIN, LANES), F32), pltpu.VMEM((WIN, LANES), F32)],
        compiler_params=pltpu.CompilerParams(dimension_semantics=("parallel", "parallel"),
                                             vmem_limit_bytes=VMEM_LIMIT),
        name="hgrn_scan",
    )(hq, hv, kf, bf, kb, bb, hvc, kfc, bfc, kbc, bbc)


def _out_ffn_kernel(x_ref, att_ref, hg_ref, sg_ref, mod_ref, ghg_ref, wo_ref, g2_ref, wgu_ref, wd_ref, gf_ref,
                    o_ref):
    d = x_ref.shape[-1]
    dff = wd_ref.shape[0]
    x = x_ref[0]
    gate1 = mod_ref[0, :, 2 * d:3 * d]
    shift2 = mod_ref[0, :, 3 * d:4 * d]
    scale2 = mod_ref[0, :, 4 * d:5 * d]
    gate2 = mod_ref[0, :, 5 * d:6 * d]

    mix = jnp.dot(att_ref[0], wo_ref[0:ATT_WIDTH, :], preferred_element_type=F32)
    for hd in range(HG_HEADS):
        cs = slice(hd * HG_HEAD_DIM, (hd + 1) * HG_HEAD_DIM)
        o = _rms_rows(hg_ref[0, :, cs]) * ghg_ref[...]
        o = (o * sg_ref[0, :, cs].astype(F32)).astype(BF16)
        mix = mix + jnp.dot(o, wo_ref[ATT_WIDTH + hd * HG_HEAD_DIM:ATT_WIDTH + (hd + 1) * HG_HEAD_DIM, :],
                            preferred_element_type=F32)
    x1 = x + gate1 * mix
    h = ((_rms_rows(x1) * g2_ref[...]) * (1.0 + scale2) + shift2).astype(BF16)
    a = jnp.dot(h, wgu_ref[:, 0:dff], preferred_element_type=F32)
    g = jnp.dot(h, wgu_ref[:, dff:2 * dff], preferred_element_type=F32)
    act = (_silu(a) * g).astype(BF16)
    y = jnp.dot(act, wd_ref[...], preferred_element_type=F32)
    x2 = x1 + gate2 * y
    o_ref[0] = _rms_rows(x2) * gf_ref[...]


def _out_ffn(x, att, hg, sg, mod3, ghg, w_out, g2, w_gu, w_down, gf):
    b, t, d = x.shape
    tm = min(ROW_TILE, t)
    bspec = lambda width: pl.BlockSpec((1, tm, width), lambda i, j: (i, j, 0))
    full = lambda a: pl.BlockSpec(a.shape, lambda i, j: (0,) * a.ndim)
    return pl.pallas_call(
        _out_ffn_kernel,
        out_shape=jax.ShapeDtypeStruct((b, t, d), F32),
        grid=(b, t // tm),
        in_specs=[bspec(d), bspec(ATT_WIDTH), bspec(HG_WIDTH), bspec(HG_WIDTH),
                  pl.BlockSpec((1, 1, mod3.shape[-1]), lambda i, j: (i, 0, 0)),
                  full(ghg), full(w_out), full(g2), full(w_gu), full(w_down), full(gf)],
        out_specs=bspec(d),
        compiler_params=pltpu.CompilerParams(dimension_semantics=("parallel", "parallel"),
                                             vmem_limit_bytes=VMEM_LIMIT),
        name="out_ffn",
    )(x, att, hg, sg, mod3, ghg, w_out, g2, w_gu, w_down, gf)


def _rope_tables(t):
    n_rows = t // GRID_W
    row = jnp.repeat(jnp.arange(n_rows, dtype=F32), GRID_W)
    col = jnp.tile(jnp.arange(GRID_W, dtype=F32), n_rows)
    inv = ROPE_THETA ** (-jnp.arange(0, AXIS_DIM, 2, dtype=F32) / AXIS_DIM)
    ar = row[:, None] * inv
    ac = col[:, None] * inv
    ang = jnp.concatenate([ar, ar, ac, ac], axis=-1)
    sign = jnp.where((jnp.arange(ATT_HEAD_DIM) % AXIS_DIM) < AXIS_DIM // 2, -1.0, 1.0).astype(F32)
    reps = LANES // ATT_HEAD_DIM
    return jnp.tile(jnp.cos(ang), (1, reps)), jnp.tile(jnp.sin(ang) * sign, (1, reps))


def kernel(x, c, ctx, c_ctx, w_mod, b_mod, g_norm1, w_in, g_q, g_k, lb_raw, g_hg, w_out, g_norm2, w_gu, w_down,
           g_final):
    b, t, d = x.shape
    assert w_mod.shape[0] == 1 and lb_raw.shape[0] == 2, "single-layer block"
    assert t % WIN == 0 and (t // WIN) % 2 == 0 and ctx.shape[1] % WIN == 0 and t % KV_TILE == 0

    pad = (-(b + 1)) % 8
    cc = jnp.concatenate([c, c_ctx[None, :], jnp.zeros((pad, d), F32)], axis=0)
    mods = _modulation(cc, w_mod[0], b_mod)
    mod3 = mods.reshape(mods.shape[0], 1, mods.shape[1])

    cos, sin = _rope_tables(t)
    reps = LANES // ATT_HEAD_DIM
    gq = jnp.tile(g_q[0], reps)[None, :]
    gk = jnp.tile(g_k[0], reps)[None, :]
    w_in_b = w_in[0].astype(BF16)

    qa, kx, vtx, hq, hv, kf, bf, kb, bb, sg = _input_projection(
        x, mod3, None, g_norm1, w_in_b, gq, gk, lb_raw, cos, sin, True)
    kc, vtc, hvc, kfc, bfc, kbc, bbc = _input_projection(
        ctx, mod3, b, g_norm1, w_in_b, gq, gk, lb_raw, cos, sin, False)

    att = _attention(qa, kc, vtc, kx, vtx)
    hg = _hgrn_scan(hq, hv, kf, bf, kb, bb, hvc, kfc, bfc, kbc, bbc)
    return _out_ffn(x, att, hg, sg, mod3, g_hg, w_out[0].astype(BF16), g_norm2, w_gu[0].astype(BF16),
                    w_down[0].astype(BF16), g_final[None, :])
```

```python
import functools

import jax
import jax.numpy as jnp
import numpy as np
from jax import lax
from jax.experimental import pallas as pl
from jax.experimental.pallas import tpu as pltpu

F32 = jnp.float32
BF16 = jnp.bfloat16

LANES = 128
GRID_W = 64
ATT_HEADS = 8
ATT_KV_HEADS = 2
ATT_GROUP = ATT_HEADS // ATT_KV_HEADS
ATT_HEAD_DIM = 64
ATT_WIDTH = ATT_HEADS * ATT_HEAD_DIM
ATT_KV_WIDTH = ATT_KV_HEADS * ATT_HEAD_DIM
AXIS_DIM = ATT_HEAD_DIM // 2
ROPE_THETA = 10000.0
HG_HEADS = 4
HG_HEAD_DIM = 128
HG_WIDTH = HG_HEADS * HG_HEAD_DIM
EPS = 1e-6

SUB = 16
WIN = 128
ROW_TILE = 256
Q_TILE = 256
KV_TILE = 512
KV_SLOTS = 4
KV_AHEAD = 2
ONES_ROWS = 16
LOG2E = 1.4426950408889634
VMEM_LIMIT = 56 * 1024 * 1024

C_AQ = 0
C_AK = C_AQ + ATT_WIDTH
C_AV = C_AK + ATT_KV_WIDTH
C_HQ = C_AV + ATT_KV_WIDTH
C_HI = C_HQ + HG_WIDTH
C_FF = C_HI + HG_WIDTH
C_FB = C_FF + HG_WIDTH
C_HG = C_FB + HG_WIDTH
C_END = C_HG + HG_WIDTH


def _silu(v):
    return v * jax.nn.sigmoid(v)


def _rms_rows(v):
    return v * lax.rsqrt(jnp.mean(v * v, axis=-1, keepdims=True) + EPS)


def _mod_kernel(c_ref, w_ref, b_ref, o_ref):
    a = _silu(c_ref[...])
    o_ref[...] = jnp.dot(a, w_ref[...], precision=lax.Precision.HIGHEST,
                         preferred_element_type=F32) + b_ref[...]


def _modulation(cc, w_mod, b_mod):
    rows, d = cc.shape
    n = w_mod.shape[1]
    bn = n // 4
    return pl.pallas_call(
        _mod_kernel,
        out_shape=jax.ShapeDtypeStruct((rows, n), F32),
        grid=(n // bn,),
        in_specs=[pl.BlockSpec((rows, d), lambda j: (0, 0)),
                  pl.BlockSpec((d, bn), lambda j: (0, j)),
                  pl.BlockSpec((1, bn), lambda j: (0, j))],
        out_specs=pl.BlockSpec((rows, bn), lambda j: (0, j)),
        compiler_params=pltpu.CompilerParams(dimension_semantics=("arbitrary",),
                                             vmem_limit_bytes=VMEM_LIMIT),
        name="modulation",
    )(cc, w_mod, b_mod)


def _head64_meansq(p, seg2):
    sq = p * p
    hi = sq.astype(BF16)
    lo = (sq - hi.astype(F32)).astype(BF16)
    s = jnp.dot(jnp.concatenate([hi, lo], axis=1), seg2, preferred_element_type=F32)
    return s * (1.0 / ATT_HEAD_DIM)


def _rope(v, cos, sin_signed, first_half):
    fwd = pltpu.roll(v, LANES - AXIS_DIM // 2, 1)
    bwd = pltpu.roll(v, AXIS_DIM // 2, 1)
    return v * cos + jnp.where(first_half, fwd, bwd) * sin_signed


def _gate_prep(f, lb, suffix):
    fg = lb + (1.0 - lb) * jax.nn.sigmoid(f)
    cum = jnp.log2(fg)
    rows = cum.shape[0]
    pos = lax.broadcasted_iota(jnp.int32, cum.shape, 0) % SUB
    shift = 1
    while shift < SUB:
        if suffix:
            moved = pltpu.roll(cum, rows - shift, 0)
            cum = cum + jnp.where(pos < SUB - shift, moved, 0.0)
        else:
            moved = pltpu.roll(cum, shift, 0)
            cum = cum + jnp.where(pos >= shift, moved, 0.0)
        shift *= 2
    return (1.0 - fg).astype(BF16), cum


def _inproj_kernel(x_ref, mod_ref, g1_ref, w_ref, gq_ref, gk_ref, lb_ref, cos_ref, sin_ref, *out_refs, latent):
    d = x_ref.shape[-1]
    x = x_ref[0]
    shift = mod_ref[0, :, 0:d]
    scale = mod_ref[0, :, d:2 * d]
    h = (_rms_rows(x) * g1_ref[...]) * (1.0 + scale) + shift
    hb = h.astype(BF16)

    def proj(c0, width):
        return jnp.dot(hb, w_ref[:, c0:c0 + width], preferred_element_type=F32)

    lane = lax.broadcasted_iota(jnp.int32, (2 * LANES, LANES), 0) % LANES
    lane_c = lax.broadcasted_iota(jnp.int32, (2 * LANES, LANES), 1)
    seg = (lane // ATT_HEAD_DIM == lane_c // ATT_HEAD_DIM).astype(BF16)
    row_lane = lax.broadcasted_iota(jnp.int32, (x.shape[0], LANES), 1)
    first_half = (row_lane % AXIS_DIM) < (AXIS_DIM // 2)
    upper = row_lane >= ATT_HEAD_DIM

    r0 = lb_ref[0]
    r1 = lb_ref[1]
    rm = jnp.maximum(r0, r1)
    e0 = jnp.exp(r0 - rm)
    e1 = jnp.exp(r1 - rm)
    lb_all = e0 / (e0 + e1)

    if latent:
        (qa_ref, ka_ref, vat_ref, hq_ref, hv_ref, kf_ref, bf_ref, kb_ref, bb_ref, sg_ref) = out_refs
        cos = cos_ref[...]
        sin = sin_ref[...]
    else:
        (ka_ref, vat_ref, hv_ref, kf_ref, bf_ref, kb_ref, bb_ref) = out_refs

    p_ff = proj(C_FF, HG_WIDTH)
    p_fb = proj(C_FB, HG_WIDTH)
    pk = proj(C_AK, ATT_KV_WIDTH)
    pv = proj(C_AV, ATT_KV_WIDTH)
    if latent:
        pq = proj(C_AQ, ATT_WIDTH)
    else:
        p_hv = proj(C_HI, HG_WIDTH)
    ms_k = _head64_meansq(pk, seg)
    if latent:
        p_hq = proj(C_HQ, HG_WIDTH)
        ms_q = [_head64_meansq(pq[:, c * LANES:(c + 1) * LANES], seg) for c in range(ATT_WIDTH // LANES)]
        p_hv = proj(C_HI, HG_WIDTH)

    kn = pk * lax.rsqrt(ms_k + EPS) * gk_ref[...]
    if latent:
        kn = _rope(kn, cos, sin, first_half)
    ka_ref[0] = kn.astype(BF16)
    vat_ref[0] = pv.T.astype(BF16)

    if latent:
        for c in range(ATT_WIDTH // LANES):
            t = pq[:, c * LANES:(c + 1) * LANES]
            t = t * lax.rsqrt(ms_q[c] + EPS) * gq_ref[...]
            t = _rope(t, cos, sin, first_half) * (ATT_HEAD_DIM ** -0.5 * LOG2E)
            t_swapped = pltpu.roll(t, ATT_HEAD_DIM, 1)
            for half in range(2):
                head = 2 * c + half
                kv = head // ATT_GROUP
                src = t if half == kv else t_swapped
                keep = upper if kv == 1 else jnp.logical_not(upper)
                qa_ref[0, :, head * LANES:(head + 1) * LANES] = jnp.where(keep, src, 0.0).astype(BF16)
        hq_ref[0] = (_silu(p_hq) * (HG_HEAD_DIM ** -0.5)).astype(BF16)
        sg_ref[0] = _silu(proj(C_HG, HG_WIDTH)).astype(BF16)

    hv_ref[0] = p_hv.astype(BF16)
    kf, bf = _gate_prep(p_ff, lb_all[0:1, :], False)
    kf_ref[0] = kf
    bf_ref[0] = bf
    kb, bb = _gate_prep(p_fb, lb_all[1:2, :], True)
    kb_ref[0] = kb
    bb_ref[0] = bb


def _input_projection(rows, mod3, mod_row, g1, w_in, gq, gk, lb_raw, cos, sin, latent):
    b, t, d = rows.shape
    tm = min(ROW_TILE, t)
    nt = t // tm
    bspec = lambda width: pl.BlockSpec((1, tm, width), lambda i, j: (i, j, 0))
    full = lambda a: pl.BlockSpec(a.shape, lambda i, j: (0,) * a.ndim)
    sds = lambda width, dt: jax.ShapeDtypeStruct((b, t, width), dt)
    vt_sds = jax.ShapeDtypeStruct((b, ATT_KV_WIDTH, t), BF16)
    vt_spec = pl.BlockSpec((1, ATT_KV_WIDTH, tm), lambda i, j: (i, 0, j))
    if latent:
        out_shape = [sds(ATT_HEADS * LANES, BF16), sds(ATT_KV_WIDTH, BF16), vt_sds,
                     sds(HG_WIDTH, BF16), sds(HG_WIDTH, BF16), sds(HG_WIDTH, BF16), sds(HG_WIDTH, F32),
                     sds(HG_WIDTH, BF16), sds(HG_WIDTH, F32), sds(HG_WIDTH, BF16)]
        out_specs = [bspec(ATT_HEADS * LANES), bspec(ATT_KV_WIDTH), vt_spec, bspec(HG_WIDTH),
                     bspec(HG_WIDTH), bspec(HG_WIDTH), bspec(HG_WIDTH), bspec(HG_WIDTH), bspec(HG_WIDTH),
                     bspec(HG_WIDTH)]
        mod_spec = pl.BlockSpec((1, 1, mod3.shape[-1]), lambda i, j: (i, 0, 0))
    else:
        out_shape = [sds(ATT_KV_WIDTH, BF16), vt_sds, sds(HG_WIDTH, BF16), sds(HG_WIDTH, BF16),
                     sds(HG_WIDTH, F32), sds(HG_WIDTH, BF16), sds(HG_WIDTH, F32)]
        out_specs = [bspec(ATT_KV_WIDTH), vt_spec, bspec(HG_WIDTH), bspec(HG_WIDTH), bspec(HG_WIDTH),
                     bspec(HG_WIDTH), bspec(HG_WIDTH)]
        mod_spec = pl.BlockSpec((1, 1, mod3.shape[-1]), lambda i, j: (mod_row, 0, 0))
    table_spec = pl.BlockSpec((tm, LANES), lambda i, j: (j, 0))
    return pl.pallas_call(
        functools.partial(_inproj_kernel, latent=latent),
        out_shape=out_shape,
        grid=(b, nt),
        in_specs=[bspec(d), mod_spec, full(g1), full(w_in), full(gq), full(gk), full(lb_raw),
                  table_spec, table_spec],
        out_specs=out_specs,
        compiler_params=pltpu.CompilerParams(dimension_semantics=("parallel", "parallel"),
                                             vmem_limit_bytes=VMEM_LIMIT),
        name="inproj_latent" if latent else "inproj_context",
    )(rows, mod3, g1, w_in, gq, gk, lb_raw, cos, sin)


def _attention_schedule(q_ref, kc_ref, vtc_ref, kx_ref, vtx_ref, o_ref, sc_ref, s_ref):
    tq = q_ref.shape[1]
    n_kv = kx_ref.shape[1] // KV_TILE
    nt_dims = (((1,), (1,)), ((), ()))
    cols = ATT_GROUP * tq
    acc_rows = ATT_HEAD_DIM + ONES_ROWS

    qgs = [jnp.concatenate(
        [q_ref[0, :, (kv * ATT_GROUP + j) * LANES:(kv * ATT_GROUP + j + 1) * LANES] for j in range(ATT_GROUP)],
        axis=0) for kv in range(ATT_KV_HEADS)]

    def scores(kt, dst):
        for kv in range(ATT_KV_HEADS):
            dst(kv, lax.dot_general(kt, qgs[kv], nt_dims, preferred_element_type=F32))

    def consume(src, vt_all, carry):
        ones = jnp.ones((ONES_ROWS, vt_all.shape[1]), BF16)
        out = []
        for kv in range(ATT_KV_HEADS):
            m, acc = carry[kv]
            m_new = jnp.maximum(m, jnp.max(src(kv), axis=0, keepdims=True))
            alpha = jnp.exp2(m - m_new)
            p = jnp.exp2(src(kv) - m_new).astype(BF16)
            vt = jnp.concatenate([vt_all[kv * ATT_HEAD_DIM:(kv + 1) * ATT_HEAD_DIM], ones], axis=0)
            acc = alpha * acc + jnp.dot(vt, p, preferred_element_type=F32)
            out.append((m_new, acc))
        return tuple(out)

    def put_ctx(kv, val):
        sc_ref[kv] = val

    def put(slot):
        def dst(kv, val):
            s_ref[slot, kv] = val
        return dst

    def k_tile(j):
        return kx_ref[0, j * KV_TILE:(j + 1) * KV_TILE, :]

    def vt_tile(j):
        return vtx_ref[0, :, j * KV_TILE:(j + 1) * KV_TILE]

    init = (jnp.full((1, cols), -jnp.inf, F32), jnp.zeros((acc_rows, cols), F32))
    scores(kc_ref[0], put_ctx)
    for j in range(min(KV_AHEAD, n_kv)):
        scores(k_tile(j), put(j % KV_SLOTS))
    carry = consume(lambda kv: sc_ref[kv], vtc_ref[0], (init,) * ATT_KV_HEADS)
    for j in range(n_kv):
        if j + KV_AHEAD < n_kv:
            scores(k_tile(j + KV_AHEAD), put((j + KV_AHEAD) % KV_SLOTS))
        carry = consume(lambda kv, slot=j % KV_SLOTS: s_ref[slot, kv], vt_tile(j), carry)
        yield

    for kv in range(ATT_KV_HEADS):
        acc = carry[kv][1]
        o = acc[0:ATT_HEAD_DIM] / acc[ATT_HEAD_DIM:ATT_HEAD_DIM + 1]
        for jj in range(ATT_GROUP // 2):
            pair = jnp.concatenate([o[:, (2 * jj) * tq:(2 * jj + 1) * tq],
                                    o[:, (2 * jj + 1) * tq:(2 * jj + 2) * tq]], axis=0)
            col = kv * (ATT_GROUP // 2) + jj
            o_ref[0, :, col * LANES:(col + 1) * LANES] = pair.T.astype(BF16)


NT_DIMS = (((1,), (1,)), ((), ()))


def _zero_rows(n):
    return [jnp.zeros((n, LANES), F32)] if n else []


def _scan_window(st, q, k, v, b, b_row, k_row, reverse, want_out):
    nsub = WIN // SUB
    half = SUB // 2
    last = 0 if reverse else SUB - 1
    sub_of = (lambda p: nsub - 1 - p) if reverse else (lambda p: p)
    blk = lambda a, i: a[i * SUB:(i + 1) * SUB]

    b_last = [b_row(i * SUB + last) for i in range(nsub)]
    cum = []
    for p in range(nsub):
        cum.append(b_last[sub_of(p)] if p == 0 else cum[-1] + b_last[sub_of(p)])
    ks = [blk(k, i) * jnp.exp2(b_last[i] - blk(b, i)) for i in range(nsub)]

    k_end = jnp.concatenate(
        [ks[i] if sub_of(i) == nsub - 1 else ks[i] * jnp.exp2(cum[nsub - 1] - cum[sub_of(i)]) for i in range(nsub)],
        axis=0)
    st_new = jnp.exp2(cum[nsub - 1]) * st + jnp.dot(v.T, k_end, preferred_element_type=F32)
    if not want_out:
        return st_new, None

    qs = [blk(q, i) * jnp.exp2(blk(b, i)) for i in range(nsub)]
    q_start = jnp.concatenate(
        [qs[i] if sub_of(i) == 0 else qs[i] * jnp.exp2(cum[sub_of(i) - 1]) for i in range(nsub)], axis=0)
    o_carried = lax.dot_general(q_start, st, NT_DIMS, preferred_element_type=F32)

    rows = [[] for _ in range(nsub)]
    for h in (nsub // 2, nsub // 4, nsub // 8):
        for ub in range(1, nsub // h, 2):
            ref = ub * h - 1
            upper = {sub_of(p): p for p in range(ub * h, (ub + 1) * h)}
            lower = {sub_of(p): p for p in range((ub - 1) * h, ub * h)}
            iu, il = sorted(upper), sorted(lower)
            lhs = jnp.concatenate(
                [qs[i] if upper[i] - 1 == ref else qs[i] * jnp.exp2(cum[upper[i] - 1] - cum[ref]) for i in iu], axis=0)
            rhs = jnp.concatenate(
                _zero_rows(il[0] * SUB)
                + [ks[i] if lower[i] == ref else ks[i] * jnp.exp2(cum[ref] - cum[lower[i]]) for i in il]
                + _zero_rows((nsub - 1 - il[-1]) * SUB), axis=0)
            c = lax.dot_general(lhs, rhs, NT_DIMS, preferred_element_type=F32)
            for n, i in enumerate(iu):
                rows[i].append(c[n * SUB:(n + 1) * SUB])

    ones = jnp.ones((LANES, LANES), BF16)
    lane =lax.broadcasted_iota(jnp.int32, (half, LANES), 1)
    t_loc = lax.broadcasted_iota(jnp.int32, (half, LANES), 0)
    a_rows = []
    for i in range(nsub):
        bi, qi = blk(b, i), blk(q, i)
        parts, meta = [], []
        for s in range(SUB):
            if reverse:
                lo, hi = (0, SUB) if s >= half else (0, half)
            else:
                lo, hi = (0, SUB) if s < half else (half, SUB)
            w = jnp.exp2(jnp.minimum(bi[lo:hi] - b_row(i * SUB + s), 0.0))
            p = w * qi[lo:hi] * k_row(i * SUB + s)
            for r0 in range(lo, hi, half):
                parts.append(p[r0 - lo:r0 - lo + half])
                meta.append((s, r0 // half))
        summed = jnp.dot(jnp.concatenate(parts[0::2], axis=0), ones, preferred_element_type=F32)
        diag = [jnp.zeros((half, LANES), F32), jnp.zeros((half, LANES), F32)]
        for n, (s, hh) in enumerate(meta):
            if n % 2 == 0:
                r = summed[(n // 2) * half:(n // 2 + 1) * half]
            else:
                r = jnp.sum(parts[n], axis=-1, keepdims=True)
            diag[hh] = jnp.where(lane == i * SUB + s, r, diag[hh])
        tiles = []
        for hh in range(2):
            s_loc = lane - (i * SUB + hh * half)
            keep = (s_loc >= t_loc) if reverse else (s_loc <= t_loc)
            tiles.append(jnp.where(keep, diag[hh], 0.0))
        a = jnp.concatenate(tiles, axis=0)
        for c in rows[i]:
            a = a + c
        a_rows.append(a)
    scores = jnp.concatenate(a_rows, axis=0)
    return st_new, (scores.astype(BF16), o_carried)


def _hgrn_steps(q_ref, v_ref, kf_ref, bf_ref, kb_ref, bb_ref, vc_ref, kfc_ref, bfc_ref, kbc_ref, bbc_ref,
                o_ref, stf_ref, stb_ref, kwf_ref, kwb_ref, scf_ref, scb_ref, ocf_ref, ocb_ref, w0, n_steps):
    t = q_ref.shape[1]
    ctx = vc_ref.shape[1]
    nw = t // WIN

    def window(refs, start, st, kw_ref, reverse, want_out):
        qr, vr, kr, br = refs
        sl = pl.ds(start, WIN)
        vw = vr[0, sl, :].astype(F32)
        kw = kr[0, sl, :].astype(F32)
        bw = br[0, sl, :]
        qw = qr[0, sl, :].astype(F32) if want_out else None
        kw_ref[...] = kw
        b_row = lambda r: br[0, pl.ds(start + r, 1), :]
        k_row = lambda r: kw_ref[pl.ds(r, 1), :]
        return _scan_window(st, qw, kw, vw, bw, b_row, k_row, reverse, want_out)

    def restart():
        zero = jnp.zeros((LANES, LANES), F32)
        stf, stb = zero, zero
        n_cw = ctx // WIN
        for w in range(n_cw):
            stf, _ = window((None, vc_ref, kfc_ref, bfc_ref), w * WIN, stf, kwf_ref, False, False)
            stb, _ = window((None, vc_ref, kbc_ref, bbc_ref), (n_cw - 1 - w) * WIN, stb, kwb_ref, True, False)
        stf_ref[...] = stf
        stb_ref[...] = stb
        o_ref[...] = jnp.zeros(o_ref.shape, F32)

    def starts(w):
        return pl.multiple_of(w * WIN, WIN), pl.multiple_of((nw - 1 - w) * WIN, WIN)

    def scores_stage(w):
        fs, bs = starts(w)
        stf, (sf, cf) = window((q_ref, v_ref, kf_ref, bf_ref), fs, stf_ref[...], kwf_ref, False, True)
        stb, (sb, cb) = window((q_ref, v_ref, kb_ref, bb_ref), bs, stb_ref[...], kwb_ref, True, True)
        stf_ref[...] = stf
        stb_ref[...] = stb
        scf_ref[...] = sf
        scb_ref[...] = sb
        ocf_ref[...] = cf
        ocb_ref[...] = cb

    def output_stage(w):
        fs, bs = starts(w)
        of = jnp.dot(scf_ref[...], v_ref[0, pl.ds(fs, WIN), :], preferred_element_type=F32) + ocf_ref[...]
        ob = jnp.dot(scb_ref[...], v_ref[0, pl.ds(bs, WIN), :], preferred_element_type=F32) + ocb_ref[...]
        o_ref[0, pl.ds(fs, WIN), :] += of
        o_ref[0, pl.ds(bs, WIN), :] += ob

    def make_step(u):
        def step():
            if u > 0:
                output_stage(w0 + u - 1)
            if u < n_steps:
                scores_stage(w0 + u)
        return step

    return restart, [make_step(u) for u in range(n_steps + 1)]


def _mixer_kernel(qa_ref, kc_ref, vtc_ref, kx_ref, vtx_ref,
                  hq_ref, hv_ref, kf_ref, bf_ref, kb_ref, bb_ref, hvc_ref, kfc_ref, bfc_ref, kbc_ref, bbc_ref,
                  att_ref, hg_ref, sc_ref, s_ref, *hg_scratch, steps_per_head):
    nw = hq_ref.shape[1] // WIN
    n_steps = nw // steps_per_head
    phase = pl.program_id(1) % steps_per_head
    restart, hg_steps = _hgrn_steps(hq_ref, hv_ref, kf_ref, bf_ref, kb_ref, bb_ref,
                                    hvc_ref, kfc_ref, bfc_ref, kbc_ref, bbc_ref, hg_ref, *hg_scratch,
                                    w0=phase * n_steps, n_steps=n_steps)
    pl.when(phase == 0)(restart)

    pending = iter(hg_steps)
    for _ in _attention_schedule(qa_ref, kc_ref, vtc_ref, kx_ref, vtx_ref, att_ref, sc_ref, s_ref):
        step = next(pending, None)
        if step is not None:
            step()
    for step in pending:
        step()


def _mixers(qa, kc, vtc, kx, vtx, hq, hv, kf, bf, kb, bb, hvc, kfc, bfc, kbc, bbc):
    b, t, _ = qa.shape
    ctx = hvc.shape[1]
    tq = min(Q_TILE, t)
    n_q = t // tq
    nw = t // WIN
    assert tq % LANES == 0 and t % tq == 0 and t % KV_TILE == 0 and KV_AHEAD < KV_SLOTS
    assert n_q % HG_HEADS == 0 and nw % (n_q // HG_HEADS) == 0
    steps_per_head = n_q // HG_HEADS
    cols = ATT_GROUP * tq
    whole = lambda a: pl.BlockSpec((1,) + a.shape[1:], lambda i, j: (i, 0, 0))
    lat = pl.BlockSpec((1, t, HG_HEAD_DIM), lambda i, j: (i, 0, j // steps_per_head))
    con = pl.BlockSpec((1, ctx, HG_HEAD_DIM), lambda i, j: (i, 0, j // steps_per_head))
    return pl.pallas_call(
        functools.partial(_mixer_kernel, steps_per_head=steps_per_head),
        out_shape=[jax.ShapeDtypeStruct((b, t, ATT_WIDTH), BF16), jax.ShapeDtypeStruct((b, t, HG_WIDTH), F32)],
        grid=(b, n_q),
        in_specs=[pl.BlockSpec((1, tq, qa.shape[-1]), lambda i, j: (i, j, 0)),
                  whole(kc), whole(vtc), whole(kx), whole(vtx)] + [lat] * 6 + [con] * 5,
        out_specs=[pl.BlockSpec((1, tq, ATT_WIDTH), lambda i, j: (i, j, 0)), lat],
        scratch_shapes=[pltpu.VMEM((ATT_KV_HEADS, kc.shape[1], cols), F32),
                        pltpu.VMEM((KV_SLOTS, ATT_KV_HEADS, KV_TILE, cols), F32),
                        pltpu.VMEM((LANES, LANES), F32), pltpu.VMEM((LANES, LANES), F32),
                        pltpu.VMEM((WIN, LANES), F32), pltpu.VMEM((WIN, LANES), F32),
                        pltpu.VMEM((WIN, WIN), BF16), pltpu.VMEM((WIN, WIN), BF16),
                        pltpu.VMEM((WIN, LANES), F32), pltpu.VMEM((WIN, LANES), F32)],
        compiler_params=pltpu.CompilerParams(dimension_semantics=("arbitrary", "arbitrary"),
                                             vmem_limit_bytes=VMEM_LIMIT),
        name="mixers",
    )(qa, kc, vtc, kx, vtx, hq, hv, kf, bf, kb, bb, hvc, kfc, bfc, kbc, bbc)


def _out_ffn_kernel(x_ref, att_ref, hg_ref, sg_ref, mod_ref, ghg_ref, wo_ref, g2_ref, wgu_ref, wd_ref, gf_ref,
                    o_ref):
    d = x_ref.shape[-1]
    dff = wd_ref.shape[0]
    x = x_ref[0]
    gate1 = mod_ref[0, :, 2 * d:3 * d]
    shift2 = mod_ref[0, :, 3 * d:4 * d]
    scale2 = mod_ref[0, :, 4 * d:5 * d]
    gate2 = mod_ref[0, :, 5 * d:6 * d]

    mix = jnp.dot(att_ref[0], wo_ref[0:ATT_WIDTH, :], preferred_element_type=F32)
    for hd in range(HG_HEADS):
        cs = slice(hd * HG_HEAD_DIM, (hd + 1) * HG_HEAD_DIM)
        o = _rms_rows(hg_ref[0, :, cs]) * ghg_ref[...]
        o = (o * sg_ref[0, :, cs].astype(F32)).astype(BF16)
        mix = mix + jnp.dot(o, wo_ref[ATT_WIDTH + hd * HG_HEAD_DIM:ATT_WIDTH + (hd + 1) * HG_HEAD_DIM, :],
                            preferred_element_type=F32)
    x1 = x + gate1 * mix
    h = ((_rms_rows(x1) * g2_ref[...]) * (1.0 + scale2) + shift2).astype(BF16)
    a = jnp.dot(h, wgu_ref[:, 0:dff], preferred_element_type=F32)
    g = jnp.dot(h, wgu_ref[:, dff:2 * dff], preferred_element_type=F32)
    act = (_silu(a) * g).astype(BF16)
    y = jnp.dot(act, wd_ref[...], preferred_element_type=F32)
    x2 = x1 + gate2 * y
    o_ref[0] = _rms_rows(x2) * gf_ref[...]


def _out_ffn(x, att, hg, sg, mod3, ghg, w_out, g2, w_gu, w_down, gf):
    b, t, d = x.shape
    tm = min(ROW_TILE, t)
    bspec = lambda width: pl.BlockSpec((1, tm, width), lambda i, j: (i, j, 0))
    full = lambda a: pl.BlockSpec(a.shape, lambda i, j: (0,) * a.ndim)
    return pl.pallas_call(
        _out_ffn_kernel,
        out_shape=jax.ShapeDtypeStruct((b, t, d), F32),
        grid=(b, t // tm),
        in_specs=[bspec(d), bspec(ATT_WIDTH), bspec(HG_WIDTH), bspec(HG_WIDTH),
                  pl.BlockSpec((1, 1, mod3.shape[-1]), lambda i, j: (i, 0, 0)),
                  full(ghg), full(w_out), full(g2), full(w_gu), full(w_down), full(gf)],
        out_specs=bspec(d),
        compiler_params=pltpu.CompilerParams(dimension_semantics=("parallel", "parallel"),
                                             vmem_limit_bytes=VMEM_LIMIT),
        name="out_ffn",
    )(x, att, hg, sg, mod3, ghg, w_out, g2, w_gu, w_down, gf)


def _rope_tables(t):
    n_rows = t // GRID_W
    row = jnp.repeat(jnp.arange(n_rows, dtype=F32), GRID_W)
    col = jnp.tile(jnp.arange(GRID_W, dtype=F32), n_rows)
    inv = ROPE_THETA ** (-jnp.arange(0, AXIS_DIM, 2, dtype=F32) / AXIS_DIM)
    ar = row[:, None] * inv
    ac = col[:, None] * inv
    ang = jnp.concatenate([ar, ar, ac, ac], axis=-1)
    sign = jnp.where((jnp.arange(ATT_HEAD_DIM) % AXIS_DIM) < AXIS_DIM // 2, -1.0, 1.0).astype(F32)
    reps = LANES // ATT_HEAD_DIM
    return jnp.tile(jnp.cos(ang), (1, reps)), jnp.tile(jnp.sin(ang) * sign, (1, reps))


def kernel(x, c, ctx, c_ctx, w_mod, b_mod, g_norm1, w_in, g_q, g_k, lb_raw, g_hg, w_out, g_norm2, w_gu, w_down,
           g_final):
    b, t, d = x.shape
    assert w_mod.shape[0] == 1 and lb_raw.shape[0] == 2, "single-layer block"
    assert t % WIN == 0 and (t // WIN) % 2 == 0 and ctx.shape[1] % WIN == 0 and t % KV_TILE == 0

    pad = (-(b + 1)) % 8
    cc = jnp.concatenate([c, c_ctx[None, :], jnp.zeros((pad, d), F32)], axis=0)
    mods = _modulation(cc, w_mod[0], b_mod)
    mod3 = mods.reshape(mods.shape[0], 1, mods.shape[1])

    cos, sin = _rope_tables(t)
    reps = LANES // ATT_HEAD_DIM
    gq = jnp.tile(g_q[0], reps)[None, :]
    gk = jnp.tile(g_k[0], reps)[None, :]
    w_in_b = w_in[0].astype(BF16)

    qa, kx, vtx, hq, hv, kf, bf, kb, bb, sg = _input_projection(
        x, mod3, None, g_norm1, w_in_b, gq, gk, lb_raw, cos, sin, True)
    kc, vtc, hvc, kfc, bfc, kbc, bbc = _input_projection(
        ctx, mod3, b, g_norm1, w_in_b, gq, gk, lb_raw, cos, sin, False)

    att, hg = _mixers(qa, kc, vtc, kx, vtx, hq, hv, kf, bf, kb, bb, hvc, kfc, bfc, kbc, bbc)
    return _out_ffn(x, att, hg, sg, mod3, g_hg, w_out[0].astype(BF16), g_norm2, w_gu[0].astype(BF16),
                    w_down[0].astype(BF16), g_final[None, :])
```

```python
import functools

import jax
import jax.numpy as jnp
import numpy as np
from jax import lax
from jax.experimental import pallas as pl
from jax.experimental.pallas import tpu as pltpu

F32 = jnp.float32
BF16 = jnp.bfloat16

LANES = 128
GRID_W = 64
ATT_HEADS = 8
ATT_KV_HEADS = 2
ATT_GROUP = ATT_HEADS // ATT_KV_HEADS
ATT_HEAD_DIM = 64
ATT_WIDTH = ATT_HEADS * ATT_HEAD_DIM
ATT_KV_WIDTH = ATT_KV_HEADS * ATT_HEAD_DIM
AXIS_DIM = ATT_HEAD_DIM // 2
ROPE_THETA = 10000.0
HG_HEADS = 4
HG_HEAD_DIM = 128
HG_WIDTH = HG_HEADS * HG_HEAD_DIM
EPS = 1e-6

SUB = 16
WIN = 128
ROW_TILE = 256
Q_TILE = 256
KV_TILE = 512
KV_SLOTS = 4
KV_AHEAD = 2
ONES_ROWS = 16
LOG2E = 1.4426950408889634
VMEM_LIMIT = 56 * 1024 * 1024

C_AQ = 0
C_AK = C_AQ + ATT_WIDTH
C_AV = C_AK + ATT_KV_WIDTH
C_HQ = C_AV + ATT_KV_WIDTH
C_HI = C_HQ + HG_WIDTH
C_FF = C_HI + HG_WIDTH
C_FB = C_FF + HG_WIDTH
C_HG = C_FB + HG_WIDTH
C_END = C_HG + HG_WIDTH


def _silu(v):
    return v * jax.nn.sigmoid(v)


def _rms_rows(v):
    return v * lax.rsqrt(jnp.mean(v * v, axis=-1, keepdims=True) + EPS)


def _mod_kernel(c_ref, w_ref, b_ref, o_ref):
    a = _silu(c_ref[...])
    o_ref[...] = jnp.dot(a, w_ref[...], precision=lax.Precision.HIGHEST,
                         preferred_element_type=F32) + b_ref[...]


def _modulation(cc, w_mod, b_mod):
    rows, d = cc.shape
    n = w_mod.shape[1]
    bn = n // 4
    return pl.pallas_call(
        _mod_kernel,
        out_shape=jax.ShapeDtypeStruct((rows, n), F32),
        grid=(n // bn,),
        in_specs=[pl.BlockSpec((rows, d), lambda j: (0, 0)),
                  pl.BlockSpec((d, bn), lambda j: (0, j)),
                  pl.BlockSpec((1, bn), lambda j: (0, j))],
        out_specs=pl.BlockSpec((rows, bn), lambda j: (0, j)),
        compiler_params=pltpu.CompilerParams(dimension_semantics=("arbitrary",),
                                             vmem_limit_bytes=VMEM_LIMIT),
        name="modulation",
    )(cc, w_mod, b_mod)


def _head64_meansq(p, seg2):
    sq = p * p
    hi = sq.astype(BF16)
    lo = (sq - hi.astype(F32)).astype(BF16)
    s = jnp.dot(jnp.concatenate([hi, lo], axis=1), seg2, preferred_element_type=F32)
    return s * (1.0 / ATT_HEAD_DIM)


def _rope(v, cos, sin_signed, first_half):
    fwd = pltpu.roll(v, LANES - AXIS_DIM // 2, 1)
    bwd = pltpu.roll(v, AXIS_DIM // 2, 1)
    return v * cos + jnp.where(first_half, fwd, bwd) * sin_signed


def _gate_prep(f, lb, suffix):
    fg = lb + (1.0 - lb) * jax.nn.sigmoid(f)
    cum = jnp.log2(fg)
    rows = cum.shape[0]
    pos = lax.broadcasted_iota(jnp.int32, cum.shape, 0) % SUB
    shift = 1
    while shift < SUB:
        if suffix:
            moved = pltpu.roll(cum, rows - shift, 0)
            cum = cum + jnp.where(pos < SUB - shift, moved, 0.0)
        else:
            moved = pltpu.roll(cum, shift, 0)
            cum = cum + jnp.where(pos >= shift, moved, 0.0)
        shift *= 2
    return (1.0 - fg).astype(BF16), cum


def _inproj_kernel(x_ref, mod_ref, g1_ref, w_ref, gq_ref, gk_ref, lb_ref, cos_ref, sin_ref, *out_refs, latent):
    d = x_ref.shape[-1]
    x = x_ref[0]
    shift = mod_ref[0, :, 0:d]
    scale = mod_ref[0, :, d:2 * d]
    h = (_rms_rows(x) * g1_ref[...]) * (1.0 + scale) + shift
    hb = h.astype(BF16)

    def proj(c0, width):
        return jnp.dot(hb, w_ref[:, c0:c0 + width], preferred_element_type=F32)

    lane = lax.broadcasted_iota(jnp.int32, (2 * LANES, LANES), 0) % LANES
    lane_c = lax.broadcasted_iota(jnp.int32, (2 * LANES, LANES), 1)
    seg = (lane // ATT_HEAD_DIM == lane_c // ATT_HEAD_DIM).astype(BF16)
    row_lane = lax.broadcasted_iota(jnp.int32, (x.shape[0], LANES), 1)
    first_half = (row_lane % AXIS_DIM) < (AXIS_DIM // 2)
    upper = row_lane >= ATT_HEAD_DIM

    r0 = lb_ref[0]
    r1 = lb_ref[1]
    rm = jnp.maximum(r0, r1)
    e0 = jnp.exp(r0 - rm)
    e1 = jnp.exp(r1 - rm)
    lb_all = e0 / (e0 + e1)

    if latent:
        (qa_ref, ka_ref, vat_ref, hq_ref, hv_ref, kf_ref, bf_ref, kb_ref, bb_ref, sg_ref) = out_refs
        cos = cos_ref[...]
        sin = sin_ref[...]
    else:
        (ka_ref, vat_ref, hv_ref, kf_ref, bf_ref, kb_ref, bb_ref) = out_refs

    p_ff = proj(C_FF, HG_WIDTH)
    p_fb = proj(C_FB, HG_WIDTH)
    pk = proj(C_AK, ATT_KV_WIDTH)
    pv = proj(C_AV, ATT_KV_WIDTH)
    if latent:
        pq = proj(C_AQ, ATT_WIDTH)
    else:
        p_hv = proj(C_HI, HG_WIDTH)
    ms_k = _head64_meansq(pk, seg)
    if latent:
        p_hq = proj(C_HQ, HG_WIDTH)
        ms_q = [_head64_meansq(pq[:, c * LANES:(c + 1) * LANES], seg) for c in range(ATT_WIDTH // LANES)]
        p_hv = proj(C_HI, HG_WIDTH)

    kn = pk * lax.rsqrt(ms_k + EPS) * gk_ref[...]
    if latent:
        kn = _rope(kn, cos, sin, first_half)
    ka_ref[0] = kn.astype(BF16)
    vat_ref[0] = pv.T.astype(BF16)

    if latent:
        for c in range(ATT_WIDTH // LANES):
            t = pq[:, c * LANES:(c + 1) * LANES]
            t = t * lax.rsqrt(ms_q[c] + EPS) * gq_ref[...]
            t = _rope(t, cos, sin, first_half) * (ATT_HEAD_DIM ** -0.5 * LOG2E)
            t_swapped = pltpu.roll(t, ATT_HEAD_DIM, 1)
            for half in range(2):
                head = 2 * c + half
                kv = head // ATT_GROUP
                src = t if half == kv else t_swapped
                keep = upper if kv == 1 else jnp.logical_not(upper)
                qa_ref[0, :, head * LANES:(head + 1) * LANES] = jnp.where(keep, src, 0.0).astype(BF16)
        hq_ref[0] = (_silu(p_hq) * (HG_HEAD_DIM ** -0.5)).astype(BF16)
        sg_ref[0] = _silu(proj(C_HG, HG_WIDTH)).astype(BF16)

    hv_ref[0] = p_hv.astype(BF16)
    kf, bf = _gate_prep(p_ff, lb_all[0:1, :], False)
    kf_ref[0] = kf
    bf_ref[0] = bf
    kb, bb = _gate_prep(p_fb, lb_all[1:2, :], True)
    kb_ref[0] = kb
    bb_ref[0] = bb


def _input_projection(rows, mod3, mod_row, g1, w_in, gq, gk, lb_raw, cos, sin, latent):
    b, t, d = rows.shape
    tm = min(ROW_TILE, t)
    nt = t // tm
    bspec = lambda width: pl.BlockSpec((1, tm, width), lambda i, j: (i, j, 0))
    full = lambda a: pl.BlockSpec(a.shape, lambda i, j: (0,) * a.ndim)
    sds = lambda width, dt: jax.ShapeDtypeStruct((b, t, width), dt)
    vt_sds = jax.ShapeDtypeStruct((b, ATT_KV_WIDTH, t), BF16)
    vt_spec = pl.BlockSpec((1, ATT_KV_WIDTH, tm), lambda i, j: (i, 0, j))
    if latent:
        out_shape = [sds(ATT_HEADS * LANES, BF16), sds(ATT_KV_WIDTH, BF16), vt_sds,
                     sds(HG_WIDTH, BF16), sds(HG_WIDTH, BF16), sds(HG_WIDTH, BF16), sds(HG_WIDTH, F32),
                     sds(HG_WIDTH, BF16), sds(HG_WIDTH, F32), sds(HG_WIDTH, BF16)]
        out_specs = [bspec(ATT_HEADS * LANES), bspec(ATT_KV_WIDTH), vt_spec, bspec(HG_WIDTH),
                     bspec(HG_WIDTH), bspec(HG_WIDTH), bspec(HG_WIDTH), bspec(HG_WIDTH), bspec(HG_WIDTH),
                     bspec(HG_WIDTH)]
        mod_spec = pl.BlockSpec((1, 1, mod3.shape[-1]), lambda i, j: (i, 0, 0))
    else:
        out_shape = [sds(ATT_KV_WIDTH, BF16), vt_sds, sds(HG_WIDTH, BF16), sds(HG_WIDTH, BF16),
                     sds(HG_WIDTH, F32), sds(HG_WIDTH, BF16), sds(HG_WIDTH, F32)]
        out_specs = [bspec(ATT_KV_WIDTH), vt_spec, bspec(HG_WIDTH), bspec(HG_WIDTH), bspec(HG_WIDTH),
                     bspec(HG_WIDTH), bspec(HG_WIDTH)]
        mod_spec = pl.BlockSpec((1, 1, mod3.shape[-1]), lambda i, j: (mod_row, 0, 0))
    table_spec = pl.BlockSpec((tm, LANES), lambda i, j: (j, 0))
    return pl.pallas_call(
        functools.partial(_inproj_kernel, latent=latent),
        out_shape=out_shape,
        grid=(b, nt),
        in_specs=[bspec(d), mod_spec, full(g1), full(w_in), full(gq), full(gk), full(lb_raw),
                  table_spec, table_spec],
        out_specs=out_specs,
        compiler_params=pltpu.CompilerParams(dimension_semantics=("parallel", "parallel"),
                                             vmem_limit_bytes=VMEM_LIMIT),
        name="inproj_latent" if latent else "inproj_context",
    )(rows, mod3, g1, w_in, gq, gk, lb_raw, cos, sin)


def _attention_schedule(q_ref, kc_ref, vtc_ref, kx_ref, vtx_ref, o_ref, sc_ref, s_ref):
    tq = q_ref.shape[1]
    n_kv = kx_ref.shape[1] // KV_TILE
    nt_dims = (((1,), (1,)), ((), ()))
    cols = ATT_GROUP * tq
    acc_rows = ATT_HEAD_DIM + ONES_ROWS

    qgs = [jnp.concatenate(
        [q_ref[0, :, (kv * ATT_GROUP + j) * LANES:(kv * ATT_GROUP + j + 1) * LANES] for j in range(ATT_GROUP)],
        axis=0) for kv in range(ATT_KV_HEADS)]

    def scores(kt, dst):
        for kv in range(ATT_KV_HEADS):
            dst(kv, lax.dot_general(kt, qgs[kv], nt_dims, preferred_element_type=F32))

    def consume(src, vt_all, carry):
        ones = jnp.ones((ONES_ROWS, vt_all.shape[1]), BF16)
        out = []
        for kv in range(ATT_KV_HEADS):
            m, acc = carry[kv]
            m_new = jnp.maximum(m, jnp.max(src(kv), axis=0, keepdims=True))
            alpha = jnp.exp2(m - m_new)
            p = jnp.exp2(src(kv) - m_new).astype(BF16)
            vt = jnp.concatenate([vt_all[kv * ATT_HEAD_DIM:(kv + 1) * ATT_HEAD_DIM], ones], axis=0)
            acc = alpha * acc + jnp.dot(vt, p, preferred_element_type=F32)
            out.append((m_new, acc))
        return tuple(out)

    def put_ctx(kv, val):
        sc_ref[kv] = val

    def put(slot):
        def dst(kv, val):
            s_ref[slot, kv] = val
        return dst

    def k_tile(j):
        return kx_ref[0, j * KV_TILE:(j + 1) * KV_TILE, :]

    def vt_tile(j):
        return vtx_ref[0, :, j * KV_TILE:(j + 1) * KV_TILE]

    init = (jnp.full((1, cols), -jnp.inf, F32), jnp.zeros((acc_rows, cols), F32))
    scores(kc_ref[0], put_ctx)
    for j in range(min(KV_AHEAD, n_kv)):
        scores(k_tile(j), put(j % KV_SLOTS))
    carry = consume(lambda kv: sc_ref[kv], vtc_ref[0], (init,) * ATT_KV_HEADS)
    for j in range(n_kv):
        if j + KV_AHEAD < n_kv:
            scores(k_tile(j + KV_AHEAD), put((j + KV_AHEAD) % KV_SLOTS))
        carry = consume(lambda kv, slot=j % KV_SLOTS: s_ref[slot, kv], vt_tile(j), carry)
        yield

    for kv in range(ATT_KV_HEADS):
        acc = carry[kv][1]
        o = acc[0:ATT_HEAD_DIM] / acc[ATT_HEAD_DIM:ATT_HEAD_DIM + 1]
        for jj in range(ATT_GROUP // 2):
            pair = jnp.concatenate([o[:, (2 * jj) * tq:(2 * jj + 1) * tq],
                                    o[:, (2 * jj + 1) * tq:(2 * jj + 2) * tq]], axis=0)
            col = kv * (ATT_GROUP // 2) + jj
            o_ref[0, :, col * LANES:(col + 1) * LANES] = pair.T.astype(BF16)


NT_DIMS = (((1,), (1,)), ((), ()))


def _zero_rows(n):
    return [jnp.zeros((n, LANES), F32)] if n else []


def _scan_window(st, q, k, v, b, b_row, msk_ref, reverse, want_out):
    nsub = WIN // SUB
    last = 0 if reverse else SUB - 1
    sub_of = (lambda p: nsub - 1 - p) if reverse else (lambda p: p)
    blk = lambda a, i: a[i * SUB:(i + 1) * SUB]
    d = int(reverse)

    b_last = [b_row(i * SUB + last) for i in range(nsub)]
    cum = []
    for p in range(nsub):
        cum.append(b_last[sub_of(p)] if p == 0 else cum[-1] + b_last[sub_of(p)])
    ks = [blk(k, i) * jnp.exp2(b_last[i] - blk(b, i)) for i in range(nsub)]

    k_end = jnp.concatenate(
        [ks[i] if sub_of(i) == nsub - 1 else ks[i] * jnp.exp2(cum[nsub - 1] - cum[sub_of(i)]) for i in range(nsub)],
        axis=0)
    st_new = jnp.exp2(cum[nsub - 1]) * st + jnp.dot(v.T, k_end, preferred_element_type=F32)
    if not want_out:
        return st_new, None

    qs = [blk(q, i) * jnp.exp2(blk(b, i)) for i in range(nsub)]
    q_start = jnp.concatenate(
        [qs[i] if sub_of(i) == 0 else qs[i] * jnp.exp2(cum[sub_of(i) - 1]) for i in range(nsub)], axis=0)
    o_carried = lax.dot_general(q_start, st, NT_DIMS, preferred_element_type=F32)

    rows = [[] for _ in range(nsub)]
    for h in (nsub // 2, nsub // 4, nsub // 8):
        for ub in range(1, nsub // h, 2):
            ref = ub * h - 1
            upper = {sub_of(p): p for p in range(ub * h, (ub + 1) * h)}
            lower = {sub_of(p): p for p in range((ub - 1) * h, ub * h)}
            iu, il = sorted(upper), sorted(lower)
            lhs = jnp.concatenate(
                [qs[i] if upper[i] - 1 == ref else qs[i] * jnp.exp2(cum[upper[i] - 1] - cum[ref]) for i in iu], axis=0)
            rhs = jnp.concatenate(
                _zero_rows(il[0] * SUB)
                + [ks[i] if lower[i] == ref else ks[i] * jnp.exp2(cum[ref] - cum[lower[i]]) for i in il]
                + _zero_rows((nsub - 1 - il[-1]) * SUB), axis=0)
            c = lax.dot_general(lhs, rhs, NT_DIMS, preferred_element_type=F32)
            for n, i in enumerate(iu):
                rows[i].append(c[n * SUB:(n + 1) * SUB])

    def level_scores(level, q_exponent, k_exponent):
        q_l = q * jnp.exp2(jnp.minimum(q_exponent, 0.0))
        k_l = k if k_exponent is None else k * jnp.exp2(jnp.minimum(k_exponent, 0.0))
        return lax.dot_general(q_l, k_l, NT_DIMS, preferred_element_type=F32) * msk_ref[d, level]

    scores = jnp.dot(q * k, jnp.ones((LANES, LANES), BF16), preferred_element_type=F32) * msk_ref[d, N_FINE]
    for level, h in ((0, SUB // 2), (1, SUB // 4)):
        ref_rows = [b_row(s * 2 * h + (h if reverse else h - 1)) for s in range(WIN // (2 * h))]
        diff = b - jnp.concatenate([jnp.broadcast_to(r, (2 * h, LANES)) for r in ref_rows], axis=0)
        scores = scores + level_scores(level, diff, -diff)
    before = lambda n: pltpu.roll(b, n, 0)
    after = lambda n: pltpu.roll(b, WIN - n, 0)
    off4 = lax.broadcasted_iota(jnp.int32, (WIN, LANES), 0) % 4
    if reverse:
        q_exp = b - jnp.where(off4 == 0, after(2), after(1))
        k_exp = jnp.where(off4 == 2, 0.0, before(1) - b)
    else:
        q_exp = b - jnp.where(off4 == 3, before(2), before(1))
        k_exp = jnp.where(off4 == 1, 0.0, after(1) - b)
    scores = scores + level_scores(2, q_exp, k_exp)
    scores = scores + level_scores(3, b - (after(1) if reverse else before(1)), None)
    a_rows = []
    for i in range(nsub):
        a = blk(scores, i)
        for c in rows[i]:
            a = a + c
        a_rows.append(a)
    scores = jnp.concatenate(a_rows, axis=0)
    return st_new, (scores.astype(BF16), o_carried)


N_FINE = 4


def _pair_masks():
    msk = np.zeros((2, N_FINE + 1, WIN, WIN), np.float32)
    pos = np.arange(WIN)
    for d in range(2):
        for level in range(N_FINE):
            h = SUB >> (level + 1)
            span, off = pos // (2 * h), pos % (2 * h)
            later = (off < h) if d else (off >= h)
            msk[d, level] = (span[:, None] == span[None, :]) & later[:, None] & ~later[None, :]
        msk[d, N_FINE] = np.eye(WIN)
    return jnp.asarray(msk, F32)


def _hgrn_steps(q_ref, v_ref, kf_ref, bf_ref, kb_ref, bb_ref, vc_ref, kfc_ref, bfc_ref, kbc_ref, bbc_ref,
                msk_ref, o_ref, stf_ref, stb_ref, scf_ref, scb_ref, ocf_ref, ocb_ref, w0, n_steps):
    t = q_ref.shape[1]
    ctx = vc_ref.shape[1]
    nw = t // WIN

    def window(refs, start, st, reverse, want_out):
        qr, vr, kr, br = refs
        sl = pl.ds(start, WIN)
        vw = vr[0, sl, :].astype(F32)
        kw = kr[0, sl, :].astype(F32)
        bw = br[0, sl, :]
        qw = qr[0, sl, :].astype(F32) if want_out else None
        b_row = lambda r: br[0, pl.ds(start + r, 1), :]
        return _scan_window(st, qw, kw, vw, bw, b_row, msk_ref, reverse, want_out)

    def restart():
        zero = jnp.zeros((LANES, LANES), F32)
        stf, stb = zero, zero
        n_cw = ctx // WIN
        for w in range(n_cw):
            stf, _ = window((None, vc_ref, kfc_ref, bfc_ref), w * WIN, stf, False, False)
            stb, _ = window((None, vc_ref, kbc_ref, bbc_ref), (n_cw - 1 - w) * WIN, stb, True, False)
        stf_ref[...] = stf
        stb_ref[...] = stb
        o_ref[...] = jnp.zeros(o_ref.shape, F32)

    def starts(w):
        return pl.multiple_of(w * WIN, WIN), pl.multiple_of((nw - 1 - w) * WIN, WIN)

    def scores_stage(w):
        fs, bs = starts(w)
        stf, (sf, cf) = window((q_ref, v_ref, kf_ref, bf_ref), fs, stf_ref[...], False, True)
        stf_ref[...] = stf
        scf_ref[...] = sf
        ocf_ref[...] = cf
        stb, (sb, cb) = window((q_ref, v_ref, kb_ref, bb_ref), bs, stb_ref[...], True, True)
        stb_ref[...] = stb
        scb_ref[...] = sb
        ocb_ref[...] = cb

    def output_stage(w):
        fs, bs = starts(w)
        o_ref[0, pl.ds(fs, WIN), :] += (
            jnp.dot(scf_ref[...], v_ref[0, pl.ds(fs, WIN), :], preferred_element_type=F32) + ocf_ref[...])
        o_ref[0, pl.ds(bs, WIN), :] += (
            jnp.dot(scb_ref[...], v_ref[0, pl.ds(bs, WIN), :], preferred_element_type=F32) + ocb_ref[...])

    def schedule():
        for u in range(n_steps + 1):
            if u > 0:
                output_stage(w0 + u - 1)
            if u < n_steps:
                scores_stage(w0 + u)
            yield

    return restart, schedule()


def _mixer_kernel(qa_ref, kc_ref, vtc_ref, kx_ref, vtx_ref,
                  hq_ref, hv_ref, kf_ref, bf_ref, kb_ref, bb_ref, hvc_ref, kfc_ref, bfc_ref, kbc_ref, bbc_ref,
                  msk_ref, att_ref, hg_ref, sc_ref, s_ref, *hg_scratch, steps_per_head):
    nw = hq_ref.shape[1] // WIN
    n_steps = nw // steps_per_head
    phase = pl.program_id(1) % steps_per_head
    restart, scan = _hgrn_steps(hq_ref, hv_ref, kf_ref, bf_ref, kb_ref, bb_ref,
                                hvc_ref, kfc_ref, bfc_ref, kbc_ref, bbc_ref, msk_ref, hg_ref, *hg_scratch,
                                w0=phase * n_steps, n_steps=n_steps)
    pl.when(phase == 0)(restart)

    for _ in _attention_schedule(qa_ref, kc_ref, vtc_ref, kx_ref, vtx_ref, att_ref, sc_ref, s_ref):
        next(scan, None)
    for _ in scan:
        pass


def _mixers(qa, kc, vtc, kx, vtx, hq, hv, kf, bf, kb, bb, hvc, kfc, bfc, kbc, bbc):
    b, t, _ = qa.shape
    ctx = hvc.shape[1]
    tq = min(Q_TILE, t)
    n_q = t // tq
    nw = t // WIN
    assert tq % LANES == 0 and t % tq == 0 and t % KV_TILE == 0 and KV_AHEAD < KV_SLOTS
    assert n_q % HG_HEADS == 0 and nw % (n_q // HG_HEADS) == 0
    steps_per_head = n_q // HG_HEADS
    cols = ATT_GROUP * tq
    whole = lambda a: pl.BlockSpec((1,) + a.shape[1:], lambda i, j: (i, 0, 0))
    lat = pl.BlockSpec((1, t, HG_HEAD_DIM), lambda i, j: (i, 0, j // steps_per_head))
    con = pl.BlockSpec((1, ctx, HG_HEAD_DIM), lambda i, j: (i, 0, j // steps_per_head))
    msk = _pair_masks()
    const = lambda a: pl.BlockSpec(a.shape, lambda i, j: (0,) * a.ndim)
    return pl.pallas_call(
        functools.partial(_mixer_kernel, steps_per_head=steps_per_head),
        out_shape=[jax.ShapeDtypeStruct((b, t, ATT_WIDTH), BF16), jax.ShapeDtypeStruct((b, t, HG_WIDTH), F32)],
        grid=(b, n_q),
        in_specs=[pl.BlockSpec((1, tq, qa.shape[-1]), lambda i, j: (i, j, 0)),
                  whole(kc), whole(vtc), whole(kx), whole(vtx)] + [lat] * 6 + [con] * 5
                 + [const(msk)],
        out_specs=[pl.BlockSpec((1, tq, ATT_WIDTH), lambda i, j: (i, j, 0)), lat],
        scratch_shapes=[pltpu.VMEM((ATT_KV_HEADS, kc.shape[1], cols), F32),
                        pltpu.VMEM((KV_SLOTS, ATT_KV_HEADS, KV_TILE, cols), F32),
                        pltpu.VMEM((LANES, LANES), F32), pltpu.VMEM((LANES, LANES), F32),
                        pltpu.VMEM((WIN, WIN), BF16), pltpu.VMEM((WIN, WIN), BF16),
                        pltpu.VMEM((WIN, LANES), F32), pltpu.VMEM((WIN, LANES), F32)],
        compiler_params=pltpu.CompilerParams(dimension_semantics=("arbitrary", "arbitrary"),
                                             vmem_limit_bytes=VMEM_LIMIT),
        name="mixers",
    )(qa, kc, vtc, kx, vtx, hq, hv, kf, bf, kb, bb, hvc, kfc, bfc, kbc, bbc, msk)


def _out_ffn_kernel(x_ref, att_ref, hg_ref, sg_ref, mod_ref, ghg_ref, wo_ref, g2_ref, wgu_ref, wd_ref, gf_ref,
                    o_ref):
    d = x_ref.shape[-1]
    dff = wd_ref.shape[0]
    x = x_ref[0]
    gate1 = mod_ref[0, :, 2 * d:3 * d]
    shift2 = mod_ref[0, :, 3 * d:4 * d]
    scale2 = mod_ref[0, :, 4 * d:5 * d]
    gate2 = mod_ref[0, :, 5 * d:6 * d]

    mix = jnp.dot(att_ref[0], wo_ref[0:ATT_WIDTH, :], preferred_element_type=F32)
    for hd in range(HG_HEADS):
        cs = slice(hd * HG_HEAD_DIM, (hd + 1) * HG_HEAD_DIM)
        o = _rms_rows(hg_ref[0, :, cs]) * ghg_ref[...]
        o = (o * sg_ref[0, :, cs].astype(F32)).astype(BF16)
        mix = mix + jnp.dot(o, wo_ref[ATT_WIDTH + hd * HG_HEAD_DIM:ATT_WIDTH + (hd + 1) * HG_HEAD_DIM, :],
                            preferred_element_type=F32)
    x1 = x + gate1 * mix
    h = ((_rms_rows(x1) * g2_ref[...]) * (1.0 + scale2) + shift2).astype(BF16)
    a = jnp.dot(h, wgu_ref[:, 0:dff], preferred_element_type=F32)
    g = jnp.dot(h, wgu_ref[:, dff:2 * dff], preferred_element_type=F32)
    act = (_silu(a) * g).astype(BF16)
    y = jnp.dot(act, wd_ref[...], preferred_element_type=F32)
    x2 = x1 + gate2 * y
    o_ref[0] = _rms_rows(x2) * gf_ref[...]


def _out_ffn(x, att, hg, sg, mod3, ghg, w_out, g2, w_gu, w_down, gf):
    b, t, d = x.shape
    tm = min(ROW_TILE, t)
    bspec = lambda width: pl.BlockSpec((1, tm, width), lambda i, j: (i, j, 0))
    full = lambda a: pl.BlockSpec(a.shape, lambda i, j: (0,) * a.ndim)
    return pl.pallas_call(
        _out_ffn_kernel,
        out_shape=jax.ShapeDtypeStruct((b, t, d), F32),
        grid=(b, t // tm),
        in_specs=[bspec(d), bspec(ATT_WIDTH), bspec(HG_WIDTH), bspec(HG_WIDTH),
                  pl.BlockSpec((1, 1, mod3.shape[-1]), lambda i, j: (i, 0, 0)),
                  full(ghg), full(w_out), full(g2), full(w_gu), full(w_down), full(gf)],
        out_specs=bspec(d),
        compiler_params=pltpu.CompilerParams(dimension_semantics=("parallel", "parallel"),
                                             vmem_limit_bytes=VMEM_LIMIT),
        name="out_ffn",
    )(x, att, hg, sg, mod3, ghg, w_out, g2, w_gu, w_down, gf)


def _rope_tables(t):
    n_rows = t // GRID_W
    row = jnp.repeat(jnp.arange(n_rows, dtype=F32), GRID_W)
    col = jnp.tile(jnp.arange(GRID_W, dtype=F32), n_rows)
    inv = ROPE_THETA ** (-jnp.arange(0, AXIS_DIM, 2, dtype=F32) / AXIS_DIM)
    ar = row[:, None] * inv
    ac = col[:, None] * inv
    ang = jnp.concatenate([ar, ar, ac, ac], axis=-1)
    sign = jnp.where((jnp.arange(ATT_HEAD_DIM) % AXIS_DIM) < AXIS_DIM // 2, -1.0, 1.0).astype(F32)
    reps = LANES // ATT_HEAD_DIM
    return jnp.tile(jnp.cos(ang), (1, reps)), jnp.tile(jnp.sin(ang) * sign, (1, reps))


def kernel(x, c, ctx, c_ctx, w_mod, b_mod, g_norm1, w_in, g_q, g_k, lb_raw, g_hg, w_out, g_norm2, w_gu, w_down,
           g_final):
    b, t, d = x.shape
    assert w_mod.shape[0] == 1 and lb_raw.shape[0] == 2, "single-layer block"
    assert t % WIN == 0 and (t // WIN) % 2 == 0 and ctx.shape[1] % WIN == 0 and t % KV_TILE == 0

    pad = (-(b + 1)) % 8
    cc = jnp.concatenate([c, c_ctx[None, :], jnp.zeros((pad, d), F32)], axis=0)
    mods = _modulation(cc, w_mod[0], b_mod)
    mod3 = mods.reshape(mods.shape[0], 1, mods.shape[1])

    cos, sin = _rope_tables(t)
    reps = LANES // ATT_HEAD_DIM
    gq = jnp.tile(g_q[0], reps)[None, :]
    gk = jnp.tile(g_k[0], reps)[None, :]
    w_in_b = w_in[0].astype(BF16)

    qa, kx, vtx, hq, hv, kf, bf, kb, bb, sg = _input_projection(
        x, mod3, None, g_norm1, w_in_b, gq, gk, lb_raw, cos, sin, True)
    kc, vtc, hvc, kfc, bfc, kbc, bbc = _input_projection(
        ctx, mod3, b, g_norm1, w_in_b, gq, gk, lb_raw, cos, sin, False)

    att, hg = _mixers(qa, kc, vtc, kx, vtx, hq, hv, kf, bf, kb, bb, hvc, kfc, bfc, kbc, bbc)
    return _out_ffn(x, att, hg, sg, mod3, g_hg, w_out[0].astype(BF16), g_norm2, w_gu[0].astype(BF16),
                    w_down[0].astype(BF16), g_final[None, :])
```

```python
import functools

import jax
import jax.numpy as jnp
import numpy as np
from jax import lax
from jax.experimental import pallas as pl
from jax.experimental.pallas import tpu as pltpu

F32 = jnp.float32
BF16 = jnp.bfloat16

LANES = 128
GRID_W = 64
ATT_HEADS = 8
ATT_KV_HEADS = 2
ATT_GROUP = ATT_HEADS // ATT_KV_HEADS
ATT_HEAD_DIM = 64
ATT_WIDTH = ATT_HEADS * ATT_HEAD_DIM
ATT_KV_WIDTH = ATT_KV_HEADS * ATT_HEAD_DIM
AXIS_DIM = ATT_HEAD_DIM // 2
ROPE_THETA = 10000.0
HG_HEADS = 4
HG_HEAD_DIM = 128
HG_WIDTH = HG_HEADS * HG_HEAD_DIM
EPS = 1e-6

SUB = 16
WIN = 128
ROW_TILE = 256
OUT_ROW_TILE = 512
OUT_PART_ROWS = 256
Q_TILE = 256
KV_TILE = 512
KV_SLOTS = 4
KV_AHEAD = 2
ONES_ROWS = 16
LOG2E = 1.4426950408889634
VMEM_LIMIT = 56 * 1024 * 1024

C_AQ = 0
C_AK = C_AQ + ATT_WIDTH
C_AV = C_AK + ATT_KV_WIDTH
C_HQ = C_AV + ATT_KV_WIDTH
C_HI = C_HQ + HG_WIDTH
C_FF = C_HI + HG_WIDTH
C_FB = C_FF + HG_WIDTH
C_HG = C_FB + HG_WIDTH
C_END = C_HG + HG_WIDTH


def _silu(v):
    return v * jax.nn.sigmoid(v)


def _rms_rows(v):
    return v * lax.rsqrt(jnp.mean(v * v, axis=-1, keepdims=True) + EPS)


def _mod_kernel(c_ref, w_ref, b_ref, o_ref):
    a = _silu(c_ref[...])
    o_ref[...] = jnp.dot(a, w_ref[...], precision=lax.Precision.HIGHEST,
                         preferred_element_type=F32) + b_ref[...]


def _modulation(cc, w_mod, b_mod):
    rows, d = cc.shape
    n = w_mod.shape[1]
    bn = n // 4
    return pl.pallas_call(
        _mod_kernel,
        out_shape=jax.ShapeDtypeStruct((rows, n), F32),
        grid=(n // bn,),
        in_specs=[pl.BlockSpec((rows, d), lambda j: (0, 0)),
                  pl.BlockSpec((d, bn), lambda j: (0, j)),
                  pl.BlockSpec((1, bn), lambda j: (0, j))],
        out_specs=pl.BlockSpec((rows, bn), lambda j: (0, j)),
        compiler_params=pltpu.CompilerParams(dimension_semantics=("arbitrary",),
                                             vmem_limit_bytes=VMEM_LIMIT),
        name="modulation",
    )(cc, w_mod, b_mod)


def _head64_meansq(p, seg2):
    sq = p * p
    hi = sq.astype(BF16)
    lo = (sq - hi.astype(F32)).astype(BF16)
    s = jnp.dot(jnp.concatenate([hi, lo], axis=1), seg2, preferred_element_type=F32)
    return s * (1.0 / ATT_HEAD_DIM)


def _rope(v, cos, sin_signed, first_half):
    fwd = pltpu.roll(v, LANES - AXIS_DIM // 2, 1)
    bwd = pltpu.roll(v, AXIS_DIM // 2, 1)
    return v * cos + jnp.where(first_half, fwd, bwd) * sin_signed


def _gate_prep(f, lb, suffix):
    fg = lb + (1.0 - lb) * jax.nn.sigmoid(f)
    cum = jnp.log2(fg)
    rows = cum.shape[0]
    pos = lax.broadcasted_iota(jnp.int32, cum.shape, 0) % SUB
    shift = 1
    while shift < SUB:
        if suffix:
            moved = pltpu.roll(cum, rows - shift, 0)
            cum = cum + jnp.where(pos < SUB - shift, moved, 0.0)
        else:
            moved = pltpu.roll(cum, shift, 0)
            cum = cum + jnp.where(pos >= shift, moved, 0.0)
        shift *= 2
    return (1.0 - fg).astype(BF16), cum


def _inproj_kernel(x_ref, mod_ref, g1_ref, w_ref, gq_ref, gk_ref, lb_ref, cos_ref, sin_ref, *out_refs, latent):
    d = x_ref.shape[-1]
    x = x_ref[0]
    shift = mod_ref[0, :, 0:d]
    scale = mod_ref[0, :, d:2 * d]
    h = (_rms_rows(x) * g1_ref[...]) * (1.0 + scale) + shift
    hb = h.astype(BF16)

    def proj(c0, width):
        return jnp.dot(hb, w_ref[:, c0:c0 + width], preferred_element_type=F32)

    lane = lax.broadcasted_iota(jnp.int32, (2 * LANES, LANES), 0) % LANES
    lane_c = lax.broadcasted_iota(jnp.int32, (2 * LANES, LANES), 1)
    seg = (lane // ATT_HEAD_DIM == lane_c // ATT_HEAD_DIM).astype(BF16)
    row_lane = lax.broadcasted_iota(jnp.int32, (x.shape[0], LANES), 1)
    first_half = (row_lane % AXIS_DIM) < (AXIS_DIM // 2)
    upper = row_lane >= ATT_HEAD_DIM

    r0 = lb_ref[0]
    r1 = lb_ref[1]
    rm = jnp.maximum(r0, r1)
    e0 = jnp.exp(r0 - rm)
    e1 = jnp.exp(r1 - rm)
    lb_all = e0 / (e0 + e1)

    if latent:
        (qa_ref, ka_ref, vat_ref, hq_ref, hv_ref, kf_ref, bf_ref, kb_ref, bb_ref, sg_ref) = out_refs
        cos = cos_ref[...]
        sin = sin_ref[...]
    else:
        (ka_ref, vat_ref, hv_ref, kf_ref, bf_ref, kb_ref, bb_ref) = out_refs

    p_ff = proj(C_FF, HG_WIDTH)
    p_fb = proj(C_FB, HG_WIDTH)
    pk = proj(C_AK, ATT_KV_WIDTH)
    pv = proj(C_AV, ATT_KV_WIDTH)
    if latent:
        pq = proj(C_AQ, ATT_WIDTH)
    else:
        p_hv = proj(C_HI, HG_WIDTH)
    ms_k = _head64_meansq(pk, seg)
    if latent:
        p_hq = proj(C_HQ, HG_WIDTH)
        ms_q = [_head64_meansq(pq[:, c * LANES:(c + 1) * LANES], seg) for c in range(ATT_WIDTH // LANES)]
        p_hv = proj(C_HI, HG_WIDTH)

    kn = pk * lax.rsqrt(ms_k + EPS) * gk_ref[...]
    if latent:
        kn = _rope(kn, cos, sin, first_half)
    ka_ref[0] = kn.astype(BF16)
    vat_ref[0] = pv.T.astype(BF16)

    if latent:
        for c in range(ATT_WIDTH // LANES):
            t = pq[:, c * LANES:(c + 1) * LANES]
            t = t * lax.rsqrt(ms_q[c] + EPS) * gq_ref[...]
            t = _rope(t, cos, sin, first_half) * (ATT_HEAD_DIM ** -0.5 * LOG2E)
            t_swapped = pltpu.roll(t, ATT_HEAD_DIM, 1)
            for half in range(2):
                head = 2 * c + half
                kv = head // ATT_GROUP
                src = t if half == kv else t_swapped
                keep = upper if kv == 1 else jnp.logical_not(upper)
                qa_ref[0, :, head * LANES:(head + 1) * LANES] = jnp.where(keep, src, 0.0).astype(BF16)
        hq_ref[0] = (_silu(p_hq) * (HG_HEAD_DIM ** -0.5)).astype(BF16)
        sg_ref[0] = _silu(proj(C_HG, HG_WIDTH)).astype(BF16)

    hv_ref[0] = p_hv.astype(BF16)
    kf, bf = _gate_prep(p_ff, lb_all[0:1, :], False)
    kf_ref[0] = kf
    bf_ref[0] = bf
    kb, bb = _gate_prep(p_fb, lb_all[1:2, :], True)
    kb_ref[0] = kb
    bb_ref[0] = bb


def _input_projection(rows, mod3, mod_row, g1, w_in, gq, gk, lb_raw, cos, sin, latent):
    b, t, d = rows.shape
    tm = min(ROW_TILE, t)
    nt = t // tm
    bspec = lambda width: pl.BlockSpec((1, tm, width), lambda i, j: (i, j, 0))
    full = lambda a: pl.BlockSpec(a.shape, lambda i, j: (0,) * a.ndim)
    sds = lambda width, dt: jax.ShapeDtypeStruct((b, t, width), dt)
    vt_sds = jax.ShapeDtypeStruct((b, ATT_KV_WIDTH, t), BF16)
    vt_spec = pl.BlockSpec((1, ATT_KV_WIDTH, tm), lambda i, j: (i, 0, j))
    if latent:
        out_shape = [sds(ATT_HEADS * LANES, BF16), sds(ATT_KV_WIDTH, BF16), vt_sds,
                     sds(HG_WIDTH, BF16), sds(HG_WIDTH, BF16), sds(HG_WIDTH, BF16), sds(HG_WIDTH, F32),
                     sds(HG_WIDTH, BF16), sds(HG_WIDTH, F32), sds(HG_WIDTH, BF16)]
        out_specs = [bspec(ATT_HEADS * LANES), bspec(ATT_KV_WIDTH), vt_spec, bspec(HG_WIDTH),
                     bspec(HG_WIDTH), bspec(HG_WIDTH), bspec(HG_WIDTH), bspec(HG_WIDTH), bspec(HG_WIDTH),
                     bspec(HG_WIDTH)]
        mod_spec = pl.BlockSpec((1, 1, mod3.shape[-1]), lambda i, j: (i, 0, 0))
    else:
        out_shape = [sds(ATT_KV_WIDTH, BF16), vt_sds, sds(HG_WIDTH, BF16), sds(HG_WIDTH, BF16),
                     sds(HG_WIDTH, F32), sds(HG_WIDTH, BF16), sds(HG_WIDTH, F32)]
        out_specs = [bspec(ATT_KV_WIDTH), vt_spec, bspec(HG_WIDTH), bspec(HG_WIDTH), bspec(HG_WIDTH),
                     bspec(HG_WIDTH), bspec(HG_WIDTH)]
        mod_spec = pl.BlockSpec((1, 1, mod3.shape[-1]), lambda i, j: (mod_row, 0, 0))
    table_spec = pl.BlockSpec((tm, LANES), lambda i, j: (j, 0))
    return pl.pallas_call(
        functools.partial(_inproj_kernel, latent=latent),
        out_shape=out_shape,
        grid=(b, nt),
        in_specs=[bspec(d), mod_spec, full(g1), full(w_in), full(gq), full(gk), full(lb_raw),
                  table_spec, table_spec],
        out_specs=out_specs,
        compiler_params=pltpu.CompilerParams(dimension_semantics=("parallel", "parallel"),
                                             vmem_limit_bytes=VMEM_LIMIT),
        name="inproj_latent" if latent else "inproj_context",
    )(rows, mod3, g1, w_in, gq, gk, lb_raw, cos, sin)


def _attention_schedule(q_ref, kc_ref, vtc_ref, kx_ref, vtx_ref, o_ref, sc_ref, s_ref, acc_ref):
    tq = q_ref.shape[1]
    n_kv = kx_ref.shape[1] // KV_TILE
    nt_dims = (((1,), (1,)), ((), ()))
    cols = ATT_GROUP * tq

    qgs = [jnp.concatenate(
        [q_ref[0, :, (kv * ATT_GROUP + j) * LANES:(kv * ATT_GROUP + j + 1) * LANES] for j in range(ATT_GROUP)],
        axis=0) for kv in range(ATT_KV_HEADS)]

    def scores(kt, dst):
        for kv in range(ATT_KV_HEADS):
            dst(kv, lax.dot_general(kt, qgs[kv], nt_dims, preferred_element_type=F32))

    def consume(src, vt_all, maxima):
        ones = jnp.ones((ONES_ROWS, vt_all.shape[1]), BF16)
        out = []
        for kv in range(ATT_KV_HEADS):
            m = maxima[kv]
            m_new = jnp.maximum(m, jnp.max(src(kv), axis=0, keepdims=True))
            alpha = jnp.exp2(m - m_new)
            p = jnp.exp2(src(kv) - m_new).astype(BF16)
            vt = jnp.concatenate([vt_all[kv * ATT_HEAD_DIM:(kv + 1) * ATT_HEAD_DIM], ones], axis=0)
            acc_ref[kv] = alpha * acc_ref[kv] + jnp.dot(vt, p, preferred_element_type=F32)
            out.append(m_new)
        return tuple(out)

    def put_ctx(kv, val):
        sc_ref[kv] = val

    def put(slot):
        def dst(kv, val):
            s_ref[slot, kv] = val
        return dst

    def k_tile(j):
        return kx_ref[0, j * KV_TILE:(j + 1) * KV_TILE, :]

    def vt_tile(j):
        return vtx_ref[0, :, j * KV_TILE:(j + 1) * KV_TILE]

    acc_ref[...] = jnp.zeros(acc_ref.shape, F32)
    scores(kc_ref[0], put_ctx)
    for j in range(min(KV_AHEAD, n_kv)):
        scores(k_tile(j), put(j % KV_SLOTS))
    maxima = consume(lambda kv: sc_ref[kv], vtc_ref[0], (jnp.full((1, cols), -jnp.inf, F32),) * ATT_KV_HEADS)
    for j in range(n_kv):
        if j + KV_AHEAD < n_kv:
            scores(k_tile(j + KV_AHEAD), put((j + KV_AHEAD) % KV_SLOTS))
        maxima = consume(lambda kv, slot=j % KV_SLOTS: s_ref[slot, kv], vt_tile(j), maxima)
        yield

    for kv in range(ATT_KV_HEADS):
        o = acc_ref[kv, 0:ATT_HEAD_DIM] / acc_ref[kv, ATT_HEAD_DIM:ATT_HEAD_DIM + 1]
        for jj in range(ATT_GROUP // 2):
            pair = jnp.concatenate([o[:, (2 * jj) * tq:(2 * jj + 1) * tq],
                                    o[:, (2 * jj + 1) * tq:(2 * jj + 2) * tq]], axis=0)
            col = kv * (ATT_GROUP // 2) + jj
            o_ref[0, :, col * LANES:(col + 1) * LANES] = pair.T.astype(BF16)


NT_DIMS = (((1,), (1,)), ((), ()))


def _zero_rows(n):
    return [jnp.zeros((n, LANES), F32)] if n else []


def _scan_window(st, q, k, v, b, b_row, msk_ref, reverse, want_out):
    nsub = WIN // SUB
    last = 0 if reverse else SUB - 1
    sub_of = (lambda p: nsub - 1 - p) if reverse else (lambda p: p)
    blk = lambda a, i: a[i * SUB:(i + 1) * SUB]
    d = int(reverse)

    b_last = [b_row(i * SUB + last) for i in range(nsub)]
    cum = []
    for p in range(nsub):
        cum.append(b_last[sub_of(p)] if p == 0 else cum[-1] + b_last[sub_of(p)])
    ks = [blk(k, i) * jnp.exp2(b_last[i] - blk(b, i)) for i in range(nsub)]

    k_end = jnp.concatenate(
        [ks[i] if sub_of(i) == nsub - 1 else ks[i] * jnp.exp2(cum[nsub - 1] - cum[sub_of(i)]) for i in range(nsub)],
        axis=0)
    st_new = jnp.exp2(cum[nsub - 1]) * st + jnp.dot(v.T, k_end, preferred_element_type=F32)
    if not want_out:
        return st_new, None

    qs = [blk(q, i) * jnp.exp2(blk(b, i)) for i in range(nsub)]
    q_start = jnp.concatenate(
        [qs[i] if sub_of(i) == 0 else qs[i] * jnp.exp2(cum[sub_of(i) - 1]) for i in range(nsub)], axis=0)
    o_carried = lax.dot_general(q_start, st, NT_DIMS, preferred_element_type=F32)

    rows = [[] for _ in range(nsub)]
    for h in (nsub // 2, nsub // 4, nsub // 8):
        for ub in range(1, nsub // h, 2):
            ref = ub * h - 1
            upper = {sub_of(p): p for p in range(ub * h, (ub + 1) * h)}
            lower = {sub_of(p): p for p in range((ub - 1) * h, ub * h)}
            iu, il = sorted(upper), sorted(lower)
            lhs = jnp.concatenate(
                [qs[i] if upper[i] - 1 == ref else qs[i] * jnp.exp2(cum[upper[i] - 1] - cum[ref]) for i in iu], axis=0)
            rhs = jnp.concatenate(
                _zero_rows(il[0] * SUB)
                + [ks[i] if lower[i] == ref else ks[i] * jnp.exp2(cum[ref] - cum[lower[i]]) for i in il]
                + _zero_rows((nsub - 1 - il[-1]) * SUB), axis=0)
            c = lax.dot_general(lhs, rhs, NT_DIMS, preferred_element_type=F32)
            for n, i in enumerate(iu):
                rows[i].append(c[n * SUB:(n + 1) * SUB])

    def level_scores(level, q_exponent, k_exponent):
        q_l = q * jnp.exp2(jnp.minimum(q_exponent, 0.0))
        k_l = k if k_exponent is None else k * jnp.exp2(jnp.minimum(k_exponent, 0.0))
        return lax.dot_general(q_l, k_l, NT_DIMS, preferred_element_type=F32) * msk_ref[d, level]

    scores = jnp.dot(q * k, jnp.ones((LANES, LANES), BF16), preferred_element_type=F32) * msk_ref[d, N_FINE]
    for level, h in ((0, SUB // 2), (1, SUB // 4)):
        ref_rows = [b_row(s * 2 * h + (h if reverse else h - 1)) for s in range(WIN // (2 * h))]
        diff = b - jnp.concatenate([jnp.broadcast_to(r, (2 * h, LANES)) for r in ref_rows], axis=0)
        scores = scores + level_scores(level, diff, -diff)
    before = lambda n: pltpu.roll(b, n, 0)
    after = lambda n: pltpu.roll(b, WIN - n, 0)
    off4 = lax.broadcasted_iota(jnp.int32, (WIN, LANES), 0) % 4
    if reverse:
        q_exp = b - jnp.where(off4 == 0, after(2), after(1))
        k_exp = jnp.where(off4 == 2, 0.0, before(1) - b)
    else:
        q_exp = b - jnp.where(off4 == 3, before(2), before(1))
        k_exp = jnp.where(off4 == 1, 0.0, after(1) - b)
    scores = scores + level_scores(2, q_exp, k_exp)
    scores = scores + level_scores(3, b - (after(1) if reverse else before(1)), None)
    a_rows = []
    for i in range(nsub):
        a = blk(scores, i)
        for c in rows[i]:
            a = a + c
        a_rows.append(a)
    scores = jnp.concatenate(a_rows, axis=0)
    return st_new, (scores.astype(BF16), o_carried)


N_FINE = 4


def _pair_masks():
    msk = np.zeros((2, N_FINE + 1, WIN, WIN), np.float32)
    pos = np.arange(WIN)
    for d in range(2):
        for level in range(N_FINE):
            h = SUB >> (level + 1)
            span, off = pos // (2 * h), pos % (2 * h)
            later = (off < h) if d else (off >= h)
            msk[d, level] = (span[:, None] == span[None, :]) & later[:, None] & ~later[None, :]
        msk[d, N_FINE] = np.eye(WIN)
    return jnp.asarray(msk, F32)


def _hgrn_steps(q_ref, v_ref, kf_ref, bf_ref, kb_ref, bb_ref, vc_ref, kfc_ref, bfc_ref, kbc_ref, bbc_ref,
                msk_ref, o_ref, stf_ref, stb_ref, scf_ref, scb_ref, ocf_ref, ocb_ref, w0, n_steps):
    t = q_ref.shape[1]
    ctx = vc_ref.shape[1]
    nw = t // WIN

    def window(refs, start, st, reverse, want_out):
        qr, vr, kr, br = refs
        sl = pl.ds(start, WIN)
        vw = vr[0, sl, :].astype(F32)
        kw = kr[0, sl, :].astype(F32)
        bw = br[0, sl, :]
        qw = qr[0, sl, :].astype(F32) if want_out else None
        b_row = lambda r: br[0, pl.ds(start + r, 1), :]
        return _scan_window(st, qw, kw, vw, bw, b_row, msk_ref, reverse, want_out)

    def restart():
        zero = jnp.zeros((LANES, LANES), F32)
        stf, stb = zero, zero
        n_cw = ctx // WIN
        for w in range(n_cw):
            stf, _ = window((None, vc_ref, kfc_ref, bfc_ref), w * WIN, stf, False, False)
            stb, _ = window((None, vc_ref, kbc_ref, bbc_ref), (n_cw - 1 - w) * WIN, stb, True, False)
        stf_ref[...] = stf
        stb_ref[...] = stb
        o_ref[...] = jnp.zeros(o_ref.shape, F32)

    def starts(w):
        return pl.multiple_of(w * WIN, WIN), pl.multiple_of((nw - 1 - w) * WIN, WIN)

    def scores_stage(w):
        fs, bs = starts(w)
        stf, (sf, cf) = window((q_ref, v_ref, kf_ref, bf_ref), fs, stf_ref[...], False, True)
        stf_ref[...] = stf
        scf_ref[...] = sf
        ocf_ref[...] = cf
        stb, (sb, cb) = window((q_ref, v_ref, kb_ref, bb_ref), bs, stb_ref[...], True, True)
        stb_ref[...] = stb
        scb_ref[...] = sb
        ocb_ref[...] = cb

    def output_stage(w):
        fs, bs = starts(w)
        o_ref[0, pl.ds(fs, WIN), :] += (
            jnp.dot(scf_ref[...], v_ref[0, pl.ds(fs, WIN), :], preferred_element_type=F32) + ocf_ref[...])
        o_ref[0, pl.ds(bs, WIN), :] += (
            jnp.dot(scb_ref[...], v_ref[0, pl.ds(bs, WIN), :], preferred_element_type=F32) + ocb_ref[...])

    def schedule():
        for u in range(n_steps + 1):
            if u > 0:
                output_stage(w0 + u - 1)
            if u < n_steps:
                scores_stage(w0 + u)
            yield

    return restart, schedule()


def _mixer_kernel(qa_ref, kc_ref, vtc_ref, kx_ref, vtx_ref,
                  hq_ref, hv_ref, kf_ref, bf_ref, kb_ref, bb_ref, hvc_ref, kfc_ref, bfc_ref, kbc_ref, bbc_ref,
                  msk_ref, att_ref, hg_ref, sc_ref, s_ref, acc_ref, *hg_scratch, steps_per_head):
    nw = hq_ref.shape[1] // WIN
    n_steps = nw // steps_per_head
    phase = pl.program_id(1) % steps_per_head
    restart, scan = _hgrn_steps(hq_ref, hv_ref, kf_ref, bf_ref, kb_ref, bb_ref,
                                hvc_ref, kfc_ref, bfc_ref, kbc_ref, bbc_ref, msk_ref, hg_ref, *hg_scratch,
                                w0=phase * n_steps, n_steps=n_steps)
    pl.when(phase == 0)(restart)

    for _ in _attention_schedule(qa_ref, kc_ref, vtc_ref, kx_ref, vtx_ref, att_ref, sc_ref, s_ref, acc_ref):
        next(scan, None)
    for _ in scan:
        pass


def _mixers(qa, kc, vtc, kx, vtx, hq, hv, kf, bf, kb, bb, hvc, kfc, bfc, kbc, bbc):
    b, t, _ = qa.shape
    ctx = hvc.shape[1]
    tq = min(Q_TILE, t)
    n_q = t // tq
    nw = t // WIN
    assert tq % LANES == 0 and t % tq == 0 and t % KV_TILE == 0 and KV_AHEAD < KV_SLOTS
    assert n_q % HG_HEADS == 0 and nw % (n_q // HG_HEADS) == 0
    steps_per_head = n_q // HG_HEADS
    cols = ATT_GROUP * tq
    whole = lambda a: pl.BlockSpec((1,) + a.shape[1:], lambda i, j: (i, 0, 0))
    lat = pl.BlockSpec((1, t, HG_HEAD_DIM), lambda i, j: (i, 0, j // steps_per_head))
    con = pl.BlockSpec((1, ctx, HG_HEAD_DIM), lambda i, j: (i, 0, j // steps_per_head))
    msk = _pair_masks()
    const = lambda a: pl.BlockSpec(a.shape, lambda i, j: (0,) * a.ndim)
    return pl.pallas_call(
        functools.partial(_mixer_kernel, steps_per_head=steps_per_head),
        out_shape=[jax.ShapeDtypeStruct((b, t, ATT_WIDTH), BF16), jax.ShapeDtypeStruct((b, t, HG_WIDTH), F32)],
        grid=(b, n_q),
        in_specs=[pl.BlockSpec((1, tq, qa.shape[-1]), lambda i, j: (i, j, 0)),
                  whole(kc), whole(vtc), whole(kx), whole(vtx)] + [lat] * 6 + [con] * 5
                 + [const(msk)],
        out_specs=[pl.BlockSpec((1, tq, ATT_WIDTH), lambda i, j: (i, j, 0)), lat],
        scratch_shapes=[pltpu.VMEM((ATT_KV_HEADS, kc.shape[1], cols), F32),
                        pltpu.VMEM((KV_SLOTS, ATT_KV_HEADS, KV_TILE, cols), F32),
                        pltpu.VMEM((ATT_KV_HEADS, ATT_HEAD_DIM + ONES_ROWS, cols), F32),
                        pltpu.VMEM((LANES, LANES), F32), pltpu.VMEM((LANES, LANES), F32),
                        pltpu.VMEM((WIN, WIN), BF16), pltpu.VMEM((WIN, WIN), BF16),
                        pltpu.VMEM((WIN, LANES), F32), pltpu.VMEM((WIN, LANES), F32)],
        compiler_params=pltpu.CompilerParams(dimension_semantics=("arbitrary", "arbitrary"),
                                             vmem_limit_bytes=VMEM_LIMIT),
        name="mixers",
    )(qa, kc, vtc, kx, vtx, hq, hv, kf, bf, kb, bb, hvc, kfc, bfc, kbc, bbc, msk)


def _out_ffn_kernel(x_ref, att_ref, hg_ref, sg_ref, mod_ref, ghg_ref, wo_ref, g2_ref, wgu_ref, wd_ref, gf_ref,
                    o_ref):
    d = x_ref.shape[-1]
    dff = wd_ref.shape[0]
    gate1 = mod_ref[0, :, 2 * d:3 * d]
    shift2 = mod_ref[0, :, 3 * d:4 * d]
    scale2 = mod_ref[0, :, 4 * d:5 * d]
    gate2 = mod_ref[0, :, 5 * d:6 * d]

    parts = [slice(r0, r0 + OUT_PART_ROWS) for r0 in range(0, x_ref.shape[1], OUT_PART_ROWS)]

    def mixed(rows):
        mix = jnp.dot(att_ref[0, rows, :], wo_ref[0:ATT_WIDTH, :], preferred_element_type=F32)
        for hd in range(HG_HEADS):
            cs = slice(hd * HG_HEAD_DIM, (hd + 1) * HG_HEAD_DIM)
            o = _rms_rows(hg_ref[0, rows, cs]) * ghg_ref[...]
            o = (o * sg_ref[0, rows, cs].astype(F32)).astype(BF16)
            mix = mix + jnp.dot(o, wo_ref[ATT_WIDTH + hd * HG_HEAD_DIM:ATT_WIDTH + (hd + 1) * HG_HEAD_DIM, :],
                                preferred_element_type=F32)
        x1 = x_ref[0, rows, :] + gate1 * mix
        return x1, ((_rms_rows(x1) * g2_ref[...]) * (1.0 + scale2) + shift2).astype(BF16)

    def gate_up(h):
        return (jnp.dot(h, wgu_ref[:, 0:dff], preferred_element_type=F32),
                jnp.dot(h, wgu_ref[:, dff:2 * dff], preferred_element_type=F32))

    def down(a, g):
        return jnp.dot((_silu(a) * g).astype(BF16), wd_ref[...], preferred_element_type=F32)

    firsts = [mixed(rows) for rows in parts]
    ups = [gate_up(h) for _, h in firsts]
    ys = [down(a, g) for a, g in ups]
    for rows, (x1, _), y in zip(parts, firsts, ys):
        o_ref[0, rows, :] = _rms_rows(x1 + gate2 * y) * gf_ref[...]


def _out_ffn(x, att, hg, sg, mod3, ghg, w_out, g2, w_gu, w_down, gf):
    b, t, d = x.shape
    tm = min(OUT_ROW_TILE, t)
    assert tm % OUT_PART_ROWS == 0
    bspec = lambda width: pl.BlockSpec((1, tm, width), lambda i, j: (i, j, 0))
    full = lambda a: pl.BlockSpec(a.shape, lambda i, j: (0,) * a.ndim, pipeline_mode=pl.Buffered(1))
    return pl.pallas_call(
        _out_ffn_kernel,
        out_shape=jax.ShapeDtypeStruct((b, t, d), F32),
        grid=(b, t // tm),
        in_specs=[bspec(d), bspec(ATT_WIDTH), bspec(HG_WIDTH), bspec(HG_WIDTH),
                  pl.BlockSpec((1, 1, mod3.shape[-1]), lambda i, j: (i, 0, 0)),
                  full(ghg), full(w_out), full(g2), full(w_gu), full(w_down), full(gf)],
        out_specs=bspec(d),
        compiler_params=pltpu.CompilerParams(dimension_semantics=("parallel", "parallel"),
                                             vmem_limit_bytes=VMEM_LIMIT),
        name="out_ffn",
    )(x, att, hg, sg, mod3, ghg, w_out, g2, w_gu, w_down, gf)


def _rope_tables(t):
    n_rows = t // GRID_W
    row = np.repeat(np.arange(n_rows, dtype=np.float32), GRID_W)
    col = np.tile(np.arange(GRID_W, dtype=np.float32), n_rows)
    inv = (np.float32(ROPE_THETA) ** (-np.arange(0, AXIS_DIM, 2, dtype=np.float32) / np.float32(AXIS_DIM)))
    inv = inv.astype(np.float32)
    ar = row[:, None] * inv
    ac = col[:, None] * inv
    ang = np.concatenate([ar, ar, ac, ac], axis=-1).astype(np.float32)
    sign = np.where((np.arange(ATT_HEAD_DIM) % AXIS_DIM) < AXIS_DIM // 2, -1.0, 1.0)
    reps = LANES // ATT_HEAD_DIM
    cos = np.tile(np.cos(ang.astype(np.float64)), (1, reps)).astype(np.float32)
    sin = np.tile(np.sin(ang.astype(np.float64)) * sign, (1, reps)).astype(np.float32)
    return jnp.asarray(cos), jnp.asarray(sin)


def kernel(x, c, ctx, c_ctx, w_mod, b_mod, g_norm1, w_in, g_q, g_k, lb_raw, g_hg, w_out, g_norm2, w_gu, w_down,
           g_final):
    b, t, d = x.shape
    assert w_mod.shape[0] == 1 and lb_raw.shape[0] == 2, "single-layer block"
    assert t % WIN == 0 and (t // WIN) % 2 == 0 and ctx.shape[1] % WIN == 0 and t % KV_TILE == 0

    pad = (-(b + 1)) % 8
    cc = jnp.concatenate([c, c_ctx[None, :], jnp.zeros((pad, d), F32)], axis=0)
    mods = _modulation(cc, w_mod[0], b_mod)
    mod3 = mods.reshape(mods.shape[0], 1, mods.shape[1])

    cos, sin = _rope_tables(t)
    reps = LANES // ATT_HEAD_DIM
    gq = jnp.tile(g_q[0], reps)[None, :]
    gk = jnp.tile(g_k[0], reps)[None, :]
    w_in_b = w_in[0].astype(BF16)

    qa, kx, vtx, hq, hv, kf, bf, kb, bb, sg = _input_projection(
        x, mod3, None, g_norm1, w_in_b, gq, gk, lb_raw, cos, sin, True)
    kc, vtc, hvc, kfc, bfc, kbc, bbc = _input_projection(
        ctx, mod3, b, g_norm1, w_in_b, gq, gk, lb_raw, cos, sin, False)

    att, hg = _mixers(qa, kc, vtc, kx, vtx, hq, hv, kf, bf, kb, bb, hvc, kfc, bfc, kbc, bbc)
    return _out_ffn(x, att, hg, sg, mod3, g_hg, w_out[0].astype(BF16), g_norm2, w_gu[0].astype(BF16),
                    w_down[0].astype(BF16), g_final[None, :])
```

```python
import functools

import jax
import jax.numpy as jnp
import numpy as np
from jax import lax
from jax.experimental import pallas as pl
from jax.experimental.pallas import tpu as pltpu

F32 = jnp.float32
BF16 = jnp.bfloat16

LANES = 128
GRID_W = 64
ATT_HEADS = 8
ATT_KV_HEADS = 2
ATT_GROUP = ATT_HEADS // ATT_KV_HEADS
ATT_HEAD_DIM = 64
ATT_WIDTH = ATT_HEADS * ATT_HEAD_DIM
ATT_KV_WIDTH = ATT_KV_HEADS * ATT_HEAD_DIM
AXIS_DIM = ATT_HEAD_DIM // 2
ROPE_THETA = 10000.0
HG_HEADS = 4
HG_HEAD_DIM = 128
HG_WIDTH = HG_HEADS * HG_HEAD_DIM
EPS = 1e-6

SUB = 16
WIN = 128
IN_ROW_TILE = 512
IN_PART_ROWS = 256
OUT_ROW_TILE = 512
OUT_PART_ROWS = 256
Q_TILE = 256
KV_TILE = 512
KV_SLOTS = 4
KV_AHEAD = 2
ONES_ROWS = 16
LOG2E = 1.4426950408889634
VMEM_LIMIT = 56 * 1024 * 1024

C_AQ = 0
C_AK = C_AQ + ATT_WIDTH
C_AV = C_AK + ATT_KV_WIDTH
C_HQ = C_AV + ATT_KV_WIDTH
C_HI = C_HQ + HG_WIDTH
C_FF = C_HI + HG_WIDTH
C_FB = C_FF + HG_WIDTH
C_HG = C_FB + HG_WIDTH
C_END = C_HG + HG_WIDTH


def _silu(v):
    return v * jax.nn.sigmoid(v)


def _rms_rows(v):
    return v * lax.rsqrt(jnp.mean(v * v, axis=-1, keepdims=True) + EPS)


def _mod_kernel(c_ref, w_ref, b_ref, o_ref):
    a = _silu(c_ref[...])
    o_ref[...] = jnp.dot(a, w_ref[...], precision=lax.Precision.HIGHEST,
                         preferred_element_type=F32) + b_ref[...]


def _modulation(cc, w_mod, b_mod):
    rows, d = cc.shape
    n = w_mod.shape[1]
    bn = n // 4
    return pl.pallas_call(
        _mod_kernel,
        out_shape=jax.ShapeDtypeStruct((rows, n), F32),
        grid=(n // bn,),
        in_specs=[pl.BlockSpec((rows, d), lambda j: (0, 0)),
                  pl.BlockSpec((d, bn), lambda j: (0, j)),
                  pl.BlockSpec((1, bn), lambda j: (0, j))],
        out_specs=pl.BlockSpec((rows, bn), lambda j: (0, j)),
        compiler_params=pltpu.CompilerParams(dimension_semantics=("arbitrary",),
                                             vmem_limit_bytes=VMEM_LIMIT),
        name="modulation",
    )(cc, w_mod, b_mod)


def _head64_meansq(p, seg2):
    sq = p * p
    hi = sq.astype(BF16)
    lo = (sq - hi.astype(F32)).astype(BF16)
    s = jnp.dot(jnp.concatenate([hi, lo], axis=1), seg2, preferred_element_type=F32)
    return s * (1.0 / ATT_HEAD_DIM)


def _rope(v, cos, sin_signed, first_half):
    fwd = pltpu.roll(v, LANES - AXIS_DIM // 2, 1)
    bwd = pltpu.roll(v, AXIS_DIM // 2, 1)
    return v * cos + jnp.where(first_half, fwd, bwd) * sin_signed


def _gate_prep(f, lb, suffix):
    fg = lb + (1.0 - lb) * jax.nn.sigmoid(f)
    cum = jnp.log2(fg)
    rows = cum.shape[0]
    pos = lax.broadcasted_iota(jnp.int32, cum.shape, 0) % SUB
    shift = 1
    while shift < SUB:
        if suffix:
            moved = pltpu.roll(cum, rows - shift, 0)
            cum = cum + jnp.where(pos < SUB - shift, moved, 0.0)
        else:
            moved = pltpu.roll(cum, shift, 0)
            cum = cum + jnp.where(pos >= shift, moved, 0.0)
        shift *= 2
    return (1.0 - fg).astype(BF16), cum


def _inproj_kernel(x_ref, mod_ref, g1_ref, w_ref, gq_ref, gk_ref, lb_ref, cos_ref, sin_ref, *out_refs, latent):
    d = x_ref.shape[-1]
    shift = mod_ref[0, :, 0:d]
    scale = mod_ref[0, :, d:2 * d]
    part_rows = min(IN_PART_ROWS, x_ref.shape[1])
    parts = [slice(r0, r0 + part_rows) for r0 in range(0, x_ref.shape[1], part_rows)]

    lane = lax.broadcasted_iota(jnp.int32, (2 * LANES, LANES), 0) % LANES
    lane_c = lax.broadcasted_iota(jnp.int32, (2 * LANES, LANES), 1)
    seg = (lane // ATT_HEAD_DIM == lane_c // ATT_HEAD_DIM).astype(BF16)
    row_lane = lax.broadcasted_iota(jnp.int32, (part_rows, LANES), 1)
    first_half = (row_lane % AXIS_DIM) < (AXIS_DIM // 2)
    upper = row_lane >= ATT_HEAD_DIM

    r0 = lb_ref[0]
    r1 = lb_ref[1]
    rm = jnp.maximum(r0, r1)
    e0 = jnp.exp(r0 - rm)
    e1 = jnp.exp(r1 - rm)
    lb_all = e0 / (e0 + e1)

    if latent:
        (qa_ref, ka_ref, vat_ref, hq_ref, hv_ref, kf_ref, bf_ref, kb_ref, bb_ref, sg_ref) = out_refs
    else:
        (ka_ref, vat_ref, hv_ref, kf_ref, bf_ref, kb_ref, bb_ref) = out_refs

    hbs = [((_rms_rows(x_ref[0, rows, :]) * g1_ref[...]) * (1.0 + scale) + shift).astype(BF16) for rows in parts]
    for rows, hb in zip(parts, hbs):
        _inproj_part(rows, hb, w_ref, gq_ref, gk_ref, cos_ref, sin_ref, seg, first_half, upper, lb_all,
                     out_refs, latent)


def _inproj_part(rows, hb, w_ref, gq_ref, gk_ref, cos_ref, sin_ref, seg, first_half, upper, lb_all, out_refs,
                 latent):
    def proj(c0, width):
        return jnp.dot(hb, w_ref[:, c0:c0 + width], preferred_element_type=F32)

    if latent:
        (qa_ref, ka_ref, vat_ref, hq_ref, hv_ref, kf_ref, bf_ref, kb_ref, bb_ref, sg_ref) = out_refs
        cos = cos_ref[rows, :]
        sin = sin_ref[rows, :]
    else:
        (ka_ref, vat_ref, hv_ref, kf_ref, bf_ref, kb_ref, bb_ref) = out_refs

    p_ff = proj(C_FF, HG_WIDTH)
    p_fb = proj(C_FB, HG_WIDTH)
    pk = proj(C_AK, ATT_KV_WIDTH)
    pv = proj(C_AV, ATT_KV_WIDTH)
    if latent:
        pq = proj(C_AQ, ATT_WIDTH)
    else:
        p_hv = proj(C_HI, HG_WIDTH)
    ms_k = _head64_meansq(pk, seg)
    if latent:
        p_hq = proj(C_HQ, HG_WIDTH)
        ms_q = [_head64_meansq(pq[:, c * LANES:(c + 1) * LANES], seg) for c in range(ATT_WIDTH // LANES)]
        p_hv = proj(C_HI, HG_WIDTH)

    kn = pk * lax.rsqrt(ms_k + EPS) * gk_ref[...]
    if latent:
        kn = _rope(kn, cos, sin, first_half)
    ka_ref[0, rows, :] = kn.astype(BF16)
    vat_ref[0, :, rows] = pv.T.astype(BF16)

    if latent:
        for c in range(ATT_WIDTH // LANES):
            t = pq[:, c * LANES:(c + 1) * LANES]
            t = t * lax.rsqrt(ms_q[c] + EPS) * gq_ref[...]
            t = _rope(t, cos, sin, first_half) * (ATT_HEAD_DIM ** -0.5 * LOG2E)
            t_swapped = pltpu.roll(t, ATT_HEAD_DIM, 1)
            for half in range(2):
                head = 2 * c + half
                kv = head // ATT_GROUP
                src = t if half == kv else t_swapped
                keep = upper if kv == 1 else jnp.logical_not(upper)
                qa_ref[0, rows, head * LANES:(head + 1) * LANES] = jnp.where(keep, src, 0.0).astype(BF16)
        hq_ref[0, rows, :] = (_silu(p_hq) * (HG_HEAD_DIM ** -0.5)).astype(BF16)
        sg_ref[0, rows, :] = _silu(proj(C_HG, HG_WIDTH)).astype(BF16)

    hv_ref[0, rows, :] = p_hv.astype(BF16)
    kf, bf = _gate_prep(p_ff, lb_all[0:1, :], False)
    kf_ref[0, rows, :] = kf
    bf_ref[0, rows, :] = bf
    kb, bb = _gate_prep(p_fb, lb_all[1:2, :], True)
    kb_ref[0, rows, :] = kb
    bb_ref[0, rows, :] = bb


def _input_projection(rows, mod3, mod_row, g1, w_in, gq, gk, lb_raw, cos, sin, latent):
    b, t, d = rows.shape
    tm = min(IN_ROW_TILE, t)
    nt = t // tm
    assert tm % min(IN_PART_ROWS, tm) == 0
    bspec = lambda width: pl.BlockSpec((1, tm, width), lambda i, j: (i, j, 0))
    full = lambda a: pl.BlockSpec(a.shape, lambda i, j: (0,) * a.ndim, pipeline_mode=pl.Buffered(1))
    sds = lambda width, dt: jax.ShapeDtypeStruct((b, t, width), dt)
    vt_sds = jax.ShapeDtypeStruct((b, ATT_KV_WIDTH, t), BF16)
    vt_spec = pl.BlockSpec((1, ATT_KV_WIDTH, tm), lambda i, j: (i, 0, j))
    if latent:
        out_shape = [sds(ATT_HEADS * LANES, BF16), sds(ATT_KV_WIDTH, BF16), vt_sds,
                     sds(HG_WIDTH, BF16), sds(HG_WIDTH, BF16), sds(HG_WIDTH, BF16), sds(HG_WIDTH, F32),
                     sds(HG_WIDTH, BF16), sds(HG_WIDTH, F32), sds(HG_WIDTH, BF16)]
        out_specs = [bspec(ATT_HEADS * LANES), bspec(ATT_KV_WIDTH), vt_spec, bspec(HG_WIDTH),
                     bspec(HG_WIDTH), bspec(HG_WIDTH), bspec(HG_WIDTH), bspec(HG_WIDTH), bspec(HG_WIDTH),
                     bspec(HG_WIDTH)]
        mod_spec = pl.BlockSpec((1, 1, mod3.shape[-1]), lambda i, j: (i, 0, 0))
    else:
        out_shape = [sds(ATT_KV_WIDTH, BF16), vt_sds, sds(HG_WIDTH, BF16), sds(HG_WIDTH, BF16),
                     sds(HG_WIDTH, F32), sds(HG_WIDTH, BF16), sds(HG_WIDTH, F32)]
        out_specs = [bspec(ATT_KV_WIDTH), vt_spec, bspec(HG_WIDTH), bspec(HG_WIDTH), bspec(HG_WIDTH),
                     bspec(HG_WIDTH), bspec(HG_WIDTH)]
        mod_spec = pl.BlockSpec((1, 1, mod3.shape[-1]), lambda i, j: (mod_row, 0, 0))
    table_spec = pl.BlockSpec((tm, LANES), lambda i, j: (j, 0))
    return pl.pallas_call(
        functools.partial(_inproj_kernel, latent=latent),
        out_shape=out_shape,
        grid=(b, nt),
        in_specs=[bspec(d), mod_spec, full(g1), full(w_in), full(gq), full(gk), full(lb_raw),
                  table_spec, table_spec],
        out_specs=out_specs,
        compiler_params=pltpu.CompilerParams(dimension_semantics=("parallel", "parallel"),
                                             vmem_limit_bytes=VMEM_LIMIT),
        name="inproj_latent" if latent else "inproj_context",
    )(rows, mod3, g1, w_in, gq, gk, lb_raw, cos, sin)


def _attention_schedule(q_ref, kc_ref, vtc_ref, kx_ref, vtx_ref, o_ref, sc_ref, s_ref, acc_ref):
    tq = q_ref.shape[1]
    n_kv = kx_ref.shape[1] // KV_TILE
    nt_dims = (((1,), (1,)), ((), ()))
    cols = ATT_GROUP * tq
    ctx_slot = KV_SLOTS
    s_bufs = [s_ref.at[slot] for slot in range(KV_SLOTS)] + [sc_ref]

    qgs = [jnp.concatenate(
        [q_ref[0, :, (kv * ATT_GROUP + j) * LANES:(kv * ATT_GROUP + j + 1) * LANES] for j in range(ATT_GROUP)],
        axis=0) for kv in range(ATT_KV_HEADS)]

    def scores(kt, slot):
        for kv in range(ATT_KV_HEADS):
            s_bufs[slot][kv, 0:kt.shape[0]] = lax.dot_general(kt, qgs[kv], nt_dims,
                                                              preferred_element_type=F32)

    def consume(slot, vt_all, maxima):
        keys = vt_all.shape[1]
        src = lambda kv: s_bufs[slot][kv, 0:keys]
        ones = jnp.ones((ONES_ROWS, keys), BF16)
        out = []
        for kv in range(ATT_KV_HEADS):
            m = maxima[kv]
            m_new = jnp.maximum(m, jnp.max(src(kv), axis=0, keepdims=True))
            alpha = jnp.exp2(m - m_new)
            p = jnp.exp2(src(kv) - m_new).astype(BF16)
            vt = jnp.concatenate([vt_all[kv * ATT_HEAD_DIM:(kv + 1) * ATT_HEAD_DIM], ones], axis=0)
            acc_ref[kv] = alpha * acc_ref[kv] + jnp.dot(vt, p, preferred_element_type=F32)
            out.append(m_new)
        return tuple(out)

    def k_tile(j):
        return kx_ref[0, j * KV_TILE:(j + 1) * KV_TILE, :]

    def vt_tile(j):
        return vtx_ref[0, :, j * KV_TILE:(j + 1) * KV_TILE]

    acc_ref[...] = jnp.zeros(acc_ref.shape, F32)
    scores(kc_ref[0], ctx_slot)
    for j in range(min(KV_AHEAD, n_kv)):
        scores(k_tile(j), j % KV_SLOTS)
    maxima = consume(ctx_slot, vtc_ref[0], (jnp.full((1, cols), -jnp.inf, F32),) * ATT_KV_HEADS)
    for j in range(n_kv):
        if j + KV_AHEAD < n_kv:
            scores(k_tile(j + KV_AHEAD), (j + KV_AHEAD) % KV_SLOTS)
        maxima = consume(j % KV_SLOTS, vt_tile(j), maxima)
        yield

    for kv in range(ATT_KV_HEADS):
        o = acc_ref[kv, 0:ATT_HEAD_DIM] / acc_ref[kv, ATT_HEAD_DIM:ATT_HEAD_DIM + 1]
        for jj in range(ATT_GROUP // 2):
            pair = jnp.concatenate([o[:, (2 * jj) * tq:(2 * jj + 1) * tq],
                                    o[:, (2 * jj + 1) * tq:(2 * jj + 2) * tq]], axis=0)
            col = kv * (ATT_GROUP // 2) + jj
            o_ref[0, :, col * LANES:(col + 1) * LANES] = pair.T.astype(BF16)


NT_DIMS = (((1,), (1,)), ((), ()))


def _zero_rows(n):
    return [jnp.zeros((n, LANES), F32)] if n else []


def _scan_window(st, q, k, v, b, b_row, msk_ref, reverse, want_out):
    nsub = WIN // SUB
    last = 0 if reverse else SUB - 1
    sub_of = (lambda p: nsub - 1 - p) if reverse else (lambda p: p)
    blk = lambda a, i: a[i * SUB:(i + 1) * SUB]
    d = int(reverse)

    b_last = [b_row(i * SUB + last) for i in range(nsub)]
    cum = []
    for p in range(nsub):
        cum.append(b_last[sub_of(p)] if p == 0 else cum[-1] + b_last[sub_of(p)])
    ks = [blk(k, i) * jnp.exp2(b_last[i] - blk(b, i)) for i in range(nsub)]

    k_end = jnp.concatenate(
        [ks[i] if sub_of(i) == nsub - 1 else ks[i] * jnp.exp2(cum[nsub - 1] - cum[sub_of(i)]) for i in range(nsub)],
        axis=0)
    st_new = jnp.exp2(cum[nsub - 1]) * st + jnp.dot(v.T, k_end, preferred_element_type=F32)
    if not want_out:
        return st_new, None

    qs = [blk(q, i) * jnp.exp2(blk(b, i)) for i in range(nsub)]
    q_start = jnp.concatenate(
        [qs[i] if sub_of(i) == 0 else qs[i] * jnp.exp2(cum[sub_of(i) - 1]) for i in range(nsub)], axis=0)
    o_carried = lax.dot_general(q_start, st, NT_DIMS, preferred_element_type=F32)

    rows = [[] for _ in range(nsub)]
    for h in (nsub // 2, nsub // 4, nsub // 8):
        for ub in range(1, nsub // h, 2):
            ref = ub * h - 1
            upper = {sub_of(p): p for p in range(ub * h, (ub + 1) * h)}
            lower = {sub_of(p): p for p in range((ub - 1) * h, ub * h)}
            iu, il = sorted(upper), sorted(lower)
            lhs = jnp.concatenate(
                [qs[i] if upper[i] - 1 == ref else qs[i] * jnp.exp2(cum[upper[i] - 1] - cum[ref]) for i in iu], axis=0)
            rhs = jnp.concatenate(
                _zero_rows(il[0] * SUB)
                + [ks[i] if lower[i] == ref else ks[i] * jnp.exp2(cum[ref] - cum[lower[i]]) for i in il]
                + _zero_rows((nsub - 1 - il[-1]) * SUB), axis=0)
            c = lax.dot_general(lhs, rhs, NT_DIMS, preferred_element_type=F32)
            for n, i in enumerate(iu):
                rows[i].append(c[n * SUB:(n + 1) * SUB])

    def level_scores(level, q_exponent, k_exponent):
        q_l = q * jnp.exp2(jnp.minimum(q_exponent, 0.0))
        k_l = k if k_exponent is None else k * jnp.exp2(jnp.minimum(k_exponent, 0.0))
        return lax.dot_general(q_l, k_l, NT_DIMS, preferred_element_type=F32) * msk_ref[d, level]

    scores = jnp.dot(q * k, jnp.ones((LANES, LANES), BF16), preferred_element_type=F32) * msk_ref[d, N_FINE]
    for level, h in ((0, SUB // 2), (1, SUB // 4)):
        ref_rows = [b_row(s * 2 * h + (h if reverse else h - 1)) for s in range(WIN // (2 * h))]
        diff = b - jnp.concatenate([jnp.broadcast_to(r, (2 * h, LANES)) for r in ref_rows], axis=0)
        scores = scores + level_scores(level, diff, -diff)
    before = lambda n: pltpu.roll(b, n, 0)
    after = lambda n: pltpu.roll(b, WIN - n, 0)
    off4 = lax.broadcasted_iota(jnp.int32, (WIN, LANES), 0) % 4
    if reverse:
        q_exp = b - jnp.where(off4 == 0, after(2), after(1))
        k_exp = jnp.where(off4 == 2, 0.0, before(1) - b)
    else:
        q_exp = b - jnp.where(off4 == 3, before(2), before(1))
        k_exp = jnp.where(off4 == 1, 0.0, after(1) - b)
    scores = scores + level_scores(2, q_exp, k_exp)
    scores = scores + level_scores(3, b - (after(1) if reverse else before(1)), None)
    a_rows = []
    for i in range(nsub):
        a = blk(scores, i)
        for c in rows[i]:
            a = a + c
        a_rows.append(a)
    scores = jnp.concatenate(a_rows, axis=0)
    return st_new, (scores.astype(BF16), o_carried)


N_FINE = 4


def _pair_masks():
    msk = np.zeros((2, N_FINE + 1, WIN, WIN), np.float32)
    pos = np.arange(WIN)
    for d in range(2):
        for level in range(N_FINE):
            h = SUB >> (level + 1)
            span, off = pos // (2 * h), pos % (2 * h)
            later = (off < h) if d else (off >= h)
            msk[d, level] = (span[:, None] == span[None, :]) & later[:, None] & ~later[None, :]
        msk[d, N_FINE] = np.eye(WIN)
    return jnp.asarray(msk, F32)


def _hgrn_steps(q_ref, v_ref, kf_ref, bf_ref, kb_ref, bb_ref, vc_ref, kfc_ref, bfc_ref, kbc_ref, bbc_ref,
                msk_ref, o_ref, stf_ref, stb_ref, scf_ref, scb_ref, ocf_ref, ocb_ref, w0, n_steps):
    t = q_ref.shape[1]
    ctx = vc_ref.shape[1]
    nw = t // WIN

    def window(refs, start, st, reverse, want_out):
        qr, vr, kr, br = refs
        sl = pl.ds(start, WIN)
        vw = vr[0, sl, :].astype(F32)
        kw = kr[0, sl, :].astype(F32)
        bw = br[0, sl, :]
        qw = qr[0, sl, :].astype(F32) if want_out else None
        b_row = lambda r: br[0, pl.ds(start + r, 1), :]
        return _scan_window(st, qw, kw, vw, bw, b_row, msk_ref, reverse, want_out)

    def restart():
        zero = jnp.zeros((LANES, LANES), F32)
        stf, stb = zero, zero
        n_cw = ctx // WIN
        for w in range(n_cw):
            stf, _ = window((None, vc_ref, kfc_ref, bfc_ref), w * WIN, stf, False, False)
            stb, _ = window((None, vc_ref, kbc_ref, bbc_ref), (n_cw - 1 - w) * WIN, stb, True, False)
        stf_ref[...] = stf
        stb_ref[...] = stb
        o_ref[...] = jnp.zeros(o_ref.shape, F32)

    def starts(w):
        return pl.multiple_of(w * WIN, WIN), pl.multiple_of((nw - 1 - w) * WIN, WIN)

    def scores_stage(w):
        fs, bs = starts(w)
        stf, (sf, cf) = window((q_ref, v_ref, kf_ref, bf_ref), fs, stf_ref[...], False, True)
        stf_ref[...] = stf
        scf_ref[...] = sf
        ocf_ref[...] = cf
        stb, (sb, cb) = window((q_ref, v_ref, kb_ref, bb_ref), bs, stb_ref[...], True, True)
        stb_ref[...] = stb
        scb_ref[...] = sb
        ocb_ref[...] = cb

    def output_stage(w):
        fs, bs = starts(w)
        o_ref[0, pl.ds(fs, WIN), :] += (
            jnp.dot(scf_ref[...], v_ref[0, pl.ds(fs, WIN), :], preferred_element_type=F32) + ocf_ref[...])
        o_ref[0, pl.ds(bs, WIN), :] += (
            jnp.dot(scb_ref[...], v_ref[0, pl.ds(bs, WIN), :], preferred_element_type=F32) + ocb_ref[...])

    def schedule():
        for u in range(n_steps + 1):
            if u > 0:
                output_stage(w0 + u - 1)
            if u < n_steps:
                scores_stage(w0 + u)
            yield

    return restart, schedule()


def _mixer_kernel(qa_ref, kc_ref, vtc_ref, kx_ref, vtx_ref,
                  hq_ref, hv_ref, kf_ref, bf_ref, kb_ref, bb_ref, hvc_ref, kfc_ref, bfc_ref, kbc_ref, bbc_ref,
                  msk_ref, att_ref, hg_ref, sc_ref, s_ref, acc_ref, *hg_scratch, steps_per_head):
    nw = hq_ref.shape[1] // WIN
    n_steps = nw // steps_per_head
    phase = pl.program_id(1) % steps_per_head
    restart, scan = _hgrn_steps(hq_ref, hv_ref, kf_ref, bf_ref, kb_ref, bb_ref,
                                hvc_ref, kfc_ref, bfc_ref, kbc_ref, bbc_ref, msk_ref, hg_ref, *hg_scratch,
                                w0=phase * n_steps, n_steps=n_steps)
    pl.when(phase == 0)(restart)

    for _ in _attention_schedule(qa_ref, kc_ref, vtc_ref, kx_ref, vtx_ref, att_ref, sc_ref, s_ref, acc_ref):
        next(scan, None)
    for _ in scan:
        pass


def _mixers(qa, kc, vtc, kx, vtx, hq, hv, kf, bf, kb, bb, hvc, kfc, bfc, kbc, bbc):
    b, t, _ = qa.shape
    ctx = hvc.shape[1]
    tq = min(Q_TILE, t)
    n_q = t // tq
    nw = t // WIN
    assert tq % LANES == 0 and t % tq == 0 and t % KV_TILE == 0 and KV_AHEAD < KV_SLOTS
    assert n_q % HG_HEADS == 0 and nw % (n_q // HG_HEADS) == 0
    steps_per_head = n_q // HG_HEADS
    cols = ATT_GROUP * tq
    whole = lambda a: pl.BlockSpec((1,) + a.shape[1:], lambda i, j: (i, 0, 0))
    lat = pl.BlockSpec((1, t, HG_HEAD_DIM), lambda i, j: (i, 0, j // steps_per_head))
    con = pl.BlockSpec((1, ctx, HG_HEAD_DIM), lambda i, j: (i, 0, j // steps_per_head))
    msk = _pair_masks()
    const = lambda a: pl.BlockSpec(a.shape, lambda i, j: (0,) * a.ndim)
    return pl.pallas_call(
        functools.partial(_mixer_kernel, steps_per_head=steps_per_head),
        out_shape=[jax.ShapeDtypeStruct((b, t, ATT_WIDTH), BF16), jax.ShapeDtypeStruct((b, t, HG_WIDTH), F32)],
        grid=(b, n_q),
        in_specs=[pl.BlockSpec((1, tq, qa.shape[-1]), lambda i, j: (i, j, 0)),
                  whole(kc), whole(vtc), whole(kx), whole(vtx)] + [lat] * 6 + [con] * 5
                 + [const(msk)],
        out_specs=[pl.BlockSpec((1, tq, ATT_WIDTH), lambda i, j: (i, j, 0)), lat],
        scratch_shapes=[pltpu.VMEM((ATT_KV_HEADS, kc.shape[1], cols), F32),
                        pltpu.VMEM((KV_SLOTS, ATT_KV_HEADS, KV_TILE, cols), F32),
                        pltpu.VMEM((ATT_KV_HEADS, ATT_HEAD_DIM + ONES_ROWS, cols), F32),
                        pltpu.VMEM((LANES, LANES), F32), pltpu.VMEM((LANES, LANES), F32),
                        pltpu.VMEM((WIN, WIN), BF16), pltpu.VMEM((WIN, WIN), BF16),
                        pltpu.VMEM((WIN, LANES), F32), pltpu.VMEM((WIN, LANES), F32)],
        compiler_params=pltpu.CompilerParams(dimension_semantics=("arbitrary", "arbitrary"),
                                             vmem_limit_bytes=VMEM_LIMIT),
        name="mixers",
    )(qa, kc, vtc, kx, vtx, hq, hv, kf, bf, kb, bb, hvc, kfc, bfc, kbc, bbc, msk)


def _out_ffn_kernel(x_ref, att_ref, hg_ref, sg_ref, mod_ref, ghg_ref, wo_ref, g2_ref, wgu_ref, wd_ref, gf_ref,
                    o_ref):
    d = x_ref.shape[-1]
    dff = wd_ref.shape[0]
    gate1 = mod_ref[0, :, 2 * d:3 * d]
    shift2 = mod_ref[0, :, 3 * d:4 * d]
    scale2 = mod_ref[0, :, 4 * d:5 * d]
    gate2 = mod_ref[0, :, 5 * d:6 * d]

    parts = [slice(r0, r0 + OUT_PART_ROWS) for r0 in range(0, x_ref.shape[1], OUT_PART_ROWS)]

    def mixed(rows):
        mix = jnp.dot(att_ref[0, rows, :], wo_ref[0:ATT_WIDTH, :], preferred_element_type=F32)
        for hd in range(HG_HEADS):
            cs = slice(hd * HG_HEAD_DIM, (hd + 1) * HG_HEAD_DIM)
            o = _rms_rows(hg_ref[0, rows, cs]) * ghg_ref[...]
            o = (o * sg_ref[0, rows, cs].astype(F32)).astype(BF16)
            mix = mix + jnp.dot(o, wo_ref[ATT_WIDTH + hd * HG_HEAD_DIM:ATT_WIDTH + (hd + 1) * HG_HEAD_DIM, :],
                                preferred_element_type=F32)
        x1 = x_ref[0, rows, :] + gate1 * mix
        return x1, ((_rms_rows(x1) * g2_ref[...]) * (1.0 + scale2) + shift2).astype(BF16)

    def gate_up(h):
        return (jnp.dot(h, wgu_ref[:, 0:dff], preferred_element_type=F32),
                jnp.dot(h, wgu_ref[:, dff:2 * dff], preferred_element_type=F32))

    def down(a, g):
        return jnp.dot((_silu(a) * g).astype(BF16), wd_ref[...], preferred_element_type=F32)

    firsts = [mixed(rows) for rows in parts]
    ups = [gate_up(h) for _, h in firsts]
    ys = [down(a, g) for a, g in ups]
    for rows, (x1, _), y in zip(parts, firsts, ys):
        o_ref[0, rows, :] = _rms_rows(x1 + gate2 * y) * gf_ref[...]


def _out_ffn(x, att, hg, sg, mod3, ghg, w_out, g2, w_gu, w_down, gf):
    b, t, d = x.shape
    tm = min(OUT_ROW_TILE, t)
    assert tm % OUT_PART_ROWS == 0
    bspec = lambda width: pl.BlockSpec((1, tm, width), lambda i, j: (i, j, 0))
    full = lambda a: pl.BlockSpec(a.shape, lambda i, j: (0,) * a.ndim, pipeline_mode=pl.Buffered(1))
    return pl.pallas_call(
        _out_ffn_kernel,
        out_shape=jax.ShapeDtypeStruct((b, t, d), F32),
        grid=(b, t // tm),
        in_specs=[bspec(d), bspec(ATT_WIDTH), bspec(HG_WIDTH), bspec(HG_WIDTH),
                  pl.BlockSpec((1, 1, mod3.shape[-1]), lambda i, j: (i, 0, 0)),
                  full(ghg), full(w_out), full(g2), full(w_gu), full(w_down), full(gf)],
        out_specs=bspec(d),
        compiler_params=pltpu.CompilerParams(dimension_semantics=("parallel", "parallel"),
                                             vmem_limit_bytes=VMEM_LIMIT),
        name="out_ffn",
    )(x, att, hg, sg, mod3, ghg, w_out, g2, w_gu, w_down, gf)


def _rope_tables(t):
    n_rows = t // GRID_W
    row = np.repeat(np.arange(n_rows, dtype=np.float32), GRID_W)
    col = np.tile(np.arange(GRID_W, dtype=np.float32), n_rows)
    inv = (np.float32(ROPE_THETA) ** (-np.arange(0, AXIS_DIM, 2, dtype=np.float32) / np.float32(AXIS_DIM)))
    inv = inv.astype(np.float32)
    ar = row[:, None] * inv
    ac = col[:, None] * inv
    ang = np.concatenate([ar, ar, ac, ac], axis=-1).astype(np.float32)
    sign = np.where((np.arange(ATT_HEAD_DIM) % AXIS_DIM) < AXIS_DIM // 2, -1.0, 1.0)
    reps = LANES // ATT_HEAD_DIM
    cos = np.tile(np.cos(ang.astype(np.float64)), (1, reps)).astype(np.float32)
    sin = np.tile(np.sin(ang.astype(np.float64)) * sign, (1, reps)).astype(np.float32)
    return jnp.asarray(cos), jnp.asarray(sin)


def kernel(x, c, ctx, c_ctx, w_mod, b_mod, g_norm1, w_in, g_q, g_k, lb_raw, g_hg, w_out, g_norm2, w_gu, w_down,
           g_final):
    b, t, d = x.shape
    assert w_mod.shape[0] == 1 and lb_raw.shape[0] == 2, "single-layer block"
    assert t % WIN == 0 and (t // WIN) % 2 == 0 and ctx.shape[1] % WIN == 0 and t % KV_TILE == 0

    pad = (-(b + 1)) % 8
    cc = jnp.concatenate([c, c_ctx[None, :], jnp.zeros((pad, d), F32)], axis=0)
    mods = _modulation(cc, w_mod[0], b_mod)
    mod3 = mods.reshape(mods.shape[0], 1, mods.shape[1])

    cos, sin = _rope_tables(t)
    reps = LANES // ATT_HEAD_DIM
    gq = jnp.tile(g_q[0], reps)[None, :]
    gk = jnp.tile(g_k[0], reps)[None, :]
    w_in_b = w_in[0].astype(BF16)

    qa, kx, vtx, hq, hv, kf, bf, kb, bb, sg = _input_projection(
        x, mod3, None, g_norm1, w_in_b, gq, gk, lb_raw, cos, sin, True)
    kc, vtc, hvc, kfc, bfc, kbc, bbc = _input_projection(
        ctx, mod3, b, g_norm1, w_in_b, gq, gk, lb_raw, cos, sin, False)

    att, hg = _mixers(qa, kc, vtc, kx, vtx, hq, hv, kf, bf, kb, bb, hvc, kfc, bfc, kbc, bbc)
    return _out_ffn(x, att, hg, sg, mod3, g_hg, w_out[0].astype(BF16), g_norm2, w_gu[0].astype(BF16),
                    w_down[0].astype(BF16), g_final[None, :])
```

```python
import functools

import jax
import jax.numpy as jnp
import numpy as np
from jax import lax
from jax.experimental import pallas as pl
from jax.experimental.pallas import tpu as pltpu

F32 = jnp.float32
BF16 = jnp.bfloat16

LANES = 128
GRID_W = 64
ATT_HEADS = 8
ATT_KV_HEADS = 2
ATT_GROUP = ATT_HEADS // ATT_KV_HEADS
ATT_HEAD_DIM = 64
ATT_WIDTH = ATT_HEADS * ATT_HEAD_DIM
ATT_KV_WIDTH = ATT_KV_HEADS * ATT_HEAD_DIM
AXIS_DIM = ATT_HEAD_DIM // 2
ROPE_THETA = 10000.0
HG_HEADS = 4
HG_HEAD_DIM = 128
HG_WIDTH = HG_HEADS * HG_HEAD_DIM
EPS = 1e-6

SUB = 16
WIN = 128
ROW_TILE = 256
OUT_ROW_TILE = 512
OUT_PART_ROWS = 256
Q_TILE = 256
KV_TILE = 512
KV_SLOTS = 4
KV_AHEAD = 2
ONES_ROWS = 16
LOG2E = 1.4426950408889634
VMEM_LIMIT = 56 * 1024 * 1024

C_AQ = 0
C_AK = C_AQ + ATT_WIDTH
C_AV = C_AK + ATT_KV_WIDTH
C_HQ = C_AV + ATT_KV_WIDTH
C_HI = C_HQ + HG_WIDTH
C_FF = C_HI + HG_WIDTH
C_FB = C_FF + HG_WIDTH
C_HG = C_FB + HG_WIDTH
C_END = C_HG + HG_WIDTH


def _silu(v):
    return v * jax.nn.sigmoid(v)


def _rms_rows(v):
    return v * lax.rsqrt(jnp.mean(v * v, axis=-1, keepdims=True) + EPS)


def _mod_kernel(c_ref, w_ref, b_ref, o_ref):
    a = _silu(c_ref[...])
    o_ref[...] = jnp.dot(a, w_ref[...], precision=lax.Precision.HIGHEST,
                         preferred_element_type=F32) + b_ref[...]


def _modulation(cc, w_mod, b_mod):
    rows, d = cc.shape
    n = w_mod.shape[1]
    bn = n // 4
    return pl.pallas_call(
        _mod_kernel,
        out_shape=jax.ShapeDtypeStruct((rows, n), F32),
        grid=(n // bn,),
        in_specs=[pl.BlockSpec((rows, d), lambda j: (0, 0)),
                  pl.BlockSpec((d, bn), lambda j: (0, j)),
                  pl.BlockSpec((1, bn), lambda j: (0, j))],
        out_specs=pl.BlockSpec((rows, bn), lambda j: (0, j)),
        compiler_params=pltpu.CompilerParams(dimension_semantics=("arbitrary",),
                                             vmem_limit_bytes=VMEM_LIMIT),
        name="modulation",
    )(cc, w_mod, b_mod)


def _head64_meansq(p, seg2):
    sq = p * p
    hi = sq.astype(BF16)
    lo = (sq - hi.astype(F32)).astype(BF16)
    s = jnp.dot(jnp.concatenate([hi, lo], axis=1), seg2, preferred_element_type=F32)
    return s * (1.0 / ATT_HEAD_DIM)


def _rope(v, cos, sin_signed, first_half):
    fwd = pltpu.roll(v, LANES - AXIS_DIM // 2, 1)
    bwd = pltpu.roll(v, AXIS_DIM // 2, 1)
    return v * cos + jnp.where(first_half, fwd, bwd) * sin_signed


def _gate_prep(f, lb, suffix):
    fg = lb + (1.0 - lb) * jax.nn.sigmoid(f)
    cum = jnp.log2(fg)
    rows = cum.shape[0]
    pos = lax.broadcasted_iota(jnp.int32, cum.shape, 0) % SUB
    shift = 1
    while shift < SUB:
        if suffix:
            moved = pltpu.roll(cum, rows - shift, 0)
            cum = cum + jnp.where(pos < SUB - shift, moved, 0.0)
        else:
            moved = pltpu.roll(cum, shift, 0)
            cum = cum + jnp.where(pos >= shift, moved, 0.0)
        shift *= 2
    return (1.0 - fg).astype(BF16), cum


def _inproj_kernel(x_ref, mod_ref, g1_ref, w_ref, gq_ref, gk_ref, lb_ref, cos_ref, sin_ref, *out_refs, latent):
    d = x_ref.shape[-1]
    x = x_ref[0]
    shift = mod_ref[0, :, 0:d]
    scale = mod_ref[0, :, d:2 * d]
    h = (_rms_rows(x) * g1_ref[...]) * (1.0 + scale) + shift
    hb = h.astype(BF16)

    def proj(c0, width):
        return jnp.dot(hb, w_ref[:, c0:c0 + width], preferred_element_type=F32)

    lane = lax.broadcasted_iota(jnp.int32, (2 * LANES, LANES), 0) % LANES
    lane_c = lax.broadcasted_iota(jnp.int32, (2 * LANES, LANES), 1)
    seg = (lane // ATT_HEAD_DIM == lane_c // ATT_HEAD_DIM).astype(BF16)
    row_lane = lax.broadcasted_iota(jnp.int32, (x.shape[0], LANES), 1)
    first_half = (row_lane % AXIS_DIM) < (AXIS_DIM // 2)
    upper = row_lane >= ATT_HEAD_DIM

    r0 = lb_ref[0]
    r1 = lb_ref[1]
    rm = jnp.maximum(r0, r1)
    e0 = jnp.exp(r0 - rm)
    e1 = jnp.exp(r1 - rm)
    lb_all = e0 / (e0 + e1)

    if latent:
        (qa_ref, ka_ref, vat_ref, hq_ref, hv_ref, kf_ref, bf_ref, kb_ref, bb_ref, sg_ref) = out_refs
        cos = cos_ref[...]
        sin = sin_ref[...]
    else:
        (ka_ref, vat_ref, hv_ref, kf_ref, bf_ref, kb_ref, bb_ref) = out_refs

    p_ff = proj(C_FF, HG_WIDTH)
    if latent:
        pq = proj(C_AQ, ATT_WIDTH)
    pk = proj(C_AK, ATT_KV_WIDTH)
    pv = proj(C_AV, ATT_KV_WIDTH)
    p_fb = proj(C_FB, HG_WIDTH)
    if latent:
        p_hq = proj(C_HQ, HG_WIDTH)
    p_hv = proj(C_HI, HG_WIDTH)
    ms_k = _head64_meansq(pk, seg)
    if latent:
        ms_q = [_head64_meansq(pq[:, c * LANES:(c + 1) * LANES], seg) for c in range(ATT_WIDTH // LANES)]

    kn = pk * lax.rsqrt(ms_k + EPS) * gk_ref[...]
    if latent:
        kn = _rope(kn, cos, sin, first_half)
    ka_ref[0] = kn.astype(BF16)
    vat_ref[0] = pv.T.astype(BF16)

    if latent:
        for c in range(ATT_WIDTH // LANES):
            t = pq[:, c * LANES:(c + 1) * LANES]
            t = t * lax.rsqrt(ms_q[c] + EPS) * gq_ref[...]
            t = _rope(t, cos, sin, first_half) * (ATT_HEAD_DIM ** -0.5 * LOG2E)
            t_swapped = pltpu.roll(t, ATT_HEAD_DIM, 1)
            for half in range(2):
                head = 2 * c + half
                kv = head // ATT_GROUP
                src = t if half == kv else t_swapped
                keep = upper if kv == 1 else jnp.logical_not(upper)
                qa_ref[0, :, head * LANES:(head + 1) * LANES] = jnp.where(keep, src, 0.0).astype(BF16)
        hq_ref[0] = (_silu(p_hq) * (HG_HEAD_DIM ** -0.5)).astype(BF16)
        sg_ref[0] = _silu(proj(C_HG, HG_WIDTH)).astype(BF16)

    hv_ref[0] = p_hv.astype(BF16)
    kf, bf = _gate_prep(p_ff, lb_all[0:1, :], False)
    kf_ref[0] = kf
    bf_ref[0] = bf
    kb, bb = _gate_prep(p_fb, lb_all[1:2, :], True)
    kb_ref[0] = kb
    bb_ref[0] = bb


def _input_projection(rows, mod3, mod_row, g1, w_in, gq, gk, lb_raw, cos, sin, latent):
    b, t, d = rows.shape
    tm = min(ROW_TILE, t)
    nt = t // tm
    bspec = lambda width: pl.BlockSpec((1, tm, width), lambda i, j: (i, j, 0))
    full = lambda a: pl.BlockSpec(a.shape, lambda i, j: (0,) * a.ndim)
    sds = lambda width, dt: jax.ShapeDtypeStruct((b, t, width), dt)
    vt_sds = jax.ShapeDtypeStruct((b, ATT_KV_WIDTH, t), BF16)
    vt_spec = pl.BlockSpec((1, ATT_KV_WIDTH, tm), lambda i, j: (i, 0, j))
    if latent:
        out_shape = [sds(ATT_HEADS * LANES, BF16), sds(ATT_KV_WIDTH, BF16), vt_sds,
                     sds(HG_WIDTH, BF16), sds(HG_WIDTH, BF16), sds(HG_WIDTH, BF16), sds(HG_WIDTH, F32),
                     sds(HG_WIDTH, BF16), sds(HG_WIDTH, F32), sds(HG_WIDTH, BF16)]
        out_specs = [bspec(ATT_HEADS * LANES), bspec(ATT_KV_WIDTH), vt_spec, bspec(HG_WIDTH),
                     bspec(HG_WIDTH), bspec(HG_WIDTH), bspec(HG_WIDTH), bspec(HG_WIDTH), bspec(HG_WIDTH),
                     bspec(HG_WIDTH)]
        mod_spec = pl.BlockSpec((1, 1, mod3.shape[-1]), lambda i, j: (i, 0, 0))
    else:
        out_shape = [sds(ATT_KV_WIDTH, BF16), vt_sds, sds(HG_WIDTH, BF16), sds(HG_WIDTH, BF16),
                     sds(HG_WIDTH, F32), sds(HG_WIDTH, BF16), sds(HG_WIDTH, F32)]
        out_specs = [bspec(ATT_KV_WIDTH), vt_spec, bspec(HG_WIDTH), bspec(HG_WIDTH), bspec(HG_WIDTH),
                     bspec(HG_WIDTH), bspec(HG_WIDTH)]
        mod_spec = pl.BlockSpec((1, 1, mod3.shape[-1]), lambda i, j: (mod_row, 0, 0))
    table_spec = pl.BlockSpec((tm, LANES), lambda i, j: (j, 0))
    return pl.pallas_call(
        functools.partial(_inproj_kernel, latent=latent),
        out_shape=out_shape,
        grid=(b, nt),
        in_specs=[bspec(d), mod_spec, full(g1), full(w_in), full(gq), full(gk), full(lb_raw),
                  table_spec, table_spec],
        out_specs=out_specs,
        compiler_params=pltpu.CompilerParams(dimension_semantics=("parallel", "parallel"),
                                             vmem_limit_bytes=VMEM_LIMIT),
        name="inproj_latent" if latent else "inproj_context",
    )(rows, mod3, g1, w_in, gq, gk, lb_raw, cos, sin)


def _attention_schedule(q_ref, kc_ref, vtc_ref, kx_ref, vtx_ref, o_ref, sc_ref, s_ref, acc_ref):
    tq = q_ref.shape[1]
    n_kv = kx_ref.shape[1] // KV_TILE
    nt_dims = (((1,), (1,)), ((), ()))
    cols = ATT_GROUP * tq

    qgs = [jnp.concatenate(
        [q_ref[0, :, (kv * ATT_GROUP + j) * LANES:(kv * ATT_GROUP + j + 1) * LANES] for j in range(ATT_GROUP)],
        axis=0) for kv in range(ATT_KV_HEADS)]

    def scores(kt, dst):
        for kv in range(ATT_KV_HEADS):
            dst(kv, lax.dot_general(kt, qgs[kv], nt_dims, preferred_element_type=F32))

    def consume(src, vt_all, maxima):
        ones = jnp.ones((ONES_ROWS, vt_all.shape[1]), BF16)
        out = []
        for kv in range(ATT_KV_HEADS):
            m = maxima[kv]
            m_new = jnp.maximum(m, jnp.max(src(kv), axis=0, keepdims=True))
            alpha = jnp.exp2(m - m_new)
            p = jnp.exp2(src(kv) - m_new).astype(BF16)
            vt = jnp.concatenate([vt_all[kv * ATT_HEAD_DIM:(kv + 1) * ATT_HEAD_DIM], ones], axis=0)
            acc_ref[kv] = alpha * acc_ref[kv] + jnp.dot(vt, p, preferred_element_type=F32)
            out.append(m_new)
        return tuple(out)

    def put_ctx(kv, val):
        sc_ref[kv] = val

    def put(slot):
        def dst(kv, val):
            s_ref[slot, kv] = val
        return dst

    def k_tile(j):
        return kx_ref[0, j * KV_TILE:(j + 1) * KV_TILE, :]

    def vt_tile(j):
        return vtx_ref[0, :, j * KV_TILE:(j + 1) * KV_TILE]

    acc_ref[...] = jnp.zeros(acc_ref.shape, F32)
    scores(kc_ref[0], put_ctx)
    for j in range(min(KV_AHEAD, n_kv)):
        scores(k_tile(j), put(j % KV_SLOTS))
    maxima = consume(lambda kv: sc_ref[kv], vtc_ref[0], (jnp.full((1, cols), -jnp.inf, F32),) * ATT_KV_HEADS)
    for j in range(n_kv):
        if j + KV_AHEAD < n_kv:
            scores(k_tile(j + KV_AHEAD), put((j + KV_AHEAD) % KV_SLOTS))
        maxima = consume(lambda kv, slot=j % KV_SLOTS: s_ref[slot, kv], vt_tile(j), maxima)
        yield

    for kv in range(ATT_KV_HEADS):
        o = acc_ref[kv, 0:ATT_HEAD_DIM] / acc_ref[kv, ATT_HEAD_DIM:ATT_HEAD_DIM + 1]
        for jj in range(ATT_GROUP // 2):
            pair = jnp.concatenate([o[:, (2 * jj) * tq:(2 * jj + 1) * tq],
                                    o[:, (2 * jj + 1) * tq:(2 * jj + 2) * tq]], axis=0)
            col = kv * (ATT_GROUP // 2) + jj
            o_ref[0, :, col * LANES:(col + 1) * LANES] = pair.T.astype(BF16)


NT_DIMS = (((1,), (1,)), ((), ()))


def _zero_rows(n):
    return [jnp.zeros((n, LANES), F32)] if n else []


def _scan_window(st, q, k, v, b, b_row, msk_ref, reverse, want_out):
    nsub = WIN // SUB
    last = 0 if reverse else SUB - 1
    sub_of = (lambda p: nsub - 1 - p) if reverse else (lambda p: p)
    blk = lambda a, i: a[i * SUB:(i + 1) * SUB]
    d = int(reverse)

    b_last = [b_row(i * SUB + last) for i in range(nsub)]
    cum = []
    for p in range(nsub):
        cum.append(b_last[sub_of(p)] if p == 0 else cum[-1] + b_last[sub_of(p)])
    ks = [blk(k, i) * jnp.exp2(b_last[i] - blk(b, i)) for i in range(nsub)]

    k_end = jnp.concatenate(
        [ks[i] if sub_of(i) == nsub - 1 else ks[i] * jnp.exp2(cum[nsub - 1] - cum[sub_of(i)]) for i in range(nsub)],
        axis=0)
    st_new = jnp.exp2(cum[nsub - 1]) * st + jnp.dot(v.T, k_end, preferred_element_type=F32)
    if not want_out:
        return st_new, None

    qs = [blk(q, i) * jnp.exp2(blk(b, i)) for i in range(nsub)]
    q_start = jnp.concatenate(
        [qs[i] if sub_of(i) == 0 else qs[i] * jnp.exp2(cum[sub_of(i) - 1]) for i in range(nsub)], axis=0)
    o_carried = lax.dot_general(q_start, st, NT_DIMS, preferred_element_type=F32)

    rows = [[] for _ in range(nsub)]
    for h in (nsub // 2, nsub // 4, nsub // 8):
        for ub in range(1, nsub // h, 2):
            ref = ub * h - 1
            upper = {sub_of(p): p for p in range(ub * h, (ub + 1) * h)}
            lower = {sub_of(p): p for p in range((ub - 1) * h, ub * h)}
            iu, il = sorted(upper), sorted(lower)
            lhs = jnp.concatenate(
                [qs[i] if upper[i] - 1 == ref else qs[i] * jnp.exp2(cum[upper[i] - 1] - cum[ref]) for i in iu], axis=0)
            rhs = jnp.concatenate(
                _zero_rows(il[0] * SUB)
                + [ks[i] if lower[i] == ref else ks[i] * jnp.exp2(cum[ref] - cum[lower[i]]) for i in il]
                + _zero_rows((nsub - 1 - il[-1]) * SUB), axis=0)
            c = lax.dot_general(lhs, rhs, NT_DIMS, preferred_element_type=F32)
            for n, i in enumerate(iu):
                rows[i].append(c[n * SUB:(n + 1) * SUB])

    def level_scores(level, q_exponent, k_exponent):
        q_l = q * jnp.exp2(jnp.minimum(q_exponent, 0.0))
        k_l = k if k_exponent is None else k * jnp.exp2(jnp.minimum(k_exponent, 0.0))
        return lax.dot_general(q_l, k_l, NT_DIMS, preferred_element_type=F32) * msk_ref[d, level]

    scores = jnp.dot(q * k, jnp.ones((LANES, LANES), BF16), preferred_element_type=F32) * msk_ref[d, N_FINE]
    for level, h in ((0, SUB // 2), (1, SUB // 4)):
        ref_rows = [b_row(s * 2 * h + (h if reverse else h - 1)) for s in range(WIN // (2 * h))]
        diff = b - jnp.concatenate([jnp.broadcast_to(r, (2 * h, LANES)) for r in ref_rows], axis=0)
        scores = scores + level_scores(level, diff, -diff)
    before = lambda n: pltpu.roll(b, n, 0)
    after = lambda n: pltpu.roll(b, WIN - n, 0)
    off4 = lax.broadcasted_iota(jnp.int32, (WIN, LANES), 0) % 4
    if reverse:
        q_exp = b - jnp.where(off4 == 0, after(2), after(1))
        k_exp = jnp.where(off4 == 2, 0.0, before(1) - b)
    else:
        q_exp = b - jnp.where(off4 == 3, before(2), before(1))
        k_exp = jnp.where(off4 == 1, 0.0, after(1) - b)
    scores = scores + level_scores(2, q_exp, k_exp)
    scores = scores + level_scores(3, b - (after(1) if reverse else before(1)), None)
    a_rows = []
    for i in range(nsub):
        a = blk(scores, i)
        for c in rows[i]:
            a = a + c
        a_rows.append(a)
    scores = jnp.concatenate(a_rows, axis=0)
    return st_new, (scores.astype(BF16), o_carried)


N_FINE = 4


def _pair_masks():
    msk = np.zeros((2, N_FINE + 1, WIN, WIN), np.float32)
    pos = np.arange(WIN)
    for d in range(2):
        for level in range(N_FINE):
            h = SUB >> (level + 1)
            span, off = pos // (2 * h), pos % (2 * h)
            later = (off < h) if d else (off >= h)
            msk[d, level] = (span[:, None] == span[None, :]) & later[:, None] & ~later[None, :]
        msk[d, N_FINE] = np.eye(WIN)
    return jnp.asarray(msk, F32)


def _hgrn_steps(q_ref, v_ref, kf_ref, bf_ref, kb_ref, bb_ref, vc_ref, kfc_ref, bfc_ref, kbc_ref, bbc_ref,
                msk_ref, o_ref, stf_ref, stb_ref, scf_ref, scb_ref, ocf_ref, ocb_ref, w0, n_steps):
    t = q_ref.shape[1]
    ctx = vc_ref.shape[1]
    nw = t // WIN

    def window(refs, start, st, reverse, want_out):
        qr, vr, kr, br = refs
        sl = pl.ds(start, WIN)
        vw = vr[0, sl, :].astype(F32)
        kw = kr[0, sl, :].astype(F32)
        bw = br[0, sl, :]
        qw = qr[0, sl, :].astype(F32) if want_out else None
        b_row = lambda r: br[0, pl.ds(start + r, 1), :]
        return _scan_window(st, qw, kw, vw, bw, b_row, msk_ref, reverse, want_out)

    def restart():
        zero = jnp.zeros((LANES, LANES), F32)
        stf, stb = zero, zero
        n_cw = ctx // WIN
        for w in range(n_cw):
            stf, _ = window((None, vc_ref, kfc_ref, bfc_ref), w * WIN, stf, False, False)
            stb, _ = window((None, vc_ref, kbc_ref, bbc_ref), (n_cw - 1 - w) * WIN, stb, True, False)
        stf_ref[...] = stf
        stb_ref[...] = stb
        o_ref[...] = jnp.zeros(o_ref.shape, F32)

    def starts(w):
        return pl.multiple_of(w * WIN, WIN), pl.multiple_of((nw - 1 - w) * WIN, WIN)

    def scores_stage(w):
        fs, bs = starts(w)
        stf, (sf, cf) = window((q_ref, v_ref, kf_ref, bf_ref), fs, stf_ref[...], False, True)
        stf_ref[...] = stf
        scf_ref[...] = sf
        ocf_ref[...] = cf
        stb, (sb, cb) = window((q_ref, v_ref, kb_ref, bb_ref), bs, stb_ref[...], True, True)
        stb_ref[...] = stb
        scb_ref[...] = sb
        ocb_ref[...] = cb

    def output_stage(w):
        fs, bs = starts(w)
        o_ref[0, pl.ds(fs, WIN), :] += (
            jnp.dot(scf_ref[...], v_ref[0, pl.ds(fs, WIN), :], preferred_element_type=F32) + ocf_ref[...])
        o_ref[0, pl.ds(bs, WIN), :] += (
            jnp.dot(scb_ref[...], v_ref[0, pl.ds(bs, WIN), :], preferred_element_type=F32) + ocb_ref[...])

    def schedule():
        for u in range(n_steps + 1):
            if u > 0:
                output_stage(w0 + u - 1)
            if u < n_steps:
                scores_stage(w0 + u)
            yield

    return restart, schedule()


def _mixer_kernel(qa_ref, kc_ref, vtc_ref, kx_ref, vtx_ref,
                  hq_ref, hv_ref, kf_ref, bf_ref, kb_ref, bb_ref, hvc_ref, kfc_ref, bfc_ref, kbc_ref, bbc_ref,
                  msk_ref, att_ref, hg_ref, sc_ref, s_ref, acc_ref, *hg_scratch, steps_per_head):
    nw = hq_ref.shape[1] // WIN
    n_steps = nw // steps_per_head
    phase = pl.program_id(1) % steps_per_head
    restart, scan = _hgrn_steps(hq_ref, hv_ref, kf_ref, bf_ref, kb_ref, bb_ref,
                                hvc_ref, kfc_ref, bfc_ref, kbc_ref, bbc_ref, msk_ref, hg_ref, *hg_scratch,
                                w0=phase * n_steps, n_steps=n_steps)
    pl.when(phase == 0)(restart)

    for _ in _attention_schedule(qa_ref, kc_ref, vtc_ref, kx_ref, vtx_ref, att_ref, sc_ref, s_ref, acc_ref):
        next(scan, None)
    for _ in scan:
        pass


def _mixers(qa, kc, vtc, kx, vtx, hq, hv, kf, bf, kb, bb, hvc, kfc, bfc, kbc, bbc):
    b, t, _ = qa.shape
    ctx = hvc.shape[1]
    tq = min(Q_TILE, t)
    n_q = t // tq
    nw = t // WIN
    assert tq % LANES == 0 and t % tq == 0 and t % KV_TILE == 0 and KV_AHEAD < KV_SLOTS
    assert n_q % HG_HEADS == 0 and nw % (n_q // HG_HEADS) == 0
    steps_per_head = n_q // HG_HEADS
    cols = ATT_GROUP * tq
    whole = lambda a: pl.BlockSpec((1,) + a.shape[1:], lambda i, j: (i, 0, 0))
    lat = pl.BlockSpec((1, t, HG_HEAD_DIM), lambda i, j: (i, 0, j // steps_per_head))
    con = pl.BlockSpec((1, ctx, HG_HEAD_DIM), lambda i, j: (i, 0, j // steps_per_head))
    msk = _pair_masks()
    const = lambda a: pl.BlockSpec(a.shape, lambda i, j: (0,) * a.ndim)
    return pl.pallas_call(
        functools.partial(_mixer_kernel, steps_per_head=steps_per_head),
        out_shape=[jax.ShapeDtypeStruct((b, t, ATT_WIDTH), BF16), jax.ShapeDtypeStruct((b, t, HG_WIDTH), F32)],
        grid=(b, n_q),
        in_specs=[pl.BlockSpec((1, tq, qa.shape[-1]), lambda i, j: (i, j, 0)),
                  whole(kc), whole(vtc), whole(kx), whole(vtx)] + [lat] * 6 + [con] * 5
                 + [const(msk)],
        out_specs=[pl.BlockSpec((1, tq, ATT_WIDTH), lambda i, j: (i, j, 0)), lat],
        scratch_shapes=[pltpu.VMEM((ATT_KV_HEADS, kc.shape[1], cols), F32),
                        pltpu.VMEM((KV_SLOTS, ATT_KV_HEADS, KV_TILE, cols), F32),
                        pltpu.VMEM((ATT_KV_HEADS, ATT_HEAD_DIM + ONES_ROWS, cols), F32),
                        pltpu.VMEM((LANES, LANES), F32), pltpu.VMEM((LANES, LANES), F32),
                        pltpu.VMEM((WIN, WIN), BF16), pltpu.VMEM((WIN, WIN), BF16),
                        pltpu.VMEM((WIN, LANES), F32), pltpu.VMEM((WIN, LANES), F32)],
        compiler_params=pltpu.CompilerParams(dimension_semantics=("arbitrary", "arbitrary"),
                                             vmem_limit_bytes=VMEM_LIMIT),
        name="mixers",
    )(qa, kc, vtc, kx, vtx, hq, hv, kf, bf, kb, bb, hvc, kfc, bfc, kbc, bbc, msk)


def _out_ffn_kernel(x_ref, att_ref, hg_ref, sg_ref, mod_ref, ghg_ref, wo_ref, g2_ref, wgu_ref, wd_ref, gf_ref,
                    o_ref):
    d = x_ref.shape[-1]
    dff = wd_ref.shape[0]
    gate1 = mod_ref[0, :, 2 * d:3 * d]
    shift2 = mod_ref[0, :, 3 * d:4 * d]
    scale2 = mod_ref[0, :, 4 * d:5 * d]
    gate2 = mod_ref[0, :, 5 * d:6 * d]

    parts = [slice(r0, r0 + OUT_PART_ROWS) for r0 in range(0, x_ref.shape[1], OUT_PART_ROWS)]

    def mixed(rows):
        mix = jnp.dot(att_ref[0, rows, :], wo_ref[0:ATT_WIDTH, :], preferred_element_type=F32)
        for hd in range(HG_HEADS):
            cs = slice(hd * HG_HEAD_DIM, (hd + 1) * HG_HEAD_DIM)
            o = _rms_rows(hg_ref[0, rows, cs]) * ghg_ref[...]
            o = (o * sg_ref[0, rows, cs].astype(F32)).astype(BF16)
            mix = mix + jnp.dot(o, wo_ref[ATT_WIDTH + hd * HG_HEAD_DIM:ATT_WIDTH + (hd + 1) * HG_HEAD_DIM, :],
                                preferred_element_type=F32)
        x1 = x_ref[0, rows, :] + gate1 * mix
        return x1, ((_rms_rows(x1) * g2_ref[...]) * (1.0 + scale2) + shift2).astype(BF16)

    def gate_up(h):
        return (jnp.dot(h, wgu_ref[:, 0:dff], preferred_element_type=F32),
                jnp.dot(h, wgu_ref[:, dff:2 * dff], preferred_element_type=F32))

    def down(a, g):
        return jnp.dot((_silu(a) * g).astype(BF16), wd_ref[...], preferred_element_type=F32)

    firsts = [mixed(rows) for rows in parts]
    ups = [gate_up(h) for _, h in firsts]
    ys = [down(a, g) for a, g in ups]
    for rows, (x1, _), y in zip(parts, firsts, ys):
        o_ref[0, rows, :] = _rms_rows(x1 + gate2 * y) * gf_ref[...]


def _out_ffn(x, att, hg, sg, mod3, ghg, w_out, g2, w_gu, w_down, gf):
    b, t, d = x.shape
    tm = min(OUT_ROW_TILE, t)
    assert tm % OUT_PART_ROWS == 0
    bspec = lambda width: pl.BlockSpec((1, tm, width), lambda i, j: (i, j, 0))
    full = lambda a: pl.BlockSpec(a.shape, lambda i, j: (0,) * a.ndim, pipeline_mode=pl.Buffered(1))
    return pl.pallas_call(
        _out_ffn_kernel,
        out_shape=jax.ShapeDtypeStruct((b, t, d), F32),
        grid=(b, t // tm),
        in_specs=[bspec(d), bspec(ATT_WIDTH), bspec(HG_WIDTH), bspec(HG_WIDTH),
                  pl.BlockSpec((1, 1, mod3.shape[-1]), lambda i, j: (i, 0, 0)),
                  full(ghg), full(w_out), full(g2), full(w_gu), full(w_down), full(gf)],
        out_specs=bspec(d),
        compiler_params=pltpu.CompilerParams(dimension_semantics=("parallel", "parallel"),
                                             vmem_limit_bytes=VMEM_LIMIT),
        name="out_ffn",
    )(x, att, hg, sg, mod3, ghg, w_out, g2, w_gu, w_down, gf)


def _rope_tables(t):
    n_rows = t // GRID_W
    row = np.repeat(np.arange(n_rows, dtype=np.float32), GRID_W)
    col = np.tile(np.arange(GRID_W, dtype=np.float32), n_rows)
    inv = (np.float32(ROPE_THETA) ** (-np.arange(0, AXIS_DIM, 2, dtype=np.float32) / np.float32(AXIS_DIM)))
    inv = inv.astype(np.float32)
    ar = row[:, None] * inv
    ac = col[:, None] * inv
    ang = np.concatenate([ar, ar, ac, ac], axis=-1).astype(np.float32)
    sign = np.where((np.arange(ATT_HEAD_DIM) % AXIS_DIM) < AXIS_DIM // 2, -1.0, 1.0)
    reps = LANES // ATT_HEAD_DIM
    cos = np.tile(np.cos(ang.astype(np.float64)), (1, reps)).astype(np.float32)
    sin = np.tile(np.sin(ang.astype(np.float64)) * sign, (1, reps)).astype(np.float32)
    return jnp.asarray(cos), jnp.asarray(sin)


def kernel(x, c, ctx, c_ctx, w_mod, b_mod, g_norm1, w_in, g_q, g_k, lb_raw, g_hg, w_out, g_norm2, w_gu, w_down,
           g_final):
    b, t, d = x.shape
    assert w_mod.shape[0] == 1 and lb_raw.shape[0] == 2, "single-layer block"
    assert t % WIN == 0 and (t // WIN) % 2 == 0 and ctx.shape[1] % WIN == 0 and t % KV_TILE == 0

    pad = (-(b + 1)) % 8
    cc = jnp.concatenate([c, c_ctx[None, :], jnp.zeros((pad, d), F32)], axis=0)
    mods = _modulation(cc, w_mod[0], b_mod)
    mod3 = mods.reshape(mods.shape[0], 1, mods.shape[1])

    cos, sin = _rope_tables(t)
    reps = LANES // ATT_HEAD_DIM
    gq = jnp.tile(g_q[0], reps)[None, :]
    gk = jnp.tile(g_k[0], reps)[None, :]
    w_in_b = w_in[0].astype(BF16)

    qa, kx, vtx, hq, hv, kf, bf, kb, bb, sg = _input_projection(
        x, mod3, None, g_norm1, w_in_b, gq, gk, lb_raw, cos, sin, True)
    kc, vtc, hvc, kfc, bfc, kbc, bbc = _input_projection(
        ctx, mod3, b, g_norm1, w_in_b, gq, gk, lb_raw, cos, sin, False)

    att, hg = _mixers(qa, kc, vtc, kx, vtx, hq, hv, kf, bf, kb, bb, hvc, kfc, bfc, kbc, bbc)
    return _out_ffn(x, att, hg, sg, mod3, g_hg, w_out[0].astype(BF16), g_norm2, w_gu[0].astype(BF16),
                    w_down[0].astype(BF16), g_final[None, :])
```

```python
import functools

import jax
import jax.numpy as jnp
import numpy as np
from jax import lax
from jax.experimental import pallas as pl
from jax.experimental.pallas import tpu as pltpu

F32 = jnp.float32
BF16 = jnp.bfloat16

LANES = 128
GRID_W = 64
ATT_HEADS = 8
ATT_KV_HEADS = 2
ATT_GROUP = ATT_HEADS // ATT_KV_HEADS
ATT_HEAD_DIM = 64
ATT_WIDTH = ATT_HEADS * ATT_HEAD_DIM
ATT_KV_WIDTH = ATT_KV_HEADS * ATT_HEAD_DIM
AXIS_DIM = ATT_HEAD_DIM // 2
ROPE_THETA = 10000.0
HG_HEADS = 4
HG_HEAD_DIM = 128
HG_WIDTH = HG_HEADS * HG_HEAD_DIM
EPS = 1e-6

MOD_COLUMN_BLOCKS = 4
SUB = 16
WIN = 128
ROW_TILE = 256
OUT_ROW_TILE = 512
OUT_PART_ROWS = 256
Q_TILE = 256
KV_TILE = 512
KV_SLOTS = 4
KV_AHEAD = 2
ONES_ROWS = 16
LOG2E = 1.4426950408889634
VMEM_LIMIT = 56 * 1024 * 1024

C_AQ = 0
C_AK = C_AQ + ATT_WIDTH
C_AV = C_AK + ATT_KV_WIDTH
C_HQ = C_AV + ATT_KV_WIDTH
C_HI = C_HQ + HG_WIDTH
C_FF = C_HI + HG_WIDTH
C_FB = C_FF + HG_WIDTH
C_HG = C_FB + HG_WIDTH


def _silu(v):
    return v * jax.nn.sigmoid(v)


def _rms_rows(v):
    return v * lax.rsqrt(jnp.mean(v * v, axis=-1, keepdims=True) + EPS)


def _mod_kernel(c_ref, w_ref, b_ref, o_ref):
    a = _silu(c_ref[...])
    o_ref[...] = jnp.dot(a, w_ref[...], precision=lax.Precision.HIGHEST,
                         preferred_element_type=F32) + b_ref[...]


def _modulation(cc, w_mod, b_mod):
    rows, d = cc.shape
    n = w_mod.shape[1]
    bn = n // MOD_COLUMN_BLOCKS
    return pl.pallas_call(
        _mod_kernel,
        out_shape=jax.ShapeDtypeStruct((rows, n), F32),
        grid=(n // bn,),
        in_specs=[pl.BlockSpec((rows, d), lambda j: (0, 0)),
                  pl.BlockSpec((d, bn), lambda j: (0, j)),
                  pl.BlockSpec((1, bn), lambda j: (0, j))],
        out_specs=pl.BlockSpec((rows, bn), lambda j: (0, j)),
        compiler_params=pltpu.CompilerParams(dimension_semantics=("arbitrary",),
                                             vmem_limit_bytes=VMEM_LIMIT),
        name="modulation",
    )(cc, w_mod, b_mod)


def _head64_meansq(p, seg2):
    sq = p * p
    hi = sq.astype(BF16)
    lo = (sq - hi.astype(F32)).astype(BF16)
    s = jnp.dot(jnp.concatenate([hi, lo], axis=1), seg2, preferred_element_type=F32)
    return s * (1.0 / ATT_HEAD_DIM)


def _rope(v, cos, sin_signed, first_half):
    fwd = pltpu.roll(v, LANES - AXIS_DIM // 2, 1)
    bwd = pltpu.roll(v, AXIS_DIM // 2, 1)
    return v * cos + jnp.where(first_half, fwd, bwd) * sin_signed


def _gate_prep(f, lb, suffix):
    fg = lb + (1.0 - lb) * jax.nn.sigmoid(f)
    cum = jnp.log2(fg)
    rows = cum.shape[0]
    pos = lax.broadcasted_iota(jnp.int32, cum.shape, 0) % SUB
    shift = 1
    while shift < SUB:
        if suffix:
            moved = pltpu.roll(cum, rows - shift, 0)
            cum = cum + jnp.where(pos < SUB - shift, moved, 0.0)
        else:
            moved = pltpu.roll(cum, shift, 0)
            cum = cum + jnp.where(pos >= shift, moved, 0.0)
        shift *= 2
    return (1.0 - fg).astype(BF16), cum


def _inproj_kernel(x_ref, mod_ref, g1_ref, w_ref, gq_ref, gk_ref, lb_ref, cos_ref, sin_ref, *out_refs, latent):
    d = x_ref.shape[-1]
    x = x_ref[0]
    shift = mod_ref[0, :, 0:d]
    scale = mod_ref[0, :, d:2 * d]
    h = (_rms_rows(x) * g1_ref[...]) * (1.0 + scale) + shift
    hb = h.astype(BF16)

    def proj(c0, width):
        return jnp.dot(hb, w_ref[:, c0:c0 + width], preferred_element_type=F32)

    lane = lax.broadcasted_iota(jnp.int32, (2 * LANES, LANES), 0) % LANES
    lane_c = lax.broadcasted_iota(jnp.int32, (2 * LANES, LANES), 1)
    seg = (lane // ATT_HEAD_DIM == lane_c // ATT_HEAD_DIM).astype(BF16)
    row_lane = lax.broadcasted_iota(jnp.int32, (x.shape[0], LANES), 1)
    first_half = (row_lane % AXIS_DIM) < (AXIS_DIM // 2)
    upper = row_lane >= ATT_HEAD_DIM

    r0 = lb_ref[0]
    r1 = lb_ref[1]
    rm = jnp.maximum(r0, r1)
    e0 = jnp.exp(r0 - rm)
    e1 = jnp.exp(r1 - rm)
    lb_all = e0 / (e0 + e1)

    if latent:
        (qa_ref, ka_ref, vat_ref, hq_ref, hv_ref, kf_ref, bf_ref, kb_ref, bb_ref, sg_ref) = out_refs
        cos = cos_ref[...]
        sin = sin_ref[...]
    else:
        (ka_ref, vat_ref, hv_ref, kf_ref, bf_ref, kb_ref, bb_ref) = out_refs

    p_ff = proj(C_FF, HG_WIDTH)
    if latent:
        pq = proj(C_AQ, ATT_WIDTH)
    pk = proj(C_AK, ATT_KV_WIDTH)
    pv = proj(C_AV, ATT_KV_WIDTH)
    p_fb = proj(C_FB, HG_WIDTH)
    if latent:
        p_hq = proj(C_HQ, HG_WIDTH)
    p_hv = proj(C_HI, HG_WIDTH)
    ms_k = _head64_meansq(pk, seg)
    if latent:
        ms_q = [_head64_meansq(pq[:, c * LANES:(c + 1) * LANES], seg) for c in range(ATT_WIDTH // LANES)]

    kn = pk * lax.rsqrt(ms_k + EPS) * gk_ref[...]
    if latent:
        kn = _rope(kn, cos, sin, first_half)
    ka_ref[0] = kn.astype(BF16)
    vat_ref[0] = pv.T.astype(BF16)

    if latent:
        for c in range(ATT_WIDTH // LANES):
            t = pq[:, c * LANES:(c + 1) * LANES]
            t = t * lax.rsqrt(ms_q[c] + EPS) * gq_ref[...]
            t = _rope(t, cos, sin, first_half) * (ATT_HEAD_DIM ** -0.5 * LOG2E)
            t_swapped = pltpu.roll(t, ATT_HEAD_DIM, 1)
            for half in range(2):
                head = 2 * c + half
                kv = head // ATT_GROUP
                src = t if half == kv else t_swapped
                keep = upper if kv == 1 else jnp.logical_not(upper)
                qa_ref[0, :, head * LANES:(head + 1) * LANES] = jnp.where(keep, src, 0.0).astype(BF16)
        hq_ref[0] = (_silu(p_hq) * (HG_HEAD_DIM ** -0.5)).astype(BF16)
        sg_ref[0] = _silu(proj(C_HG, HG_WIDTH)).astype(BF16)

    hv_ref[0] = p_hv.astype(BF16)
    kf, bf = _gate_prep(p_ff, lb_all[0:1, :], False)
    kf_ref[0] = kf
    bf_ref[0] = bf
    kb, bb = _gate_prep(p_fb, lb_all[1:2, :], True)
    kb_ref[0] = kb
    bb_ref[0] = bb


def _input_projection(rows, mod3, mod_row, g1, w_in, gq, gk, lb_raw, cos, sin, latent):
    b, t, d = rows.shape
    tm = min(ROW_TILE, t)
    nt = t // tm
    bspec = lambda width: pl.BlockSpec((1, tm, width), lambda i, j: (i, j, 0))
    full = lambda a: pl.BlockSpec(a.shape, lambda i, j: (0,) * a.ndim)
    sds = lambda width, dt: jax.ShapeDtypeStruct((b, t, width), dt)
    vt_sds = jax.ShapeDtypeStruct((b, ATT_KV_WIDTH, t), BF16)
    vt_spec = pl.BlockSpec((1, ATT_KV_WIDTH, tm), lambda i, j: (i, 0, j))
    if latent:
        out_shape = [sds(ATT_HEADS * LANES, BF16), sds(ATT_KV_WIDTH, BF16), vt_sds,
                     sds(HG_WIDTH, BF16), sds(HG_WIDTH, BF16), sds(HG_WIDTH, BF16), sds(HG_WIDTH, F32),
                     sds(HG_WIDTH, BF16), sds(HG_WIDTH, F32), sds(HG_WIDTH, BF16)]
        out_specs = [bspec(ATT_HEADS * LANES), bspec(ATT_KV_WIDTH), vt_spec, bspec(HG_WIDTH),
                     bspec(HG_WIDTH), bspec(HG_WIDTH), bspec(HG_WIDTH), bspec(HG_WIDTH), bspec(HG_WIDTH),
                     bspec(HG_WIDTH)]
        mod_spec = pl.BlockSpec((1, 1, mod3.shape[-1]), lambda i, j: (i, 0, 0))
    else:
        out_shape = [sds(ATT_KV_WIDTH, BF16), vt_sds, sds(HG_WIDTH, BF16), sds(HG_WIDTH, BF16),
                     sds(HG_WIDTH, F32), sds(HG_WIDTH, BF16), sds(HG_WIDTH, F32)]
        out_specs = [bspec(ATT_KV_WIDTH), vt_spec, bspec(HG_WIDTH), bspec(HG_WIDTH), bspec(HG_WIDTH),
                     bspec(HG_WIDTH), bspec(HG_WIDTH)]
        mod_spec = pl.BlockSpec((1, 1, mod3.shape[-1]), lambda i, j: (mod_row, 0, 0))
    table_spec = pl.BlockSpec((tm, LANES), lambda i, j: (j, 0))
    return pl.pallas_call(
        functools.partial(_inproj_kernel, latent=latent),
        out_shape=out_shape,
        grid=(b, nt),
        in_specs=[bspec(d), mod_spec, full(g1), full(w_in), full(gq), full(gk), full(lb_raw),
                  table_spec, table_spec],
        out_specs=out_specs,
        compiler_params=pltpu.CompilerParams(dimension_semantics=("parallel", "parallel"),
                                             vmem_limit_bytes=VMEM_LIMIT),
        name="inproj_latent" if latent else "inproj_context",
    )(rows, mod3, g1, w_in, gq, gk, lb_raw, cos, sin)


def _attention_schedule(q_ref, kc_ref, vtc_ref, kx_ref, vtx_ref, o_ref, sc_ref, s_ref, acc_ref):
    tq = q_ref.shape[1]
    n_kv = kx_ref.shape[1] // KV_TILE
    nt_dims = (((1,), (1,)), ((), ()))
    cols = ATT_GROUP * tq

    qgs = [jnp.concatenate(
        [q_ref[0, :, (kv * ATT_GROUP + j) * LANES:(kv * ATT_GROUP + j + 1) * LANES] for j in range(ATT_GROUP)],
        axis=0) for kv in range(ATT_KV_HEADS)]

    def scores(kt, dst):
        for kv in range(ATT_KV_HEADS):
            dst(kv, lax.dot_general(kt, qgs[kv], nt_dims, preferred_element_type=F32))

    def consume(src, vt_all, maxima):
        ones = jnp.ones((ONES_ROWS, vt_all.shape[1]), BF16)
        out = []
        for kv in range(ATT_KV_HEADS):
            m = maxima[kv]
            m_new = jnp.maximum(m, jnp.max(src(kv), axis=0, keepdims=True))
            alpha = jnp.exp2(m - m_new)
            p = jnp.exp2(src(kv) - m_new).astype(BF16)
            vt = jnp.concatenate([vt_all[kv * ATT_HEAD_DIM:(kv + 1) * ATT_HEAD_DIM], ones], axis=0)
            acc_ref[kv] = alpha * acc_ref[kv] + jnp.dot(vt, p, preferred_element_type=F32)
            out.append(m_new)
        return tuple(out)

    def put_ctx(kv, val):
        sc_ref[kv] = val

    def put(slot):
        def dst(kv, val):
            s_ref[slot, kv] = val
        return dst

    def k_tile(j):
        return kx_ref[0, j * KV_TILE:(j + 1) * KV_TILE, :]

    def vt_tile(j):
        return vtx_ref[0, :, j * KV_TILE:(j + 1) * KV_TILE]

    acc_ref[...] = jnp.zeros(acc_ref.shape, F32)
    scores(kc_ref[0], put_ctx)
    for j in range(min(KV_AHEAD, n_kv)):
        scores(k_tile(j), put(j % KV_SLOTS))
    maxima = consume(lambda kv: sc_ref[kv], vtc_ref[0], (jnp.full((1, cols), -jnp.inf, F32),) * ATT_KV_HEADS)
    for j in range(n_kv):
        if j + KV_AHEAD < n_kv:
            scores(k_tile(j + KV_AHEAD), put((j + KV_AHEAD) % KV_SLOTS))
        maxima = consume(lambda kv, slot=j % KV_SLOTS: s_ref[slot, kv], vt_tile(j), maxima)
        yield

    for kv in range(ATT_KV_HEADS):
        o = acc_ref[kv, 0:ATT_HEAD_DIM] / acc_ref[kv, ATT_HEAD_DIM:ATT_HEAD_DIM + 1]
        for jj in range(ATT_GROUP // 2):
            pair = jnp.concatenate([o[:, (2 * jj) * tq:(2 * jj + 1) * tq],
                                    o[:, (2 * jj + 1) * tq:(2 * jj + 2) * tq]], axis=0)
            col = kv * (ATT_GROUP // 2) + jj
            o_ref[0, :, col * LANES:(col + 1) * LANES] = pair.T.astype(BF16)


NT_DIMS = (((1,), (1,)), ((), ()))


def _zero_rows(n):
    return [jnp.zeros((n, LANES), F32)] if n else []


def _scan_window(st, q, k, v, b, b_row, msk_ref, reverse, want_out):
    nsub = WIN // SUB
    last = 0 if reverse else SUB - 1
    sub_of = (lambda p: nsub - 1 - p) if reverse else (lambda p: p)
    blk = lambda a, i: a[i * SUB:(i + 1) * SUB]
    d = int(reverse)

    b_last = [b_row(i * SUB + last) for i in range(nsub)]
    cum = []
    for p in range(nsub):
        cum.append(b_last[sub_of(p)] if p == 0 else cum[-1] + b_last[sub_of(p)])
    ks = [blk(k, i) * jnp.exp2(b_last[i] - blk(b, i)) for i in range(nsub)]

    k_end = jnp.concatenate(
        [ks[i] if sub_of(i) == nsub - 1 else ks[i] * jnp.exp2(cum[nsub - 1] - cum[sub_of(i)]) for i in range(nsub)],
        axis=0)
    st_new = jnp.exp2(cum[nsub - 1]) * st + jnp.dot(v.T, k_end, preferred_element_type=F32)
    if not want_out:
        return st_new, None

    qs = [blk(q, i) * jnp.exp2(blk(b, i)) for i in range(nsub)]
    q_start = jnp.concatenate(
        [qs[i] if sub_of(i) == 0 else qs[i] * jnp.exp2(cum[sub_of(i) - 1]) for i in range(nsub)], axis=0)
    o_carried = lax.dot_general(q_start, st, NT_DIMS, preferred_element_type=F32)

    rows = [[] for _ in range(nsub)]
    for h in (nsub // 2, nsub // 4, nsub // 8):
        for ub in range(1, nsub // h, 2):
            ref = ub * h - 1
            upper = {sub_of(p): p for p in range(ub * h, (ub + 1) * h)}
            lower = {sub_of(p): p for p in range((ub - 1) * h, ub * h)}
            iu, il = sorted(upper), sorted(lower)
            lhs = jnp.concatenate(
                [qs[i] if upper[i] - 1 == ref else qs[i] * jnp.exp2(cum[upper[i] - 1] - cum[ref]) for i in iu], axis=0)
            rhs = jnp.concatenate(
                _zero_rows(il[0] * SUB)
                + [ks[i] if lower[i] == ref else ks[i] * jnp.exp2(cum[ref] - cum[lower[i]]) for i in il]
                + _zero_rows((nsub - 1 - il[-1]) * SUB), axis=0)
            c = lax.dot_general(lhs, rhs, NT_DIMS, preferred_element_type=F32)
            for n, i in enumerate(iu):
                rows[i].append(c[n * SUB:(n + 1) * SUB])

    def level_scores(level, q_exponent, k_exponent):
        q_l = q * jnp.exp2(jnp.minimum(q_exponent, 0.0))
        k_l = k if k_exponent is None else k * jnp.exp2(jnp.minimum(k_exponent, 0.0))
        return lax.dot_general(q_l, k_l, NT_DIMS, preferred_element_type=F32) * msk_ref[d, level]

    scores = jnp.dot(q * k, jnp.ones((LANES, LANES), BF16), preferred_element_type=F32) * msk_ref[d, N_FINE]
    for level, h in ((0, SUB // 2), (1, SUB // 4)):
        ref_rows = [b_row(s * 2 * h + (h if reverse else h - 1)) for s in range(WIN // (2 * h))]
        diff = b - jnp.concatenate([jnp.broadcast_to(r, (2 * h, LANES)) for r in ref_rows], axis=0)
        decay = jnp.exp2(jnp.minimum(diff, -diff))
        scores = scores + (lax.dot_general(q * decay, k * decay, NT_DIMS, preferred_element_type=F32)
                           * msk_ref[d, level])
    before = lambda n: pltpu.roll(b, n, 0)
    after = lambda n: pltpu.roll(b, WIN - n, 0)
    off4 = lax.broadcasted_iota(jnp.int32, (WIN, LANES), 0) % 4
    if reverse:
        q_exp = b - jnp.where(off4 == 0, after(2), after(1))
        k_exp = jnp.where(off4 == 2, 0.0, before(1) - b)
    else:
        q_exp = b - jnp.where(off4 == 3, before(2), before(1))
        k_exp = jnp.where(off4 == 1, 0.0, after(1) - b)
    scores = scores + level_scores(2, q_exp, k_exp)
    scores = scores + level_scores(3, b - (after(1) if reverse else before(1)), None)
    a_rows = []
    for i in range(nsub):
        a = blk(scores, i)
        for c in rows[i]:
            a = a + c
        a_rows.append(a)
    scores = jnp.concatenate(a_rows, axis=0)
    return st_new, (scores.astype(BF16), o_carried)


N_FINE = 4


def _pair_masks():
    msk = np.zeros((2, N_FINE + 1, WIN, WIN), np.float32)
    pos = np.arange(WIN)
    for d in range(2):
        for level in range(N_FINE):
            h = SUB >> (level + 1)
            span, off = pos // (2 * h), pos % (2 * h)
            later = (off < h) if d else (off >= h)
            msk[d, level] = (span[:, None] == span[None, :]) & later[:, None] & ~later[None, :]
        msk[d, N_FINE] = np.eye(WIN)
    return jnp.asarray(msk, F32)


def _hgrn_steps(q_ref, v_ref, kf_ref, bf_ref, kb_ref, bb_ref, vc_ref, kfc_ref, bfc_ref, kbc_ref, bbc_ref,
                msk_ref, o_ref, stf_ref, stb_ref, scf_ref, scb_ref, ocf_ref, ocb_ref, w0, n_steps):
    t = q_ref.shape[1]
    ctx = vc_ref.shape[1]
    nw = t // WIN

    def window(refs, start, st, reverse, want_out):
        qr, vr, kr, br = refs
        sl = pl.ds(start, WIN)
        vw = vr[0, sl, :].astype(F32)
        kw = kr[0, sl, :].astype(F32)
        bw = br[0, sl, :]
        qw = qr[0, sl, :].astype(F32) if want_out else None
        b_row = lambda r: br[0, pl.ds(start + r, 1), :]
        return _scan_window(st, qw, kw, vw, bw, b_row, msk_ref, reverse, want_out)

    def restart():
        zero = jnp.zeros((LANES, LANES), F32)
        stf, stb = zero, zero
        n_cw = ctx // WIN
        for w in range(n_cw):
            stf, _ = window((None, vc_ref, kfc_ref, bfc_ref), w * WIN, stf, False, False)
            stb, _ = window((None, vc_ref, kbc_ref, bbc_ref), (n_cw - 1 - w) * WIN, stb, True, False)
        stf_ref[...] = stf
        stb_ref[...] = stb
        o_ref[...] = jnp.zeros(o_ref.shape, F32)

    def starts(w):
        return pl.multiple_of(w * WIN, WIN), pl.multiple_of((nw - 1 - w) * WIN, WIN)

    def scores_stage(w):
        fs, bs = starts(w)
        stf, (sf, cf) = window((q_ref, v_ref, kf_ref, bf_ref), fs, stf_ref[...], False, True)
        stf_ref[...] = stf
        scf_ref[...] = sf
        ocf_ref[...] = cf
        stb, (sb, cb) = window((q_ref, v_ref, kb_ref, bb_ref), bs, stb_ref[...], True, True)
        stb_ref[...] = stb
        scb_ref[...] = sb
        ocb_ref[...] = cb

    def output_stage(w):
        fs, bs = starts(w)
        o_ref[0, pl.ds(fs, WIN), :] += (
            jnp.dot(scf_ref[...], v_ref[0, pl.ds(fs, WIN), :], preferred_element_type=F32) + ocf_ref[...])
        o_ref[0, pl.ds(bs, WIN), :] += (
            jnp.dot(scb_ref[...], v_ref[0, pl.ds(bs, WIN), :], preferred_element_type=F32) + ocb_ref[...])

    def schedule():
        for u in range(n_steps + 1):
            if u > 0:
                output_stage(w0 + u - 1)
            if u < n_steps:
                scores_stage(w0 + u)
            yield

    return restart, schedule()


def _mixer_kernel(qa_ref, kc_ref, vtc_ref, kx_ref, vtx_ref,
                  hq_ref, hv_ref, kf_ref, bf_ref, kb_ref, bb_ref, hvc_ref, kfc_ref, bfc_ref, kbc_ref, bbc_ref,
                  msk_ref, att_ref, hg_ref, sc_ref, s_ref, acc_ref, *hg_scratch, steps_per_head):
    nw = hq_ref.shape[1] // WIN
    n_steps = nw // steps_per_head
    phase = pl.program_id(1) % steps_per_head
    restart, scan = _hgrn_steps(hq_ref, hv_ref, kf_ref, bf_ref, kb_ref, bb_ref,
                                hvc_ref, kfc_ref, bfc_ref, kbc_ref, bbc_ref, msk_ref, hg_ref, *hg_scratch,
                                w0=phase * n_steps, n_steps=n_steps)
    pl.when(phase == 0)(restart)

    for _ in _attention_schedule(qa_ref, kc_ref, vtc_ref, kx_ref, vtx_ref, att_ref, sc_ref, s_ref, acc_ref):
        next(scan, None)
    for _ in scan:
        pass


def _mixers(qa, kc, vtc, kx, vtx, hq, hv, kf, bf, kb, bb, hvc, kfc, bfc, kbc, bbc):
    b, t, _ = qa.shape
    ctx = hvc.shape[1]
    tq = min(Q_TILE, t)
    n_q = t // tq
    nw = t // WIN
    assert tq % LANES == 0 and t % tq == 0 and t % KV_TILE == 0 and KV_AHEAD < KV_SLOTS
    assert n_q % HG_HEADS == 0 and nw % (n_q // HG_HEADS) == 0
    steps_per_head = n_q // HG_HEADS
    cols = ATT_GROUP * tq
    whole = lambda a: pl.BlockSpec((1,) + a.shape[1:], lambda i, j: (i, 0, 0))
    lat = pl.BlockSpec((1, t, HG_HEAD_DIM), lambda i, j: (i, 0, j // steps_per_head))
    con = pl.BlockSpec((1, ctx, HG_HEAD_DIM), lambda i, j: (i, 0, j // steps_per_head))
    msk = _pair_masks()
    const = lambda a: pl.BlockSpec(a.shape, lambda i, j: (0,) * a.ndim)
    return pl.pallas_call(
        functools.partial(_mixer_kernel, steps_per_head=steps_per_head),
        out_shape=[jax.ShapeDtypeStruct((b, t, ATT_WIDTH), BF16), jax.ShapeDtypeStruct((b, t, HG_WIDTH), F32)],
        grid=(b, n_q),
        in_specs=[pl.BlockSpec((1, tq, qa.shape[-1]), lambda i, j: (i, j, 0)),
                  whole(kc), whole(vtc), whole(kx), whole(vtx)] + [lat] * 6 + [con] * 5
                 + [const(msk)],
        out_specs=[pl.BlockSpec((1, tq, ATT_WIDTH), lambda i, j: (i, j, 0)), lat],
        scratch_shapes=[pltpu.VMEM((ATT_KV_HEADS, kc.shape[1], cols), F32),
                        pltpu.VMEM((KV_SLOTS, ATT_KV_HEADS, KV_TILE, cols), F32),
                        pltpu.VMEM((ATT_KV_HEADS, ATT_HEAD_DIM + ONES_ROWS, cols), F32),
                        pltpu.VMEM((LANES, LANES), F32), pltpu.VMEM((LANES, LANES), F32),
                        pltpu.VMEM((WIN, WIN), BF16), pltpu.VMEM((WIN, WIN), BF16),
                        pltpu.VMEM((WIN, LANES), F32), pltpu.VMEM((WIN, LANES), F32)],
        compiler_params=pltpu.CompilerParams(dimension_semantics=("arbitrary", "arbitrary"),
                                             vmem_limit_bytes=VMEM_LIMIT),
        name="mixers",
    )(qa, kc, vtc, kx, vtx, hq, hv, kf, bf, kb, bb, hvc, kfc, bfc, kbc, bbc, msk)


def _out_ffn_kernel(x_ref, att_ref, hg_ref, sg_ref, mod_ref, ghg_ref, wo_ref, g2_ref, wgu_ref, wd_ref, gf_ref,
                    o_ref):
    d = x_ref.shape[-1]
    dff = wd_ref.shape[0]
    gate1 = mod_ref[0, :, 2 * d:3 * d]
    shift2 = mod_ref[0, :, 3 * d:4 * d]
    scale2 = mod_ref[0, :, 4 * d:5 * d]
    gate2 = mod_ref[0, :, 5 * d:6 * d]

    parts = [slice(r0, r0 + OUT_PART_ROWS) for r0 in range(0, x_ref.shape[1], OUT_PART_ROWS)]

    def mixed(rows):
        mix = jnp.dot(att_ref[0, rows, :], wo_ref[0:ATT_WIDTH, :], preferred_element_type=F32)
        for hd in range(HG_HEADS):
            cs = slice(hd * HG_HEAD_DIM, (hd + 1) * HG_HEAD_DIM)
            o = _rms_rows(hg_ref[0, rows, cs]) * ghg_ref[...]
            o = (o * sg_ref[0, rows, cs].astype(F32)).astype(BF16)
            mix = mix + jnp.dot(o, wo_ref[ATT_WIDTH + hd * HG_HEAD_DIM:ATT_WIDTH + (hd + 1) * HG_HEAD_DIM, :],
                                preferred_element_type=F32)
        x1 = x_ref[0, rows, :] + gate1 * mix
        return x1, ((_rms_rows(x1) * g2_ref[...]) * (1.0 + scale2) + shift2).astype(BF16)

    def gate_up(h):
        return (jnp.dot(h, wgu_ref[:, 0:dff], preferred_element_type=F32),
                jnp.dot(h, wgu_ref[:, dff:2 * dff], preferred_element_type=F32))

    def down(a, g):
        return jnp.dot((_silu(a) * g).astype(BF16), wd_ref[...], preferred_element_type=F32)

    firsts = [mixed(rows) for rows in parts]
    ups = [gate_up(h) for _, h in firsts]
    ys = [down(a, g) for a, g in ups]
    for rows, (x1, _), y in zip(parts, firsts, ys):
        o_ref[0, rows, :] = _rms_rows(x1 + gate2 * y) * gf_ref[...]


def _out_ffn(x, att, hg, sg, mod3, ghg, w_out, g2, w_gu, w_down, gf):
    b, t, d = x.shape
    tm = min(OUT_ROW_TILE, t)
    assert tm % OUT_PART_ROWS == 0
    bspec = lambda width: pl.BlockSpec((1, tm, width), lambda i, j: (i, j, 0))
    full = lambda a: pl.BlockSpec(a.shape, lambda i, j: (0,) * a.ndim, pipeline_mode=pl.Buffered(1))
    return pl.pallas_call(
        _out_ffn_kernel,
        out_shape=jax.ShapeDtypeStruct((b, t, d), F32),
        grid=(b, t // tm),
        in_specs=[bspec(d), bspec(ATT_WIDTH), bspec(HG_WIDTH), bspec(HG_WIDTH),
                  pl.BlockSpec((1, 1, mod3.shape[-1]), lambda i, j: (i, 0, 0)),
                  full(ghg), full(w_out), full(g2), full(w_gu), full(w_down), full(gf)],
        out_specs=bspec(d),
        compiler_params=pltpu.CompilerParams(dimension_semantics=("parallel", "parallel"),
                                             vmem_limit_bytes=VMEM_LIMIT),
        name="out_ffn",
    )(x, att, hg, sg, mod3, ghg, w_out, g2, w_gu, w_down, gf)


def _rope_tables(t):
    n_rows = t // GRID_W
    row = np.repeat(np.arange(n_rows, dtype=np.float32), GRID_W)
    col = np.tile(np.arange(GRID_W, dtype=np.float32), n_rows)
    inv = (np.float32(ROPE_THETA) ** (-np.arange(0, AXIS_DIM, 2, dtype=np.float32) / np.float32(AXIS_DIM)))
    inv = inv.astype(np.float32)
    ar = row[:, None] * inv
    ac = col[:, None] * inv
    ang = np.concatenate([ar, ar, ac, ac], axis=-1).astype(np.float32)
    sign = np.where((np.arange(ATT_HEAD_DIM) % AXIS_DIM) < AXIS_DIM // 2, -1.0, 1.0)
    reps = LANES // ATT_HEAD_DIM
    cos = np.tile(np.cos(ang.astype(np.float64)), (1, reps)).astype(np.float32)
    sin = np.tile(np.sin(ang.astype(np.float64)) * sign, (1, reps)).astype(np.float32)
    return jnp.asarray(cos), jnp.asarray(sin)


def kernel(x, c, ctx, c_ctx, w_mod, b_mod, g_norm1, w_in, g_q, g_k, lb_raw, g_hg, w_out, g_norm2, w_gu, w_down,
           g_final):
    b, t, d = x.shape
    assert w_mod.shape[0] == 1 and lb_raw.shape[0] == 2, "single-layer block"
    assert t % WIN == 0 and (t // WIN) % 2 == 0 and ctx.shape[1] % WIN == 0 and t % KV_TILE == 0

    pad = (-(b + 1)) % 8
    cc = jnp.concatenate([c, c_ctx[None, :], jnp.zeros((pad, d), F32)], axis=0)
    mods = _modulation(cc, w_mod[0], b_mod)
    mod3 = mods.reshape(mods.shape[0], 1, mods.shape[1])

    cos, sin = _rope_tables(t)
    reps = LANES // ATT_HEAD_DIM
    gq = jnp.tile(g_q[0], reps)[None, :]
    gk = jnp.tile(g_k[0], reps)[None, :]
    w_in_b = w_in[0].astype(BF16)

    qa, kx, vtx, hq, hv, kf, bf, kb, bb, sg = _input_projection(
        x, mod3, None, g_norm1, w_in_b, gq, gk, lb_raw, cos, sin, True)
    kc, vtc, hvc, kfc, bfc, kbc, bbc = _input_projection(
        ctx, mod3, b, g_norm1, w_in_b, gq, gk, lb_raw, cos, sin, False)

    att, hg = _mixers(qa, kc, vtc, kx, vtx, hq, hv, kf, bf, kb, bb, hvc, kfc, bfc, kbc, bbc)
    return _out_ffn(x, att, hg, sg, mod3, g_hg, w_out[0].astype(BF16), g_norm2, w_gu[0].astype(BF16),
                    w_down[0].astype(BF16), g_final[None, :])
```

```python
import functools

import jax
import jax.numpy as jnp
import numpy as np
from jax import lax
from jax.experimental import pallas as pl
from jax.experimental.pallas import tpu as pltpu

F32 = jnp.float32
BF16 = jnp.bfloat16

LANES = 128
GRID_W = 64
ATT_HEADS = 8
ATT_KV_HEADS = 2
ATT_GROUP = ATT_HEADS // ATT_KV_HEADS
ATT_HEAD_DIM = 64
ATT_WIDTH = ATT_HEADS * ATT_HEAD_DIM
ATT_KV_WIDTH = ATT_KV_HEADS * ATT_HEAD_DIM
AXIS_DIM = ATT_HEAD_DIM // 2
ROPE_THETA = 10000.0
HG_HEADS = 4
HG_HEAD_DIM = 128
HG_WIDTH = HG_HEADS * HG_HEAD_DIM
EPS = 1e-6

MOD_COLUMN_BLOCKS = 4
SUB = 16
WIN = 128
ROW_TILE = 256
OUT_ROW_TILE = 512
OUT_PART_ROWS = 256
Q_TILE = 256
KV_TILE = 512
KV_SLOTS = 4
KV_AHEAD = 2
ONES_ROWS = 16
LOG2E = 1.4426950408889634
VMEM_LIMIT = 56 * 1024 * 1024

C_AQ = 0
C_AK = C_AQ + ATT_WIDTH
C_AV = C_AK + ATT_KV_WIDTH
C_HQ = C_AV + ATT_KV_WIDTH
C_HI = C_HQ + HG_WIDTH
C_FF = C_HI + HG_WIDTH
C_FB = C_FF + HG_WIDTH
C_HG = C_FB + HG_WIDTH


def _silu(v):
    return v * jax.nn.sigmoid(v)


def _rms_rows(v):
    return v * lax.rsqrt(jnp.mean(v * v, axis=-1, keepdims=True) + EPS)


def _mod_kernel(c_ref, w_ref, b_ref, o_ref):
    a = _silu(c_ref[...])
    o_ref[...] = jnp.dot(a, w_ref[...], precision=lax.Precision.HIGHEST,
                         preferred_element_type=F32) + b_ref[...]


def _modulation(cc, w_mod, b_mod):
    rows, d = cc.shape
    n = w_mod.shape[1]
    bn = n // MOD_COLUMN_BLOCKS
    return pl.pallas_call(
        _mod_kernel,
        out_shape=jax.ShapeDtypeStruct((rows, n), F32),
        grid=(n // bn,),
        in_specs=[pl.BlockSpec((rows, d), lambda j: (0, 0)),
                  pl.BlockSpec((d, bn), lambda j: (0, j)),
                  pl.BlockSpec((1, bn), lambda j: (0, j))],
        out_specs=pl.BlockSpec((rows, bn), lambda j: (0, j)),
        compiler_params=pltpu.CompilerParams(dimension_semantics=("arbitrary",),
                                             vmem_limit_bytes=VMEM_LIMIT),
        name="modulation",
    )(cc, w_mod, b_mod)


def _head64_meansq(p, seg2):
    sq = p * p
    hi = sq.astype(BF16)
    lo = (sq - hi.astype(F32)).astype(BF16)
    s = jnp.dot(jnp.concatenate([hi, lo], axis=1), seg2, preferred_element_type=F32)
    return s * (1.0 / ATT_HEAD_DIM)


def _rope(v, cos, sin_signed, first_half):
    fwd = pltpu.roll(v, LANES - AXIS_DIM // 2, 1)
    bwd = pltpu.roll(v, AXIS_DIM // 2, 1)
    return v * cos + jnp.where(first_half, fwd, bwd) * sin_signed


def _gate_prep(f, lb, suffix):
    fg = lb + (1.0 - lb) * jax.nn.sigmoid(f)
    cum = jnp.log2(fg)
    rows = cum.shape[0]
    pos = lax.broadcasted_iota(jnp.int32, cum.shape, 0) % SUB
    shift = 1
    while shift < SUB:
        if suffix:
            moved = pltpu.roll(cum, rows - shift, 0)
            cum = cum + jnp.where(pos < SUB - shift, moved, 0.0)
        else:
            moved = pltpu.roll(cum, shift, 0)
            cum = cum + jnp.where(pos >= shift, moved, 0.0)
        shift *= 2
    return (1.0 - fg).astype(BF16), cum


def _inproj_kernel(x_ref, mod_ref, g1_ref, w_ref, gq_ref, gk_ref, lb_ref, cos_ref, sin_ref, *out_refs, latent):
    d = x_ref.shape[-1]
    x = x_ref[0]
    shift = mod_ref[0, :, 0:d]
    scale = mod_ref[0, :, d:2 * d]
    h = (_rms_rows(x) * g1_ref[...]) * (1.0 + scale) + shift
    hb = h.astype(BF16)

    def proj(c0, width):
        return jnp.dot(hb, w_ref[:, c0:c0 + width], preferred_element_type=F32)

    lane = lax.broadcasted_iota(jnp.int32, (2 * LANES, LANES), 0) % LANES
    lane_c = lax.broadcasted_iota(jnp.int32, (2 * LANES, LANES), 1)
    seg = (lane // ATT_HEAD_DIM == lane_c // ATT_HEAD_DIM).astype(BF16)
    row_lane = lax.broadcasted_iota(jnp.int32, (x.shape[0], LANES), 1)
    first_half = (row_lane % AXIS_DIM) < (AXIS_DIM // 2)
    upper = row_lane >= ATT_HEAD_DIM

    r0 = lb_ref[0]
    r1 = lb_ref[1]
    rm = jnp.maximum(r0, r1)
    e0 = jnp.exp(r0 - rm)
    e1 = jnp.exp(r1 - rm)
    lb_all = e0 / (e0 + e1)

    if latent:
        (qa_ref, ka_ref, vat_ref, hq_ref, hv_ref, kf_ref, bf_ref, kb_ref, bb_ref, sg_ref) = out_refs
        cos = cos_ref[...]
        sin = sin_ref[...]
    else:
        (ka_ref, vat_ref, hv_ref, kf_ref, bf_ref, kb_ref, bb_ref) = out_refs

    p_ff = proj(C_FF, HG_WIDTH)
    if latent:
        pq = proj(C_AQ, ATT_WIDTH)
    pk = proj(C_AK, ATT_KV_WIDTH)
    pv = proj(C_AV, ATT_KV_WIDTH)
    p_fb = proj(C_FB, HG_WIDTH)
    if latent:
        p_hq = proj(C_HQ, HG_WIDTH)
    p_hv = proj(C_HI, HG_WIDTH)
    ms_k = _head64_meansq(pk, seg)
    if latent:
        ms_q = [_head64_meansq(pq[:, c * LANES:(c + 1) * LANES], seg) for c in range(ATT_WIDTH // LANES)]

    kn = pk * lax.rsqrt(ms_k + EPS) * gk_ref[...]
    if latent:
        kn = _rope(kn, cos, sin, first_half)
    ka_ref[0] = kn.astype(BF16)
    vat_ref[0] = pv.T.astype(BF16)

    if latent:
        for c in range(ATT_WIDTH // LANES):
            t = pq[:, c * LANES:(c + 1) * LANES]
            t = t * lax.rsqrt(ms_q[c] + EPS) * gq_ref[...]
            t = _rope(t, cos, sin, first_half) * (ATT_HEAD_DIM ** -0.5 * LOG2E)
            t_swapped = pltpu.roll(t, ATT_HEAD_DIM, 1)
            for half in range(2):
                head = 2 * c + half
                kv = head // ATT_GROUP
                src = t if half == kv else t_swapped
                keep = upper if kv == 1 else jnp.logical_not(upper)
                qa_ref[0, :, head * LANES:(head + 1) * LANES] = jnp.where(keep, src, 0.0).astype(BF16)
        hq_ref[0] = (_silu(p_hq) * (HG_HEAD_DIM ** -0.5)).astype(BF16)
        sg_ref[0] = _silu(proj(C_HG, HG_WIDTH)).astype(BF16)

    hv_ref[0] = p_hv.astype(BF16)
    kf, bf = _gate_prep(p_ff, lb_all[0:1, :], False)
    kf_ref[0] = kf
    bf_ref[0] = bf
    kb, bb = _gate_prep(p_fb, lb_all[1:2, :], True)
    kb_ref[0] = kb
    bb_ref[0] = bb


def _input_projection(rows, mod3, mod_row, g1, w_in, gq, gk, lb_raw, cos, sin, latent):
    b, t, d = rows.shape
    tm = min(ROW_TILE, t)
    nt = t // tm
    bspec = lambda width: pl.BlockSpec((1, tm, width), lambda i, j: (i, j, 0))
    full = lambda a: pl.BlockSpec(a.shape, lambda i, j: (0,) * a.ndim)
    sds = lambda width, dt: jax.ShapeDtypeStruct((b, t, width), dt)
    vt_sds = jax.ShapeDtypeStruct((b, ATT_KV_WIDTH, t), BF16)
    vt_spec = pl.BlockSpec((1, ATT_KV_WIDTH, tm), lambda i, j: (i, 0, j))
    if latent:
        out_shape = [sds(ATT_HEADS * LANES, BF16), sds(ATT_KV_WIDTH, BF16), vt_sds,
                     sds(HG_WIDTH, BF16), sds(HG_WIDTH, BF16), sds(HG_WIDTH, BF16), sds(HG_WIDTH, F32),
                     sds(HG_WIDTH, BF16), sds(HG_WIDTH, F32), sds(HG_WIDTH, BF16)]
        out_specs = [bspec(ATT_HEADS * LANES), bspec(ATT_KV_WIDTH), vt_spec, bspec(HG_WIDTH),
                     bspec(HG_WIDTH), bspec(HG_WIDTH), bspec(HG_WIDTH), bspec(HG_WIDTH), bspec(HG_WIDTH),
                     bspec(HG_WIDTH)]
        mod_spec = pl.BlockSpec((1, 1, mod3.shape[-1]), lambda i, j: (i, 0, 0))
    else:
        out_shape = [sds(ATT_KV_WIDTH, BF16), vt_sds, sds(HG_WIDTH, BF16), sds(HG_WIDTH, BF16),
                     sds(HG_WIDTH, F32), sds(HG_WIDTH, BF16), sds(HG_WIDTH, F32)]
        out_specs = [bspec(ATT_KV_WIDTH), vt_spec, bspec(HG_WIDTH), bspec(HG_WIDTH), bspec(HG_WIDTH),
                     bspec(HG_WIDTH), bspec(HG_WIDTH)]
        mod_spec = pl.BlockSpec((1, 1, mod3.shape[-1]), lambda i, j: (mod_row, 0, 0))
    table_spec = pl.BlockSpec((tm, LANES), lambda i, j: (j, 0))
    return pl.pallas_call(
        functools.partial(_inproj_kernel, latent=latent),
        out_shape=out_shape,
        grid=(b, nt),
        in_specs=[bspec(d), mod_spec, full(g1), full(w_in), full(gq), full(gk), full(lb_raw),
                  table_spec, table_spec],
        out_specs=out_specs,
        compiler_params=pltpu.CompilerParams(dimension_semantics=("parallel", "parallel"),
                                             vmem_limit_bytes=VMEM_LIMIT),
        name="inproj_latent" if latent else "inproj_context",
    )(rows, mod3, g1, w_in, gq, gk, lb_raw, cos, sin)


def _attention_schedule(q_ref, kc_ref, vtc_ref, kx_ref, vtx_ref, o_ref, sc_ref, s_ref, acc_ref):
    tq = q_ref.shape[1]
    n_kv = kx_ref.shape[1] // KV_TILE
    nt_dims = (((1,), (1,)), ((), ()))
    cols = ATT_GROUP * tq

    qgs = [jnp.concatenate(
        [q_ref[0, :, (kv * ATT_GROUP + j) * LANES:(kv * ATT_GROUP + j + 1) * LANES] for j in range(ATT_GROUP)],
        axis=0) for kv in range(ATT_KV_HEADS)]

    def scores(kt, dst):
        for kv in range(ATT_KV_HEADS):
            dst(kv, lax.dot_general(kt, qgs[kv], nt_dims, preferred_element_type=F32))

    def consume(src, vt_all, maxima):
        ones = jnp.ones((ONES_ROWS, vt_all.shape[1]), BF16)
        out = []
        for kv in range(ATT_KV_HEADS):
            m = maxima[kv]
            m_new = jnp.maximum(m, jnp.max(src(kv), axis=0, keepdims=True))
            alpha = jnp.exp2(m - m_new)
            p = jnp.exp2(src(kv) - m_new).astype(BF16)
            vt = jnp.concatenate([vt_all[kv * ATT_HEAD_DIM:(kv + 1) * ATT_HEAD_DIM], ones], axis=0)
            acc_ref[kv] = alpha * acc_ref[kv] + jnp.dot(vt, p, preferred_element_type=F32)
            out.append(m_new)
        return tuple(out)

    def put_ctx(kv, val):
        sc_ref[kv] = val

    def put(slot):
        def dst(kv, val):
            s_ref[slot, kv] = val
        return dst

    def k_tile(j):
        return kx_ref[0, j * KV_TILE:(j + 1) * KV_TILE, :]

    def vt_tile(j):
        return vtx_ref[0, :, j * KV_TILE:(j + 1) * KV_TILE]

    acc_ref[...] = jnp.zeros(acc_ref.shape, F32)
    scores(kc_ref[0], put_ctx)
    for j in range(min(KV_AHEAD, n_kv)):
        scores(k_tile(j), put(j % KV_SLOTS))
    maxima = consume(lambda kv: sc_ref[kv], vtc_ref[0], (jnp.full((1, cols), -jnp.inf, F32),) * ATT_KV_HEADS)
    for j in range(n_kv):
        if j + KV_AHEAD < n_kv:
            scores(k_tile(j + KV_AHEAD), put((j + KV_AHEAD) % KV_SLOTS))
        maxima = consume(lambda kv, slot=j % KV_SLOTS: s_ref[slot, kv], vt_tile(j), maxima)
        yield

    for kv in range(ATT_KV_HEADS):
        o = acc_ref[kv, 0:ATT_HEAD_DIM] / acc_ref[kv, ATT_HEAD_DIM:ATT_HEAD_DIM + 1]
        for jj in range(ATT_GROUP // 2):
            pair = jnp.concatenate([o[:, (2 * jj) * tq:(2 * jj + 1) * tq],
                                    o[:, (2 * jj + 1) * tq:(2 * jj + 2) * tq]], axis=0)
            col = kv * (ATT_GROUP // 2) + jj
            o_ref[0, :, col * LANES:(col + 1) * LANES] = pair.T.astype(BF16)


NT_DIMS = (((1,), (1,)), ((), ()))


def _zero_rows(n):
    return [jnp.zeros((n, LANES), F32)] if n else []


def _scan_window(st, q, k, v, b, b_row, msk_ref, reverse, want_out):
    nsub = WIN // SUB
    last = 0 if reverse else SUB - 1
    sub_of = (lambda p: nsub - 1 - p) if reverse else (lambda p: p)
    blk = lambda a, i: a[i * SUB:(i + 1) * SUB]
    d = int(reverse)

    b_last = [b_row(i * SUB + last) for i in range(nsub)]
    cum = []
    for p in range(nsub):
        cum.append(b_last[sub_of(p)] if p == 0 else cum[-1] + b_last[sub_of(p)])
    ks = [blk(k, i) * jnp.exp2(b_last[i] - blk(b, i)) for i in range(nsub)]

    k_end = jnp.concatenate(
        [ks[i] if sub_of(i) == nsub - 1 else ks[i] * jnp.exp2(cum[nsub - 1] - cum[sub_of(i)]) for i in range(nsub)],
        axis=0)
    st_new = jnp.exp2(cum[nsub - 1]) * st + jnp.dot(v.T, k_end, preferred_element_type=F32)
    if not want_out:
        return st_new, None

    qs = [blk(q, i) * jnp.exp2(blk(b, i)) for i in range(nsub)]
    q_start = jnp.concatenate(
        [qs[i] if sub_of(i) == 0 else qs[i] * jnp.exp2(cum[sub_of(i) - 1]) for i in range(nsub)], axis=0)
    o_carried = lax.dot_general(q_start, st, NT_DIMS, preferred_element_type=F32)

    rows = [[] for _ in range(nsub)]
    for h in (nsub // 2, nsub // 4, nsub // 8):
        for ub in range(1, nsub // h, 2):
            ref = ub * h - 1
            upper = {sub_of(p): p for p in range(ub * h, (ub + 1) * h)}
            lower = {sub_of(p): p for p in range((ub - 1) * h, ub * h)}
            iu, il = sorted(upper), sorted(lower)
            lhs = jnp.concatenate(
                [qs[i] if upper[i] - 1 == ref else qs[i] * jnp.exp2(cum[upper[i] - 1] - cum[ref]) for i in iu], axis=0)
            rhs = jnp.concatenate(
                _zero_rows(il[0] * SUB)
                + [ks[i] if lower[i] == ref else ks[i] * jnp.exp2(cum[ref] - cum[lower[i]]) for i in il]
                + _zero_rows((nsub - 1 - il[-1]) * SUB), axis=0)
            c = lax.dot_general(lhs, rhs, NT_DIMS, preferred_element_type=F32)
            for n, i in enumerate(iu):
                rows[i].append(c[n * SUB:(n + 1) * SUB])

    def level_scores(level, ref):
        diff = b - ref
        decay = jnp.exp2(jnp.minimum(diff, -diff))
        return lax.dot_general(q * decay, k * decay, NT_DIMS, preferred_element_type=F32) * msk_ref[d, level]

    scores = jnp.dot(q * k, jnp.ones((LANES, LANES), BF16), preferred_element_type=F32) * msk_ref[d, N_FINE]
    for level, h in ((0, SUB // 2), (1, SUB // 4)):
        ref_rows = [b_row(s * 2 * h + (h if reverse else h - 1)) for s in range(WIN // (2 * h))]
        scores = scores + level_scores(
            level, jnp.concatenate([jnp.broadcast_to(r, (2 * h, LANES)) for r in ref_rows], axis=0))
    before = lambda n: pltpu.roll(b, n, 0)
    after = lambda n: pltpu.roll(b, WIN - n, 0)
    off4 = lax.broadcasted_iota(jnp.int32, (WIN, LANES), 0) % 4
    if reverse:
        ref = jnp.where(off4 == 0, after(2), jnp.where(off4 == 1, after(1), jnp.where(off4 == 2, b, before(1))))
    else:
        ref = jnp.where(off4 == 3, before(2), jnp.where(off4 == 2, before(1), jnp.where(off4 == 1, b, after(1))))
    scores = scores + level_scores(2, ref)
    q_l = q * jnp.exp2(jnp.minimum(b - (after(1) if reverse else before(1)), 0.0))
    scores = scores + lax.dot_general(q_l, k, NT_DIMS, preferred_element_type=F32) * msk_ref[d, 3]
    a_rows = []
    for i in range(nsub):
        a = blk(scores, i)
        for c in rows[i]:
            a = a + c
        a_rows.append(a)
    scores = jnp.concatenate(a_rows, axis=0)
    return st_new, (scores.astype(BF16), o_carried)


N_FINE = 4


def _pair_masks():
    msk = np.zeros((2, N_FINE + 1, WIN, WIN), np.float32)
    pos = np.arange(WIN)
    for d in range(2):
        for level in range(N_FINE):
            h = SUB >> (level + 1)
            span, off = pos // (2 * h), pos % (2 * h)
            later = (off < h) if d else (off >= h)
            msk[d, level] = (span[:, None] == span[None, :]) & later[:, None] & ~later[None, :]
        msk[d, N_FINE] = np.eye(WIN)
    return jnp.asarray(msk, F32)


def _hgrn_steps(q_ref, v_ref, kf_ref, bf_ref, kb_ref, bb_ref, vc_ref, kfc_ref, bfc_ref, kbc_ref, bbc_ref,
                msk_ref, o_ref, stf_ref, stb_ref, scf_ref, scb_ref, ocf_ref, ocb_ref, w0, n_steps):
    t = q_ref.shape[1]
    ctx = vc_ref.shape[1]
    nw = t // WIN

    def window(refs, start, st, reverse, want_out):
        qr, vr, kr, br = refs
        sl = pl.ds(start, WIN)
        vw = vr[0, sl, :].astype(F32)
        kw = kr[0, sl, :].astype(F32)
        bw = br[0, sl, :]
        qw = qr[0, sl, :].astype(F32) if want_out else None
        b_row = lambda r: br[0, pl.ds(start + r, 1), :]
        return _scan_window(st, qw, kw, vw, bw, b_row, msk_ref, reverse, want_out)

    def restart():
        zero = jnp.zeros((LANES, LANES), F32)
        stf, stb = zero, zero
        n_cw = ctx // WIN
        for w in range(n_cw):
            stf, _ = window((None, vc_ref, kfc_ref, bfc_ref), w * WIN, stf, False, False)
            stb, _ = window((None, vc_ref, kbc_ref, bbc_ref), (n_cw - 1 - w) * WIN, stb, True, False)
        stf_ref[...] = stf
        stb_ref[...] = stb
        o_ref[...] = jnp.zeros(o_ref.shape, F32)

    def starts(w):
        return pl.multiple_of(w * WIN, WIN), pl.multiple_of((nw - 1 - w) * WIN, WIN)

    def scores_stage(w):
        fs, bs = starts(w)
        stf, (sf, cf) = window((q_ref, v_ref, kf_ref, bf_ref), fs, stf_ref[...], False, True)
        stf_ref[...] = stf
        scf_ref[...] = sf
        ocf_ref[...] = cf
        stb, (sb, cb) = window((q_ref, v_ref, kb_ref, bb_ref), bs, stb_ref[...], True, True)
        stb_ref[...] = stb
        scb_ref[...] = sb
        ocb_ref[...] = cb

    def output_stage(w):
        fs, bs = starts(w)
        o_ref[0, pl.ds(fs, WIN), :] += (
            jnp.dot(scf_ref[...], v_ref[0, pl.ds(fs, WIN), :], preferred_element_type=F32) + ocf_ref[...])
        o_ref[0, pl.ds(bs, WIN), :] += (
            jnp.dot(scb_ref[...], v_ref[0, pl.ds(bs, WIN), :], preferred_element_type=F32) + ocb_ref[...])

    def schedule():
        for u in range(n_steps + 1):
            if u > 0:
                output_stage(w0 + u - 1)
            if u < n_steps:
                scores_stage(w0 + u)
            yield

    return restart, schedule()


def _mixer_kernel(qa_ref, kc_ref, vtc_ref, kx_ref, vtx_ref,
                  hq_ref, hv_ref, kf_ref, bf_ref, kb_ref, bb_ref, hvc_ref, kfc_ref, bfc_ref, kbc_ref, bbc_ref,
                  msk_ref, att_ref, hg_ref, sc_ref, s_ref, acc_ref, *hg_scratch, steps_per_head):
    nw = hq_ref.shape[1] // WIN
    n_steps = nw // steps_per_head
    phase = pl.program_id(1) % steps_per_head
    restart, scan = _hgrn_steps(hq_ref, hv_ref, kf_ref, bf_ref, kb_ref, bb_ref,
                                hvc_ref, kfc_ref, bfc_ref, kbc_ref, bbc_ref, msk_ref, hg_ref, *hg_scratch,
                                w0=phase * n_steps, n_steps=n_steps)
    pl.when(phase == 0)(restart)

    for _ in _attention_schedule(qa_ref, kc_ref, vtc_ref, kx_ref, vtx_ref, att_ref, sc_ref, s_ref, acc_ref):
        next(scan, None)
    for _ in scan:
        pass


def _mixers(qa, kc, vtc, kx, vtx, hq, hv, kf, bf, kb, bb, hvc, kfc, bfc, kbc, bbc):
    b, t, _ = qa.shape
    ctx = hvc.shape[1]
    tq = min(Q_TILE, t)
    n_q = t // tq
    nw = t // WIN
    assert tq % LANES == 0 and t % tq == 0 and t % KV_TILE == 0 and KV_AHEAD < KV_SLOTS
    assert n_q % HG_HEADS == 0 and nw % (n_q // HG_HEADS) == 0
    steps_per_head = n_q // HG_HEADS
    cols = ATT_GROUP * tq
    whole = lambda a: pl.BlockSpec((1,) + a.shape[1:], lambda i, j: (i, 0, 0))
    lat = pl.BlockSpec((1, t, HG_HEAD_DIM), lambda i, j: (i, 0, j // steps_per_head))
    con = pl.BlockSpec((1, ctx, HG_HEAD_DIM), lambda i, j: (i, 0, j // steps_per_head))
    msk = _pair_masks()
    const = lambda a: pl.BlockSpec(a.shape, lambda i, j: (0,) * a.ndim)
    return pl.pallas_call(
        functools.partial(_mixer_kernel, steps_per_head=steps_per_head),
        out_shape=[jax.ShapeDtypeStruct((b, t, ATT_WIDTH), BF16), jax.ShapeDtypeStruct((b, t, HG_WIDTH), F32)],
        grid=(b, n_q),
        in_specs=[pl.BlockSpec((1, tq, qa.shape[-1]), lambda i, j: (i, j, 0)),
                  whole(kc), whole(vtc), whole(kx), whole(vtx)] + [lat] * 6 + [con] * 5
                 + [const(msk)],
        out_specs=[pl.BlockSpec((1, tq, ATT_WIDTH), lambda i, j: (i, j, 0)), lat],
        scratch_shapes=[pltpu.VMEM((ATT_KV_HEADS, kc.shape[1], cols), F32),
                        pltpu.VMEM((KV_SLOTS, ATT_KV_HEADS, KV_TILE, cols), F32),
                        pltpu.VMEM((ATT_KV_HEADS, ATT_HEAD_DIM + ONES_ROWS, cols), F32),
                        pltpu.VMEM((LANES, LANES), F32), pltpu.VMEM((LANES, LANES), F32),
                        pltpu.VMEM((WIN, WIN), BF16), pltpu.VMEM((WIN, WIN), BF16),
                        pltpu.VMEM((WIN, LANES), F32), pltpu.VMEM((WIN, LANES), F32)],
        compiler_params=pltpu.CompilerParams(dimension_semantics=("arbitrary", "arbitrary"),
                                             vmem_limit_bytes=VMEM_LIMIT),
        name="mixers",
    )(qa, kc, vtc, kx, vtx, hq, hv, kf, bf, kb, bb, hvc, kfc, bfc, kbc, bbc, msk)


def _out_ffn_kernel(x_ref, att_ref, hg_ref, sg_ref, mod_ref, ghg_ref, wo_ref, g2_ref, wgu_ref, wd_ref, gf_ref,
                    o_ref):
    d = x_ref.shape[-1]
    dff = wd_ref.shape[0]
    gate1 = mod_ref[0, :, 2 * d:3 * d]
    shift2 = mod_ref[0, :, 3 * d:4 * d]
    scale2 = mod_ref[0, :, 4 * d:5 * d]
    gate2 = mod_ref[0, :, 5 * d:6 * d]

    parts = [slice(r0, r0 + OUT_PART_ROWS) for r0 in range(0, x_ref.shape[1], OUT_PART_ROWS)]

    def mixed(rows):
        mix = jnp.dot(att_ref[0, rows, :], wo_ref[0:ATT_WIDTH, :], preferred_element_type=F32)
        for hd in range(HG_HEADS):
            cs = slice(hd * HG_HEAD_DIM, (hd + 1) * HG_HEAD_DIM)
            o = _rms_rows(hg_ref[0, rows, cs]) * ghg_ref[...]
            o = (o * sg_ref[0, rows, cs].astype(F32)).astype(BF16)
            mix = mix + jnp.dot(o, wo_ref[ATT_WIDTH + hd * HG_HEAD_DIM:ATT_WIDTH + (hd + 1) * HG_HEAD_DIM, :],
                                preferred_element_type=F32)
        x1 = x_ref[0, rows, :] + gate1 * mix
        return x1, ((_rms_rows(x1) * g2_ref[...]) * (1.0 + scale2) + shift2).astype(BF16)

    def gate_up(h):
        return (jnp.dot(h, wgu_ref[:, 0:dff], preferred_element_type=F32),
                jnp.dot(h, wgu_ref[:, dff:2 * dff], preferred_element_type=F32))

    def down(a, g):
        return jnp.dot((_silu(a) * g).astype(BF16), wd_ref[...], preferred_element_type=F32)

    firsts = [mixed(rows) for rows in parts]
    ups = [gate_up(h) for _, h in firsts]
    ys = [down(a, g) for a, g in ups]
    for rows, (x1, _), y in zip(parts, firsts, ys):
        o_ref[0, rows, :] = _rms_rows(x1 + gate2 * y) * gf_ref[...]


def _out_ffn(x, att, hg, sg, mod3, ghg, w_out, g2, w_gu, w_down, gf):
    b, t, d = x.shape
    tm = min(OUT_ROW_TILE, t)
    assert tm % OUT_PART_ROWS == 0
    bspec = lambda width: pl.BlockSpec((1, tm, width), lambda i, j: (i, j, 0))
    full = lambda a: pl.BlockSpec(a.shape, lambda i, j: (0,) * a.ndim, pipeline_mode=pl.Buffered(1))
    return pl.pallas_call(
        _out_ffn_kernel,
        out_shape=jax.ShapeDtypeStruct((b, t, d), F32),
        grid=(b, t // tm),
        in_specs=[bspec(d), bspec(ATT_WIDTH), bspec(HG_WIDTH), bspec(HG_WIDTH),
                  pl.BlockSpec((1, 1, mod3.shape[-1]), lambda i, j: (i, 0, 0)),
                  full(ghg), full(w_out), full(g2), full(w_gu), full(w_down), full(gf)],
        out_specs=bspec(d),
        compiler_params=pltpu.CompilerParams(dimension_semantics=("parallel", "parallel"),
                                             vmem_limit_bytes=VMEM_LIMIT),
        name="out_ffn",
    )(x, att, hg, sg, mod3, ghg, w_out, g2, w_gu, w_down, gf)


def _rope_tables(t):
    n_rows = t // GRID_W
    row = np.repeat(np.arange(n_rows, dtype=np.float32), GRID_W)
    col = np.tile(np.arange(GRID_W, dtype=np.float32), n_rows)
    inv = (np.float32(ROPE_THETA) ** (-np.arange(0, AXIS_DIM, 2, dtype=np.float32) / np.float32(AXIS_DIM)))
    inv = inv.astype(np.float32)
    ar = row[:, None] * inv
    ac = col[:, None] * inv
    ang = np.concatenate([ar, ar, ac, ac], axis=-1).astype(np.float32)
    sign = np.where((np.arange(ATT_HEAD_DIM) % AXIS_DIM) < AXIS_DIM // 2, -1.0, 1.0)
    reps = LANES // ATT_HEAD_DIM
    cos = np.tile(np.cos(ang.astype(np.float64)), (1, reps)).astype(np.float32)
    sin = np.tile(np.sin(ang.astype(np.float64)) * sign, (1, reps)).astype(np.float32)
    return jnp.asarray(cos), jnp.asarray(sin)


def kernel(x, c, ctx, c_ctx, w_mod, b_mod, g_norm1, w_in, g_q, g_k, lb_raw, g_hg, w_out, g_norm2, w_gu, w_down,
           g_final):
    b, t, d = x.shape
    assert w_mod.shape[0] == 1 and lb_raw.shape[0] == 2, "single-layer block"
    assert t % WIN == 0 and (t // WIN) % 2 == 0 and ctx.shape[1] % WIN == 0 and t % KV_TILE == 0

    pad = (-(b + 1)) % 8
    cc = jnp.concatenate([c, c_ctx[None, :], jnp.zeros((pad, d), F32)], axis=0)
    mods = _modulation(cc, w_mod[0], b_mod)
    mod3 = mods.reshape(mods.shape[0], 1, mods.shape[1])

    cos, sin = _rope_tables(t)
    reps = LANES // ATT_HEAD_DIM
    gq = jnp.tile(g_q[0], reps)[None, :]
    gk = jnp.tile(g_k[0], reps)[None, :]
    w_in_b = w_in[0].astype(BF16)

    qa, kx, vtx, hq, hv, kf, bf, kb, bb, sg = _input_projection(
        x, mod3, None, g_norm1, w_in_b, gq, gk, lb_raw, cos, sin, True)
    kc, vtc, hvc, kfc, bfc, kbc, bbc = _input_projection(
        ctx, mod3, b, g_norm1, w_in_b, gq, gk, lb_raw, cos, sin, False)

    att, hg = _mixers(qa, kc, vtc, kx, vtx, hq, hv, kf, bf, kb, bb, hvc, kfc, bfc, kbc, bbc)
    return _out_ffn(x, att, hg, sg, mod3, g_hg, w_out[0].astype(BF16), g_norm2, w_gu[0].astype(BF16),
                    w_down[0].astype(BF16), g_final[None, :])
```

```python
import functools

import jax
import jax.numpy as jnp
import numpy as np
from jax import lax
from jax.experimental import pallas as pl
from jax.experimental.pallas import tpu as pltpu

F32 = jnp.float32
BF16 = jnp.bfloat16

LANES = 128
GRID_W = 64
ATT_HEADS = 8
ATT_KV_HEADS = 2
ATT_GROUP = ATT_HEADS // ATT_KV_HEADS
ATT_HEAD_DIM = 64
ATT_WIDTH = ATT_HEADS * ATT_HEAD_DIM
ATT_KV_WIDTH = ATT_KV_HEADS * ATT_HEAD_DIM
AXIS_DIM = ATT_HEAD_DIM // 2
ROPE_THETA = 10000.0
HG_HEADS = 4
HG_HEAD_DIM = 128
HG_WIDTH = HG_HEADS * HG_HEAD_DIM
EPS = 1e-6

MOD_COLUMN_BLOCKS = 4
SUB = 16
WIN = 128
ROW_TILE = 256
OUT_ROW_TILE = 512
OUT_PART_ROWS = 256
Q_TILE = 256
KV_TILE = 512
KV_SLOTS = 4
KV_AHEAD = 2
ONES_ROWS = 16
LOG2E = 1.4426950408889634
VMEM_LIMIT = 56 * 1024 * 1024

C_AQ = 0
C_AK = C_AQ + ATT_WIDTH
C_AV = C_AK + ATT_KV_WIDTH
C_HQ = C_AV + ATT_KV_WIDTH
C_HI = C_HQ + HG_WIDTH
C_FF = C_HI + HG_WIDTH
C_FB = C_FF + HG_WIDTH
C_HG = C_FB + HG_WIDTH


def _silu(v):
    return v * jax.nn.sigmoid(v)


def _rms_rows(v):
    return v * lax.rsqrt(jnp.mean(v * v, axis=-1, keepdims=True) + EPS)


def _mod_kernel(c_ref, w_ref, b_ref, o_ref):
    a = _silu(c_ref[...])
    o_ref[...] = jnp.dot(a, w_ref[...], precision=lax.Precision.HIGHEST,
                         preferred_element_type=F32) + b_ref[...]


def _modulation(cc, w_mod, b_mod):
    rows, d = cc.shape
    n = w_mod.shape[1]
    bn = n // MOD_COLUMN_BLOCKS
    return pl.pallas_call(
        _mod_kernel,
        out_shape=jax.ShapeDtypeStruct((rows, n), F32),
        grid=(n // bn,),
        in_specs=[pl.BlockSpec((rows, d), lambda j: (0, 0)),
                  pl.BlockSpec((d, bn), lambda j: (0, j)),
                  pl.BlockSpec((1, bn), lambda j: (0, j))],
        out_specs=pl.BlockSpec((rows, bn), lambda j: (0, j)),
        compiler_params=pltpu.CompilerParams(dimension_semantics=("arbitrary",),
                                             vmem_limit_bytes=VMEM_LIMIT),
        name="modulation",
    )(cc, w_mod, b_mod)


def _head64_meansq(p, seg2):
    sq = p * p
    hi = sq.astype(BF16)
    lo = (sq - hi.astype(F32)).astype(BF16)
    s = jnp.dot(jnp.concatenate([hi, lo], axis=1), seg2, preferred_element_type=F32)
    return s * (1.0 / ATT_HEAD_DIM)


def _rope(v, cos, sin_signed, first_half):
    fwd = pltpu.roll(v, LANES - AXIS_DIM // 2, 1)
    bwd = pltpu.roll(v, AXIS_DIM // 2, 1)
    return v * cos + jnp.where(first_half, fwd, bwd) * sin_signed


def _gate_prep(f, lb, suffix):
    fg = lb + (1.0 - lb) * jax.nn.sigmoid(f)
    cum = jnp.log2(fg)
    rows = cum.shape[0]
    pos = lax.broadcasted_iota(jnp.int32, cum.shape, 0) % SUB
    shift = 1
    while shift < SUB:
        if suffix:
            moved = pltpu.roll(cum, rows - shift, 0)
            cum = cum + jnp.where(pos < SUB - shift, moved, 0.0)
        else:
            moved = pltpu.roll(cum, shift, 0)
            cum = cum + jnp.where(pos >= shift, moved, 0.0)
        shift *= 2
    return (1.0 - fg).astype(BF16), cum


def _inproj_kernel(x_ref, mod_ref, g1_ref, w_ref, gq_ref, gk_ref, lb_ref, cos_ref, sin_ref, *out_refs, latent):
    d = x_ref.shape[-1]
    x = x_ref[0]
    shift = mod_ref[0, :, 0:d]
    scale = mod_ref[0, :, d:2 * d]
    h = (_rms_rows(x) * g1_ref[...]) * (1.0 + scale) + shift
    hb = h.astype(BF16)

    def proj(c0, width):
        return jnp.dot(hb, w_ref[:, c0:c0 + width], preferred_element_type=F32)

    lane = lax.broadcasted_iota(jnp.int32, (2 * LANES, LANES), 0) % LANES
    lane_c = lax.broadcasted_iota(jnp.int32, (2 * LANES, LANES), 1)
    seg = (lane // ATT_HEAD_DIM == lane_c // ATT_HEAD_DIM).astype(BF16)
    row_lane = lax.broadcasted_iota(jnp.int32, (x.shape[0], LANES), 1)
    first_half = (row_lane % AXIS_DIM) < (AXIS_DIM // 2)
    upper = row_lane >= ATT_HEAD_DIM

    r0 = lb_ref[0]
    r1 = lb_ref[1]
    rm = jnp.maximum(r0, r1)
    e0 = jnp.exp(r0 - rm)
    e1 = jnp.exp(r1 - rm)
    lb_all = e0 / (e0 + e1)

    if latent:
        (qa_ref, ka_ref, vat_ref, hq_ref, hv_ref, kf_ref, bf_ref, kb_ref, bb_ref, sg_ref) = out_refs
        cos = cos_ref[...]
        sin = sin_ref[...]
    else:
        (ka_ref, vat_ref, hv_ref, kf_ref, bf_ref, kb_ref, bb_ref) = out_refs

    p_ff = proj(C_FF, HG_WIDTH)
    if latent:
        pq = proj(C_AQ, ATT_WIDTH)
    pk = proj(C_AK, ATT_KV_WIDTH)
    pv = proj(C_AV, ATT_KV_WIDTH)
    p_fb = proj(C_FB, HG_WIDTH)
    if latent:
        p_hq = proj(C_HQ, HG_WIDTH)
    p_hv = proj(C_HI, HG_WIDTH)
    ms_k = _head64_meansq(pk, seg)
    if latent:
        ms_q = [_head64_meansq(pq[:, c * LANES:(c + 1) * LANES], seg) for c in range(ATT_WIDTH // LANES)]

    kn = pk * lax.rsqrt(ms_k + EPS) * gk_ref[...]
    if latent:
        kn = _rope(kn, cos, sin, first_half)
    ka_ref[0] = kn.astype(BF16)
    vat_ref[0] = pv.T.astype(BF16)

    if latent:
        for c in range(ATT_WIDTH // LANES):
            t = pq[:, c * LANES:(c + 1) * LANES]
            t = t * lax.rsqrt(ms_q[c] + EPS) * gq_ref[...]
            t = _rope(t, cos, sin, first_half) * (ATT_HEAD_DIM ** -0.5 * LOG2E)
            t_swapped = pltpu.roll(t, ATT_HEAD_DIM, 1)
            for half in range(2):
                head = 2 * c + half
                kv = head // ATT_GROUP
                src = t if half == kv else t_swapped
                keep = upper if kv == 1 else jnp.logical_not(upper)
                qa_ref[0, :, head * LANES:(head + 1) * LANES] = jnp.where(keep, src, 0.0).astype(BF16)
        hq_ref[0] = (_silu(p_hq) * (HG_HEAD_DIM ** -0.5)).astype(BF16)
        sg_ref[0] = _silu(proj(C_HG, HG_WIDTH)).astype(BF16)

    hv_ref[0] = p_hv.astype(BF16)
    kf, bf = _gate_prep(p_ff, lb_all[0:1, :], False)
    kf_ref[0] = kf
    bf_ref[0] = bf
    kb, bb = _gate_prep(p_fb, lb_all[1:2, :], True)
    kb_ref[0] = kb
    bb_ref[0] = bb


def _input_projection(rows, mod3, mod_row, g1, w_in, gq, gk, lb_raw, cos, sin, latent):
    b, t, d = rows.shape
    tm = min(ROW_TILE, t)
    nt = t // tm
    bspec = lambda width: pl.BlockSpec((1, tm, width), lambda i, j: (i, j, 0))
    full = lambda a: pl.BlockSpec(a.shape, lambda i, j: (0,) * a.ndim)
    sds = lambda width, dt: jax.ShapeDtypeStruct((b, t, width), dt)
    vt_sds = jax.ShapeDtypeStruct((b, ATT_KV_WIDTH, t), BF16)
    vt_spec = pl.BlockSpec((1, ATT_KV_WIDTH, tm), lambda i, j: (i, 0, j))
    if latent:
        out_shape = [sds(ATT_HEADS * LANES, BF16), sds(ATT_KV_WIDTH, BF16), vt_sds,
                     sds(HG_WIDTH, BF16), sds(HG_WIDTH, BF16), sds(HG_WIDTH, BF16), sds(HG_WIDTH, F32),
                     sds(HG_WIDTH, BF16), sds(HG_WIDTH, F32), sds(HG_WIDTH, BF16)]
        out_specs = [bspec(ATT_HEADS * LANES), bspec(ATT_KV_WIDTH), vt_spec, bspec(HG_WIDTH),
                     bspec(HG_WIDTH), bspec(HG_WIDTH), bspec(HG_WIDTH), bspec(HG_WIDTH), bspec(HG_WIDTH),
                     bspec(HG_WIDTH)]
        mod_spec = pl.BlockSpec((1, 1, mod3.shape[-1]), lambda i, j: (i, 0, 0))
    else:
        out_shape = [sds(ATT_KV_WIDTH, BF16), vt_sds, sds(HG_WIDTH, BF16), sds(HG_WIDTH, BF16),
                     sds(HG_WIDTH, F32), sds(HG_WIDTH, BF16), sds(HG_WIDTH, F32)]
        out_specs = [bspec(ATT_KV_WIDTH), vt_spec, bspec(HG_WIDTH), bspec(HG_WIDTH), bspec(HG_WIDTH),
                     bspec(HG_WIDTH), bspec(HG_WIDTH)]
        mod_spec = pl.BlockSpec((1, 1, mod3.shape[-1]), lambda i, j: (mod_row, 0, 0))
    table_spec = pl.BlockSpec((tm, LANES), lambda i, j: (j, 0))
    return pl.pallas_call(
        functools.partial(_inproj_kernel, latent=latent),
        out_shape=out_shape,
        grid=(b, nt),
        in_specs=[bspec(d), mod_spec, full(g1), full(w_in), full(gq), full(gk), full(lb_raw),
                  table_spec, table_spec],
        out_specs=out_specs,
        compiler_params=pltpu.CompilerParams(dimension_semantics=("parallel", "parallel"),
                                             vmem_limit_bytes=VMEM_LIMIT),
        name="inproj_latent" if latent else "inproj_context",
    )(rows, mod3, g1, w_in, gq, gk, lb_raw, cos, sin)


def _attention_schedule(q_ref, kc_ref, vtc_ref, kx_ref, vtx_ref, o_ref, sc_ref, s_ref, acc_ref):
    tq = q_ref.shape[1]
    n_kv = kx_ref.shape[1] // KV_TILE
    nt_dims = (((1,), (1,)), ((), ()))
    cols = ATT_GROUP * tq

    qgs = [jnp.concatenate(
        [q_ref[0, :, (kv * ATT_GROUP + j) * LANES:(kv * ATT_GROUP + j + 1) * LANES] for j in range(ATT_GROUP)],
        axis=0) for kv in range(ATT_KV_HEADS)]

    def scores(kt, dst):
        for kv in range(ATT_KV_HEADS):
            dst(kv, lax.dot_general(kt, qgs[kv], nt_dims, preferred_element_type=F32))

    def consume(src, vt_all, maxima):
        ones = jnp.ones((ONES_ROWS, vt_all.shape[1]), BF16)
        out = []
        for kv in range(ATT_KV_HEADS):
            m = maxima[kv]
            m_new = jnp.maximum(m, jnp.max(src(kv), axis=0, keepdims=True))
            alpha = jnp.exp2(m - m_new)
            p = jnp.exp2(src(kv) - m_new).astype(BF16)
            vt = jnp.concatenate([vt_all[kv * ATT_HEAD_DIM:(kv + 1) * ATT_HEAD_DIM], ones], axis=0)
            acc_ref[kv] = alpha * acc_ref[kv] + jnp.dot(vt, p, preferred_element_type=F32)
            out.append(m_new)
        return tuple(out)

    def put_ctx(kv, val):
        sc_ref[kv] = val

    def put(slot):
        def dst(kv, val):
            s_ref[slot, kv] = val
        return dst

    def k_tile(j):
        return kx_ref[0, j * KV_TILE:(j + 1) * KV_TILE, :]

    def vt_tile(j):
        return vtx_ref[0, :, j * KV_TILE:(j + 1) * KV_TILE]

    acc_ref[...] = jnp.zeros(acc_ref.shape, F32)
    scores(kc_ref[0], put_ctx)
    for j in range(min(KV_AHEAD, n_kv)):
        scores(k_tile(j), put(j % KV_SLOTS))
    maxima = consume(lambda kv: sc_ref[kv], vtc_ref[0], (jnp.full((1, cols), -jnp.inf, F32),) * ATT_KV_HEADS)
    for j in range(n_kv):
        if j + KV_AHEAD < n_kv:
            scores(k_tile(j + KV_AHEAD), put((j + KV_AHEAD) % KV_SLOTS))
        maxima = consume(lambda kv, slot=j % KV_SLOTS: s_ref[slot, kv], vt_tile(j), maxima)
        yield

    for kv in range(ATT_KV_HEADS):
        o = acc_ref[kv, 0:ATT_HEAD_DIM] / acc_ref[kv, ATT_HEAD_DIM:ATT_HEAD_DIM + 1]
        for jj in range(ATT_GROUP // 2):
            pair = jnp.concatenate([o[:, (2 * jj) * tq:(2 * jj + 1) * tq],
                                    o[:, (2 * jj + 1) * tq:(2 * jj + 2) * tq]], axis=0)
            col = kv * (ATT_GROUP // 2) + jj
            o_ref[0, :, col * LANES:(col + 1) * LANES] = pair.T.astype(BF16)


NT_DIMS = (((1,), (1,)), ((), ()))


def _zero_rows(n):
    return [jnp.zeros((n, LANES), F32)] if n else []


def _scan_window(st, q, k, v, b, b_row, msk_ref, reverse, want_out):
    nsub = WIN // SUB
    last = 0 if reverse else SUB - 1
    sub_of = (lambda p: nsub - 1 - p) if reverse else (lambda p: p)
    blk = lambda a, i: a[i * SUB:(i + 1) * SUB]
    d = int(reverse)

    b_last = [b_row(i * SUB + last) for i in range(nsub)]
    cum = []
    for p in range(nsub):
        cum.append(b_last[sub_of(p)] if p == 0 else cum[-1] + b_last[sub_of(p)])
    ks = [blk(k, i) * jnp.exp2(b_last[i] - blk(b, i)) for i in range(nsub)]

    k_end = jnp.concatenate(
        [ks[i] if sub_of(i) == nsub - 1 else ks[i] * jnp.exp2(cum[nsub - 1] - cum[sub_of(i)]) for i in range(nsub)],
        axis=0)
    st_new = jnp.exp2(cum[nsub - 1]) * st + jnp.dot(v.T, k_end, preferred_element_type=F32)
    if not want_out:
        return st_new, None

    qs = [blk(q, i) * jnp.exp2(blk(b, i)) for i in range(nsub)]
    q_start = jnp.concatenate(
        [qs[i] if sub_of(i) == 0 else qs[i] * jnp.exp2(cum[sub_of(i) - 1]) for i in range(nsub)], axis=0)
    o_carried = lax.dot_general(q_start, st, NT_DIMS, preferred_element_type=F32)

    rows = [[] for _ in range(nsub)]
    for h in (nsub // 2, nsub // 4, nsub // 8):
        for ub in range(1, nsub // h, 2):
            ref = ub * h - 1
            upper = {sub_of(p): p for p in range(ub * h, (ub + 1) * h)}
            lower = {sub_of(p): p for p in range((ub - 1) * h, ub * h)}
            iu, il = sorted(upper), sorted(lower)
            lhs = jnp.concatenate(
                [qs[i] if upper[i] - 1 == ref else qs[i] * jnp.exp2(cum[upper[i] - 1] - cum[ref]) for i in iu], axis=0)
            rhs = jnp.concatenate(
                _zero_rows(il[0] * SUB)
                + [ks[i] if lower[i] == ref else ks[i] * jnp.exp2(cum[ref] - cum[lower[i]]) for i in il]
                + _zero_rows((nsub - 1 - il[-1]) * SUB), axis=0)
            c = lax.dot_general(lhs, rhs, NT_DIMS, preferred_element_type=F32)
            for n, i in enumerate(iu):
                rows[i].append(c[n * SUB:(n + 1) * SUB])

    def level_scores(level, q_exponent, k_exponent):
        q_l = q * jnp.exp2(jnp.minimum(q_exponent, 0.0))
        k_l = k if k_exponent is None else k * jnp.exp2(jnp.minimum(k_exponent, 0.0))
        return lax.dot_general(q_l, k_l, NT_DIMS, preferred_element_type=F32) * msk_ref[d, level]

    scores = jnp.dot(q * k, jnp.ones((LANES, LANES), BF16), preferred_element_type=F32) * msk_ref[d, N_FINE]
    for level, h in ((0, SUB // 2), (1, SUB // 4)):
        ref_rows = [b_row(s * 2 * h + (h if reverse else h - 1)) for s in range(WIN // (2 * h))]
        diff = b - jnp.concatenate([jnp.broadcast_to(r, (2 * h, LANES)) for r in ref_rows], axis=0)
        decay = jnp.exp2(jnp.minimum(diff, -diff))
        scores = scores + (lax.dot_general(q * decay, k * decay, NT_DIMS, preferred_element_type=F32)
                           * msk_ref[d, level])
    before = lambda n: pltpu.roll(b, n, 0)
    after = lambda n: pltpu.roll(b, WIN - n, 0)
    off4 = lax.broadcasted_iota(jnp.int32, (WIN, LANES), 0) % 4
    if reverse:
        q_exp = b - jnp.where(off4 == 0, after(2), after(1))
        k_exp = jnp.where(off4 == 2, 0.0, before(1) - b)
    else:
        q_exp = b - jnp.where(off4 == 3, before(2), before(1))
        k_exp = jnp.where(off4 == 1, 0.0, after(1) - b)
    scores = scores + level_scores(2, q_exp, k_exp)
    scores = scores + level_scores(3, b - (after(1) if reverse else before(1)), None)
    a_rows = []
    for i in range(nsub):
        a = blk(scores, i)
        for c in rows[i]:
            a = a + c
        a_rows.append(a)
    scores = jnp.concatenate(a_rows, axis=0)
    return st_new, (scores.astype(BF16), o_carried)


N_FINE = 4


def _pair_masks():
    msk = np.zeros((2, N_FINE + 1, WIN, WIN), np.float32)
    pos = np.arange(WIN)
    for d in range(2):
        for level in range(N_FINE):
            h = SUB >> (level + 1)
            span, off = pos // (2 * h), pos % (2 * h)
            later = (off < h) if d else (off >= h)
            msk[d, level] = (span[:, None] == span[None, :]) & later[:, None] & ~later[None, :]
        msk[d, N_FINE] = np.eye(WIN)
    return jnp.asarray(msk, F32)


def _hgrn_steps(q_ref, v_ref, kf_ref, bf_ref, kb_ref, bb_ref, vc_ref, kfc_ref, bfc_ref, kbc_ref, bbc_ref,
                msk_ref, o_ref, stf_ref, stb_ref, scf_ref, scb_ref, ocf_ref, ocb_ref, w0, n_steps):
    t = q_ref.shape[1]
    ctx = vc_ref.shape[1]
    nw = t // WIN

    def window(refs, start, st, reverse, want_out):
        qr, vr, kr, br = refs
        sl = pl.ds(start, WIN)
        vw = vr[0, sl, :].astype(F32)
        kw = kr[0, sl, :].astype(F32)
        bw = br[0, sl, :]
        qw = qr[0, sl, :].astype(F32) if want_out else None
        b_row = lambda r: br[0, pl.ds(start + r, 1), :]
        return _scan_window(st, qw, kw, vw, bw, b_row, msk_ref, reverse, want_out)

    def restart():
        zero = jnp.zeros((LANES, LANES), F32)
        stf, stb = zero, zero
        n_cw = ctx // WIN
        for w in range(n_cw):
            stf, _ = window((None, vc_ref, kfc_ref, bfc_ref), w * WIN, stf, False, False)
            stb, _ = window((None, vc_ref, kbc_ref, bbc_ref), (n_cw - 1 - w) * WIN, stb, True, False)
        stf_ref[...] = stf
        stb_ref[...] = stb
        o_ref[...] = jnp.zeros(o_ref.shape, F32)

    def starts(w):
        return pl.multiple_of(w * WIN, WIN), pl.multiple_of((nw - 1 - w) * WIN, WIN)

    def scores_stage(w):
        fs, bs = starts(w)
        stf, (sf, cf) = window((q_ref, v_ref, kf_ref, bf_ref), fs, stf_ref[...], False, True)
        stf_ref[...] = stf
        scf_ref[...] = sf
        ocf_ref[...] = cf
        stb, (sb, cb) = window((q_ref, v_ref, kb_ref, bb_ref), bs, stb_ref[...], True, True)
        stb_ref[...] = stb
        scb_ref[...] = sb
        ocb_ref[...] = cb

    def output_stage(w):
        fs, bs = starts(w)
        o_ref[0, pl.ds(fs, WIN), :] += (
            jnp.dot(scf_ref[...], v_ref[0, pl.ds(fs, WIN), :], preferred_element_type=F32) + ocf_ref[...])
        o_ref[0, pl.ds(bs, WIN), :] += (
            jnp.dot(scb_ref[...], v_ref[0, pl.ds(bs, WIN), :], preferred_element_type=F32) + ocb_ref[...])

    def schedule():
        for u in range(n_steps + 1):
            if u > 0:
                output_stage(w0 + u - 1)
            if u < n_steps:
                scores_stage(w0 + u)
            yield

    return restart, schedule()


def _mixer_kernel(qa_ref, kc_ref, vtc_ref, kx_ref, vtx_ref,
                  hq_ref, hv_ref, kf_ref, bf_ref, kb_ref, bb_ref, hvc_ref, kfc_ref, bfc_ref, kbc_ref, bbc_ref,
                  msk_ref, att_ref, hg_ref, sc_ref, s_ref, acc_ref, *hg_scratch, steps_per_head):
    nw = hq_ref.shape[1] // WIN
    n_steps = nw // steps_per_head
    phase = pl.program_id(1) % steps_per_head
    restart, scan = _hgrn_steps(hq_ref, hv_ref, kf_ref, bf_ref, kb_ref, bb_ref,
                                hvc_ref, kfc_ref, bfc_ref, kbc_ref, bbc_ref, msk_ref, hg_ref, *hg_scratch,
                                w0=phase * n_steps, n_steps=n_steps)
    pl.when(phase == 0)(restart)

    for _ in _attention_schedule(qa_ref, kc_ref, vtc_ref, kx_ref, vtx_ref, att_ref, sc_ref, s_ref, acc_ref):
        next(scan, None)
    for _ in scan:
        pass


def _mixers(qa, kc, vtc, kx, vtx, hq, hv, kf, bf, kb, bb, hvc, kfc, bfc, kbc, bbc):
    b, t, _ = qa.shape
    ctx = hvc.shape[1]
    tq = min(Q_TILE, t)
    n_q = t // tq
    nw = t // WIN
    assert tq % LANES == 0 and t % tq == 0 and t % KV_TILE == 0 and KV_AHEAD < KV_SLOTS
    assert n_q % HG_HEADS == 0 and nw % (n_q // HG_HEADS) == 0
    steps_per_head = n_q // HG_HEADS
    cols = ATT_GROUP * tq
    whole = lambda a: pl.BlockSpec((1,) + a.shape[1:], lambda i, j: (i, 0, 0))
    lat = pl.BlockSpec((1, t, HG_HEAD_DIM), lambda i, j: (i, 0, j // steps_per_head))
    con = pl.BlockSpec((1, ctx, HG_HEAD_DIM), lambda i, j: (i, 0, j // steps_per_head))
    msk = _pair_masks()
    const = lambda a: pl.BlockSpec(a.shape, lambda i, j: (0,) * a.ndim)
    return pl.pallas_call(
        functools.partial(_mixer_kernel, steps_per_head=steps_per_head),
        out_shape=[jax.ShapeDtypeStruct((b, t, ATT_WIDTH), BF16), jax.ShapeDtypeStruct((b, t, HG_WIDTH), F32)],
        grid=(b, n_q),
        in_specs=[pl.BlockSpec((1, tq, qa.shape[-1]), lambda i, j: (i, j, 0)),
                  whole(kc), whole(vtc), whole(kx), whole(vtx)] + [lat] * 6 + [con] * 5
                 + [const(msk)],
        out_specs=[pl.BlockSpec((1, tq, ATT_WIDTH), lambda i, j: (i, j, 0)), lat],
        scratch_shapes=[pltpu.VMEM((ATT_KV_HEADS, kc.shape[1], cols), F32),
                        pltpu.VMEM((KV_SLOTS, ATT_KV_HEADS, KV_TILE, cols), F32),
                        pltpu.VMEM((ATT_KV_HEADS, ATT_HEAD_DIM + ONES_ROWS, cols), F32),
                        pltpu.VMEM((LANES, LANES), F32), pltpu.VMEM((LANES, LANES), F32),
                        pltpu.VMEM((WIN, WIN), BF16), pltpu.VMEM((WIN, WIN), BF16),
                        pltpu.VMEM((WIN, LANES), F32), pltpu.VMEM((WIN, LANES), F32)],
        compiler_params=pltpu.CompilerParams(dimension_semantics=("arbitrary", "arbitrary"),
                                             vmem_limit_bytes=VMEM_LIMIT),
        name="mixers",
    )(qa, kc, vtc, kx, vtx, hq, hv, kf, bf, kb, bb, hvc, kfc, bfc, kbc, bbc, msk)


def _out_ffn_kernel(x_ref, att_ref, hg_ref, sg_ref, mod_ref, ghg_ref, wo_ref, g2_ref, wgu_ref, wd_ref, gf_ref,
                    o_ref):
    d = x_ref.shape[-1]
    dff = wd_ref.shape[0]
    gate1 = mod_ref[0, :, 2 * d:3 * d]
    shift2 = mod_ref[0, :, 3 * d:4 * d]
    scale2 = mod_ref[0, :, 4 * d:5 * d]
    gate2 = mod_ref[0, :, 5 * d:6 * d]

    parts = [slice(r0, r0 + OUT_PART_ROWS) for r0 in range(0, x_ref.shape[1], OUT_PART_ROWS)]

    def mixed(rows):
        heads = []
        for hd in range(HG_HEADS):
            cs = slice(hd * HG_HEAD_DIM, (hd + 1) * HG_HEAD_DIM)
            o = _rms_rows(hg_ref[0, rows, cs]) * ghg_ref[...]
            heads.append((o * sg_ref[0, rows, cs].astype(F32)).astype(BF16))
        mix = (jnp.dot(att_ref[0, rows, :], wo_ref[0:ATT_WIDTH, :], preferred_element_type=F32)
               + jnp.dot(jnp.concatenate(heads, axis=1), wo_ref[ATT_WIDTH:ATT_WIDTH + HG_WIDTH, :],
                         preferred_element_type=F32))
        x1 = x_ref[0, rows, :] + gate1 * mix
        return x1, ((_rms_rows(x1) * g2_ref[...]) * (1.0 + scale2) + shift2).astype(BF16)

    def gate_up(h):
        return (jnp.dot(h, wgu_ref[:, 0:dff], preferred_element_type=F32),
                jnp.dot(h, wgu_ref[:, dff:2 * dff], preferred_element_type=F32))

    def down(a, g):
        return jnp.dot((_silu(a) * g).astype(BF16), wd_ref[...], preferred_element_type=F32)

    firsts = [mixed(rows) for rows in parts]
    ups = [gate_up(h) for _, h in firsts]
    ys = [down(a, g) for a, g in ups]
    for rows, (x1, _), y in zip(parts, firsts, ys):
        o_ref[0, rows, :] = _rms_rows(x1 + gate2 * y) * gf_ref[...]


def _out_ffn(x, att, hg, sg, mod3, ghg, w_out, g2, w_gu, w_down, gf):
    b, t, d = x.shape
    tm = min(OUT_ROW_TILE, t)
    assert tm % OUT_PART_ROWS == 0
    bspec = lambda width: pl.BlockSpec((1, tm, width), lambda i, j: (i, j, 0))
    full = lambda a: pl.BlockSpec(a.shape, lambda i, j: (0,) * a.ndim, pipeline_mode=pl.Buffered(1))
    return pl.pallas_call(
        _out_ffn_kernel,
        out_shape=jax.ShapeDtypeStruct((b, t, d), F32),
        grid=(b, t // tm),
        in_specs=[bspec(d), bspec(ATT_WIDTH), bspec(HG_WIDTH), bspec(HG_WIDTH),
                  pl.BlockSpec((1, 1, mod3.shape[-1]), lambda i, j: (i, 0, 0)),
                  full(ghg), full(w_out), full(g2), full(w_gu), full(w_down), full(gf)],
        out_specs=bspec(d),
        compiler_params=pltpu.CompilerParams(dimension_semantics=("parallel", "parallel"),
                                             vmem_limit_bytes=VMEM_LIMIT),
        name="out_ffn",
    )(x, att, hg, sg, mod3, ghg, w_out, g2, w_gu, w_down, gf)


def _rope_tables(t):
    n_rows = t // GRID_W
    row = np.repeat(np.arange(n_rows, dtype=np.float32), GRID_W)
    col = np.tile(np.arange(GRID_W, dtype=np.float32), n_rows)
    inv = (np.float32(ROPE_THETA) ** (-np.arange(0, AXIS_DIM, 2, dtype=np.float32) / np.float32(AXIS_DIM)))
    inv = inv.astype(np.float32)
    ar = row[:, None] * inv
    ac = col[:, None] * inv
    ang = np.concatenate([ar, ar, ac, ac], axis=-1).astype(np.float32)
    sign = np.where((np.arange(ATT_HEAD_DIM) % AXIS_DIM) < AXIS_DIM // 2, -1.0, 1.0)
    reps = LANES // ATT_HEAD_DIM
    cos = np.tile(np.cos(ang.astype(np.float64)), (1, reps)).astype(np.float32)
    sin = np.tile(np.sin(ang.astype(np.float64)) * sign, (1, reps)).astype(np.float32)
    return jnp.asarray(cos), jnp.asarray(sin)


def kernel(x, c, ctx, c_ctx, w_mod, b_mod, g_norm1, w_in, g_q, g_k, lb_raw, g_hg, w_out, g_norm2, w_gu, w_down,
           g_final):
    b, t, d = x.shape
    assert w_mod.shape[0] == 1 and lb_raw.shape[0] == 2, "single-layer block"
    assert t % WIN == 0 and (t // WIN) % 2 == 0 and ctx.shape[1] % WIN == 0 and t % KV_TILE == 0

    pad = (-(b + 1)) % 8
    cc = jnp.concatenate([c, c_ctx[None, :], jnp.zeros((pad, d), F32)], axis=0)
    mods = _modulation(cc, w_mod[0], b_mod)
    mod3 = mods.reshape(mods.shape[0], 1, mods.shape[1])

    cos, sin = _rope_tables(t)
    reps = LANES // ATT_HEAD_DIM
    gq = jnp.tile(g_q[0], reps)[None, :]
    gk = jnp.tile(g_k[0], reps)[None, :]
    w_in_b = w_in[0].astype(BF16)

    qa, kx, vtx, hq, hv, kf, bf, kb, bb, sg = _input_projection(
        x, mod3, None, g_norm1, w_in_b, gq, gk, lb_raw, cos, sin, True)
    kc, vtc, hvc, kfc, bfc, kbc, bbc = _input_projection(
        ctx, mod3, b, g_norm1, w_in_b, gq, gk, lb_raw, cos, sin, False)

    att, hg = _mixers(qa, kc, vtc, kx, vtx, hq, hv, kf, bf, kb, bb, hvc, kfc, bfc, kbc, bbc)
    return _out_ffn(x, att, hg, sg, mod3, g_hg, w_out[0].astype(BF16), g_norm2, w_gu[0].astype(BF16),
                    w_down[0].astype(BF16), g_final[None, :])
```

```python
import functools

import jax
import jax.numpy as jnp
import numpy as np
from jax import lax
from jax.experimental import pallas as pl
from jax.experimental.pallas import tpu as pltpu

F32 = jnp.float32
BF16 = jnp.bfloat16

LANES = 128
GRID_W = 64
ATT_HEADS = 8
ATT_KV_HEADS = 2
ATT_GROUP = ATT_HEADS // ATT_KV_HEADS
ATT_HEAD_DIM = 64
ATT_WIDTH = ATT_HEADS * ATT_HEAD_DIM
ATT_KV_WIDTH = ATT_KV_HEADS * ATT_HEAD_DIM
AXIS_DIM = ATT_HEAD_DIM // 2
ROPE_THETA = 10000.0
HG_HEADS = 4
HG_HEAD_DIM = 128
HG_WIDTH = HG_HEADS * HG_HEAD_DIM
EPS = 1e-6

MOD_COLUMN_BLOCKS = 4
SUB = 16
WIN = 128
ROW_TILE = 256
OUT_ROW_TILE = 512
OUT_PART_ROWS = 256
Q_TILE = 256
KV_TILE = 512
KV_SLOTS = 4
KV_AHEAD = 2
ONES_ROWS = 16
LOG2E = 1.4426950408889634
VMEM_LIMIT = 56 * 1024 * 1024

C_AQ = 0
C_AK = C_AQ + ATT_WIDTH
C_AV = C_AK + ATT_KV_WIDTH
C_HQ = C_AV + ATT_KV_WIDTH
C_HI = C_HQ + HG_WIDTH
C_FF = C_HI + HG_WIDTH
C_FB = C_FF + HG_WIDTH
C_HG = C_FB + HG_WIDTH


def _silu(v):
    return v * jax.nn.sigmoid(v)


def _rms_rows(v):
    return v * lax.rsqrt(jnp.mean(v * v, axis=-1, keepdims=True) + EPS)


def _mod_kernel(c_ref, w_ref, b_ref, o_ref):
    a = _silu(c_ref[...])
    o_ref[...] = jnp.dot(a, w_ref[...], precision=lax.Precision.HIGHEST,
                         preferred_element_type=F32) + b_ref[...]


def _modulation(cc, w_mod, b_mod):
    rows, d = cc.shape
    n = w_mod.shape[1]
    bn = n // MOD_COLUMN_BLOCKS
    return pl.pallas_call(
        _mod_kernel,
        out_shape=jax.ShapeDtypeStruct((rows, n), F32),
        grid=(n // bn,),
        in_specs=[pl.BlockSpec((rows, d), lambda j: (0, 0)),
                  pl.BlockSpec((d, bn), lambda j: (0, j)),
                  pl.BlockSpec((1, bn), lambda j: (0, j))],
        out_specs=pl.BlockSpec((rows, bn), lambda j: (0, j)),
        compiler_params=pltpu.CompilerParams(dimension_semantics=("arbitrary",),
                                             vmem_limit_bytes=VMEM_LIMIT),
        name="modulation",
    )(cc, w_mod, b_mod)


def _head64_meansq(p, seg2):
    sq = p * p
    hi = sq.astype(BF16)
    lo = (sq - hi.astype(F32)).astype(BF16)
    s = jnp.dot(jnp.concatenate([hi, lo], axis=1), seg2, preferred_element_type=F32)
    return s * (1.0 / ATT_HEAD_DIM)


def _rope(v, cos, sin_signed, first_half):
    fwd = pltpu.roll(v, LANES - AXIS_DIM // 2, 1)
    bwd = pltpu.roll(v, AXIS_DIM // 2, 1)
    return v * cos + jnp.where(first_half, fwd, bwd) * sin_signed


def _gate_prep(f, lb, suffix):
    fg = lb + (1.0 - lb) * jax.nn.sigmoid(f)
    cum = jnp.log2(fg)
    rows = cum.shape[0]
    pos = lax.broadcasted_iota(jnp.int32, cum.shape, 0) % SUB
    shift = 1
    while shift < SUB:
        if suffix:
            moved = pltpu.roll(cum, rows - shift, 0)
            cum = cum + jnp.where(pos < SUB - shift, moved, 0.0)
        else:
            moved = pltpu.roll(cum, shift, 0)
            cum = cum + jnp.where(pos >= shift, moved, 0.0)
        shift *= 2
    return (1.0 - fg).astype(BF16), cum


def _inproj_kernel(x_ref, mod_ref, g1_ref, w_ref, gq_ref, gk_ref, lb_ref, cos_ref, sin_ref, *out_refs, latent):
    d = x_ref.shape[-1]
    x = x_ref[0]
    shift = mod_ref[0, :, 0:d]
    scale = mod_ref[0, :, d:2 * d]
    h = (_rms_rows(x) * g1_ref[...]) * (1.0 + scale) + shift
    hb = h.astype(BF16)

    def proj(c0, width):
        return jnp.dot(hb, w_ref[:, c0:c0 + width], preferred_element_type=F32)

    lane = lax.broadcasted_iota(jnp.int32, (2 * LANES, LANES), 0) % LANES
    lane_c = lax.broadcasted_iota(jnp.int32, (2 * LANES, LANES), 1)
    seg = (lane // ATT_HEAD_DIM == lane_c // ATT_HEAD_DIM).astype(BF16)
    row_lane = lax.broadcasted_iota(jnp.int32, (x.shape[0], LANES), 1)
    first_half = (row_lane % AXIS_DIM) < (AXIS_DIM // 2)
    upper = row_lane >= ATT_HEAD_DIM

    r0 = lb_ref[0]
    r1 = lb_ref[1]
    rm = jnp.maximum(r0, r1)
    e0 = jnp.exp(r0 - rm)
    e1 = jnp.exp(r1 - rm)
    lb_all = e0 / (e0 + e1)

    if latent:
        (qa_ref, ka_ref, vat_ref, hq_ref, hv_ref, kf_ref, bf_ref, kb_ref, bb_ref, sg_ref) = out_refs
        cos = cos_ref[...]
        sin = sin_ref[...]
    else:
        (ka_ref, vat_ref, hv_ref, kf_ref, bf_ref, kb_ref, bb_ref) = out_refs

    p_ff = proj(C_FF, HG_WIDTH)
    if latent:
        pq = proj(C_AQ, ATT_WIDTH)
    pk = proj(C_AK, ATT_KV_WIDTH)
    pv = proj(C_AV, ATT_KV_WIDTH)
    p_fb = proj(C_FB, HG_WIDTH)
    if latent:
        p_hq = proj(C_HQ, HG_WIDTH)
    p_hv = proj(C_HI, HG_WIDTH)
    ms_k = _head64_meansq(pk, seg)
    if latent:
        ms_q = [_head64_meansq(pq[:, c * LANES:(c + 1) * LANES], seg) for c in range(ATT_WIDTH // LANES)]

    kn = pk * lax.rsqrt(ms_k + EPS) * gk_ref[...]
    if latent:
        kn = _rope(kn, cos, sin, first_half)
    ka_ref[0] = kn.astype(BF16)
    vat_ref[0] = pv.T.astype(BF16)

    if latent:
        for c in range(ATT_WIDTH // LANES):
            t = pq[:, c * LANES:(c + 1) * LANES]
            t = t * lax.rsqrt(ms_q[c] + EPS) * gq_ref[...]
            t = _rope(t, cos, sin, first_half) * (ATT_HEAD_DIM ** -0.5 * LOG2E)
            t_swapped = pltpu.roll(t, ATT_HEAD_DIM, 1)
            for half in range(2):
                head = 2 * c + half
                kv = head // ATT_GROUP
                src = t if half == kv else t_swapped
                keep = upper if kv == 1 else jnp.logical_not(upper)
                qa_ref[0, :, head * LANES:(head + 1) * LANES] = jnp.where(keep, src, 0.0).astype(BF16)
        hq_ref[0] = (_silu(p_hq) * (HG_HEAD_DIM ** -0.5)).astype(BF16)
        sg_ref[0] = _silu(proj(C_HG, HG_WIDTH)).astype(BF16)

    hv_ref[0] = p_hv.astype(BF16)
    kf, bf = _gate_prep(p_ff, lb_all[0:1, :], False)
    kf_ref[0] = kf
    bf_ref[0] = bf
    kb, bb = _gate_prep(p_fb, lb_all[1:2, :], True)
    kb_ref[0] = kb
    bb_ref[0] = bb


def _input_projection(rows, mod3, mod_row, g1, w_in, gq, gk, lb_raw, cos, sin, latent):
    b, t, d = rows.shape
    tm = min(ROW_TILE, t)
    nt = t // tm
    bspec = lambda width: pl.BlockSpec((1, tm, width), lambda i, j: (i, j, 0))
    full = lambda a: pl.BlockSpec(a.shape, lambda i, j: (0,) * a.ndim)
    sds = lambda width, dt: jax.ShapeDtypeStruct((b, t, width), dt)
    vt_sds = jax.ShapeDtypeStruct((b, ATT_KV_WIDTH, t), BF16)
    vt_spec = pl.BlockSpec((1, ATT_KV_WIDTH, tm), lambda i, j: (i, 0, j))
    if latent:
        out_shape = [sds(ATT_HEADS * LANES, BF16), sds(ATT_KV_WIDTH, BF16), vt_sds,
                     sds(HG_WIDTH, BF16), sds(HG_WIDTH, BF16), sds(HG_WIDTH, BF16), sds(HG_WIDTH, F32),
                     sds(HG_WIDTH, BF16), sds(HG_WIDTH, F32), sds(HG_WIDTH, BF16)]
        out_specs = [bspec(ATT_HEADS * LANES), bspec(ATT_KV_WIDTH), vt_spec, bspec(HG_WIDTH),
                     bspec(HG_WIDTH), bspec(HG_WIDTH), bspec(HG_WIDTH), bspec(HG_WIDTH), bspec(HG_WIDTH),
                     bspec(HG_WIDTH)]
        mod_spec = pl.BlockSpec((1, 1, mod3.shape[-1]), lambda i, j: (i, 0, 0))
    else:
        out_shape = [sds(ATT_KV_WIDTH, BF16), vt_sds, sds(HG_WIDTH, BF16), sds(HG_WIDTH, BF16),
                     sds(HG_WIDTH, F32), sds(HG_WIDTH, BF16), sds(HG_WIDTH, F32)]
        out_specs = [bspec(ATT_KV_WIDTH), vt_spec, bspec(HG_WIDTH), bspec(HG_WIDTH), bspec(HG_WIDTH),
                     bspec(HG_WIDTH), bspec(HG_WIDTH)]
        mod_spec = pl.BlockSpec((1, 1, mod3.shape[-1]), lambda i, j: (mod_row, 0, 0))
    table_spec = pl.BlockSpec((tm, LANES), lambda i, j: (j, 0))
    return pl.pallas_call(
        functools.partial(_inproj_kernel, latent=latent),
        out_shape=out_shape,
        grid=(b, nt),
        in_specs=[bspec(d), mod_spec, full(g1), full(w_in), full(gq), full(gk), full(lb_raw),
                  table_spec, table_spec],
        out_specs=out_specs,
        compiler_params=pltpu.CompilerParams(dimension_semantics=("parallel", "parallel"),
                                             vmem_limit_bytes=VMEM_LIMIT),
        name="inproj_latent" if latent else "inproj_context",
    )(rows, mod3, g1, w_in, gq, gk, lb_raw, cos, sin)


def _attention_schedule(q_ref, kc_ref, vtc_ref, kx_ref, vtx_ref, o_ref, sc_ref, s_ref, acc_ref):
    tq = q_ref.shape[1]
    n_kv = kx_ref.shape[1] // KV_TILE
    nt_dims = (((1,), (1,)), ((), ()))
    cols = ATT_GROUP * tq

    qgs = [jnp.concatenate(
        [q_ref[0, :, (kv * ATT_GROUP + j) * LANES:(kv * ATT_GROUP + j + 1) * LANES] for j in range(ATT_GROUP)],
        axis=0) for kv in range(ATT_KV_HEADS)]

    def scores(kt, dst):
        for kv in range(ATT_KV_HEADS):
            dst(kv, lax.dot_general(kt, qgs[kv], nt_dims, preferred_element_type=F32))

    def consume(src, vt_all, maxima):
        ones = jnp.ones((ONES_ROWS, vt_all.shape[1]), BF16)
        out = []
        for kv in range(ATT_KV_HEADS):
            m = maxima[kv]
            m_new = jnp.maximum(m, jnp.max(src(kv), axis=0, keepdims=True))
            alpha = jnp.exp2(m - m_new)
            p = jnp.exp2(src(kv) - m_new).astype(BF16)
            vt = jnp.concatenate([vt_all[kv * ATT_HEAD_DIM:(kv + 1) * ATT_HEAD_DIM], ones], axis=0)
            acc_ref[kv] = alpha * acc_ref[kv] + jnp.dot(vt, p, preferred_element_type=F32)
            out.append(m_new)
        return tuple(out)

    def put_ctx(kv, val):
        sc_ref[kv] = val

    def put(slot):
        def dst(kv, val):
            s_ref[slot, kv] = val
        return dst

    def k_tile(j):
        return kx_ref[0, j * KV_TILE:(j + 1) * KV_TILE, :]

    def vt_tile(j):
        return vtx_ref[0, :, j * KV_TILE:(j + 1) * KV_TILE]

    acc_ref[...] = jnp.zeros(acc_ref.shape, F32)
    scores(kc_ref[0], put_ctx)
    for j in range(min(KV_AHEAD, n_kv)):
        scores(k_tile(j), put(j % KV_SLOTS))
    maxima = consume(lambda kv: sc_ref[kv], vtc_ref[0], (jnp.full((1, cols), -jnp.inf, F32),) * ATT_KV_HEADS)
    for j in range(n_kv):
        if j + KV_AHEAD < n_kv:
            scores(k_tile(j + KV_AHEAD), put((j + KV_AHEAD) % KV_SLOTS))
        maxima = consume(lambda kv, slot=j % KV_SLOTS: s_ref[slot, kv], vt_tile(j), maxima)
        yield

    for kv in range(ATT_KV_HEADS):
        o = acc_ref[kv, 0:ATT_HEAD_DIM] / acc_ref[kv, ATT_HEAD_DIM:ATT_HEAD_DIM + 1]
        for jj in range(ATT_GROUP // 2):
            pair = jnp.concatenate([o[:, (2 * jj) * tq:(2 * jj + 1) * tq],
                                    o[:, (2 * jj + 1) * tq:(2 * jj + 2) * tq]], axis=0)
            col = kv * (ATT_GROUP // 2) + jj
            o_ref[0, :, col * LANES:(col + 1) * LANES] = pair.T.astype(BF16)


NT_DIMS = (((1,), (1,)), ((), ()))


def _scan_window(st, q, k, v, b, b_row, msk_ref, reverse, want_out):
    nsub = WIN // SUB
    last = 0 if reverse else SUB - 1
    sub_of = (lambda p: nsub - 1 - p) if reverse else (lambda p: p)
    blk = lambda a, i: a[i * SUB:(i + 1) * SUB]
    d = int(reverse)

    b_last = [b_row(i * SUB + last) for i in range(nsub)]
    cum = []
    for p in range(nsub):
        cum.append(b_last[sub_of(p)] if p == 0 else cum[-1] + b_last[sub_of(p)])
    ks = [blk(k, i) * jnp.exp2(b_last[i] - blk(b, i)) for i in range(nsub)]

    k_end = jnp.concatenate(
        [ks[i] if sub_of(i) == nsub - 1 else ks[i] * jnp.exp2(cum[nsub - 1] - cum[sub_of(i)]) for i in range(nsub)],
        axis=0)
    st_new = jnp.exp2(cum[nsub - 1]) * st + jnp.dot(v.T, k_end, preferred_element_type=F32)
    if not want_out:
        return st_new, None

    qs = [blk(q, i) * jnp.exp2(blk(b, i)) for i in range(nsub)]
    q_start = jnp.concatenate(
        [qs[i] if sub_of(i) == 0 else qs[i] * jnp.exp2(cum[sub_of(i) - 1]) for i in range(nsub)], axis=0)
    o_carried = lax.dot_general(q_start, st, NT_DIMS, preferred_element_type=F32)

    scores = None
    for level, h in enumerate((nsub // 2, nsub // 4, nsub // 8)):
        q_rows, k_rows = [], []
        for i in range(nsub):
            p = sub_of(i)
            ref = (p // (2 * h)) * 2 * h + h - 1
            q_rows.append(qs[i] * jnp.exp2(cum[p - 1] - cum[ref]) if p - 1 > ref else qs[i])
            k_rows.append(ks[i] * jnp.exp2(cum[ref] - cum[p]) if p < ref else ks[i])
        term = (lax.dot_general(jnp.concatenate(q_rows, axis=0), jnp.concatenate(k_rows, axis=0), NT_DIMS,
                                preferred_element_type=F32) * msk_ref[d, level])
        scores = term if scores is None else scores + term

    def level_scores(level, q_exponent, k_exponent):
        q_l = q * jnp.exp2(jnp.minimum(q_exponent, 0.0))
        k_l = k if k_exponent is None else k * jnp.exp2(jnp.minimum(k_exponent, 0.0))
        return lax.dot_general(q_l, k_l, NT_DIMS, preferred_element_type=F32) * msk_ref[d, level]

    scores = scores + (jnp.dot(q * k, jnp.ones((LANES, LANES), BF16), preferred_element_type=F32)
                       * msk_ref[d, N_LEVELS])
    for level, h in ((3, SUB // 2), (4, SUB // 4)):
        ref_rows = [b_row(s * 2 * h + (h if reverse else h - 1)) for s in range(WIN // (2 * h))]
        diff = b - jnp.concatenate([jnp.broadcast_to(r, (2 * h, LANES)) for r in ref_rows], axis=0)
        decay = jnp.exp2(jnp.minimum(diff, -diff))
        scores = scores + (lax.dot_general(q * decay, k * decay, NT_DIMS, preferred_element_type=F32)
                           * msk_ref[d, level])
    before = lambda n: pltpu.roll(b, n, 0)
    after = lambda n: pltpu.roll(b, WIN - n, 0)
    off4 = lax.broadcasted_iota(jnp.int32, (WIN, LANES), 0) % 4
    if reverse:
        q_exp = b - jnp.where(off4 == 0, after(2), after(1))
        k_exp = jnp.where(off4 == 2, 0.0, before(1) - b)
    else:
        q_exp = b - jnp.where(off4 == 3, before(2), before(1))
        k_exp = jnp.where(off4 == 1, 0.0, after(1) - b)
    scores = scores + level_scores(5, q_exp, k_exp)
    scores = scores + level_scores(6, b - (after(1) if reverse else before(1)), None)
    return st_new, (scores.astype(BF16), o_carried)


N_LEVELS = 7


def _pair_masks():
    msk = np.zeros((2, N_LEVELS + 1, WIN, WIN), np.float32)
    pos = np.arange(WIN)
    for d in range(2):
        for level in range(N_LEVELS):
            h = WIN >> (level + 1)
            span, off = pos // (2 * h), pos % (2 * h)
            later = (off < h) if d else (off >= h)
            msk[d, level] = (span[:, None] == span[None, :]) & later[:, None] & ~later[None, :]
        msk[d, N_LEVELS] = np.eye(WIN)
    return jnp.asarray(msk, F32)


def _hgrn_steps(q_ref, v_ref, kf_ref, bf_ref, kb_ref, bb_ref, vc_ref, kfc_ref, bfc_ref, kbc_ref, bbc_ref,
                msk_ref, o_ref, stf_ref, stb_ref, scf_ref, scb_ref, ocf_ref, ocb_ref, w0, n_steps):
    t = q_ref.shape[1]
    ctx = vc_ref.shape[1]
    nw = t // WIN

    def window(refs, start, st, reverse, want_out):
        qr, vr, kr, br = refs
        sl = pl.ds(start, WIN)
        vw = vr[0, sl, :].astype(F32)
        kw = kr[0, sl, :].astype(F32)
        bw = br[0, sl, :]
        qw = qr[0, sl, :].astype(F32) if want_out else None
        b_row = lambda r: br[0, pl.ds(start + r, 1), :]
        return _scan_window(st, qw, kw, vw, bw, b_row, msk_ref, reverse, want_out)

    def restart():
        zero = jnp.zeros((LANES, LANES), F32)
        stf, stb = zero, zero
        n_cw = ctx // WIN
        for w in range(n_cw):
            stf, _ = window((None, vc_ref, kfc_ref, bfc_ref), w * WIN, stf, False, False)
            stb, _ = window((None, vc_ref, kbc_ref, bbc_ref), (n_cw - 1 - w) * WIN, stb, True, False)
        stf_ref[...] = stf
        stb_ref[...] = stb
        o_ref[...] = jnp.zeros(o_ref.shape, F32)

    def starts(w):
        return pl.multiple_of(w * WIN, WIN), pl.multiple_of((nw - 1 - w) * WIN, WIN)

    def scores_stage(w):
        fs, bs = starts(w)
        stf, (sf, cf) = window((q_ref, v_ref, kf_ref, bf_ref), fs, stf_ref[...], False, True)
        stf_ref[...] = stf
        scf_ref[...] = sf
        ocf_ref[...] = cf
        stb, (sb, cb) = window((q_ref, v_ref, kb_ref, bb_ref), bs, stb_ref[...], True, True)
        stb_ref[...] = stb
        scb_ref[...] = sb
        ocb_ref[...] = cb

    def output_stage(w):
        fs, bs = starts(w)
        o_ref[0, pl.ds(fs, WIN), :] += (
            jnp.dot(scf_ref[...], v_ref[0, pl.ds(fs, WIN), :], preferred_element_type=F32) + ocf_ref[...])
        o_ref[0, pl.ds(bs, WIN), :] += (
            jnp.dot(scb_ref[...], v_ref[0, pl.ds(bs, WIN), :], preferred_element_type=F32) + ocb_ref[...])

    def schedule():
        for u in range(n_steps + 1):
            if u > 0:
                output_stage(w0 + u - 1)
            if u < n_steps:
                scores_stage(w0 + u)
            yield

    return restart, schedule()


def _mixer_kernel(qa_ref, kc_ref, vtc_ref, kx_ref, vtx_ref,
                  hq_ref, hv_ref, kf_ref, bf_ref, kb_ref, bb_ref, hvc_ref, kfc_ref, bfc_ref, kbc_ref, bbc_ref,
                  msk_ref, att_ref, hg_ref, sc_ref, s_ref, acc_ref, *hg_scratch, steps_per_head):
    nw = hq_ref.shape[1] // WIN
    n_steps = nw // steps_per_head
    phase = pl.program_id(1) % steps_per_head
    restart, scan = _hgrn_steps(hq_ref, hv_ref, kf_ref, bf_ref, kb_ref, bb_ref,
                                hvc_ref, kfc_ref, bfc_ref, kbc_ref, bbc_ref, msk_ref, hg_ref, *hg_scratch,
                                w0=phase * n_steps, n_steps=n_steps)
    pl.when(phase == 0)(restart)

    for _ in _attention_schedule(qa_ref, kc_ref, vtc_ref, kx_ref, vtx_ref, att_ref, sc_ref, s_ref, acc_ref):
        next(scan, None)
    for _ in scan:
        pass


def _mixers(qa, kc, vtc, kx, vtx, hq, hv, kf, bf, kb, bb, hvc, kfc, bfc, kbc, bbc):
    b, t, _ = qa.shape
    ctx = hvc.shape[1]
    tq = min(Q_TILE, t)
    n_q = t // tq
    nw = t // WIN
    assert tq % LANES == 0 and t % tq == 0 and t % KV_TILE == 0 and KV_AHEAD < KV_SLOTS
    assert n_q % HG_HEADS == 0 and nw % (n_q // HG_HEADS) == 0
    steps_per_head = n_q // HG_HEADS
    cols = ATT_GROUP * tq
    whole = lambda a: pl.BlockSpec((1,) + a.shape[1:], lambda i, j: (i, 0, 0))
    lat = pl.BlockSpec((1, t, HG_HEAD_DIM), lambda i, j: (i, 0, j // steps_per_head))
    con = pl.BlockSpec((1, ctx, HG_HEAD_DIM), lambda i, j: (i, 0, j // steps_per_head))
    msk = _pair_masks()
    const = lambda a: pl.BlockSpec(a.shape, lambda i, j: (0,) * a.ndim)
    return pl.pallas_call(
        functools.partial(_mixer_kernel, steps_per_head=steps_per_head),
        out_shape=[jax.ShapeDtypeStruct((b, t, ATT_WIDTH), BF16), jax.ShapeDtypeStruct((b, t, HG_WIDTH), F32)],
        grid=(b, n_q),
        in_specs=[pl.BlockSpec((1, tq, qa.shape[-1]), lambda i, j: (i, j, 0)),
                  whole(kc), whole(vtc), whole(kx), whole(vtx)] + [lat] * 6 + [con] * 5
                 + [const(msk)],
        out_specs=[pl.BlockSpec((1, tq, ATT_WIDTH), lambda i, j: (i, j, 0)), lat],
        scratch_shapes=[pltpu.VMEM((ATT_KV_HEADS, kc.shape[1], cols), F32),
                        pltpu.VMEM((KV_SLOTS, ATT_KV_HEADS, KV_TILE, cols), F32),
                        pltpu.VMEM((ATT_KV_HEADS, ATT_HEAD_DIM + ONES_ROWS, cols), F32),
                        pltpu.VMEM((LANES, LANES), F32), pltpu.VMEM((LANES, LANES), F32),
                        pltpu.VMEM((WIN, WIN), BF16), pltpu.VMEM((WIN, WIN), BF16),
                        pltpu.VMEM((WIN, LANES), F32), pltpu.VMEM((WIN, LANES), F32)],
        compiler_params=pltpu.CompilerParams(dimension_semantics=("arbitrary", "arbitrary"),
                                             vmem_limit_bytes=VMEM_LIMIT),
        name="mixers",
    )(qa, kc, vtc, kx, vtx, hq, hv, kf, bf, kb, bb, hvc, kfc, bfc, kbc, bbc, msk)


def _out_ffn_kernel(x_ref, att_ref, hg_ref, sg_ref, mod_ref, ghg_ref, wo_ref, g2_ref, wgu_ref, wd_ref, gf_ref,
                    o_ref):
    d = x_ref.shape[-1]
    dff = wd_ref.shape[0]
    gate1 = mod_ref[0, :, 2 * d:3 * d]
    shift2 = mod_ref[0, :, 3 * d:4 * d]
    scale2 = mod_ref[0, :, 4 * d:5 * d]
    gate2 = mod_ref[0, :, 5 * d:6 * d]

    parts = [slice(r0, r0 + OUT_PART_ROWS) for r0 in range(0, x_ref.shape[1], OUT_PART_ROWS)]

    def mixed(rows):
        heads = []
        for hd in range(HG_HEADS):
            cs = slice(hd * HG_HEAD_DIM, (hd + 1) * HG_HEAD_DIM)
            o = _rms_rows(hg_ref[0, rows, cs]) * ghg_ref[...]
            heads.append((o * sg_ref[0, rows, cs].astype(F32)).astype(BF16))
        mix = (jnp.dot(att_ref[0, rows, :], wo_ref[0:ATT_WIDTH, :], preferred_element_type=F32)
               + jnp.dot(jnp.concatenate(heads, axis=1), wo_ref[ATT_WIDTH:ATT_WIDTH + HG_WIDTH, :],
                         preferred_element_type=F32))
        x1 = x_ref[0, rows, :] + gate1 * mix
        return x1, ((_rms_rows(x1) * g2_ref[...]) * (1.0 + scale2) + shift2).astype(BF16)

    def gate_up(h):
        return (jnp.dot(h, wgu_ref[:, 0:dff], preferred_element_type=F32),
                jnp.dot(h, wgu_ref[:, dff:2 * dff], preferred_element_type=F32))

    def down(a, g):
        return jnp.dot((_silu(a) * g).astype(BF16), wd_ref[...], preferred_element_type=F32)

    firsts = [mixed(rows) for rows in parts]
    ups = [gate_up(h) for _, h in firsts]
    ys = [down(a, g) for a, g in ups]
    for rows, (x1, _), y in zip(parts, firsts, ys):
        o_ref[0, rows, :] = _rms_rows(x1 + gate2 * y) * gf_ref[...]


def _out_ffn(x, att, hg, sg, mod3, ghg, w_out, g2, w_gu, w_down, gf):
    b, t, d = x.shape
    tm = min(OUT_ROW_TILE, t)
    assert tm % OUT_PART_ROWS == 0
    bspec = lambda width: pl.BlockSpec((1, tm, width), lambda i, j: (i, j, 0))
    full = lambda a: pl.BlockSpec(a.shape, lambda i, j: (0,) * a.ndim, pipeline_mode=pl.Buffered(1))
    return pl.pallas_call(
        _out_ffn_kernel,
        out_shape=jax.ShapeDtypeStruct((b, t, d), F32),
        grid=(b, t // tm),
        in_specs=[bspec(d), bspec(ATT_WIDTH), bspec(HG_WIDTH), bspec(HG_WIDTH),
                  pl.BlockSpec((1, 1, mod3.shape[-1]), lambda i, j: (i, 0, 0)),
                  full(ghg), full(w_out), full(g2), full(w_gu), full(w_down), full(gf)],
        out_specs=bspec(d),
        compiler_params=pltpu.CompilerParams(dimension_semantics=("parallel", "parallel"),
                                             vmem_limit_bytes=VMEM_LIMIT),
        name="out_ffn",
    )(x, att, hg, sg, mod3, ghg, w_out, g2, w_gu, w_down, gf)


def _rope_tables(t):
    n_rows = t // GRID_W
    row = np.repeat(np.arange(n_rows, dtype=np.float32), GRID_W)
    col = np.tile(np.arange(GRID_W, dtype=np.float32), n_rows)
    inv = (np.float32(ROPE_THETA) ** (-np.arange(0, AXIS_DIM, 2, dtype=np.float32) / np.float32(AXIS_DIM)))
    inv = inv.astype(np.float32)
    ar = row[:, None] * inv
    ac = col[:, None] * inv
    ang = np.concatenate([ar, ar, ac, ac], axis=-1).astype(np.float32)
    sign = np.where((np.arange(ATT_HEAD_DIM) % AXIS_DIM) < AXIS_DIM // 2, -1.0, 1.0)
    reps = LANES // ATT_HEAD_DIM
    cos = np.tile(np.cos(ang.astype(np.float64)), (1, reps)).astype(np.float32)
    sin = np.tile(np.sin(ang.astype(np.float64)) * sign, (1, reps)).astype(np.float32)
    return jnp.asarray(cos), jnp.asarray(sin)


def kernel(x, c, ctx, c_ctx, w_mod, b_mod, g_norm1, w_in, g_q, g_k, lb_raw, g_hg, w_out, g_norm2, w_gu, w_down,
           g_final):
    b, t, d = x.shape
    assert w_mod.shape[0] == 1 and lb_raw.shape[0] == 2, "single-layer block"
    assert t % WIN == 0 and (t // WIN) % 2 == 0 and ctx.shape[1] % WIN == 0 and t % KV_TILE == 0

    pad = (-(b + 1)) % 8
    cc = jnp.concatenate([c, c_ctx[None, :], jnp.zeros((pad, d), F32)], axis=0)
    mods = _modulation(cc, w_mod[0], b_mod)
    mod3 = mods.reshape(mods.shape[0], 1, mods.shape[1])

    cos, sin = _rope_tables(t)
    reps = LANES // ATT_HEAD_DIM
    gq = jnp.tile(g_q[0], reps)[None, :]
    gk = jnp.tile(g_k[0], reps)[None, :]
    w_in_b = w_in[0].astype(BF16)

    qa, kx, vtx, hq, hv, kf, bf, kb, bb, sg = _input_projection(
        x, mod3, None, g_norm1, w_in_b, gq, gk, lb_raw, cos, sin, True)
    kc, vtc, hvc, kfc, bfc, kbc, bbc = _input_projection(
        ctx, mod3, b, g_norm1, w_in_b, gq, gk, lb_raw, cos, sin, False)

    att, hg = _mixers(qa, kc, vtc, kx, vtx, hq, hv, kf, bf, kb, bb, hvc, kfc, bfc, kbc, bbc)
    return _out_ffn(x, att, hg, sg, mod3, g_hg, w_out[0].astype(BF16), g_norm2, w_gu[0].astype(BF16),
                    w_down[0].astype(BF16), g_final[None, :])
```

```python
import functools

import jax
import jax.numpy as jnp
import numpy as np
from jax import lax
from jax.experimental import pallas as pl
from jax.experimental.pallas import tpu as pltpu

F32 = jnp.float32
BF16 = jnp.bfloat16

LANES = 128
GRID_W = 64
ATT_HEADS = 8
ATT_KV_HEADS = 2
ATT_GROUP = ATT_HEADS // ATT_KV_HEADS
ATT_HEAD_DIM = 64
ATT_WIDTH = ATT_HEADS * ATT_HEAD_DIM
ATT_KV_WIDTH = ATT_KV_HEADS * ATT_HEAD_DIM
AXIS_DIM = ATT_HEAD_DIM // 2
ROPE_THETA = 10000.0
HG_HEADS = 4
HG_HEAD_DIM = 128
HG_WIDTH = HG_HEADS * HG_HEAD_DIM
EPS = 1e-6

MOD_COLUMN_BLOCKS = 4
SUB = 16
WIN = 128
ROW_TILE = 256
OUT_ROW_TILE = 512
OUT_PART_ROWS = 256
OUT_WEIGHT_STEPS = 16
Q_TILE = 256
KV_TILE = 512
KV_SLOTS = 4
KV_AHEAD = 2
ONES_ROWS = 16
LOG2E = 1.4426950408889634
VMEM_LIMIT = 56 * 1024 * 1024

C_AQ = 0
C_AK = C_AQ + ATT_WIDTH
C_AV = C_AK + ATT_KV_WIDTH
C_HQ = C_AV + ATT_KV_WIDTH
C_HI = C_HQ + HG_WIDTH
C_FF = C_HI + HG_WIDTH
C_FB = C_FF + HG_WIDTH
C_HG = C_FB + HG_WIDTH


def _silu(v):
    return v * jax.nn.sigmoid(v)


def _rms_rows(v):
    return v * lax.rsqrt(jnp.mean(v * v, axis=-1, keepdims=True) + EPS)


def _mod_kernel(c_ref, w_ref, b_ref, o_ref):
    a = _silu(c_ref[...])
    o_ref[...] = jnp.dot(a, w_ref[...], precision=lax.Precision.HIGHEST,
                         preferred_element_type=F32) + b_ref[...]


def _modulation(cc, w_mod, b_mod):
    rows, d = cc.shape
    n = w_mod.shape[1]
    bn = n // MOD_COLUMN_BLOCKS
    return pl.pallas_call(
        _mod_kernel,
        out_shape=jax.ShapeDtypeStruct((rows, n), F32),
        grid=(n // bn,),
        in_specs=[pl.BlockSpec((rows, d), lambda j: (0, 0)),
                  pl.BlockSpec((d, bn), lambda j: (0, j)),
                  pl.BlockSpec((1, bn), lambda j: (0, j))],
        out_specs=pl.BlockSpec((rows, bn), lambda j: (0, j)),
        compiler_params=pltpu.CompilerParams(dimension_semantics=("arbitrary",),
                                             vmem_limit_bytes=VMEM_LIMIT),
        name="modulation",
    )(cc, w_mod, b_mod)


def _head64_meansq(p, seg2):
    sq = p * p
    hi = sq.astype(BF16)
    lo = (sq - hi.astype(F32)).astype(BF16)
    s = jnp.dot(jnp.concatenate([hi, lo], axis=1), seg2, preferred_element_type=F32)
    return s * (1.0 / ATT_HEAD_DIM)


def _rope(v, cos, sin_signed, first_half):
    fwd = pltpu.roll(v, LANES - AXIS_DIM // 2, 1)
    bwd = pltpu.roll(v, AXIS_DIM // 2, 1)
    return v * cos + jnp.where(first_half, fwd, bwd) * sin_signed


def _gate_prep(f, lb, suffix):
    fg = lb + (1.0 - lb) * jax.nn.sigmoid(f)
    cum = jnp.log2(fg)
    rows = cum.shape[0]
    pos = lax.broadcasted_iota(jnp.int32, cum.shape, 0) % SUB
    shift = 1
    while shift < SUB:
        if suffix:
            moved = pltpu.roll(cum, rows - shift, 0)
            cum = cum + jnp.where(pos < SUB - shift, moved, 0.0)
        else:
            moved = pltpu.roll(cum, shift, 0)
            cum = cum + jnp.where(pos >= shift, moved, 0.0)
        shift *= 2
    return (1.0 - fg).astype(BF16), cum


def _inproj_kernel(x_ref, mod_ref, g1_ref, w_ref, gq_ref, gk_ref, lb_ref, cos_ref, sin_ref, *out_refs, latent):
    d = x_ref.shape[-1]
    x = x_ref[0]
    shift = mod_ref[0, :, 0:d]
    scale = mod_ref[0, :, d:2 * d]
    h = (_rms_rows(x) * g1_ref[...]) * (1.0 + scale) + shift
    hb = h.astype(BF16)

    def proj(c0, width):
        return jnp.dot(hb, w_ref[:, c0:c0 + width], preferred_element_type=F32)

    lane = lax.broadcasted_iota(jnp.int32, (2 * LANES, LANES), 0) % LANES
    lane_c = lax.broadcasted_iota(jnp.int32, (2 * LANES, LANES), 1)
    seg = (lane // ATT_HEAD_DIM == lane_c // ATT_HEAD_DIM).astype(BF16)
    row_lane = lax.broadcasted_iota(jnp.int32, (x.shape[0], LANES), 1)
    first_half = (row_lane % AXIS_DIM) < (AXIS_DIM // 2)
    upper = row_lane >= ATT_HEAD_DIM

    r0 = lb_ref[0]
    r1 = lb_ref[1]
    rm = jnp.maximum(r0, r1)
    e0 = jnp.exp(r0 - rm)
    e1 = jnp.exp(r1 - rm)
    lb_all = e0 / (e0 + e1)

    if latent:
        (qa_ref, ka_ref, vat_ref, hq_ref, hv_ref, kf_ref, bf_ref, kb_ref, bb_ref, sg_ref) = out_refs
        cos = cos_ref[...]
        sin = sin_ref[...]
    else:
        (ka_ref, vat_ref, hv_ref, kf_ref, bf_ref, kb_ref, bb_ref) = out_refs

    p_ff = proj(C_FF, HG_WIDTH)
    if latent:
        pq = proj(C_AQ, ATT_WIDTH)
    pk = proj(C_AK, ATT_KV_WIDTH)
    pv = proj(C_AV, ATT_KV_WIDTH)
    p_fb = proj(C_FB, HG_WIDTH)
    if latent:
        p_hq = proj(C_HQ, HG_WIDTH)
    p_hv = proj(C_HI, HG_WIDTH)
    ms_k = _head64_meansq(pk, seg)
    if latent:
        ms_q = [_head64_meansq(pq[:, c * LANES:(c + 1) * LANES], seg) for c in range(ATT_WIDTH // LANES)]

    kn = pk * lax.rsqrt(ms_k + EPS) * gk_ref[...]
    if latent:
        kn = _rope(kn, cos, sin, first_half)
    ka_ref[0] = kn.astype(BF16)
    vat_ref[0] = pv.T.astype(BF16)

    if latent:
        for c in range(ATT_WIDTH // LANES):
            t = pq[:, c * LANES:(c + 1) * LANES]
            t = t * lax.rsqrt(ms_q[c] + EPS) * gq_ref[...]
            t = _rope(t, cos, sin, first_half) * (ATT_HEAD_DIM ** -0.5 * LOG2E)
            t_swapped = pltpu.roll(t, ATT_HEAD_DIM, 1)
            for half in range(2):
                head = 2 * c + half
                kv = head // ATT_GROUP
                src = t if half == kv else t_swapped
                keep = upper if kv == 1 else jnp.logical_not(upper)
                qa_ref[0, :, head * LANES:(head + 1) * LANES] = jnp.where(keep, src, 0.0).astype(BF16)
        hq_ref[0] = (_silu(p_hq) * (HG_HEAD_DIM ** -0.5)).astype(BF16)
        sg_ref[0] = _silu(proj(C_HG, HG_WIDTH)).astype(BF16)

    hv_ref[0] = p_hv.astype(BF16)
    kf, bf = _gate_prep(p_ff, lb_all[0:1, :], False)
    kf_ref[0] = kf
    bf_ref[0] = bf
    kb, bb = _gate_prep(p_fb, lb_all[1:2, :], True)
    kb_ref[0] = kb
    bb_ref[0] = bb


def _input_projection(rows, mod3, mod_row, g1, w_in, gq, gk, lb_raw, cos, sin, latent):
    b, t, d = rows.shape
    tm = min(ROW_TILE, t)
    nt = t // tm
    bspec = lambda width: pl.BlockSpec((1, tm, width), lambda i, j: (i, j, 0))
    full = lambda a: pl.BlockSpec(a.shape, lambda i, j: (0,) * a.ndim)
    sds = lambda width, dt: jax.ShapeDtypeStruct((b, t, width), dt)
    vt_sds = jax.ShapeDtypeStruct((b, ATT_KV_WIDTH, t), BF16)
    vt_spec = pl.BlockSpec((1, ATT_KV_WIDTH, tm), lambda i, j: (i, 0, j))
    if latent:
        out_shape = [sds(ATT_HEADS * LANES, BF16), sds(ATT_KV_WIDTH, BF16), vt_sds,
                     sds(HG_WIDTH, BF16), sds(HG_WIDTH, BF16), sds(HG_WIDTH, BF16), sds(HG_WIDTH, F32),
                     sds(HG_WIDTH, BF16), sds(HG_WIDTH, F32), sds(HG_WIDTH, BF16)]
        out_specs = [bspec(ATT_HEADS * LANES), bspec(ATT_KV_WIDTH), vt_spec, bspec(HG_WIDTH),
                     bspec(HG_WIDTH), bspec(HG_WIDTH), bspec(HG_WIDTH), bspec(HG_WIDTH), bspec(HG_WIDTH),
                     bspec(HG_WIDTH)]
        mod_spec = pl.BlockSpec((1, 1, mod3.shape[-1]), lambda i, j: (i, 0, 0))
    else:
        out_shape = [sds(ATT_KV_WIDTH, BF16), vt_sds, sds(HG_WIDTH, BF16), sds(HG_WIDTH, BF16),
                     sds(HG_WIDTH, F32), sds(HG_WIDTH, BF16), sds(HG_WIDTH, F32)]
        out_specs = [bspec(ATT_KV_WIDTH), vt_spec, bspec(HG_WIDTH), bspec(HG_WIDTH), bspec(HG_WIDTH),
                     bspec(HG_WIDTH), bspec(HG_WIDTH)]
        mod_spec = pl.BlockSpec((1, 1, mod3.shape[-1]), lambda i, j: (mod_row, 0, 0))
    table_spec = pl.BlockSpec((tm, LANES), lambda i, j: (j, 0))
    return pl.pallas_call(
        functools.partial(_inproj_kernel, latent=latent),
        out_shape=out_shape,
        grid=(b, nt),
        in_specs=[bspec(d), mod_spec, full(g1), full(w_in), full(gq), full(gk), full(lb_raw),
                  table_spec, table_spec],
        out_specs=out_specs,
        compiler_params=pltpu.CompilerParams(dimension_semantics=("parallel", "parallel"),
                                             vmem_limit_bytes=VMEM_LIMIT),
        name="inproj_latent" if latent else "inproj_context",
    )(rows, mod3, g1, w_in, gq, gk, lb_raw, cos, sin)


def _attention_schedule(q_ref, kc_ref, vtc_ref, kx_ref, vtx_ref, o_ref, sc_ref, s_ref, acc_ref):
    tq = q_ref.shape[1]
    n_kv = kx_ref.shape[1] // KV_TILE
    nt_dims = (((1,), (1,)), ((), ()))
    cols = ATT_GROUP * tq

    qgs = [jnp.concatenate(
        [q_ref[0, :, (kv * ATT_GROUP + j) * LANES:(kv * ATT_GROUP + j + 1) * LANES] for j in range(ATT_GROUP)],
        axis=0) for kv in range(ATT_KV_HEADS)]

    def scores(kt, dst):
        for kv in range(ATT_KV_HEADS):
            dst(kv, lax.dot_general(kt, qgs[kv], nt_dims, preferred_element_type=F32))

    def consume(src, vt_all, maxima):
        ones = jnp.ones((ONES_ROWS, vt_all.shape[1]), BF16)
        out = []
        for kv in range(ATT_KV_HEADS):
            m = maxima[kv]
            m_new = jnp.maximum(m, jnp.max(src(kv), axis=0, keepdims=True))
            alpha = jnp.exp2(m - m_new)
            p = jnp.exp2(src(kv) - m_new).astype(BF16)
            vt = jnp.concatenate([vt_all[kv * ATT_HEAD_DIM:(kv + 1) * ATT_HEAD_DIM], ones], axis=0)
            acc_ref[kv] = alpha * acc_ref[kv] + jnp.dot(vt, p, preferred_element_type=F32)
            out.append(m_new)
        return tuple(out)

    def put_ctx(kv, val):
        sc_ref[kv] = val

    def put(slot):
        def dst(kv, val):
            s_ref[slot, kv] = val
        return dst

    def k_tile(j):
        return kx_ref[0, j * KV_TILE:(j + 1) * KV_TILE, :]

    def vt_tile(j):
        return vtx_ref[0, :, j * KV_TILE:(j + 1) * KV_TILE]

    acc_ref[...] = jnp.zeros(acc_ref.shape, F32)
    scores(kc_ref[0], put_ctx)
    for j in range(min(KV_AHEAD, n_kv)):
        scores(k_tile(j), put(j % KV_SLOTS))
    maxima = consume(lambda kv: sc_ref[kv], vtc_ref[0], (jnp.full((1, cols), -jnp.inf, F32),) * ATT_KV_HEADS)
    for j in range(n_kv):
        if j + KV_AHEAD < n_kv:
            scores(k_tile(j + KV_AHEAD), put((j + KV_AHEAD) % KV_SLOTS))
        maxima = consume(lambda kv, slot=j % KV_SLOTS: s_ref[slot, kv], vt_tile(j), maxima)
        yield

    for kv in range(ATT_KV_HEADS):
        o = acc_ref[kv, 0:ATT_HEAD_DIM] / acc_ref[kv, ATT_HEAD_DIM:ATT_HEAD_DIM + 1]
        for jj in range(ATT_GROUP // 2):
            pair = jnp.concatenate([o[:, (2 * jj) * tq:(2 * jj + 1) * tq],
                                    o[:, (2 * jj + 1) * tq:(2 * jj + 2) * tq]], axis=0)
            col = kv * (ATT_GROUP // 2) + jj
            o_ref[0, :, col * LANES:(col + 1) * LANES] = pair.T.astype(BF16)


NT_DIMS = (((1,), (1,)), ((), ()))


def _scan_window(st, q, k, v, b, b_row, msk_ref, reverse, want_out):
    nsub = WIN // SUB
    last = 0 if reverse else SUB - 1
    sub_of = (lambda p: nsub - 1 - p) if reverse else (lambda p: p)
    blk = lambda a, i: a[i * SUB:(i + 1) * SUB]
    d = int(reverse)

    b_last = [b_row(i * SUB + last) for i in range(nsub)]
    cum = []
    for p in range(nsub):
        cum.append(b_last[sub_of(p)] if p == 0 else cum[-1] + b_last[sub_of(p)])
    ks = [blk(k, i) * jnp.exp2(b_last[i] - blk(b, i)) for i in range(nsub)]

    k_end = jnp.concatenate(
        [ks[i] if sub_of(i) == nsub - 1 else ks[i] * jnp.exp2(cum[nsub - 1] - cum[sub_of(i)]) for i in range(nsub)],
        axis=0)
    st_new = jnp.exp2(cum[nsub - 1]) * st + jnp.dot(v.T, k_end, preferred_element_type=F32)
    if not want_out:
        return st_new, None

    qs = [blk(q, i) * jnp.exp2(blk(b, i)) for i in range(nsub)]
    q_start = jnp.concatenate(
        [qs[i] if sub_of(i) == 0 else qs[i] * jnp.exp2(cum[sub_of(i) - 1]) for i in range(nsub)], axis=0)
    o_carried = lax.dot_general(q_start, st, NT_DIMS, preferred_element_type=F32)

    scores = None
    for level, h in enumerate((nsub // 2, nsub // 4, nsub // 8)):
        q_rows, k_rows = [], []
        for i in range(nsub):
            p = sub_of(i)
            ref = (p // (2 * h)) * 2 * h + h - 1
            q_rows.append(qs[i] * jnp.exp2(cum[p - 1] - cum[ref]) if p - 1 > ref else qs[i])
            k_rows.append(ks[i] * jnp.exp2(cum[ref] - cum[p]) if p < ref else ks[i])
        term = (lax.dot_general(jnp.concatenate(q_rows, axis=0), jnp.concatenate(k_rows, axis=0), NT_DIMS,
                                preferred_element_type=F32) * msk_ref[d, level])
        scores = term if scores is None else scores + term

    def level_scores(level, q_exponent, k_exponent):
        q_l = q * jnp.exp2(jnp.minimum(q_exponent, 0.0))
        k_l = k if k_exponent is None else k * jnp.exp2(jnp.minimum(k_exponent, 0.0))
        return lax.dot_general(q_l, k_l, NT_DIMS, preferred_element_type=F32) * msk_ref[d, level]

    scores = scores + (jnp.dot(q * k, jnp.ones((LANES, LANES), BF16), preferred_element_type=F32)
                       * msk_ref[d, N_LEVELS])
    for level, h in ((3, SUB // 2), (4, SUB // 4)):
        ref_rows = [b_row(s * 2 * h + (h if reverse else h - 1)) for s in range(WIN // (2 * h))]
        diff = b - jnp.concatenate([jnp.broadcast_to(r, (2 * h, LANES)) for r in ref_rows], axis=0)
        decay = jnp.exp2(jnp.minimum(diff, -diff))
        scores = scores + (lax.dot_general(q * decay, k * decay, NT_DIMS, preferred_element_type=F32)
                           * msk_ref[d, level])
    before = lambda n: pltpu.roll(b, n, 0)
    after = lambda n: pltpu.roll(b, WIN - n, 0)
    off4 = lax.broadcasted_iota(jnp.int32, (WIN, LANES), 0) % 4
    if reverse:
        q_exp = b - jnp.where(off4 == 0, after(2), after(1))
        k_exp = jnp.where(off4 == 2, 0.0, before(1) - b)
    else:
        q_exp = b - jnp.where(off4 == 3, before(2), before(1))
        k_exp = jnp.where(off4 == 1, 0.0, after(1) - b)
    scores = scores + level_scores(5, q_exp, k_exp)
    scores = scores + level_scores(6, b - (after(1) if reverse else before(1)), None)
    return st_new, (scores.astype(BF16), o_carried)


N_LEVELS = 7


def _pair_masks():
    msk = np.zeros((2, N_LEVELS + 1, WIN, WIN), np.float32)
    pos = np.arange(WIN)
    for d in range(2):
        for level in range(N_LEVELS):
            h = WIN >> (level + 1)
            span, off = pos // (2 * h), pos % (2 * h)
            later = (off < h) if d else (off >= h)
            msk[d, level] = (span[:, None] == span[None, :]) & later[:, None] & ~later[None, :]
        msk[d, N_LEVELS] = np.eye(WIN)
    return jnp.asarray(msk, F32)


def _hgrn_steps(q_ref, v_ref, kf_ref, bf_ref, kb_ref, bb_ref, vc_ref, kfc_ref, bfc_ref, kbc_ref, bbc_ref,
                msk_ref, o_ref, stf_ref, stb_ref, scf_ref, scb_ref, ocf_ref, ocb_ref, w0, n_steps):
    t = q_ref.shape[1]
    ctx = vc_ref.shape[1]
    nw = t // WIN

    def window(refs, start, st, reverse, want_out):
        qr, vr, kr, br = refs
        sl = pl.ds(start, WIN)
        vw = vr[0, sl, :].astype(F32)
        kw = kr[0, sl, :].astype(F32)
        bw = br[0, sl, :]
        qw = qr[0, sl, :].astype(F32) if want_out else None
        b_row = lambda r: br[0, pl.ds(start + r, 1), :]
        return _scan_window(st, qw, kw, vw, bw, b_row, msk_ref, reverse, want_out)

    def restart():
        zero = jnp.zeros((LANES, LANES), F32)
        stf, stb = zero, zero
        n_cw = ctx // WIN
        for w in range(n_cw):
            stf, _ = window((None, vc_ref, kfc_ref, bfc_ref), w * WIN, stf, False, False)
            stb, _ = window((None, vc_ref, kbc_ref, bbc_ref), (n_cw - 1 - w) * WIN, stb, True, False)
        stf_ref[...] = stf
        stb_ref[...] = stb
        o_ref[...] = jnp.zeros(o_ref.shape, F32)

    def starts(w):
        return pl.multiple_of(w * WIN, WIN), pl.multiple_of((nw - 1 - w) * WIN, WIN)

    def scores_stage(w):
        fs, bs = starts(w)
        stf, (sf, cf) = window((q_ref, v_ref, kf_ref, bf_ref), fs, stf_ref[...], False, True)
        stf_ref[...] = stf
        scf_ref[...] = sf
        ocf_ref[...] = cf
        stb, (sb, cb) = window((q_ref, v_ref, kb_ref, bb_ref), bs, stb_ref[...], True, True)
        stb_ref[...] = stb
        scb_ref[...] = sb
        ocb_ref[...] = cb

    def output_stage(w):
        fs, bs = starts(w)
        o_ref[0, pl.ds(fs, WIN), :] += (
            jnp.dot(scf_ref[...], v_ref[0, pl.ds(fs, WIN), :], preferred_element_type=F32) + ocf_ref[...])
        o_ref[0, pl.ds(bs, WIN), :] += (
            jnp.dot(scb_ref[...], v_ref[0, pl.ds(bs, WIN), :], preferred_element_type=F32) + ocb_ref[...])

    def schedule():
        for u in range(n_steps + 1):
            if u > 0:
                output_stage(w0 + u - 1)
            if u < n_steps:
                scores_stage(w0 + u)
            yield

    return restart, schedule()


def _mixer_kernel(qa_ref, kc_ref, vtc_ref, kx_ref, vtx_ref,
                  hq_ref, hv_ref, kf_ref, bf_ref, kb_ref, bb_ref, hvc_ref, kfc_ref, bfc_ref, kbc_ref, bbc_ref,
                  msk_ref, att_ref, hg_ref, sc_ref, s_ref, acc_ref, *hg_scratch, steps_per_head):
    nw = hq_ref.shape[1] // WIN
    n_steps = nw // steps_per_head
    phase = pl.program_id(1) % steps_per_head
    restart, scan = _hgrn_steps(hq_ref, hv_ref, kf_ref, bf_ref, kb_ref, bb_ref,
                                hvc_ref, kfc_ref, bfc_ref, kbc_ref, bbc_ref, msk_ref, hg_ref, *hg_scratch,
                                w0=phase * n_steps, n_steps=n_steps)
    pl.when(phase == 0)(restart)

    for _ in _attention_schedule(qa_ref, kc_ref, vtc_ref, kx_ref, vtx_ref, att_ref, sc_ref, s_ref, acc_ref):
        next(scan, None)
    for _ in scan:
        pass


def _mixers(qa, kc, vtc, kx, vtx, hq, hv, kf, bf, kb, bb, hvc, kfc, bfc, kbc, bbc):
    b, t, _ = qa.shape
    ctx = hvc.shape[1]
    tq = min(Q_TILE, t)
    n_q = t // tq
    nw = t // WIN
    assert tq % LANES == 0 and t % tq == 0 and t % KV_TILE == 0 and KV_AHEAD < KV_SLOTS
    assert n_q % HG_HEADS == 0 and nw % (n_q // HG_HEADS) == 0
    steps_per_head = n_q // HG_HEADS
    cols = ATT_GROUP * tq
    whole = lambda a: pl.BlockSpec((1,) + a.shape[1:], lambda i, j: (i, 0, 0))
    lat = pl.BlockSpec((1, t, HG_HEAD_DIM), lambda i, j: (i, 0, j // steps_per_head))
    con = pl.BlockSpec((1, ctx, HG_HEAD_DIM), lambda i, j: (i, 0, j // steps_per_head))
    msk = _pair_masks()
    const = lambda a: pl.BlockSpec(a.shape, lambda i, j: (0,) * a.ndim)
    return pl.pallas_call(
        functools.partial(_mixer_kernel, steps_per_head=steps_per_head),
        out_shape=[jax.ShapeDtypeStruct((b, t, ATT_WIDTH), BF16), jax.ShapeDtypeStruct((b, t, HG_WIDTH), F32)],
        grid=(b, n_q),
        in_specs=[pl.BlockSpec((1, tq, qa.shape[-1]), lambda i, j: (i, j, 0)),
                  whole(kc), whole(vtc), whole(kx), whole(vtx)] + [lat] * 6 + [con] * 5
                 + [const(msk)],
        out_specs=[pl.BlockSpec((1, tq, ATT_WIDTH), lambda i, j: (i, j, 0)), lat],
        scratch_shapes=[pltpu.VMEM((ATT_KV_HEADS, kc.shape[1], cols), F32),
                        pltpu.VMEM((KV_SLOTS, ATT_KV_HEADS, KV_TILE, cols), F32),
                        pltpu.VMEM((ATT_KV_HEADS, ATT_HEAD_DIM + ONES_ROWS, cols), F32),
                        pltpu.VMEM((LANES, LANES), F32), pltpu.VMEM((LANES, LANES), F32),
                        pltpu.VMEM((WIN, WIN), BF16), pltpu.VMEM((WIN, WIN), BF16),
                        pltpu.VMEM((WIN, LANES), F32), pltpu.VMEM((WIN, LANES), F32)],
        compiler_params=pltpu.CompilerParams(dimension_semantics=("arbitrary", "arbitrary"),
                                             vmem_limit_bytes=VMEM_LIMIT),
        name="mixers",
    )(qa, kc, vtc, kx, vtx, hq, hv, kf, bf, kb, bb, hvc, kfc, bfc, kbc, bbc, msk)


def _out_ffn_kernel(x_ref, att_ref, hg_ref, sg_ref, mod_ref, ghg_ref, wo_in, g2_ref, wgu_in, wd_in, gf_ref,
                    o_ref, wo_ref, wgu_ref, wd_ref):
    step = pl.program_id(0)

    @pl.when(step < OUT_WEIGHT_STEPS)
    def _():
        for src, dst in ((wo_in, wo_ref), (wgu_in, wgu_ref), (wd_in, wd_ref)):
            rows = src.shape[0]
            dst[pl.ds(pl.multiple_of(step * rows, rows), rows), :] = src[...].astype(BF16)

    @pl.when(step >= OUT_WEIGHT_STEPS)
    def _():
        _out_ffn_tile(x_ref, att_ref, hg_ref, sg_ref, mod_ref, ghg_ref, wo_ref, g2_ref, wgu_ref, wd_ref, gf_ref,
                      o_ref)


def _out_ffn_tile(x_ref, att_ref, hg_ref, sg_ref, mod_ref, ghg_ref, wo_ref, g2_ref, wgu_ref, wd_ref, gf_ref,
                  o_ref):
    d = x_ref.shape[-1]
    dff = wd_ref.shape[0]
    gate1 = mod_ref[0, :, 2 * d:3 * d]
    shift2 = mod_ref[0, :, 3 * d:4 * d]
    scale2 = mod_ref[0, :, 4 * d:5 * d]
    gate2 = mod_ref[0, :, 5 * d:6 * d]

    parts = [slice(r0, r0 + OUT_PART_ROWS) for r0 in range(0, x_ref.shape[1], OUT_PART_ROWS)]

    def mixed(rows):
        heads = []
        for hd in range(HG_HEADS):
            cs = slice(hd * HG_HEAD_DIM, (hd + 1) * HG_HEAD_DIM)
            o = _rms_rows(hg_ref[0, rows, cs]) * ghg_ref[...]
            heads.append((o * sg_ref[0, rows, cs].astype(F32)).astype(BF16))
        mix = (jnp.dot(att_ref[0, rows, :], wo_ref[0:ATT_WIDTH, :], preferred_element_type=F32)
               + jnp.dot(jnp.concatenate(heads, axis=1), wo_ref[ATT_WIDTH:ATT_WIDTH + HG_WIDTH, :],
                         preferred_element_type=F32))
        x1 = x_ref[0, rows, :] + gate1 * mix
        return x1, ((_rms_rows(x1) * g2_ref[...]) * (1.0 + scale2) + shift2).astype(BF16)

    def gate_up(h):
        return (jnp.dot(h, wgu_ref[:, 0:dff], preferred_element_type=F32),
                jnp.dot(h, wgu_ref[:, dff:2 * dff], preferred_element_type=F32))

    def down(a, g):
        return jnp.dot((_silu(a) * g).astype(BF16), wd_ref[...], preferred_element_type=F32)

    firsts = [mixed(rows) for rows in parts]
    ups = [gate_up(h) for _, h in firsts]
    ys = [down(a, g) for a, g in ups]
    for rows, (x1, _), y in zip(parts, firsts, ys):
        o_ref[0, rows, :] = _rms_rows(x1 + gate2 * y) * gf_ref[...]


def _out_ffn(x, att, hg, sg, mod3, ghg, w_out, g2, w_gu, w_down, gf):
    b, t, d = x.shape
    tm = min(OUT_ROW_TILE, t)
    assert tm % OUT_PART_ROWS == 0
    nt = t // tm
    tile = lambda s: jnp.maximum(s - OUT_WEIGHT_STEPS, 0)
    bspec = lambda width: pl.BlockSpec((1, tm, width), lambda s: (tile(s) // nt, tile(s) % nt, 0))
    full = lambda a: pl.BlockSpec(a.shape, lambda s: (0,) * a.ndim, pipeline_mode=pl.Buffered(1))

    def chunked(w):
        assert w.shape[0] % (OUT_WEIGHT_STEPS * SUB) == 0
        return pl.BlockSpec((w.shape[0] // OUT_WEIGHT_STEPS, w.shape[1]),
                            lambda s: (jnp.minimum(s, OUT_WEIGHT_STEPS - 1), 0))

    return pl.pallas_call(
        _out_ffn_kernel,
        out_shape=jax.ShapeDtypeStruct((b, t, d), F32),
        grid=(OUT_WEIGHT_STEPS + b * nt,),
        in_specs=[bspec(d), bspec(ATT_WIDTH), bspec(HG_WIDTH), bspec(HG_WIDTH),
                  pl.BlockSpec((1, 1, mod3.shape[-1]), lambda s: (tile(s) // nt, 0, 0)),
                  full(ghg), chunked(w_out), full(g2), chunked(w_gu), chunked(w_down), full(gf)],
        out_specs=bspec(d),
        scratch_shapes=[pltpu.VMEM(w_out.shape, BF16), pltpu.VMEM(w_gu.shape, BF16),
                        pltpu.VMEM(w_down.shape, BF16)],
        compiler_params=pltpu.CompilerParams(dimension_semantics=("arbitrary",),
                                             vmem_limit_bytes=VMEM_LIMIT),
        name="out_ffn",
    )(x, att, hg, sg, mod3, ghg, w_out, g2, w_gu, w_down, gf)


def _rope_tables(t):
    n_rows = t // GRID_W
    row = np.repeat(np.arange(n_rows, dtype=np.float32), GRID_W)
    col = np.tile(np.arange(GRID_W, dtype=np.float32), n_rows)
    inv = (np.float32(ROPE_THETA) ** (-np.arange(0, AXIS_DIM, 2, dtype=np.float32) / np.float32(AXIS_DIM)))
    inv = inv.astype(np.float32)
    ar = row[:, None] * inv
    ac = col[:, None] * inv
    ang = np.concatenate([ar, ar, ac, ac], axis=-1).astype(np.float32)
    sign = np.where((np.arange(ATT_HEAD_DIM) % AXIS_DIM) < AXIS_DIM // 2, -1.0, 1.0)
    reps = LANES // ATT_HEAD_DIM
    cos = np.tile(np.cos(ang.astype(np.float64)), (1, reps)).astype(np.float32)
    sin = np.tile(np.sin(ang.astype(np.float64)) * sign, (1, reps)).astype(np.float32)
    return jnp.asarray(cos), jnp.asarray(sin)


def kernel(x, c, ctx, c_ctx, w_mod, b_mod, g_norm1, w_in, g_q, g_k, lb_raw, g_hg, w_out, g_norm2, w_gu, w_down,
           g_final):
    b, t, d = x.shape
    assert w_mod.shape[0] == 1 and lb_raw.shape[0] == 2, "single-layer block"
    assert t % WIN == 0 and (t // WIN) % 2 == 0 and ctx.shape[1] % WIN == 0 and t % KV_TILE == 0

    pad = (-(b + 1)) % 8
    cc = jnp.concatenate([c, c_ctx[None, :], jnp.zeros((pad, d), F32)], axis=0)
    mods = _modulation(cc, w_mod[0], b_mod)
    mod3 = mods.reshape(mods.shape[0], 1, mods.shape[1])

    cos, sin = _rope_tables(t)
    reps = LANES // ATT_HEAD_DIM
    gq = jnp.tile(g_q[0], reps)[None, :]
    gk = jnp.tile(g_k[0], reps)[None, :]
    w_in_b = w_in[0].astype(BF16)

    qa, kx, vtx, hq, hv, kf, bf, kb, bb, sg = _input_projection(
        x, mod3, None, g_norm1, w_in_b, gq, gk, lb_raw, cos, sin, True)
    kc, vtc, hvc, kfc, bfc, kbc, bbc = _input_projection(
        ctx, mod3, b, g_norm1, w_in_b, gq, gk, lb_raw, cos, sin, False)

    att, hg = _mixers(qa, kc, vtc, kx, vtx, hq, hv, kf, bf, kb, bb, hvc, kfc, bfc, kbc, bbc)
    return _out_ffn(x, att, hg, sg, mod3, g_hg, w_out[0], g_norm2, w_gu[0], w_down[0], g_final[None, :])
```

```python
import functools

import jax
import jax.numpy as jnp
import numpy as np
from jax import lax
from jax.experimental import pallas as pl
from jax.experimental.pallas import tpu as pltpu

F32 = jnp.float32
BF16 = jnp.bfloat16

LANES = 128
GRID_W = 64
ATT_HEADS = 8
ATT_KV_HEADS = 2
ATT_GROUP = ATT_HEADS // ATT_KV_HEADS
ATT_HEAD_DIM = 64
ATT_WIDTH = ATT_HEADS * ATT_HEAD_DIM
ATT_KV_WIDTH = ATT_KV_HEADS * ATT_HEAD_DIM
AXIS_DIM = ATT_HEAD_DIM // 2
ROPE_THETA = 10000.0
HG_HEADS = 4
HG_HEAD_DIM = 128
HG_WIDTH = HG_HEADS * HG_HEAD_DIM
EPS = 1e-6

MOD_COLUMN_BLOCKS = 8
SUB = 16
WIN = 128
ROW_TILE = 256
OUT_ROW_TILE = 512
OUT_PART_ROWS = 256
OUT_WEIGHT_STEPS = 8
Q_TILE = 256
KV_TILE = 512
KV_SLOTS = 4
KV_AHEAD = 2
ONES_ROWS = 16
LOG2E = 1.4426950408889634
VMEM_LIMIT = 56 * 1024 * 1024

C_AQ = 0
C_AK = C_AQ + ATT_WIDTH
C_AV = C_AK + ATT_KV_WIDTH
C_HQ = C_AV + ATT_KV_WIDTH
C_HI = C_HQ + HG_WIDTH
C_FF = C_HI + HG_WIDTH
C_FB = C_FF + HG_WIDTH
C_HG = C_FB + HG_WIDTH


def _silu(v):
    return v * jax.nn.sigmoid(v)


def _rms_rows(v):
    return v * lax.rsqrt(jnp.mean(v * v, axis=-1, keepdims=True) + EPS)


def _mod_kernel(c_ref, w_ref, b_ref, o_ref):
    a = _silu(c_ref[...])
    rows = a.shape[0]
    w = w_ref[...]
    a_hi = a.astype(BF16)
    a_lo = (a - a_hi.astype(F32)).astype(BF16)
    w_hi = w.astype(BF16)
    w_lo = (w - w_hi.astype(F32)).astype(BF16)
    both = jnp.dot(jnp.concatenate([a_hi, a_lo], axis=0), w_hi, preferred_element_type=F32)
    o_ref[...] = (both[0:rows] + both[rows:2 * rows]
                  + jnp.dot(a_hi, w_lo, preferred_element_type=F32) + b_ref[...])


def _modulation(cc, w_mod, b_mod):
    rows, d = cc.shape
    n = w_mod.shape[1]
    bn = n // MOD_COLUMN_BLOCKS
    return pl.pallas_call(
        _mod_kernel,
        out_shape=jax.ShapeDtypeStruct((rows, n), F32),
        grid=(n // bn,),
        in_specs=[pl.BlockSpec((rows, d), lambda j: (0, 0)),
                  pl.BlockSpec((d, bn), lambda j: (0, j)),
                  pl.BlockSpec((1, bn), lambda j: (0, j))],
        out_specs=pl.BlockSpec((rows, bn), lambda j: (0, j)),
        compiler_params=pltpu.CompilerParams(dimension_semantics=("arbitrary",),
                                             vmem_limit_bytes=VMEM_LIMIT),
        name="modulation",
    )(cc, w_mod, b_mod)


def _head64_meansq(p, seg2):
    sq = p * p
    hi = sq.astype(BF16)
    lo = (sq - hi.astype(F32)).astype(BF16)
    s = jnp.dot(jnp.concatenate([hi, lo], axis=1), seg2, preferred_element_type=F32)
    return s * (1.0 / ATT_HEAD_DIM)


def _rope(v, cos, sin_signed, first_half):
    fwd = pltpu.roll(v, LANES - AXIS_DIM // 2, 1)
    bwd = pltpu.roll(v, AXIS_DIM // 2, 1)
    return v * cos + jnp.where(first_half, fwd, bwd) * sin_signed


def _gate_prep(f, lb, suffix):
    fg = lb + (1.0 - lb) * jax.nn.sigmoid(f)
    cum = jnp.log2(fg)
    rows = cum.shape[0]
    pos = lax.broadcasted_iota(jnp.int32, cum.shape, 0) % SUB
    shift = 1
    while shift < SUB:
        if suffix:
            moved = pltpu.roll(cum, rows - shift, 0)
            cum = cum + jnp.where(pos < SUB - shift, moved, 0.0)
        else:
            moved = pltpu.roll(cum, shift, 0)
            cum = cum + jnp.where(pos >= shift, moved, 0.0)
        shift *= 2
    return (1.0 - fg).astype(BF16), cum


def _inproj_kernel(x_ref, mod_ref, g1_ref, w_ref, gq_ref, gk_ref, lb_ref, cos_ref, sin_ref, *out_refs, latent):
    d = x_ref.shape[-1]
    x = x_ref[0]
    shift = mod_ref[0, :, 0:d]
    scale = mod_ref[0, :, d:2 * d]
    h = (_rms_rows(x) * g1_ref[...]) * (1.0 + scale) + shift
    hb = h.astype(BF16)

    def proj(c0, width):
        return jnp.dot(hb, w_ref[:, c0:c0 + width], preferred_element_type=F32)

    lane = lax.broadcasted_iota(jnp.int32, (2 * LANES, LANES), 0) % LANES
    lane_c = lax.broadcasted_iota(jnp.int32, (2 * LANES, LANES), 1)
    seg = (lane // ATT_HEAD_DIM == lane_c // ATT_HEAD_DIM).astype(BF16)
    row_lane = lax.broadcasted_iota(jnp.int32, (x.shape[0], LANES), 1)
    first_half = (row_lane % AXIS_DIM) < (AXIS_DIM // 2)
    upper = row_lane >= ATT_HEAD_DIM

    r0 = lb_ref[0]
    r1 = lb_ref[1]
    rm = jnp.maximum(r0, r1)
    e0 = jnp.exp(r0 - rm)
    e1 = jnp.exp(r1 - rm)
    lb_all = e0 / (e0 + e1)

    if latent:
        (qa_ref, ka_ref, vat_ref, hq_ref, hv_ref, kf_ref, bf_ref, kb_ref, bb_ref, sg_ref) = out_refs
        cos = cos_ref[...]
        sin = sin_ref[...]
    else:
        (ka_ref, vat_ref, hv_ref, kf_ref, bf_ref, kb_ref, bb_ref) = out_refs

    p_ff = proj(C_FF, HG_WIDTH)
    if latent:
        pq = proj(C_AQ, ATT_WIDTH)
    pk = proj(C_AK, ATT_KV_WIDTH)
    pv = proj(C_AV, ATT_KV_WIDTH)
    p_fb = proj(C_FB, HG_WIDTH)
    if latent:
        p_hq = proj(C_HQ, HG_WIDTH)
    p_hv = proj(C_HI, HG_WIDTH)
    ms_k = _head64_meansq(pk, seg)
    if latent:
        ms_q = [_head64_meansq(pq[:, c * LANES:(c + 1) * LANES], seg) for c in range(ATT_WIDTH // LANES)]

    kn = pk * lax.rsqrt(ms_k + EPS) * gk_ref[...]
    if latent:
        kn = _rope(kn, cos, sin, first_half)
    ka_ref[0] = kn.astype(BF16)
    vat_ref[0] = pv.T.astype(BF16)

    if latent:
        for c in range(ATT_WIDTH // LANES):
            t = pq[:, c * LANES:(c + 1) * LANES]
            t = t * lax.rsqrt(ms_q[c] + EPS) * gq_ref[...]
            t = _rope(t, cos, sin, first_half) * (ATT_HEAD_DIM ** -0.5 * LOG2E)
            t_swapped = pltpu.roll(t, ATT_HEAD_DIM, 1)
            for half in range(2):
                head = 2 * c + half
                kv = head // ATT_GROUP
                src = t if half == kv else t_swapped
                keep = upper if kv == 1 else jnp.logical_not(upper)
                qa_ref[0, :, head * LANES:(head + 1) * LANES] = jnp.where(keep, src, 0.0).astype(BF16)
        hq_ref[0] = (_silu(p_hq) * (HG_HEAD_DIM ** -0.5)).astype(BF16)
        sg_ref[0] = _silu(proj(C_HG, HG_WIDTH)).astype(BF16)

    hv_ref[0] = p_hv.astype(BF16)
    kf, bf = _gate_prep(p_ff, lb_all[0:1, :], False)
    kf_ref[0] = kf
    bf_ref[0] = bf
    kb, bb = _gate_prep(p_fb, lb_all[1:2, :], True)
    kb_ref[0] = kb
    bb_ref[0] = bb


def _input_projection(rows, mod3, mod_row, g1, w_in, gq, gk, lb_raw, cos, sin, latent):
    b, t, d = rows.shape
    tm = min(ROW_TILE, t)
    nt = t // tm
    bspec = lambda width: pl.BlockSpec((1, tm, width), lambda i, j: (i, j, 0))
    full = lambda a: pl.BlockSpec(a.shape, lambda i, j: (0,) * a.ndim)
    sds = lambda width, dt: jax.ShapeDtypeStruct((b, t, width), dt)
    vt_sds = jax.ShapeDtypeStruct((b, ATT_KV_WIDTH, t), BF16)
    vt_spec = pl.BlockSpec((1, ATT_KV_WIDTH, tm), lambda i, j: (i, 0, j))
    if latent:
        out_shape = [sds(ATT_HEADS * LANES, BF16), sds(ATT_KV_WIDTH, BF16), vt_sds,
                     sds(HG_WIDTH, BF16), sds(HG_WIDTH, BF16), sds(HG_WIDTH, BF16), sds(HG_WIDTH, F32),
                     sds(HG_WIDTH, BF16), sds(HG_WIDTH, F32), sds(HG_WIDTH, BF16)]
        out_specs = [bspec(ATT_HEADS * LANES), bspec(ATT_KV_WIDTH), vt_spec, bspec(HG_WIDTH),
                     bspec(HG_WIDTH), bspec(HG_WIDTH), bspec(HG_WIDTH), bspec(HG_WIDTH), bspec(HG_WIDTH),
                     bspec(HG_WIDTH)]
        mod_spec = pl.BlockSpec((1, 1, mod3.shape[-1]), lambda i, j: (i, 0, 0))
    else:
        out_shape = [sds(ATT_KV_WIDTH, BF16), vt_sds, sds(HG_WIDTH, BF16), sds(HG_WIDTH, BF16),
                     sds(HG_WIDTH, F32), sds(HG_WIDTH, BF16), sds(HG_WIDTH, F32)]
        out_specs = [bspec(ATT_KV_WIDTH), vt_spec, bspec(HG_WIDTH), bspec(HG_WIDTH), bspec(HG_WIDTH),
                     bspec(HG_WIDTH), bspec(HG_WIDTH)]
        mod_spec = pl.BlockSpec((1, 1, mod3.shape[-1]), lambda i, j: (mod_row, 0, 0))
    table_spec = pl.BlockSpec((tm, LANES), lambda i, j: (j, 0))
    return pl.pallas_call(
        functools.partial(_inproj_kernel, latent=latent),
        out_shape=out_shape,
        grid=(b, nt),
        in_specs=[bspec(d), mod_spec, full(g1), full(w_in), full(gq), full(gk), full(lb_raw),
                  table_spec, table_spec],
        out_specs=out_specs,
        compiler_params=pltpu.CompilerParams(dimension_semantics=("parallel", "parallel"),
                                             vmem_limit_bytes=VMEM_LIMIT),
        name="inproj_latent" if latent else "inproj_context",
    )(rows, mod3, g1, w_in, gq, gk, lb_raw, cos, sin)


def _attention_schedule(q_ref, kc_ref, vtc_ref, kx_ref, vtx_ref, o_ref, sc_ref, s_ref, acc_ref):
    tq = q_ref.shape[1]
    n_kv = kx_ref.shape[1] // KV_TILE
    nt_dims = (((1,), (1,)), ((), ()))
    cols = ATT_GROUP * tq

    qgs = [jnp.concatenate(
        [q_ref[0, :, (kv * ATT_GROUP + j) * LANES:(kv * ATT_GROUP + j + 1) * LANES] for j in range(ATT_GROUP)],
        axis=0) for kv in range(ATT_KV_HEADS)]

    def scores(kt, dst):
        for kv in range(ATT_KV_HEADS):
            dst(kv, lax.dot_general(kt, qgs[kv], nt_dims, preferred_element_type=F32))

    def consume(src, vt_all, maxima):
        ones = jnp.ones((ONES_ROWS, vt_all.shape[1]), BF16)
        out = []
        for kv in range(ATT_KV_HEADS):
            m = maxima[kv]
            m_new = jnp.maximum(m, jnp.max(src(kv), axis=0, keepdims=True))
            alpha = jnp.exp2(m - m_new)
            p = jnp.exp2(src(kv) - m_new).astype(BF16)
            vt = jnp.concatenate([vt_all[kv * ATT_HEAD_DIM:(kv + 1) * ATT_HEAD_DIM], ones], axis=0)
            acc_ref[kv] = alpha * acc_ref[kv] + jnp.dot(vt, p, preferred_element_type=F32)
            out.append(m_new)
        return tuple(out)

    def put_ctx(kv, val):
        sc_ref[kv] = val

    def put(slot):
        def dst(kv, val):
            s_ref[slot, kv] = val
        return dst

    def k_tile(j):
        return kx_ref[0, j * KV_TILE:(j + 1) * KV_TILE, :]

    def vt_tile(j):
        return vtx_ref[0, :, j * KV_TILE:(j + 1) * KV_TILE]

    acc_ref[...] = jnp.zeros(acc_ref.shape, F32)
    scores(kc_ref[0], put_ctx)
    for j in range(min(KV_AHEAD, n_kv)):
        scores(k_tile(j), put(j % KV_SLOTS))
    maxima = consume(lambda kv: sc_ref[kv], vtc_ref[0], (jnp.full((1, cols), -jnp.inf, F32),) * ATT_KV_HEADS)
    for j in range(n_kv):
        if j + KV_AHEAD < n_kv:
            scores(k_tile(j + KV_AHEAD), put((j + KV_AHEAD) % KV_SLOTS))
        maxima = consume(lambda kv, slot=j % KV_SLOTS: s_ref[slot, kv], vt_tile(j), maxima)
        yield

    for kv in range(ATT_KV_HEADS):
        o = acc_ref[kv, 0:ATT_HEAD_DIM] / acc_ref[kv, ATT_HEAD_DIM:ATT_HEAD_DIM + 1]
        for jj in range(ATT_GROUP // 2):
            pair = jnp.concatenate([o[:, (2 * jj) * tq:(2 * jj + 1) * tq],
                                    o[:, (2 * jj + 1) * tq:(2 * jj + 2) * tq]], axis=0)
            col = kv * (ATT_GROUP // 2) + jj
            o_ref[0, :, col * LANES:(col + 1) * LANES] = pair.T.astype(BF16)


NT_DIMS = (((1,), (1,)), ((), ()))


def _scan_window(st, q, k, v, b, b_row, msk_ref, reverse, want_out):
    nsub = WIN // SUB
    last = 0 if reverse else SUB - 1
    sub_of = (lambda p: nsub - 1 - p) if reverse else (lambda p: p)
    blk = lambda a, i: a[i * SUB:(i + 1) * SUB]
    d = int(reverse)

    b_last = [b_row(i * SUB + last) for i in range(nsub)]
    cum = []
    for p in range(nsub):
        cum.append(b_last[sub_of(p)] if p == 0 else cum[-1] + b_last[sub_of(p)])
    ks = [blk(k, i) * jnp.exp2(b_last[i] - blk(b, i)) for i in range(nsub)]

    k_end = jnp.concatenate(
        [ks[i] if sub_of(i) == nsub - 1 else ks[i] * jnp.exp2(cum[nsub - 1] - cum[sub_of(i)]) for i in range(nsub)],
        axis=0)
    st_new = jnp.exp2(cum[nsub - 1]) * st + jnp.dot(v.T, k_end, preferred_element_type=F32)
    if not want_out:
        return st_new, None

    qs = [blk(q, i) * jnp.exp2(blk(b, i)) for i in range(nsub)]
    q_start = jnp.concatenate(
        [qs[i] if sub_of(i) == 0 else qs[i] * jnp.exp2(cum[sub_of(i) - 1]) for i in range(nsub)], axis=0)
    o_carried = lax.dot_general(q_start, st, NT_DIMS, preferred_element_type=F32)

    scores = None
    for level, h in enumerate((nsub // 2, nsub // 4, nsub // 8)):
        q_rows, k_rows = [], []
        for i in range(nsub):
            p = sub_of(i)
            ref = (p // (2 * h)) * 2 * h + h - 1
            q_rows.append(qs[i] * jnp.exp2(cum[p - 1] - cum[ref]) if p - 1 > ref else qs[i])
            k_rows.append(ks[i] * jnp.exp2(cum[ref] - cum[p]) if p < ref else ks[i])
        term = (lax.dot_general(jnp.concatenate(q_rows, axis=0), jnp.concatenate(k_rows, axis=0), NT_DIMS,
                                preferred_element_type=F32) * msk_ref[d, level])
        scores = term if scores is None else scores + term

    def level_scores(level, q_exponent, k_exponent):
        q_l = q * jnp.exp2(jnp.minimum(q_exponent, 0.0))
        k_l = k if k_exponent is None else k * jnp.exp2(jnp.minimum(k_exponent, 0.0))
        return lax.dot_general(q_l, k_l, NT_DIMS, preferred_element_type=F32) * msk_ref[d, level]

    scores = scores + (jnp.dot(q * k, jnp.ones((LANES, LANES), BF16), preferred_element_type=F32)
                       * msk_ref[d, N_LEVELS])
    for level, h in ((3, SUB // 2), (4, SUB // 4)):
        ref_rows = [b_row(s * 2 * h + (h if reverse else h - 1)) for s in range(WIN // (2 * h))]
        diff = b - jnp.concatenate([jnp.broadcast_to(r, (2 * h, LANES)) for r in ref_rows], axis=0)
        decay = jnp.exp2(jnp.minimum(diff, -diff))
        scores = scores + (lax.dot_general(q * decay, k * decay, NT_DIMS, preferred_element_type=F32)
                           * msk_ref[d, level])
    before = lambda n: pltpu.roll(b, n, 0)
    after = lambda n: pltpu.roll(b, WIN - n, 0)
    off4 = lax.broadcasted_iota(jnp.int32, (WIN, LANES), 0) % 4
    if reverse:
        q_exp = b - jnp.where(off4 == 0, after(2), after(1))
        k_exp = jnp.where(off4 == 2, 0.0, before(1) - b)
    else:
        q_exp = b - jnp.where(off4 == 3, before(2), before(1))
        k_exp = jnp.where(off4 == 1, 0.0, after(1) - b)
    scores = scores + level_scores(5, q_exp, k_exp)
    scores = scores + level_scores(6, b - (after(1) if reverse else before(1)), None)
    return st_new, (scores.astype(BF16), o_carried)


N_LEVELS = 7


def _pair_masks():
    msk = np.zeros((2, N_LEVELS + 1, WIN, WIN), np.float32)
    pos = np.arange(WIN)
    for d in range(2):
        for level in range(N_LEVELS):
            h = WIN >> (level + 1)
            span, off = pos // (2 * h), pos % (2 * h)
            later = (off < h) if d else (off >= h)
            msk[d, level] = (span[:, None] == span[None, :]) & later[:, None] & ~later[None, :]
        msk[d, N_LEVELS] = np.eye(WIN)
    return jnp.asarray(msk, F32)


def _hgrn_steps(q_ref, v_ref, kf_ref, bf_ref, kb_ref, bb_ref, vc_ref, kfc_ref, bfc_ref, kbc_ref, bbc_ref,
                msk_ref, o_ref, stf_ref, stb_ref, scf_ref, scb_ref, ocf_ref, ocb_ref, w0, n_steps):
    t = q_ref.shape[1]
    ctx = vc_ref.shape[1]
    nw = t // WIN

    def window(refs, start, st, reverse, want_out):
        qr, vr, kr, br = refs
        sl = pl.ds(start, WIN)
        vw = vr[0, sl, :].astype(F32)
        kw = kr[0, sl, :].astype(F32)
        bw = br[0, sl, :]
        qw = qr[0, sl, :].astype(F32) if want_out else None
        b_row = lambda r: br[0, pl.ds(start + r, 1), :]
        return _scan_window(st, qw, kw, vw, bw, b_row, msk_ref, reverse, want_out)

    def restart():
        zero = jnp.zeros((LANES, LANES), F32)
        stf, stb = zero, zero
        n_cw = ctx // WIN
        for w in range(n_cw):
            stf, _ = window((None, vc_ref, kfc_ref, bfc_ref), w * WIN, stf, False, False)
            stb, _ = window((None, vc_ref, kbc_ref, bbc_ref), (n_cw - 1 - w) * WIN, stb, True, False)
        stf_ref[...] = stf
        stb_ref[...] = stb
        o_ref[...] = jnp.zeros(o_ref.shape, F32)

    def starts(w):
        return pl.multiple_of(w * WIN, WIN), pl.multiple_of((nw - 1 - w) * WIN, WIN)

    def scores_stage(w):
        fs, bs = starts(w)
        stf, (sf, cf) = window((q_ref, v_ref, kf_ref, bf_ref), fs, stf_ref[...], False, True)
        stf_ref[...] = stf
        scf_ref[...] = sf
        ocf_ref[...] = cf
        stb, (sb, cb) = window((q_ref, v_ref, kb_ref, bb_ref), bs, stb_ref[...], True, True)
        stb_ref[...] = stb
        scb_ref[...] = sb
        ocb_ref[...] = cb

    def output_stage(w):
        fs, bs = starts(w)
        o_ref[0, pl.ds(fs, WIN), :] += (
            jnp.dot(scf_ref[...], v_ref[0, pl.ds(fs, WIN), :], preferred_element_type=F32) + ocf_ref[...])
        o_ref[0, pl.ds(bs, WIN), :] += (
            jnp.dot(scb_ref[...], v_ref[0, pl.ds(bs, WIN), :], preferred_element_type=F32) + ocb_ref[...])

    def schedule():
        for u in range(n_steps + 1):
            if u > 0:
                output_stage(w0 + u - 1)
            if u < n_steps:
                scores_stage(w0 + u)
            yield

    return restart, schedule()


def _mixer_kernel(qa_ref, kc_ref, vtc_ref, kx_ref, vtx_ref,
                  hq_ref, hv_ref, kf_ref, bf_ref, kb_ref, bb_ref, hvc_ref, kfc_ref, bfc_ref, kbc_ref, bbc_ref,
                  msk_ref, att_ref, hg_ref, sc_ref, s_ref, acc_ref, *hg_scratch, steps_per_head):
    nw = hq_ref.shape[1] // WIN
    n_steps = nw // steps_per_head
    phase = pl.program_id(1) % steps_per_head
    restart, scan = _hgrn_steps(hq_ref, hv_ref, kf_ref, bf_ref, kb_ref, bb_ref,
                                hvc_ref, kfc_ref, bfc_ref, kbc_ref, bbc_ref, msk_ref, hg_ref, *hg_scratch,
                                w0=phase * n_steps, n_steps=n_steps)
    pl.when(phase == 0)(restart)

    for _ in _attention_schedule(qa_ref, kc_ref, vtc_ref, kx_ref, vtx_ref, att_ref, sc_ref, s_ref, acc_ref):
        next(scan, None)
    for _ in scan:
        pass


def _mixers(qa, kc, vtc, kx, vtx, hq, hv, kf, bf, kb, bb, hvc, kfc, bfc, kbc, bbc):
    b, t, _ = qa.shape
    ctx = hvc.shape[1]
    tq = min(Q_TILE, t)
    n_q = t // tq
    nw = t // WIN
    assert tq % LANES == 0 and t % tq == 0 and t % KV_TILE == 0 and KV_AHEAD < KV_SLOTS
    assert n_q % HG_HEADS == 0 and nw % (n_q // HG_HEADS) == 0
    steps_per_head = n_q // HG_HEADS
    cols = ATT_GROUP * tq
    whole = lambda a: pl.BlockSpec((1,) + a.shape[1:], lambda i, j: (i, 0, 0))
    lat = pl.BlockSpec((1, t, HG_HEAD_DIM), lambda i, j: (i, 0, j // steps_per_head))
    con = pl.BlockSpec((1, ctx, HG_HEAD_DIM), lambda i, j: (i, 0, j // steps_per_head))
    msk = _pair_masks()
    const = lambda a: pl.BlockSpec(a.shape, lambda i, j: (0,) * a.ndim)
    return pl.pallas_call(
        functools.partial(_mixer_kernel, steps_per_head=steps_per_head),
        out_shape=[jax.ShapeDtypeStruct((b, t, ATT_WIDTH), BF16), jax.ShapeDtypeStruct((b, t, HG_WIDTH), F32)],
        grid=(b, n_q),
        in_specs=[pl.BlockSpec((1, tq, qa.shape[-1]), lambda i, j: (i, j, 0)),
                  whole(kc), whole(vtc), whole(kx), whole(vtx)] + [lat] * 6 + [con] * 5
                 + [const(msk)],
        out_specs=[pl.BlockSpec((1, tq, ATT_WIDTH), lambda i, j: (i, j, 0)), lat],
        scratch_shapes=[pltpu.VMEM((ATT_KV_HEADS, kc.shape[1], cols), F32),
                        pltpu.VMEM((KV_SLOTS, ATT_KV_HEADS, KV_TILE, cols), F32),
                        pltpu.VMEM((ATT_KV_HEADS, ATT_HEAD_DIM + ONES_ROWS, cols), F32),
                        pltpu.VMEM((LANES, LANES), F32), pltpu.VMEM((LANES, LANES), F32),
                        pltpu.VMEM((WIN, WIN), BF16), pltpu.VMEM((WIN, WIN), BF16),
                        pltpu.VMEM((WIN, LANES), F32), pltpu.VMEM((WIN, LANES), F32)],
        compiler_params=pltpu.CompilerParams(dimension_semantics=("arbitrary", "arbitrary"),
                                             vmem_limit_bytes=VMEM_LIMIT),
        name="mixers",
    )(qa, kc, vtc, kx, vtx, hq, hv, kf, bf, kb, bb, hvc, kfc, bfc, kbc, bbc, msk)


def _out_ffn_kernel(x_ref, att_ref, hg_ref, sg_ref, mod_ref, ghg_ref, wo_in, g2_ref, wgu_in, wd_in, gf_ref,
                    o_ref, wo_ref, wgu_ref, wd_ref):
    step = pl.program_id(0)

    @pl.when(step < OUT_WEIGHT_STEPS)
    def _():
        for src, dst in ((wo_in, wo_ref), (wgu_in, wgu_ref), (wd_in, wd_ref)):
            rows = src.shape[0]
            dst[pl.ds(pl.multiple_of(step * rows, rows), rows), :] = src[...].astype(BF16)

    @pl.when(step >= OUT_WEIGHT_STEPS)
    def _():
        _out_ffn_tile(x_ref, att_ref, hg_ref, sg_ref, mod_ref, ghg_ref, wo_ref, g2_ref, wgu_ref, wd_ref, gf_ref,
                      o_ref)


def _out_ffn_tile(x_ref, att_ref, hg_ref, sg_ref, mod_ref, ghg_ref, wo_ref, g2_ref, wgu_ref, wd_ref, gf_ref,
                  o_ref):
    d = x_ref.shape[-1]
    dff = wd_ref.shape[0]
    gate1 = mod_ref[0, :, 2 * d:3 * d]
    shift2 = mod_ref[0, :, 3 * d:4 * d]
    scale2 = mod_ref[0, :, 4 * d:5 * d]
    gate2 = mod_ref[0, :, 5 * d:6 * d]

    parts = [slice(r0, r0 + OUT_PART_ROWS) for r0 in range(0, x_ref.shape[1], OUT_PART_ROWS)]

    def mixed(rows):
        heads = []
        for hd in range(HG_HEADS):
            cs = slice(hd * HG_HEAD_DIM, (hd + 1) * HG_HEAD_DIM)
            o = _rms_rows(hg_ref[0, rows, cs]) * ghg_ref[...]
            heads.append((o * sg_ref[0, rows, cs].astype(F32)).astype(BF16))
        mix = (jnp.dot(att_ref[0, rows, :], wo_ref[0:ATT_WIDTH, :], preferred_element_type=F32)
               + jnp.dot(jnp.concatenate(heads, axis=1), wo_ref[ATT_WIDTH:ATT_WIDTH + HG_WIDTH, :],
                         preferred_element_type=F32))
        x1 = x_ref[0, rows, :] + gate1 * mix
        return x1, ((_rms_rows(x1) * g2_ref[...]) * (1.0 + scale2) + shift2).astype(BF16)

    def gate_up(h):
        return (jnp.dot(h, wgu_ref[:, 0:dff], preferred_element_type=F32),
                jnp.dot(h, wgu_ref[:, dff:2 * dff], preferred_element_type=F32))

    def down(a, g):
        return jnp.dot((_silu(a) * g).astype(BF16), wd_ref[...], preferred_element_type=F32)

    firsts = [mixed(rows) for rows in parts]
    ups = [gate_up(h) for _, h in firsts]
    ys = [down(a, g) for a, g in ups]
    for rows, (x1, _), y in zip(parts, firsts, ys):
        o_ref[0, rows, :] = _rms_rows(x1 + gate2 * y) * gf_ref[...]


def _out_ffn(x, att, hg, sg, mod3, ghg, w_out, g2, w_gu, w_down, gf):
    b, t, d = x.shape
    tm = min(OUT_ROW_TILE, t)
    assert tm % OUT_PART_ROWS == 0
    nt = t // tm
    tile = lambda s: jnp.maximum(s - OUT_WEIGHT_STEPS, 0)
    bspec = lambda width: pl.BlockSpec((1, tm, width), lambda s: (tile(s) // nt, tile(s) % nt, 0))
    full = lambda a: pl.BlockSpec(a.shape, lambda s: (0,) * a.ndim, pipeline_mode=pl.Buffered(1))

    def chunked(w):
        assert w.shape[0] % (OUT_WEIGHT_STEPS * SUB) == 0
        return pl.BlockSpec((w.shape[0] // OUT_WEIGHT_STEPS, w.shape[1]),
                            lambda s: (jnp.minimum(s, OUT_WEIGHT_STEPS - 1), 0))

    return pl.pallas_call(
        _out_ffn_kernel,
        out_shape=jax.ShapeDtypeStruct((b, t, d), F32),
        grid=(OUT_WEIGHT_STEPS + b * nt,),
        in_specs=[bspec(d), bspec(ATT_WIDTH), bspec(HG_WIDTH), bspec(HG_WIDTH),
                  pl.BlockSpec((1, 1, mod3.shape[-1]), lambda s: (tile(s) // nt, 0, 0)),
                  full(ghg), chunked(w_out), full(g2), chunked(w_gu), chunked(w_down), full(gf)],
        out_specs=bspec(d),
        scratch_shapes=[pltpu.VMEM(w_out.shape, BF16), pltpu.VMEM(w_gu.shape, BF16),
                        pltpu.VMEM(w_down.shape, BF16)],
        compiler_params=pltpu.CompilerParams(dimension_semantics=("arbitrary",),
                                             vmem_limit_bytes=VMEM_LIMIT),
        name="out_ffn",
    )(x, att, hg, sg, mod3, ghg, w_out, g2, w_gu, w_down, gf)


def _rope_tables(t):
    n_rows = t // GRID_W
    row = np.repeat(np.arange(n_rows, dtype=np.float32), GRID_W)
    col = np.tile(np.arange(GRID_W, dtype=np.float32), n_rows)
    inv = (np.float32(ROPE_THETA) ** (-np.arange(0, AXIS_DIM, 2, dtype=np.float32) / np.float32(AXIS_DIM)))
    inv = inv.astype(np.float32)
    ar = row[:, None] * inv
    ac = col[:, None] * inv
    ang = np.concatenate([ar, ar, ac, ac], axis=-1).astype(np.float32)
    sign = np.where((np.arange(ATT_HEAD_DIM) % AXIS_DIM) < AXIS_DIM // 2, -1.0, 1.0)
    reps = LANES // ATT_HEAD_DIM
    cos = np.tile(np.cos(ang.astype(np.float64)), (1, reps)).astype(np.float32)
    sin = np.tile(np.sin(ang.astype(np.float64)) * sign, (1, reps)).astype(np.float32)
    return jnp.asarray(cos), jnp.asarray(sin)


def kernel(x, c, ctx, c_ctx, w_mod, b_mod, g_norm1, w_in, g_q, g_k, lb_raw, g_hg, w_out, g_norm2, w_gu, w_down,
           g_final):
    b, t, d = x.shape
    assert w_mod.shape[0] == 1 and lb_raw.shape[0] == 2, "single-layer block"
    assert t % WIN == 0 and (t // WIN) % 2 == 0 and ctx.shape[1] % WIN == 0 and t % KV_TILE == 0

    pad = (-(b + 1)) % 8
    cc = jnp.concatenate([c, c_ctx[None, :], jnp.zeros((pad, d), F32)], axis=0)
    mods = _modulation(cc, w_mod[0], b_mod)
    mod3 = mods.reshape(mods.shape[0], 1, mods.shape[1])

    cos, sin = _rope_tables(t)
    reps = LANES // ATT_HEAD_DIM
    gq = jnp.tile(g_q[0], reps)[None, :]
    gk = jnp.tile(g_k[0], reps)[None, :]
    w_in_b = w_in[0].astype(BF16)

    qa, kx, vtx, hq, hv, kf, bf, kb, bb, sg = _input_projection(
        x, mod3, None, g_norm1, w_in_b, gq, gk, lb_raw, cos, sin, True)
    kc, vtc, hvc, kfc, bfc, kbc, bbc = _input_projection(
        ctx, mod3, b, g_norm1, w_in_b, gq, gk, lb_raw, cos, sin, False)

    att, hg = _mixers(qa, kc, vtc, kx, vtx, hq, hv, kf, bf, kb, bb, hvc, kfc, bfc, kbc, bbc)
    return _out_ffn(x, att, hg, sg, mod3, g_hg, w_out[0], g_norm2, w_gu[0], w_down[0], g_final[None, :])
```

```python
import functools

import jax
import jax.numpy as jnp
import numpy as np
from jax import lax
from jax.experimental import pallas as pl
from jax.experimental.pallas import tpu as pltpu

F32 = jnp.float32
BF16 = jnp.bfloat16

LANES = 128
GRID_W = 64
ATT_HEADS = 8
ATT_KV_HEADS = 2
ATT_GROUP = ATT_HEADS // ATT_KV_HEADS
ATT_HEAD_DIM = 64
ATT_WIDTH = ATT_HEADS * ATT_HEAD_DIM
ATT_KV_WIDTH = ATT_KV_HEADS * ATT_HEAD_DIM
AXIS_DIM = ATT_HEAD_DIM // 2
ROPE_THETA = 10000.0
HG_HEADS = 4
HG_HEAD_DIM = 128
HG_WIDTH = HG_HEADS * HG_HEAD_DIM
EPS = 1e-6

MOD_COLUMN_BLOCKS = 8
SUB = 16
WIN = 128
ROW_TILE = 256
IN_WEIGHT_STEPS = 8
OUT_ROW_TILE = 512
OUT_PART_ROWS = 256
OUT_WEIGHT_STEPS = 8
Q_TILE = 256
KV_TILE = 512
KV_SLOTS = 4
KV_AHEAD = 2
ONES_ROWS = 16
LOG2E = 1.4426950408889634
VMEM_LIMIT = 56 * 1024 * 1024

C_AQ = 0
C_AK = C_AQ + ATT_WIDTH
C_AV = C_AK + ATT_KV_WIDTH
C_HQ = C_AV + ATT_KV_WIDTH
C_HI = C_HQ + HG_WIDTH
C_FF = C_HI + HG_WIDTH
C_FB = C_FF + HG_WIDTH
C_HG = C_FB + HG_WIDTH


def _silu(v):
    return v * jax.nn.sigmoid(v)


def _rms_rows(v):
    return v * lax.rsqrt(jnp.mean(v * v, axis=-1, keepdims=True) + EPS)


def _mod_kernel(c_ref, cx_ref, w_ref, b_ref, o_ref, a_ref):
    nb = c_ref.shape[0]
    a_ref[...] = jnp.zeros(a_ref.shape, F32)
    a_ref[0:nb, :] = c_ref[...]
    a_ref[nb:nb + 1, :] = cx_ref[...]
    a = _silu(a_ref[...])
    rows = a.shape[0]
    w = w_ref[...]
    a_hi = a.astype(BF16)
    a_lo = (a - a_hi.astype(F32)).astype(BF16)
    w_hi = w.astype(BF16)
    w_lo = (w - w_hi.astype(F32)).astype(BF16)
    both = jnp.dot(jnp.concatenate([a_hi, a_lo], axis=0), w_hi, preferred_element_type=F32)
    o_ref[...] = (both[0:rows] + both[rows:2 * rows]
                  + jnp.dot(a_hi, w_lo, preferred_element_type=F32) + b_ref[...])


def _modulation(c, c_ctx, w_mod, b_mod):
    nb, d = c.shape
    rows = -(-(nb + 1) // 8) * 8
    n = w_mod.shape[1]
    bn = n // MOD_COLUMN_BLOCKS
    return pl.pallas_call(
        _mod_kernel,
        out_shape=jax.ShapeDtypeStruct((rows, n), F32),
        grid=(n // bn,),
        in_specs=[pl.BlockSpec((nb, d), lambda j: (0, 0)),
                  pl.BlockSpec((1, d), lambda j: (0, 0)),
                  pl.BlockSpec((d, bn), lambda j: (0, j)),
                  pl.BlockSpec((1, bn), lambda j: (0, j))],
        out_specs=pl.BlockSpec((rows, bn), lambda j: (0, j)),
        scratch_shapes=[pltpu.VMEM((rows, d), F32)],
        compiler_params=pltpu.CompilerParams(dimension_semantics=("arbitrary",),
                                             vmem_limit_bytes=VMEM_LIMIT),
        name="modulation",
    )(c, c_ctx[None, :], w_mod, b_mod)


def _head64_meansq(p, seg2):
    sq = p * p
    hi = sq.astype(BF16)
    lo = (sq - hi.astype(F32)).astype(BF16)
    s = jnp.dot(jnp.concatenate([hi, lo], axis=1), seg2, preferred_element_type=F32)
    return s * (1.0 / ATT_HEAD_DIM)


def _rope(v, cos, sin_signed, first_half):
    fwd = pltpu.roll(v, LANES - AXIS_DIM // 2, 1)
    bwd = pltpu.roll(v, AXIS_DIM // 2, 1)
    return v * cos + jnp.where(first_half, fwd, bwd) * sin_signed


def _gate_prep(f, lb, suffix):
    fg = lb + (1.0 - lb) * jax.nn.sigmoid(f)
    cum = jnp.log2(fg)
    rows = cum.shape[0]
    pos = lax.broadcasted_iota(jnp.int32, cum.shape, 0) % SUB
    shift = 1
    while shift < SUB:
        if suffix:
            moved = pltpu.roll(cum, rows - shift, 0)
            cum = cum + jnp.where(pos < SUB - shift, moved, 0.0)
        else:
            moved = pltpu.roll(cum, shift, 0)
            cum = cum + jnp.where(pos >= shift, moved, 0.0)
        shift *= 2
    return (1.0 - fg).astype(BF16), cum


def _inproj_kernel(x_ref, mod_ref, g1_ref, w_ref, gq_ref, gk_ref, lb_ref, cos_ref, sin_ref, *out_refs, latent,
                   mod_row):
    d = x_ref.shape[-1]
    x = x_ref[0]
    row = pl.ds(mod_row, 1)
    shift = mod_ref[row, 0:d]
    scale = mod_ref[row, d:2 * d]
    gq = jnp.concatenate([gq_ref[...]] * (LANES // ATT_HEAD_DIM), axis=1)
    gk = jnp.concatenate([gk_ref[...]] * (LANES // ATT_HEAD_DIM), axis=1)
    h = (_rms_rows(x) * g1_ref[...]) * (1.0 + scale) + shift
    hb = h.astype(BF16)

    def proj(c0, width):
        return jnp.dot(hb, w_ref[:, c0:c0 + width], preferred_element_type=F32)

    lane = lax.broadcasted_iota(jnp.int32, (2 * LANES, LANES), 0) % LANES
    lane_c = lax.broadcasted_iota(jnp.int32, (2 * LANES, LANES), 1)
    seg = (lane // ATT_HEAD_DIM == lane_c // ATT_HEAD_DIM).astype(BF16)
    row_lane = lax.broadcasted_iota(jnp.int32, (x.shape[0], LANES), 1)
    first_half = (row_lane % AXIS_DIM) < (AXIS_DIM // 2)
    upper = row_lane >= ATT_HEAD_DIM

    r0 = lb_ref[0]
    r1 = lb_ref[1]
    rm = jnp.maximum(r0, r1)
    e0 = jnp.exp(r0 - rm)
    e1 = jnp.exp(r1 - rm)
    lb_all = e0 / (e0 + e1)

    if latent:
        (qa_ref, ka_ref, vat_ref, hq_ref, hv_ref, kf_ref, bf_ref, kb_ref, bb_ref, sg_ref) = out_refs
        cos = cos_ref[...]
        sin = sin_ref[...]
    else:
        (ka_ref, vat_ref, hv_ref, kf_ref, bf_ref, kb_ref, bb_ref) = out_refs

    p_ff = proj(C_FF, HG_WIDTH)
    if latent:
        pq = proj(C_AQ, ATT_WIDTH)
    pk = proj(C_AK, ATT_KV_WIDTH)
    pv = proj(C_AV, ATT_KV_WIDTH)
    p_fb = proj(C_FB, HG_WIDTH)
    if latent:
        p_hq = proj(C_HQ, HG_WIDTH)
    p_hv = proj(C_HI, HG_WIDTH)
    ms_k = _head64_meansq(pk, seg)
    if latent:
        ms_q = [_head64_meansq(pq[:, c * LANES:(c + 1) * LANES], seg) for c in range(ATT_WIDTH // LANES)]

    kn = pk * lax.rsqrt(ms_k + EPS) * gk
    if latent:
        kn = _rope(kn, cos, sin, first_half)
    ka_ref[0] = kn.astype(BF16)
    vat_ref[0] = pv.T.astype(BF16)

    if latent:
        for c in range(ATT_WIDTH // LANES):
            t = pq[:, c * LANES:(c + 1) * LANES]
            t = t * lax.rsqrt(ms_q[c] + EPS) * gq
            t = _rope(t, cos, sin, first_half) * (ATT_HEAD_DIM ** -0.5 * LOG2E)
            t_swapped = pltpu.roll(t, ATT_HEAD_DIM, 1)
            for half in range(2):
                head = 2 * c + half
                kv = head // ATT_GROUP
                src = t if half == kv else t_swapped
                keep = upper if kv == 1 else jnp.logical_not(upper)
                qa_ref[0, :, head * LANES:(head + 1) * LANES] = jnp.where(keep, src, 0.0).astype(BF16)
        hq_ref[0] = (_silu(p_hq) * (HG_HEAD_DIM ** -0.5)).astype(BF16)
        sg_ref[0] = _silu(proj(C_HG, HG_WIDTH)).astype(BF16)

    hv_ref[0] = p_hv.astype(BF16)
    kf, bf = _gate_prep(p_ff, lb_all[0:1, :], False)
    kf_ref[0] = kf
    bf_ref[0] = bf
    kb, bb = _gate_prep(p_fb, lb_all[1:2, :], True)
    kb_ref[0] = kb
    bb_ref[0] = bb


def _inproj_latent_kernel(x_ref, mod_ref, g1_ref, w_in, gq_ref, gk_ref, lb_ref, cos_ref, sin_ref, *refs,
                          tiles_per_batch):
    *out_refs, wb_ref, w_ref = refs
    step = pl.program_id(0)

    @pl.when(step < IN_WEIGHT_STEPS)
    def _():
        rows = w_in.shape[0]
        chunk = w_in[...].astype(BF16)
        w_ref[pl.ds(pl.multiple_of(step * rows, rows), rows), :] = chunk
        wb_ref[...] = chunk

    @pl.when(step >= IN_WEIGHT_STEPS)
    def _():
        _inproj_kernel(x_ref, mod_ref, g1_ref, w_ref, gq_ref, gk_ref, lb_ref, cos_ref, sin_ref, *out_refs,
                       latent=True, mod_row=(step - IN_WEIGHT_STEPS) // tiles_per_batch)


def _input_projection_latent(rows, mods, g1, w_in, gq, gk, lb_raw, cos, sin):
    b, t, d = rows.shape
    tm = min(ROW_TILE, t)
    nt = t // tm
    assert w_in.shape[0] % (IN_WEIGHT_STEPS * SUB) == 0
    chunk_rows = w_in.shape[0] // IN_WEIGHT_STEPS
    tile = lambda s: jnp.maximum(s - IN_WEIGHT_STEPS, 0)
    chunk = lambda s: jnp.minimum(s, IN_WEIGHT_STEPS - 1)
    bspec = lambda width: pl.BlockSpec((1, tm, width), lambda s: (tile(s) // nt, tile(s) % nt, 0))
    full = lambda a: pl.BlockSpec(a.shape, lambda s: (0,) * a.ndim)
    sds = lambda width, dt: jax.ShapeDtypeStruct((b, t, width), dt)
    table_spec = pl.BlockSpec((tm, LANES), lambda s: (tile(s) % nt, 0))
    w_spec = pl.BlockSpec((chunk_rows, w_in.shape[1]), lambda s: (chunk(s), 0))
    out_shape = [sds(ATT_HEADS * LANES, BF16), sds(ATT_KV_WIDTH, BF16),
                 jax.ShapeDtypeStruct((b, ATT_KV_WIDTH, t), BF16),
                 sds(HG_WIDTH, BF16), sds(HG_WIDTH, BF16), sds(HG_WIDTH, BF16), sds(HG_WIDTH, F32),
                 sds(HG_WIDTH, BF16), sds(HG_WIDTH, F32), sds(HG_WIDTH, BF16),
                 jax.ShapeDtypeStruct(w_in.shape, BF16)]
    out_specs = [bspec(ATT_HEADS * LANES), bspec(ATT_KV_WIDTH),
                 pl.BlockSpec((1, ATT_KV_WIDTH, tm), lambda s: (tile(s) // nt, 0, tile(s) % nt)),
                 bspec(HG_WIDTH), bspec(HG_WIDTH), bspec(HG_WIDTH), bspec(HG_WIDTH), bspec(HG_WIDTH),
                 bspec(HG_WIDTH), bspec(HG_WIDTH), w_spec]
    return pl.pallas_call(
        functools.partial(_inproj_latent_kernel, tiles_per_batch=nt),
        out_shape=out_shape,
        grid=(IN_WEIGHT_STEPS + b * nt,),
        in_specs=[bspec(d), full(mods), full(g1), w_spec, full(gq), full(gk), full(lb_raw),
                  table_spec, table_spec],
        out_specs=out_specs,
        scratch_shapes=[pltpu.VMEM(w_in.shape, BF16)],
        compiler_params=pltpu.CompilerParams(dimension_semantics=("arbitrary",),
                                             vmem_limit_bytes=VMEM_LIMIT),
        name="inproj_latent",
    )(rows, mods, g1, w_in, gq, gk, lb_raw, cos, sin)


def _input_projection_context(rows, mods, mod_row, g1, w_in, gq, gk, lb_raw):
    b, t, d = rows.shape
    tm = min(ROW_TILE, t)
    nt = t // tm
    bspec = lambda width: pl.BlockSpec((1, tm, width), lambda i, j: (i, j, 0))
    full = lambda a: pl.BlockSpec(a.shape, lambda i, j: (0,) * a.ndim)
    sds = lambda width, dt: jax.ShapeDtypeStruct((b, t, width), dt)
    out_shape = [sds(ATT_KV_WIDTH, BF16), jax.ShapeDtypeStruct((b, ATT_KV_WIDTH, t), BF16),
                 sds(HG_WIDTH, BF16), sds(HG_WIDTH, BF16), sds(HG_WIDTH, F32), sds(HG_WIDTH, BF16),
                 sds(HG_WIDTH, F32)]
    out_specs = [bspec(ATT_KV_WIDTH), pl.BlockSpec((1, ATT_KV_WIDTH, tm), lambda i, j: (i, 0, j)),
                 bspec(HG_WIDTH), bspec(HG_WIDTH), bspec(HG_WIDTH), bspec(HG_WIDTH), bspec(HG_WIDTH)]
    return pl.pallas_call(
        functools.partial(_inproj_context_kernel, mod_row=mod_row),
        out_shape=out_shape,
        grid=(b, nt),
        in_specs=[bspec(d), full(mods), full(g1), full(w_in), full(gq), full(gk), full(lb_raw)],
        out_specs=out_specs,
        compiler_params=pltpu.CompilerParams(dimension_semantics=("parallel", "parallel"),
                                             vmem_limit_bytes=VMEM_LIMIT),
        name="inproj_context",
    )(rows, mods, g1, w_in, gq, gk, lb_raw)


def _inproj_context_kernel(x_ref, mod_ref, g1_ref, w_ref, gq_ref, gk_ref, lb_ref, *out_refs, mod_row):
    _inproj_kernel(x_ref, mod_ref, g1_ref, w_ref, gq_ref, gk_ref, lb_ref, None, None, *out_refs,
                   latent=False, mod_row=mod_row)


def _attention_schedule(q_ref, kc_ref, vtc_ref, kx_ref, vtx_ref, o_ref, sc_ref, s_ref, acc_ref):
    tq = q_ref.shape[1]
    n_kv = kx_ref.shape[1] // KV_TILE
    nt_dims = (((1,), (1,)), ((), ()))
    cols = ATT_GROUP * tq

    qgs = [jnp.concatenate(
        [q_ref[0, :, (kv * ATT_GROUP + j) * LANES:(kv * ATT_GROUP + j + 1) * LANES] for j in range(ATT_GROUP)],
        axis=0) for kv in range(ATT_KV_HEADS)]

    def scores(kt, dst):
        for kv in range(ATT_KV_HEADS):
            dst(kv, lax.dot_general(kt, qgs[kv], nt_dims, preferred_element_type=F32))

    def consume(src, vt_all, maxima):
        ones = jnp.ones((ONES_ROWS, vt_all.shape[1]), BF16)
        out = []
        for kv in range(ATT_KV_HEADS):
            m = maxima[kv]
            m_new = jnp.maximum(m, jnp.max(src(kv), axis=0, keepdims=True))
            alpha = jnp.exp2(m - m_new)
            p = jnp.exp2(src(kv) - m_new).astype(BF16)
            vt = jnp.concatenate([vt_all[kv * ATT_HEAD_DIM:(kv + 1) * ATT_HEAD_DIM], ones], axis=0)
            acc_ref[kv] = alpha * acc_ref[kv] + jnp.dot(vt, p, preferred_element_type=F32)
            out.append(m_new)
        return tuple(out)

    def put_ctx(kv, val):
        sc_ref[kv] = val

    def put(slot):
        def dst(kv, val):
            s_ref[slot, kv] = val
        return dst

    def k_tile(j):
        return kx_ref[0, j * KV_TILE:(j + 1) * KV_TILE, :]

    def vt_tile(j):
        return vtx_ref[0, :, j * KV_TILE:(j + 1) * KV_TILE]

    acc_ref[...] = jnp.zeros(acc_ref.shape, F32)
    scores(kc_ref[0], put_ctx)
    for j in range(min(KV_AHEAD, n_kv)):
        scores(k_tile(j), put(j % KV_SLOTS))
    maxima = consume(lambda kv: sc_ref[kv], vtc_ref[0], (jnp.full((1, cols), -jnp.inf, F32),) * ATT_KV_HEADS)
    for j in range(n_kv):
        if j + KV_AHEAD < n_kv:
            scores(k_tile(j + KV_AHEAD), put((j + KV_AHEAD) % KV_SLOTS))
        maxima = consume(lambda kv, slot=j % KV_SLOTS: s_ref[slot, kv], vt_tile(j), maxima)
        yield

    for kv in range(ATT_KV_HEADS):
        o = acc_ref[kv, 0:ATT_HEAD_DIM] / acc_ref[kv, ATT_HEAD_DIM:ATT_HEAD_DIM + 1]
        for jj in range(ATT_GROUP // 2):
            pair = jnp.concatenate([o[:, (2 * jj) * tq:(2 * jj + 1) * tq],
                                    o[:, (2 * jj + 1) * tq:(2 * jj + 2) * tq]], axis=0)
            col = kv * (ATT_GROUP // 2) + jj
            o_ref[0, :, col * LANES:(col + 1) * LANES] = pair.T.astype(BF16)


NT_DIMS = (((1,), (1,)), ((), ()))


def _scan_window(st, q, k, v, b, b_row, msk_ref, reverse, want_out):
    nsub = WIN // SUB
    last = 0 if reverse else SUB - 1
    sub_of = (lambda p: nsub - 1 - p) if reverse else (lambda p: p)
    blk = lambda a, i: a[i * SUB:(i + 1) * SUB]
    d = int(reverse)

    b_last = [b_row(i * SUB + last) for i in range(nsub)]
    cum = []
    for p in range(nsub):
        cum.append(b_last[sub_of(p)] if p == 0 else cum[-1] + b_last[sub_of(p)])
    ks = [blk(k, i) * jnp.exp2(b_last[i] - blk(b, i)) for i in range(nsub)]

    k_end = jnp.concatenate(
        [ks[i] if sub_of(i) == nsub - 1 else ks[i] * jnp.exp2(cum[nsub - 1] - cum[sub_of(i)]) for i in range(nsub)],
        axis=0)
    st_new = jnp.exp2(cum[nsub - 1]) * st + jnp.dot(v.T, k_end, preferred_element_type=F32)
    if not want_out:
        return st_new, None

    qs = [blk(q, i) * jnp.exp2(blk(b, i)) for i in range(nsub)]
    q_start = jnp.concatenate(
        [qs[i] if sub_of(i) == 0 else qs[i] * jnp.exp2(cum[sub_of(i) - 1]) for i in range(nsub)], axis=0)
    o_carried = lax.dot_general(q_start, st, NT_DIMS, preferred_element_type=F32)

    scores = None
    for level, h in enumerate((nsub // 2, nsub // 4, nsub // 8)):
        q_rows, k_rows = [], []
        for i in range(nsub):
            p = sub_of(i)
            ref = (p // (2 * h)) * 2 * h + h - 1
            q_rows.append(qs[i] * jnp.exp2(cum[p - 1] - cum[ref]) if p - 1 > ref else qs[i])
            k_rows.append(ks[i] * jnp.exp2(cum[ref] - cum[p]) if p < ref else ks[i])
        term = (lax.dot_general(jnp.concatenate(q_rows, axis=0), jnp.concatenate(k_rows, axis=0), NT_DIMS,
                                preferred_element_type=F32) * msk_ref[d, level])
        scores = term if scores is None else scores + term

    def level_scores(level, q_exponent, k_exponent):
        q_l = q * jnp.exp2(jnp.minimum(q_exponent, 0.0))
        k_l = k if k_exponent is None else k * jnp.exp2(jnp.minimum(k_exponent, 0.0))
        return lax.dot_general(q_l, k_l, NT_DIMS, preferred_element_type=F32) * msk_ref[d, level]

    scores = scores + (jnp.dot(q * k, jnp.ones((LANES, LANES), BF16), preferred_element_type=F32)
                       * msk_ref[d, N_LEVELS])
    for level, h in ((3, SUB // 2), (4, SUB // 4)):
        ref_rows = [b_row(s * 2 * h + (h if reverse else h - 1)) for s in range(WIN // (2 * h))]
        diff = b - jnp.concatenate([jnp.broadcast_to(r, (2 * h, LANES)) for r in ref_rows], axis=0)
        decay = jnp.exp2(jnp.minimum(diff, -diff))
        scores = scores + (lax.dot_general(q * decay, k * decay, NT_DIMS, preferred_element_type=F32)
                           * msk_ref[d, level])
    before = lambda n: pltpu.roll(b, n, 0)
    after = lambda n: pltpu.roll(b, WIN - n, 0)
    off4 = lax.broadcasted_iota(jnp.int32, (WIN, LANES), 0) % 4
    if reverse:
        q_exp = b - jnp.where(off4 == 0, after(2), after(1))
        k_exp = jnp.where(off4 == 2, 0.0, before(1) - b)
    else:
        q_exp = b - jnp.where(off4 == 3, before(2), before(1))
        k_exp = jnp.where(off4 == 1, 0.0, after(1) - b)
    scores = scores + level_scores(5, q_exp, k_exp)
    scores = scores + level_scores(6, b - (after(1) if reverse else before(1)), None)
    return st_new, (scores.astype(BF16), o_carried)


N_LEVELS = 7


def _pair_masks():
    msk = np.zeros((2, N_LEVELS + 1, WIN, WIN), np.float32)
    pos = np.arange(WIN)
    for d in range(2):
        for level in range(N_LEVELS):
            h = WIN >> (level + 1)
            span, off = pos // (2 * h), pos % (2 * h)
            later = (off < h) if d else (off >= h)
            msk[d, level] = (span[:, None] == span[None, :]) & later[:, None] & ~later[None, :]
        msk[d, N_LEVELS] = np.eye(WIN)
    return jnp.asarray(msk, F32)


def _hgrn_steps(q_ref, v_ref, kf_ref, bf_ref, kb_ref, bb_ref, vc_ref, kfc_ref, bfc_ref, kbc_ref, bbc_ref,
                msk_ref, o_ref, stf_ref, stb_ref, scf_ref, scb_ref, ocf_ref, ocb_ref, w0, n_steps):
    t = q_ref.shape[1]
    ctx = vc_ref.shape[1]
    nw = t // WIN

    def window(refs, start, st, reverse, want_out):
        qr, vr, kr, br = refs
        sl = pl.ds(start, WIN)
        vw = vr[0, sl, :].astype(F32)
        kw = kr[0, sl, :].astype(F32)
        bw = br[0, sl, :]
        qw = qr[0, sl, :].astype(F32) if want_out else None
        b_row = lambda r: br[0, pl.ds(start + r, 1), :]
        return _scan_window(st, qw, kw, vw, bw, b_row, msk_ref, reverse, want_out)

    def restart():
        zero = jnp.zeros((LANES, LANES), F32)
        stf, stb = zero, zero
        n_cw = ctx // WIN
        for w in range(n_cw):
            stf, _ = window((None, vc_ref, kfc_ref, bfc_ref), w * WIN, stf, False, False)
            stb, _ = window((None, vc_ref, kbc_ref, bbc_ref), (n_cw - 1 - w) * WIN, stb, True, False)
        stf_ref[...] = stf
        stb_ref[...] = stb
        o_ref[...] = jnp.zeros(o_ref.shape, F32)

    def starts(w):
        return pl.multiple_of(w * WIN, WIN), pl.multiple_of((nw - 1 - w) * WIN, WIN)

    def scores_stage(w):
        fs, bs = starts(w)
        stf, (sf, cf) = window((q_ref, v_ref, kf_ref, bf_ref), fs, stf_ref[...], False, True)
        stf_ref[...] = stf
        scf_ref[...] = sf
        ocf_ref[...] = cf
        stb, (sb, cb) = window((q_ref, v_ref, kb_ref, bb_ref), bs, stb_ref[...], True, True)
        stb_ref[...] = stb
        scb_ref[...] = sb
        ocb_ref[...] = cb

    def output_stage(w):
        fs, bs = starts(w)
        o_ref[0, pl.ds(fs, WIN), :] += (
            jnp.dot(scf_ref[...], v_ref[0, pl.ds(fs, WIN), :], preferred_element_type=F32) + ocf_ref[...])
        o_ref[0, pl.ds(bs, WIN), :] += (
            jnp.dot(scb_ref[...], v_ref[0, pl.ds(bs, WIN), :], preferred_element_type=F32) + ocb_ref[...])

    def schedule():
        for u in range(n_steps + 1):
            if u > 0:
                output_stage(w0 + u - 1)
            if u < n_steps:
                scores_stage(w0 + u)
            yield

    return restart, schedule()


def _mixer_kernel(qa_ref, kc_ref, vtc_ref, kx_ref, vtx_ref,
                  hq_ref, hv_ref, kf_ref, bf_ref, kb_ref, bb_ref, hvc_ref, kfc_ref, bfc_ref, kbc_ref, bbc_ref,
                  msk_ref, att_ref, hg_ref, sc_ref, s_ref, acc_ref, *hg_scratch, steps_per_head):
    nw = hq_ref.shape[1] // WIN
    n_steps = nw // steps_per_head
    phase = pl.program_id(1) % steps_per_head
    restart, scan = _hgrn_steps(hq_ref, hv_ref, kf_ref, bf_ref, kb_ref, bb_ref,
                                hvc_ref, kfc_ref, bfc_ref, kbc_ref, bbc_ref, msk_ref, hg_ref, *hg_scratch,
                                w0=phase * n_steps, n_steps=n_steps)
    pl.when(phase == 0)(restart)

    for _ in _attention_schedule(qa_ref, kc_ref, vtc_ref, kx_ref, vtx_ref, att_ref, sc_ref, s_ref, acc_ref):
        next(scan, None)
    for _ in scan:
        pass


def _mixers(qa, kc, vtc, kx, vtx, hq, hv, kf, bf, kb, bb, hvc, kfc, bfc, kbc, bbc):
    b, t, _ = qa.shape
    ctx = hvc.shape[1]
    tq = min(Q_TILE, t)
    n_q = t // tq
    nw = t // WIN
    assert tq % LANES == 0 and t % tq == 0 and t % KV_TILE == 0 and KV_AHEAD < KV_SLOTS
    assert n_q % HG_HEADS == 0 and nw % (n_q // HG_HEADS) == 0
    steps_per_head = n_q // HG_HEADS
    cols = ATT_GROUP * tq
    whole = lambda a: pl.BlockSpec((1,) + a.shape[1:], lambda i, j: (i, 0, 0))
    lat = pl.BlockSpec((1, t, HG_HEAD_DIM), lambda i, j: (i, 0, j // steps_per_head))
    con = pl.BlockSpec((1, ctx, HG_HEAD_DIM), lambda i, j: (i, 0, j // steps_per_head))
    msk = _pair_masks()
    const = lambda a: pl.BlockSpec(a.shape, lambda i, j: (0,) * a.ndim)
    return pl.pallas_call(
        functools.partial(_mixer_kernel, steps_per_head=steps_per_head),
        out_shape=[jax.ShapeDtypeStruct((b, t, ATT_WIDTH), BF16), jax.ShapeDtypeStruct((b, t, HG_WIDTH), F32)],
        grid=(b, n_q),
        in_specs=[pl.BlockSpec((1, tq, qa.shape[-1]), lambda i, j: (i, j, 0)),
                  whole(kc), whole(vtc), whole(kx), whole(vtx)] + [lat] * 6 + [con] * 5
                 + [const(msk)],
        out_specs=[pl.BlockSpec((1, tq, ATT_WIDTH), lambda i, j: (i, j, 0)), lat],
        scratch_shapes=[pltpu.VMEM((ATT_KV_HEADS, kc.shape[1], cols), F32),
                        pltpu.VMEM((KV_SLOTS, ATT_KV_HEADS, KV_TILE, cols), F32),
                        pltpu.VMEM((ATT_KV_HEADS, ATT_HEAD_DIM + ONES_ROWS, cols), F32),
                        pltpu.VMEM((LANES, LANES), F32), pltpu.VMEM((LANES, LANES), F32),
                        pltpu.VMEM((WIN, WIN), BF16), pltpu.VMEM((WIN, WIN), BF16),
                        pltpu.VMEM((WIN, LANES), F32), pltpu.VMEM((WIN, LANES), F32)],
        compiler_params=pltpu.CompilerParams(dimension_semantics=("arbitrary", "arbitrary"),
                                             vmem_limit_bytes=VMEM_LIMIT),
        name="mixers",
    )(qa, kc, vtc, kx, vtx, hq, hv, kf, bf, kb, bb, hvc, kfc, bfc, kbc, bbc, msk)


def _out_ffn_kernel(x_ref, att_ref, hg_ref, sg_ref, mod_ref, ghg_ref, wo_in, g2_ref, wgu_in, wd_in, gf_ref,
                    o_ref, wo_ref, wgu_ref, wd_ref, *, tiles_per_batch):
    step = pl.program_id(0)

    @pl.when(step < OUT_WEIGHT_STEPS)
    def _():
        for src, dst in ((wo_in, wo_ref), (wgu_in, wgu_ref), (wd_in, wd_ref)):
            rows = src.shape[0]
            dst[pl.ds(pl.multiple_of(step * rows, rows), rows), :] = src[...].astype(BF16)

    @pl.when(step >= OUT_WEIGHT_STEPS)
    def _():
        _out_ffn_tile(x_ref, att_ref, hg_ref, sg_ref, mod_ref, ghg_ref, wo_ref, g2_ref, wgu_ref, wd_ref, gf_ref,
                      o_ref, (step - OUT_WEIGHT_STEPS) // tiles_per_batch)


def _out_ffn_tile(x_ref, att_ref, hg_ref, sg_ref, mod_ref, ghg_ref, wo_ref, g2_ref, wgu_ref, wd_ref, gf_ref,
                  o_ref, batch):
    d = x_ref.shape[-1]
    dff = wd_ref.shape[0]
    row = pl.ds(batch, 1)
    gate1 = mod_ref[row, 2 * d:3 * d]
    shift2 = mod_ref[row, 3 * d:4 * d]
    scale2 = mod_ref[row, 4 * d:5 * d]
    gate2 = mod_ref[row, 5 * d:6 * d]

    parts = [slice(r0, r0 + OUT_PART_ROWS) for r0 in range(0, x_ref.shape[1], OUT_PART_ROWS)]

    def mixed(rows):
        heads = []
        for hd in range(HG_HEADS):
            cs = slice(hd * HG_HEAD_DIM, (hd + 1) * HG_HEAD_DIM)
            o = _rms_rows(hg_ref[0, rows, cs]) * ghg_ref[...]
            heads.append((o * sg_ref[0, rows, cs].astype(F32)).astype(BF16))
        mix = (jnp.dot(att_ref[0, rows, :], wo_ref[0:ATT_WIDTH, :], preferred_element_type=F32)
               + jnp.dot(jnp.concatenate(heads, axis=1), wo_ref[ATT_WIDTH:ATT_WIDTH + HG_WIDTH, :],
                         preferred_element_type=F32))
        x1 = x_ref[0, rows, :] + gate1 * mix
        return x1, ((_rms_rows(x1) * g2_ref[...]) * (1.0 + scale2) + shift2).astype(BF16)

    def gate_up(h):
        return (jnp.dot(h, wgu_ref[:, 0:dff], preferred_element_type=F32),
                jnp.dot(h, wgu_ref[:, dff:2 * dff], preferred_element_type=F32))

    def down(a, g):
        return jnp.dot((_silu(a) * g).astype(BF16), wd_ref[...], preferred_element_type=F32)

    firsts = [mixed(rows) for rows in parts]
    ups = [gate_up(h) for _, h in firsts]
    ys = [down(a, g) for a, g in ups]
    for rows, (x1, _), y in zip(parts, firsts, ys):
        o_ref[0, rows, :] = _rms_rows(x1 + gate2 * y) * gf_ref[...]


def _out_ffn(x, att, hg, sg, mods, ghg, w_out, g2, w_gu, w_down, gf):
    b, t, d = x.shape
    tm = min(OUT_ROW_TILE, t)
    assert tm % OUT_PART_ROWS == 0
    nt = t // tm
    tile = lambda s: jnp.maximum(s - OUT_WEIGHT_STEPS, 0)
    bspec = lambda width: pl.BlockSpec((1, tm, width), lambda s: (tile(s) // nt, tile(s) % nt, 0))
    full = lambda a: pl.BlockSpec(a.shape, lambda s: (0,) * a.ndim, pipeline_mode=pl.Buffered(1))

    def chunked(w):
        assert w.shape[0] % (OUT_WEIGHT_STEPS * SUB) == 0
        return pl.BlockSpec((w.shape[0] // OUT_WEIGHT_STEPS, w.shape[1]),
                            lambda s: (jnp.minimum(s, OUT_WEIGHT_STEPS - 1), 0))

    return pl.pallas_call(
        functools.partial(_out_ffn_kernel, tiles_per_batch=nt),
        out_shape=jax.ShapeDtypeStruct((b, t, d), F32),
        grid=(OUT_WEIGHT_STEPS + b * nt,),
        in_specs=[bspec(d), bspec(ATT_WIDTH), bspec(HG_WIDTH), bspec(HG_WIDTH), full(mods),
                  full(ghg), chunked(w_out), full(g2), chunked(w_gu), chunked(w_down), full(gf)],
        out_specs=bspec(d),
        scratch_shapes=[pltpu.VMEM(w_out.shape, BF16), pltpu.VMEM(w_gu.shape, BF16),
                        pltpu.VMEM(w_down.shape, BF16)],
        compiler_params=pltpu.CompilerParams(dimension_semantics=("arbitrary",),
                                             vmem_limit_bytes=VMEM_LIMIT),
        name="out_ffn",
    )(x, att, hg, sg, mods, ghg, w_out, g2, w_gu, w_down, gf)


def _rope_tables(t):
    n_rows = t // GRID_W
    row = np.repeat(np.arange(n_rows, dtype=np.float32), GRID_W)
    col = np.tile(np.arange(GRID_W, dtype=np.float32), n_rows)
    inv = (np.float32(ROPE_THETA) ** (-np.arange(0, AXIS_DIM, 2, dtype=np.float32) / np.float32(AXIS_DIM)))
    inv = inv.astype(np.float32)
    ar = row[:, None] * inv
    ac = col[:, None] * inv
    ang = np.concatenate([ar, ar, ac, ac], axis=-1).astype(np.float32)
    sign = np.where((np.arange(ATT_HEAD_DIM) % AXIS_DIM) < AXIS_DIM // 2, -1.0, 1.0)
    reps = LANES // ATT_HEAD_DIM
    cos = np.tile(np.cos(ang.astype(np.float64)), (1, reps)).astype(np.float32)
    sin = np.tile(np.sin(ang.astype(np.float64)) * sign, (1, reps)).astype(np.float32)
    return jnp.asarray(cos), jnp.asarray(sin)


def kernel(x, c, ctx, c_ctx, w_mod, b_mod, g_norm1, w_in, g_q, g_k, lb_raw, g_hg, w_out, g_norm2, w_gu, w_down,
           g_final):
    b, t, d = x.shape
    assert w_mod.shape[0] == 1 and lb_raw.shape[0] == 2, "single-layer block"
    assert t % WIN == 0 and (t // WIN) % 2 == 0 and ctx.shape[1] % WIN == 0 and t % KV_TILE == 0

    mods = _modulation(c, c_ctx, w_mod[0], b_mod)

    cos, sin = _rope_tables(t)

    qa, kx, vtx, hq, hv, kf, bf, kb, bb, sg, w_in_b = _input_projection_latent(
        x, mods, g_norm1, w_in[0], g_q, g_k, lb_raw, cos, sin)
    kc, vtc, hvc, kfc, bfc, kbc, bbc = _input_projection_context(
        ctx, mods, b, g_norm1, w_in_b, g_q, g_k, lb_raw)

    att, hg = _mixers(qa, kc, vtc, kx, vtx, hq, hv, kf, bf, kb, bb, hvc, kfc, bfc, kbc, bbc)
    return _out_ffn(x, att, hg, sg, mods, g_hg, w_out[0], g_norm2, w_gu[0], w_down[0], g_final[None, :])
```

```python
import functools

import jax
import jax.numpy as jnp
import numpy as np
from jax import lax
from jax.experimental import pallas as pl
from jax.experimental.pallas import tpu as pltpu

F32 = jnp.float32
BF16 = jnp.bfloat16

LANES = 128
GRID_W = 64
ATT_HEADS = 8
ATT_KV_HEADS = 2
ATT_GROUP = ATT_HEADS // ATT_KV_HEADS
ATT_HEAD_DIM = 64
ATT_WIDTH = ATT_HEADS * ATT_HEAD_DIM
ATT_KV_WIDTH = ATT_KV_HEADS * ATT_HEAD_DIM
AXIS_DIM = ATT_HEAD_DIM // 2
ROPE_THETA = 10000.0
HG_HEADS = 4
HG_HEAD_DIM = 128
HG_WIDTH = HG_HEADS * HG_HEAD_DIM
EPS = 1e-6

MOD_COLUMN_BLOCKS = 8
SUB = 16
WIN = 128
ROW_TILE = 256
IN_WEIGHT_STEPS = 2
OUT_ROW_TILE = 512
OUT_PART_ROWS = 256
OUT_WEIGHT_STEPS = 8
Q_TILE = 256
KV_TILE = 512
KV_SLOTS = 4
KV_AHEAD = 2
ONES_ROWS = 16
LOG2E = 1.4426950408889634
VMEM_LIMIT = 56 * 1024 * 1024

C_AQ = 0
C_AK = C_AQ + ATT_WIDTH
C_AV = C_AK + ATT_KV_WIDTH
C_HQ = C_AV + ATT_KV_WIDTH
C_HI = C_HQ + HG_WIDTH
C_FF = C_HI + HG_WIDTH
C_FB = C_FF + HG_WIDTH
C_HG = C_FB + HG_WIDTH


def _silu(v):
    return v * jax.nn.sigmoid(v)


def _rms_rows(v):
    return v * lax.rsqrt(jnp.mean(v * v, axis=-1, keepdims=True) + EPS)


def _mod_kernel(c_ref, cx_ref, w_ref, b_ref, o_ref, a_ref):
    nb = c_ref.shape[0]
    a_ref[...] = jnp.zeros(a_ref.shape, F32)
    a_ref[0:nb, :] = c_ref[...]
    a_ref[nb:nb + 1, :] = cx_ref[...]
    a = _silu(a_ref[...])
    rows = a.shape[0]
    w = w_ref[...]
    a_hi = a.astype(BF16)
    a_lo = (a - a_hi.astype(F32)).astype(BF16)
    w_hi = w.astype(BF16)
    w_lo = (w - w_hi.astype(F32)).astype(BF16)
    both = jnp.dot(jnp.concatenate([a_hi, a_lo], axis=0), w_hi, preferred_element_type=F32)
    o_ref[...] = (both[0:rows] + both[rows:2 * rows]
                  + jnp.dot(a_hi, w_lo, preferred_element_type=F32) + b_ref[...])


def _modulation(c, c_ctx, w_mod, b_mod):
    nb, d = c.shape
    rows = -(-(nb + 1) // 8) * 8
    n = w_mod.shape[1]
    bn = n // MOD_COLUMN_BLOCKS
    return pl.pallas_call(
        _mod_kernel,
        out_shape=jax.ShapeDtypeStruct((rows, n), F32),
        grid=(n // bn,),
        in_specs=[pl.BlockSpec((nb, d), lambda j: (0, 0)),
                  pl.BlockSpec((1, d), lambda j: (0, 0)),
                  pl.BlockSpec((d, bn), lambda j: (0, j)),
                  pl.BlockSpec((1, bn), lambda j: (0, j))],
        out_specs=pl.BlockSpec((rows, bn), lambda j: (0, j)),
        scratch_shapes=[pltpu.VMEM((rows, d), F32)],
        compiler_params=pltpu.CompilerParams(dimension_semantics=("arbitrary",),
                                             vmem_limit_bytes=VMEM_LIMIT),
        name="modulation",
    )(c, c_ctx[None, :], w_mod, b_mod)


def _head64_meansq(p, seg2):
    sq = p * p
    hi = sq.astype(BF16)
    lo = (sq - hi.astype(F32)).astype(BF16)
    s = jnp.dot(jnp.concatenate([hi, lo], axis=1), seg2, preferred_element_type=F32)
    return s * (1.0 / ATT_HEAD_DIM)


def _rope(v, cos, sin_signed, first_half):
    fwd = pltpu.roll(v, LANES - AXIS_DIM // 2, 1)
    bwd = pltpu.roll(v, AXIS_DIM // 2, 1)
    return v * cos + jnp.where(first_half, fwd, bwd) * sin_signed


def _gate_prep(f, lb, suffix):
    fg = lb + (1.0 - lb) * jax.nn.sigmoid(f)
    cum = jnp.log2(fg)
    rows = cum.shape[0]
    pos = lax.broadcasted_iota(jnp.int32, cum.shape, 0) % SUB
    shift = 1
    while shift < SUB:
        if suffix:
            moved = pltpu.roll(cum, rows - shift, 0)
            cum = cum + jnp.where(pos < SUB - shift, moved, 0.0)
        else:
            moved = pltpu.roll(cum, shift, 0)
            cum = cum + jnp.where(pos >= shift, moved, 0.0)
        shift *= 2
    return (1.0 - fg).astype(BF16), cum


def _inproj_kernel(x_ref, mod_ref, g1_ref, w_ref, gq_ref, gk_ref, lb_ref, cos_ref, sin_ref, *out_refs, latent,
                   mod_row):
    d = x_ref.shape[-1]
    x = x_ref[0]
    row = pl.ds(mod_row, 1)
    shift = mod_ref[row, 0:d]
    scale = mod_ref[row, d:2 * d]
    gq = jnp.concatenate([gq_ref[...]] * (LANES // ATT_HEAD_DIM), axis=1)
    gk = jnp.concatenate([gk_ref[...]] * (LANES // ATT_HEAD_DIM), axis=1)
    h = (_rms_rows(x) * g1_ref[...]) * (1.0 + scale) + shift
    hb = h.astype(BF16)

    def proj(c0, width):
        return jnp.dot(hb, w_ref[:, c0:c0 + width], preferred_element_type=F32)

    lane = lax.broadcasted_iota(jnp.int32, (2 * LANES, LANES), 0) % LANES
    lane_c = lax.broadcasted_iota(jnp.int32, (2 * LANES, LANES), 1)
    seg = (lane // ATT_HEAD_DIM == lane_c // ATT_HEAD_DIM).astype(BF16)
    row_lane = lax.broadcasted_iota(jnp.int32, (x.shape[0], LANES), 1)
    first_half = (row_lane % AXIS_DIM) < (AXIS_DIM // 2)
    upper = row_lane >= ATT_HEAD_DIM

    r0 = lb_ref[0]
    r1 = lb_ref[1]
    rm = jnp.maximum(r0, r1)
    e0 = jnp.exp(r0 - rm)
    e1 = jnp.exp(r1 - rm)
    lb_all = e0 / (e0 + e1)

    if latent:
        (qa_ref, ka_ref, vat_ref, hq_ref, hv_ref, kf_ref, bf_ref, kb_ref, bb_ref, sg_ref) = out_refs
        cos = cos_ref[...]
        sin = sin_ref[...]
    else:
        (ka_ref, vat_ref, hv_ref, kf_ref, bf_ref, kb_ref, bb_ref) = out_refs

    p_ff = proj(C_FF, HG_WIDTH)
    if latent:
        pq = proj(C_AQ, ATT_WIDTH)
    pk = proj(C_AK, ATT_KV_WIDTH)
    pv = proj(C_AV, ATT_KV_WIDTH)
    p_fb = proj(C_FB, HG_WIDTH)
    if latent:
        p_hq = proj(C_HQ, HG_WIDTH)
    p_hv = proj(C_HI, HG_WIDTH)
    ms_k = _head64_meansq(pk, seg)
    if latent:
        ms_q = [_head64_meansq(pq[:, c * LANES:(c + 1) * LANES], seg) for c in range(ATT_WIDTH // LANES)]

    kn = pk * lax.rsqrt(ms_k + EPS) * gk
    if latent:
        kn = _rope(kn, cos, sin, first_half)
    ka_ref[0] = kn.astype(BF16)
    vat_ref[0] = pv.T.astype(BF16)

    if latent:
        for c in range(ATT_WIDTH // LANES):
            t = pq[:, c * LANES:(c + 1) * LANES]
            t = t * lax.rsqrt(ms_q[c] + EPS) * gq
            t = _rope(t, cos, sin, first_half) * (ATT_HEAD_DIM ** -0.5 * LOG2E)
            t_swapped = pltpu.roll(t, ATT_HEAD_DIM, 1)
            for half in range(2):
                head = 2 * c + half
                kv = head // ATT_GROUP
                src = t if half == kv else t_swapped
                keep = upper if kv == 1 else jnp.logical_not(upper)
                qa_ref[0, :, head * LANES:(head + 1) * LANES] = jnp.where(keep, src, 0.0).astype(BF16)
        hq_ref[0] = (_silu(p_hq) * (HG_HEAD_DIM ** -0.5)).astype(BF16)
        sg_ref[0] = _silu(proj(C_HG, HG_WIDTH)).astype(BF16)

    hv_ref[0] = p_hv.astype(BF16)
    kf, bf = _gate_prep(p_ff, lb_all[0:1, :], False)
    kf_ref[0] = kf
    bf_ref[0] = bf
    kb, bb = _gate_prep(p_fb, lb_all[1:2, :], True)
    kb_ref[0] = kb
    bb_ref[0] = bb


def _inproj_latent_kernel(x_ref, mod_ref, g1_ref, w_in, gq_ref, gk_ref, lb_ref, cos_ref, sin_ref, *refs,
                          tiles_per_batch):
    *out_refs, wb_ref, w_ref = refs
    step = pl.program_id(0)

    @pl.when(step < IN_WEIGHT_STEPS)
    def _():
        rows = w_in.shape[0]
        chunk = w_in[...].astype(BF16)
        w_ref[pl.ds(pl.multiple_of(step * rows, rows), rows), :] = chunk
        wb_ref[...] = chunk

    @pl.when(step >= IN_WEIGHT_STEPS)
    def _():
        _inproj_kernel(x_ref, mod_ref, g1_ref, w_ref, gq_ref, gk_ref, lb_ref, cos_ref, sin_ref, *out_refs,
                       latent=True, mod_row=(step - IN_WEIGHT_STEPS) // tiles_per_batch)


def _input_projection_latent(rows, mods, g1, w_in, gq, gk, lb_raw, cos, sin):
    b, t, d = rows.shape
    tm = min(ROW_TILE, t)
    nt = t // tm
    assert w_in.shape[0] % (IN_WEIGHT_STEPS * SUB) == 0
    chunk_rows = w_in.shape[0] // IN_WEIGHT_STEPS
    tile = lambda s: jnp.maximum(s - IN_WEIGHT_STEPS, 0)
    chunk = lambda s: jnp.minimum(s, IN_WEIGHT_STEPS - 1)
    bspec = lambda width: pl.BlockSpec((1, tm, width), lambda s: (tile(s) // nt, tile(s) % nt, 0))
    full = lambda a: pl.BlockSpec(a.shape, lambda s: (0,) * a.ndim)
    sds = lambda width, dt: jax.ShapeDtypeStruct((b, t, width), dt)
    table_spec = pl.BlockSpec((tm, LANES), lambda s: (tile(s) % nt, 0))
    w_spec = pl.BlockSpec((chunk_rows, w_in.shape[1]), lambda s: (chunk(s), 0))
    out_shape = [sds(ATT_HEADS * LANES, BF16), sds(ATT_KV_WIDTH, BF16),
                 jax.ShapeDtypeStruct((b, ATT_KV_WIDTH, t), BF16),
                 sds(HG_WIDTH, BF16), sds(HG_WIDTH, BF16), sds(HG_WIDTH, BF16), sds(HG_WIDTH, F32),
                 sds(HG_WIDTH, BF16), sds(HG_WIDTH, F32), sds(HG_WIDTH, BF16),
                 jax.ShapeDtypeStruct(w_in.shape, BF16)]
    out_specs = [bspec(ATT_HEADS * LANES), bspec(ATT_KV_WIDTH),
                 pl.BlockSpec((1, ATT_KV_WIDTH, tm), lambda s: (tile(s) // nt, 0, tile(s) % nt)),
                 bspec(HG_WIDTH), bspec(HG_WIDTH), bspec(HG_WIDTH), bspec(HG_WIDTH), bspec(HG_WIDTH),
                 bspec(HG_WIDTH), bspec(HG_WIDTH), w_spec]
    return pl.pallas_call(
        functools.partial(_inproj_latent_kernel, tiles_per_batch=nt),
        out_shape=out_shape,
        grid=(IN_WEIGHT_STEPS + b * nt,),
        in_specs=[bspec(d), full(mods), full(g1), w_spec, full(gq), full(gk), full(lb_raw),
                  table_spec, table_spec],
        out_specs=out_specs,
        scratch_shapes=[pltpu.VMEM(w_in.shape, BF16)],
        compiler_params=pltpu.CompilerParams(dimension_semantics=("arbitrary",),
                                             vmem_limit_bytes=VMEM_LIMIT),
        name="inproj_latent",
    )(rows, mods, g1, w_in, gq, gk, lb_raw, cos, sin)


def _input_projection_context(rows, mods, mod_row, g1, w_in, gq, gk, lb_raw):
    b, t, d = rows.shape
    tm = min(ROW_TILE, t)
    nt = t // tm
    bspec = lambda width: pl.BlockSpec((1, tm, width), lambda i, j: (i, j, 0))
    full = lambda a: pl.BlockSpec(a.shape, lambda i, j: (0,) * a.ndim)
    sds = lambda width, dt: jax.ShapeDtypeStruct((b, t, width), dt)
    out_shape = [sds(ATT_KV_WIDTH, BF16), jax.ShapeDtypeStruct((b, ATT_KV_WIDTH, t), BF16),
                 sds(HG_WIDTH, BF16), sds(HG_WIDTH, BF16), sds(HG_WIDTH, F32), sds(HG_WIDTH, BF16),
                 sds(HG_WIDTH, F32)]
    out_specs = [bspec(ATT_KV_WIDTH), pl.BlockSpec((1, ATT_KV_WIDTH, tm), lambda i, j: (i, 0, j)),
                 bspec(HG_WIDTH), bspec(HG_WIDTH), bspec(HG_WIDTH), bspec(HG_WIDTH), bspec(HG_WIDTH)]
    return pl.pallas_call(
        functools.partial(_inproj_context_kernel, mod_row=mod_row),
        out_shape=out_shape,
        grid=(b, nt),
        in_specs=[bspec(d), full(mods), full(g1), full(w_in), full(gq), full(gk), full(lb_raw)],
        out_specs=out_specs,
        compiler_params=pltpu.CompilerParams(dimension_semantics=("parallel", "parallel"),
                                             vmem_limit_bytes=VMEM_LIMIT),
        name="inproj_context",
    )(rows, mods, g1, w_in, gq, gk, lb_raw)


def _inproj_context_kernel(x_ref, mod_ref, g1_ref, w_ref, gq_ref, gk_ref, lb_ref, *out_refs, mod_row):
    _inproj_kernel(x_ref, mod_ref, g1_ref, w_ref, gq_ref, gk_ref, lb_ref, None, None, *out_refs,
                   latent=False, mod_row=mod_row)


def _attention_schedule(q_ref, kc_ref, vtc_ref, kx_ref, vtx_ref, o_ref, sc_ref, s_ref, acc_ref):
    tq = q_ref.shape[1]
    n_kv = kx_ref.shape[1] // KV_TILE
    nt_dims = (((1,), (1,)), ((), ()))
    cols = ATT_GROUP * tq

    qgs = [jnp.concatenate(
        [q_ref[0, :, (kv * ATT_GROUP + j) * LANES:(kv * ATT_GROUP + j + 1) * LANES] for j in range(ATT_GROUP)],
        axis=0) for kv in range(ATT_KV_HEADS)]

    def scores(kt, dst):
        for kv in range(ATT_KV_HEADS):
            dst(kv, lax.dot_general(kt, qgs[kv], nt_dims, preferred_element_type=F32))

    def consume(src, vt_all, maxima):
        ones = jnp.ones((ONES_ROWS, vt_all.shape[1]), BF16)
        out = []
        for kv in range(ATT_KV_HEADS):
            m = maxima[kv]
            m_new = jnp.maximum(m, jnp.max(src(kv), axis=0, keepdims=True))
            alpha = jnp.exp2(m - m_new)
            p = jnp.exp2(src(kv) - m_new).astype(BF16)
            vt = jnp.concatenate([vt_all[kv * ATT_HEAD_DIM:(kv + 1) * ATT_HEAD_DIM], ones], axis=0)
            acc_ref[kv] = alpha * acc_ref[kv] + jnp.dot(vt, p, preferred_element_type=F32)
            out.append(m_new)
        return tuple(out)

    def put_ctx(kv, val):
        sc_ref[kv] = val

    def put(slot):
        def dst(kv, val):
            s_ref[slot, kv] = val
        return dst

    def k_tile(j):
        return kx_ref[0, j * KV_TILE:(j + 1) * KV_TILE, :]

    def vt_tile(j):
        return vtx_ref[0, :, j * KV_TILE:(j + 1) * KV_TILE]

    acc_ref[...] = jnp.zeros(acc_ref.shape, F32)
    scores(kc_ref[0], put_ctx)
    for j in range(min(KV_AHEAD, n_kv)):
        scores(k_tile(j), put(j % KV_SLOTS))
    maxima = consume(lambda kv: sc_ref[kv], vtc_ref[0], (jnp.full((1, cols), -jnp.inf, F32),) * ATT_KV_HEADS)
    for j in range(n_kv):
        if j + KV_AHEAD < n_kv:
            scores(k_tile(j + KV_AHEAD), put((j + KV_AHEAD) % KV_SLOTS))
        maxima = consume(lambda kv, slot=j % KV_SLOTS: s_ref[slot, kv], vt_tile(j), maxima)
        yield

    for kv in range(ATT_KV_HEADS):
        o = acc_ref[kv, 0:ATT_HEAD_DIM] / acc_ref[kv, ATT_HEAD_DIM:ATT_HEAD_DIM + 1]
        for jj in range(ATT_GROUP // 2):
            pair = jnp.concatenate([o[:, (2 * jj) * tq:(2 * jj + 1) * tq],
                                    o[:, (2 * jj + 1) * tq:(2 * jj + 2) * tq]], axis=0)
            col = kv * (ATT_GROUP // 2) + jj
            o_ref[0, :, col * LANES:(col + 1) * LANES] = pair.T.astype(BF16)


NT_DIMS = (((1,), (1,)), ((), ()))


def _scan_window(st, q, k, v, b, b_row, msk_ref, reverse, want_out):
    nsub = WIN // SUB
    last = 0 if reverse else SUB - 1
    sub_of = (lambda p: nsub - 1 - p) if reverse else (lambda p: p)
    blk = lambda a, i: a[i * SUB:(i + 1) * SUB]
    d = int(reverse)

    b_last = [b_row(i * SUB + last) for i in range(nsub)]
    cum = []
    for p in range(nsub):
        cum.append(b_last[sub_of(p)] if p == 0 else cum[-1] + b_last[sub_of(p)])
    ks = [blk(k, i) * jnp.exp2(b_last[i] - blk(b, i)) for i in range(nsub)]

    k_end = jnp.concatenate(
        [ks[i] if sub_of(i) == nsub - 1 else ks[i] * jnp.exp2(cum[nsub - 1] - cum[sub_of(i)]) for i in range(nsub)],
        axis=0)
    st_new = jnp.exp2(cum[nsub - 1]) * st + jnp.dot(v.T, k_end, preferred_element_type=F32)
    if not want_out:
        return st_new, None

    qs = [blk(q, i) * jnp.exp2(blk(b, i)) for i in range(nsub)]
    q_start = jnp.concatenate(
        [qs[i] if sub_of(i) == 0 else qs[i] * jnp.exp2(cum[sub_of(i) - 1]) for i in range(nsub)], axis=0)
    o_carried = lax.dot_general(q_start, st, NT_DIMS, preferred_element_type=F32)

    scores = None
    for level, h in enumerate((nsub // 2, nsub // 4, nsub // 8)):
        q_rows, k_rows = [], []
        for i in range(nsub):
            p = sub_of(i)
            ref = (p // (2 * h)) * 2 * h + h - 1
            q_rows.append(qs[i] * jnp.exp2(cum[p - 1] - cum[ref]) if p - 1 > ref else qs[i])
            k_rows.append(ks[i] * jnp.exp2(cum[ref] - cum[p]) if p < ref else ks[i])
        term = (lax.dot_general(jnp.concatenate(q_rows, axis=0), jnp.concatenate(k_rows, axis=0), NT_DIMS,
                                preferred_element_type=F32) * msk_ref[d, level])
        scores = term if scores is None else scores + term

    def level_scores(level, q_exponent, k_exponent):
        q_l = q * jnp.exp2(jnp.minimum(q_exponent, 0.0))
        k_l = k if k_exponent is None else k * jnp.exp2(jnp.minimum(k_exponent, 0.0))
        return lax.dot_general(q_l, k_l, NT_DIMS, preferred_element_type=F32) * msk_ref[d, level]

    scores = scores + (jnp.dot(q * k, jnp.ones((LANES, LANES), BF16), preferred_element_type=F32)
                       * msk_ref[d, N_LEVELS])
    for level, h in ((3, SUB // 2), (4, SUB // 4)):
        ref_rows = [b_row(s * 2 * h + (h if reverse else h - 1)) for s in range(WIN // (2 * h))]
        diff = b - jnp.concatenate([jnp.broadcast_to(r, (2 * h, LANES)) for r in ref_rows], axis=0)
        decay = jnp.exp2(jnp.minimum(diff, -diff))
        scores = scores + (lax.dot_general(q * decay, k * decay, NT_DIMS, preferred_element_type=F32)
                           * msk_ref[d, level])
    before = lambda n: pltpu.roll(b, n, 0)
    after = lambda n: pltpu.roll(b, WIN - n, 0)
    off4 = lax.broadcasted_iota(jnp.int32, (WIN, LANES), 0) % 4
    if reverse:
        q_exp = b - jnp.where(off4 == 0, after(2), after(1))
        k_exp = jnp.where(off4 == 2, 0.0, before(1) - b)
    else:
        q_exp = b - jnp.where(off4 == 3, before(2), before(1))
        k_exp = jnp.where(off4 == 1, 0.0, after(1) - b)
    scores = scores + level_scores(5, q_exp, k_exp)
    scores = scores + level_scores(6, b - (after(1) if reverse else before(1)), None)
    return st_new, (scores.astype(BF16), o_carried)


N_LEVELS = 7


def _pair_masks():
    msk = np.zeros((2, N_LEVELS + 1, WIN, WIN), np.float32)
    pos = np.arange(WIN)
    for d in range(2):
        for level in range(N_LEVELS):
            h = WIN >> (level + 1)
            span, off = pos // (2 * h), pos % (2 * h)
            later = (off < h) if d else (off >= h)
            msk[d, level] = (span[:, None] == span[None, :]) & later[:, None] & ~later[None, :]
        msk[d, N_LEVELS] = np.eye(WIN)
    return jnp.asarray(msk, F32)


def _hgrn_steps(q_ref, v_ref, kf_ref, bf_ref, kb_ref, bb_ref, vc_ref, kfc_ref, bfc_ref, kbc_ref, bbc_ref,
                msk_ref, o_ref, stf_ref, stb_ref, scf_ref, scb_ref, ocf_ref, ocb_ref, w0, n_steps):
    t = q_ref.shape[1]
    ctx = vc_ref.shape[1]
    nw = t // WIN

    def window(refs, start, st, reverse, want_out):
        qr, vr, kr, br = refs
        sl = pl.ds(start, WIN)
        vw = vr[0, sl, :].astype(F32)
        kw = kr[0, sl, :].astype(F32)
        bw = br[0, sl, :]
        qw = qr[0, sl, :].astype(F32) if want_out else None
        b_row = lambda r: br[0, pl.ds(start + r, 1), :]
        return _scan_window(st, qw, kw, vw, bw, b_row, msk_ref, reverse, want_out)

    def restart():
        zero = jnp.zeros((LANES, LANES), F32)
        stf, stb = zero, zero
        n_cw = ctx // WIN
        for w in range(n_cw):
            stf, _ = window((None, vc_ref, kfc_ref, bfc_ref), w * WIN, stf, False, False)
            stb, _ = window((None, vc_ref, kbc_ref, bbc_ref), (n_cw - 1 - w) * WIN, stb, True, False)
        stf_ref[...] = stf
        stb_ref[...] = stb
        o_ref[...] = jnp.zeros(o_ref.shape, F32)

    def starts(w):
        return pl.multiple_of(w * WIN, WIN), pl.multiple_of((nw - 1 - w) * WIN, WIN)

    def scores_stage(w):
        fs, bs = starts(w)
        stf, (sf, cf) = window((q_ref, v_ref, kf_ref, bf_ref), fs, stf_ref[...], False, True)
        stf_ref[...] = stf
        scf_ref[...] = sf
        ocf_ref[...] = cf
        stb, (sb, cb) = window((q_ref, v_ref, kb_ref, bb_ref), bs, stb_ref[...], True, True)
        stb_ref[...] = stb
        scb_ref[...] = sb
        ocb_ref[...] = cb

    def output_stage(w):
        fs, bs = starts(w)
        o_ref[0, pl.ds(fs, WIN), :] += (
            jnp.dot(scf_ref[...], v_ref[0, pl.ds(fs, WIN), :], preferred_element_type=F32) + ocf_ref[...])
        o_ref[0, pl.ds(bs, WIN), :] += (
            jnp.dot(scb_ref[...], v_ref[0, pl.ds(bs, WIN), :], preferred_element_type=F32) + ocb_ref[...])

    def schedule():
        for u in range(n_steps + 1):
            if u > 0:
                output_stage(w0 + u - 1)
            if u < n_steps:
                scores_stage(w0 + u)
            yield

    return restart, schedule()


def _mixer_kernel(qa_ref, kc_ref, vtc_ref, kx_ref, vtx_ref,
                  hq_ref, hv_ref, kf_ref, bf_ref, kb_ref, bb_ref, hvc_ref, kfc_ref, bfc_ref, kbc_ref, bbc_ref,
                  msk_ref, att_ref, hg_ref, sc_ref, s_ref, acc_ref, *hg_scratch, steps_per_head):
    nw = hq_ref.shape[1] // WIN
    n_steps = nw // steps_per_head
    phase = pl.program_id(1) % steps_per_head
    restart, scan = _hgrn_steps(hq_ref, hv_ref, kf_ref, bf_ref, kb_ref, bb_ref,
                                hvc_ref, kfc_ref, bfc_ref, kbc_ref, bbc_ref, msk_ref, hg_ref, *hg_scratch,
                                w0=phase * n_steps, n_steps=n_steps)
    pl.when(phase == 0)(restart)

    for _ in _attention_schedule(qa_ref, kc_ref, vtc_ref, kx_ref, vtx_ref, att_ref, sc_ref, s_ref, acc_ref):
        next(scan, None)
    for _ in scan:
        pass


def _mixers(qa, kc, vtc, kx, vtx, hq, hv, kf, bf, kb, bb, hvc, kfc, bfc, kbc, bbc):
    b, t, _ = qa.shape
    ctx = hvc.shape[1]
    tq = min(Q_TILE, t)
    n_q = t // tq
    nw = t // WIN
    assert tq % LANES == 0 and t % tq == 0 and t % KV_TILE == 0 and KV_AHEAD < KV_SLOTS
    assert n_q % HG_HEADS == 0 and nw % (n_q // HG_HEADS) == 0
    steps_per_head = n_q // HG_HEADS
    cols = ATT_GROUP * tq
    whole = lambda a: pl.BlockSpec((1,) + a.shape[1:], lambda i, j: (i, 0, 0))
    lat = pl.BlockSpec((1, t, HG_HEAD_DIM), lambda i, j: (i, 0, j // steps_per_head))
    con = pl.BlockSpec((1, ctx, HG_HEAD_DIM), lambda i, j: (i, 0, j // steps_per_head))
    msk = _pair_masks()
    const = lambda a: pl.BlockSpec(a.shape, lambda i, j: (0,) * a.ndim)
    return pl.pallas_call(
        functools.partial(_mixer_kernel, steps_per_head=steps_per_head),
        out_shape=[jax.ShapeDtypeStruct((b, t, ATT_WIDTH), BF16), jax.ShapeDtypeStruct((b, t, HG_WIDTH), F32)],
        grid=(b, n_q),
        in_specs=[pl.BlockSpec((1, tq, qa.shape[-1]), lambda i, j: (i, j, 0)),
                  whole(kc), whole(vtc), whole(kx), whole(vtx)] + [lat] * 6 + [con] * 5
                 + [const(msk)],
        out_specs=[pl.BlockSpec((1, tq, ATT_WIDTH), lambda i, j: (i, j, 0)), lat],
        scratch_shapes=[pltpu.VMEM((ATT_KV_HEADS, kc.shape[1], cols), F32),
                        pltpu.VMEM((KV_SLOTS, ATT_KV_HEADS, KV_TILE, cols), F32),
                        pltpu.VMEM((ATT_KV_HEADS, ATT_HEAD_DIM + ONES_ROWS, cols), F32),
                        pltpu.VMEM((LANES, LANES), F32), pltpu.VMEM((LANES, LANES), F32),
                        pltpu.VMEM((WIN, WIN), BF16), pltpu.VMEM((WIN, WIN), BF16),
                        pltpu.VMEM((WIN, LANES), F32), pltpu.VMEM((WIN, LANES), F32)],
        compiler_params=pltpu.CompilerParams(dimension_semantics=("arbitrary", "arbitrary"),
                                             vmem_limit_bytes=VMEM_LIMIT),
        name="mixers",
    )(qa, kc, vtc, kx, vtx, hq, hv, kf, bf, kb, bb, hvc, kfc, bfc, kbc, bbc, msk)


def _out_ffn_kernel(x_ref, att_ref, hg_ref, sg_ref, mod_ref, ghg_ref, wo_in, g2_ref, wgu_in, wd_in, gf_ref,
                    o_ref, wo_ref, wgu_ref, wd_ref, *, tiles_per_batch):
    step = pl.program_id(0)

    @pl.when(step < OUT_WEIGHT_STEPS)
    def _():
        for src, dst in ((wo_in, wo_ref), (wgu_in, wgu_ref), (wd_in, wd_ref)):
            rows = src.shape[0]
            dst[pl.ds(pl.multiple_of(step * rows, rows), rows), :] = src[...].astype(BF16)

    @pl.when(step >= OUT_WEIGHT_STEPS)
    def _():
        _out_ffn_tile(x_ref, att_ref, hg_ref, sg_ref, mod_ref, ghg_ref, wo_ref, g2_ref, wgu_ref, wd_ref, gf_ref,
                      o_ref, (step - OUT_WEIGHT_STEPS) // tiles_per_batch)


def _out_ffn_tile(x_ref, att_ref, hg_ref, sg_ref, mod_ref, ghg_ref, wo_ref, g2_ref, wgu_ref, wd_ref, gf_ref,
                  o_ref, batch):
    d = x_ref.shape[-1]
    dff = wd_ref.shape[0]
    row = pl.ds(batch, 1)
    gate1 = mod_ref[row, 2 * d:3 * d]
    shift2 = mod_ref[row, 3 * d:4 * d]
    scale2 = mod_ref[row, 4 * d:5 * d]
    gate2 = mod_ref[row, 5 * d:6 * d]

    parts = [slice(r0, r0 + OUT_PART_ROWS) for r0 in range(0, x_ref.shape[1], OUT_PART_ROWS)]

    def mixed(rows):
        heads = []
        for hd in range(HG_HEADS):
            cs = slice(hd * HG_HEAD_DIM, (hd + 1) * HG_HEAD_DIM)
            o = _rms_rows(hg_ref[0, rows, cs]) * ghg_ref[...]
            heads.append((o * sg_ref[0, rows, cs].astype(F32)).astype(BF16))
        mix = (jnp.dot(att_ref[0, rows, :], wo_ref[0:ATT_WIDTH, :], preferred_element_type=F32)
               + jnp.dot(jnp.concatenate(heads, axis=1), wo_ref[ATT_WIDTH:ATT_WIDTH + HG_WIDTH, :],
                         preferred_element_type=F32))
        x1 = x_ref[0, rows, :] + gate1 * mix
        return x1, ((_rms_rows(x1) * g2_ref[...]) * (1.0 + scale2) + shift2).astype(BF16)

    def gate_up(h):
        return (jnp.dot(h, wgu_ref[:, 0:dff], preferred_element_type=F32),
                jnp.dot(h, wgu_ref[:, dff:2 * dff], preferred_element_type=F32))

    def down(a, g):
        return jnp.dot((_silu(a) * g).astype(BF16), wd_ref[...], preferred_element_type=F32)

    firsts = [mixed(rows) for rows in parts]
    ups = [gate_up(h) for _, h in firsts]
    ys = [down(a, g) for a, g in ups]
    for rows, (x1, _), y in zip(parts, firsts, ys):
        o_ref[0, rows, :] = _rms_rows(x1 + gate2 * y) * gf_ref[...]


def _out_ffn(x, att, hg, sg, mods, ghg, w_out, g2, w_gu, w_down, gf):
    b, t, d = x.shape
    tm = min(OUT_ROW_TILE, t)
    assert tm % OUT_PART_ROWS == 0
    nt = t // tm
    tile = lambda s: jnp.maximum(s - OUT_WEIGHT_STEPS, 0)
    bspec = lambda width: pl.BlockSpec((1, tm, width), lambda s: (tile(s) // nt, tile(s) % nt, 0))
    full = lambda a: pl.BlockSpec(a.shape, lambda s: (0,) * a.ndim, pipeline_mode=pl.Buffered(1))

    def chunked(w):
        assert w.shape[0] % (OUT_WEIGHT_STEPS * SUB) == 0
        return pl.BlockSpec((w.shape[0] // OUT_WEIGHT_STEPS, w.shape[1]),
                            lambda s: (jnp.minimum(s, OUT_WEIGHT_STEPS - 1), 0))

    return pl.pallas_call(
        functools.partial(_out_ffn_kernel, tiles_per_batch=nt),
        out_shape=jax.ShapeDtypeStruct((b, t, d), F32),
        grid=(OUT_WEIGHT_STEPS + b * nt,),
        in_specs=[bspec(d), bspec(ATT_WIDTH), bspec(HG_WIDTH), bspec(HG_WIDTH), full(mods),
                  full(ghg), chunked(w_out), full(g2), chunked(w_gu), chunked(w_down), full(gf)],
        out_specs=bspec(d),
        scratch_shapes=[pltpu.VMEM(w_out.shape, BF16), pltpu.VMEM(w_gu.shape, BF16),
                        pltpu.VMEM(w_down.shape, BF16)],
        compiler_params=pltpu.CompilerParams(dimension_semantics=("arbitrary",),
                                             vmem_limit_bytes=VMEM_LIMIT),
        name="out_ffn",
    )(x, att, hg, sg, mods, ghg, w_out, g2, w_gu, w_down, gf)


def _rope_tables(t):
    n_rows = t // GRID_W
    row = np.repeat(np.arange(n_rows, dtype=np.float32), GRID_W)
    col = np.tile(np.arange(GRID_W, dtype=np.float32), n_rows)
    inv = (np.float32(ROPE_THETA) ** (-np.arange(0, AXIS_DIM, 2, dtype=np.float32) / np.float32(AXIS_DIM)))
    inv = inv.astype(np.float32)
    ar = row[:, None] * inv
    ac = col[:, None] * inv
    ang = np.concatenate([ar, ar, ac, ac], axis=-1).astype(np.float32)
    sign = np.where((np.arange(ATT_HEAD_DIM) % AXIS_DIM) < AXIS_DIM // 2, -1.0, 1.0)
    reps = LANES // ATT_HEAD_DIM
    cos = np.tile(np.cos(ang.astype(np.float64)), (1, reps)).astype(np.float32)
    sin = np.tile(np.sin(ang.astype(np.float64)) * sign, (1, reps)).astype(np.float32)
    return jnp.asarray(cos), jnp.asarray(sin)


def kernel(x, c, ctx, c_ctx, w_mod, b_mod, g_norm1, w_in, g_q, g_k, lb_raw, g_hg, w_out, g_norm2, w_gu, w_down,
           g_final):
    b, t, d = x.shape
    assert w_mod.shape[0] == 1 and lb_raw.shape[0] == 2, "single-layer block"
    assert t % WIN == 0 and (t // WIN) % 2 == 0 and ctx.shape[1] % WIN == 0 and t % KV_TILE == 0

    mods = _modulation(c, c_ctx, w_mod[0], b_mod)

    cos, sin = _rope_tables(t)

    qa, kx, vtx, hq, hv, kf, bf, kb, bb, sg, w_in_b = _input_projection_latent(
        x, mods, g_norm1, w_in[0], g_q, g_k, lb_raw, cos, sin)
    kc, vtc, hvc, kfc, bfc, kbc, bbc = _input_projection_context(
        ctx, mods, b, g_norm1, w_in_b, g_q, g_k, lb_raw)

    att, hg = _mixers(qa, kc, vtc, kx, vtx, hq, hv, kf, bf, kb, bb, hvc, kfc, bfc, kbc, bbc)
    return _out_ffn(x, att, hg, sg, mods, g_hg, w_out[0], g_norm2, w_gu[0], w_down[0], g_final[None, :])
```

```python
import functools

import jax
import jax.numpy as jnp
import numpy as np
from jax import lax
from jax.experimental import pallas as pl
from jax.experimental.pallas import tpu as pltpu

F32 = jnp.float32
BF16 = jnp.bfloat16

LANES = 128
GRID_W = 64
ATT_HEADS = 8
ATT_KV_HEADS = 2
ATT_GROUP = ATT_HEADS // ATT_KV_HEADS
ATT_HEAD_DIM = 64
ATT_WIDTH = ATT_HEADS * ATT_HEAD_DIM
ATT_KV_WIDTH = ATT_KV_HEADS * ATT_HEAD_DIM
AXIS_DIM = ATT_HEAD_DIM // 2
ROPE_THETA = 10000.0
HG_HEADS = 4
HG_HEAD_DIM = 128
HG_WIDTH = HG_HEADS * HG_HEAD_DIM
EPS = 1e-6

MOD_COLUMN_BLOCKS = 8
SUB = 16
WIN = 128
ROW_TILE = 256
IN_WEIGHT_STEPS = 2
OUT_ROW_TILE = 512
OUT_PART_ROWS = 256
Q_TILE = 256
KV_TILE = 512
KV_SLOTS = 4
KV_AHEAD = 2
ONES_ROWS = 16
LOG2E = 1.4426950408889634
VMEM_LIMIT = 56 * 1024 * 1024

C_AQ = 0
C_AK = C_AQ + ATT_WIDTH
C_AV = C_AK + ATT_KV_WIDTH
C_HQ = C_AV + ATT_KV_WIDTH
C_HI = C_HQ + HG_WIDTH
C_FF = C_HI + HG_WIDTH
C_FB = C_FF + HG_WIDTH
C_HG = C_FB + HG_WIDTH


def _silu(v):
    return v * jax.nn.sigmoid(v)


def _rms_rows(v):
    return v * lax.rsqrt(jnp.mean(v * v, axis=-1, keepdims=True) + EPS)


def _mod_kernel(c_ref, cx_ref, w_ref, b_ref, o_ref, a_ref):
    nb = c_ref.shape[0]
    a_ref[...] = jnp.zeros(a_ref.shape, F32)
    a_ref[0:nb, :] = c_ref[...]
    a_ref[nb:nb + 1, :] = cx_ref[...]
    a = _silu(a_ref[...])
    rows = a.shape[0]
    w = w_ref[...]
    a_hi = a.astype(BF16)
    a_lo = (a - a_hi.astype(F32)).astype(BF16)
    w_hi = w.astype(BF16)
    w_lo = (w - w_hi.astype(F32)).astype(BF16)
    both = jnp.dot(jnp.concatenate([a_hi, a_lo], axis=0), w_hi, preferred_element_type=F32)
    o_ref[...] = (both[0:rows] + both[rows:2 * rows]
                  + jnp.dot(a_hi, w_lo, preferred_element_type=F32) + b_ref[...])


def _modulation(c, c_ctx, w_mod, b_mod):
    nb, d = c.shape
    rows = -(-(nb + 1) // 8) * 8
    n = w_mod.shape[1]
    bn = n // MOD_COLUMN_BLOCKS
    return pl.pallas_call(
        _mod_kernel,
        out_shape=jax.ShapeDtypeStruct((rows, n), F32),
        grid=(n // bn,),
        in_specs=[pl.BlockSpec((nb, d), lambda j: (0, 0)),
                  pl.BlockSpec((1, d), lambda j: (0, 0)),
                  pl.BlockSpec((d, bn), lambda j: (0, j)),
                  pl.BlockSpec((1, bn), lambda j: (0, j))],
        out_specs=pl.BlockSpec((rows, bn), lambda j: (0, j)),
        scratch_shapes=[pltpu.VMEM((rows, d), F32)],
        compiler_params=pltpu.CompilerParams(dimension_semantics=("arbitrary",),
                                             vmem_limit_bytes=VMEM_LIMIT),
        name="modulation",
    )(c, c_ctx[None, :], w_mod, b_mod)


def _head64_meansq(p, seg2):
    sq = p * p
    hi = sq.astype(BF16)
    lo = (sq - hi.astype(F32)).astype(BF16)
    s = jnp.dot(jnp.concatenate([hi, lo], axis=1), seg2, preferred_element_type=F32)
    return s * (1.0 / ATT_HEAD_DIM)


def _rope(v, cos, sin_signed, first_half):
    fwd = pltpu.roll(v, LANES - AXIS_DIM // 2, 1)
    bwd = pltpu.roll(v, AXIS_DIM // 2, 1)
    return v * cos + jnp.where(first_half, fwd, bwd) * sin_signed


def _gate_prep(f, lb, suffix):
    fg = lb + (1.0 - lb) * jax.nn.sigmoid(f)
    cum = jnp.log2(fg)
    rows = cum.shape[0]
    pos = lax.broadcasted_iota(jnp.int32, cum.shape, 0) % SUB
    shift = 1
    while shift < SUB:
        if suffix:
            moved = pltpu.roll(cum, rows - shift, 0)
            cum = cum + jnp.where(pos < SUB - shift, moved, 0.0)
        else:
            moved = pltpu.roll(cum, shift, 0)
            cum = cum + jnp.where(pos >= shift, moved, 0.0)
        shift *= 2
    return (1.0 - fg).astype(BF16), cum


def _inproj_kernel(x_ref, mod_ref, g1_ref, w_ref, gq_ref, gk_ref, lb_ref, cos_ref, sin_ref, *out_refs, latent,
                   mod_row):
    d = x_ref.shape[-1]
    x = x_ref[0]
    row = pl.ds(mod_row, 1)
    shift = mod_ref[row, 0:d]
    scale = mod_ref[row, d:2 * d]
    gq = jnp.concatenate([gq_ref[...]] * (LANES // ATT_HEAD_DIM), axis=1)
    gk = jnp.concatenate([gk_ref[...]] * (LANES // ATT_HEAD_DIM), axis=1)
    h = (_rms_rows(x) * g1_ref[...]) * (1.0 + scale) + shift
    hb = h.astype(BF16)

    def proj(c0, width):
        return jnp.dot(hb, w_ref[:, c0:c0 + width], preferred_element_type=F32)

    lane = lax.broadcasted_iota(jnp.int32, (2 * LANES, LANES), 0) % LANES
    lane_c = lax.broadcasted_iota(jnp.int32, (2 * LANES, LANES), 1)
    seg = (lane // ATT_HEAD_DIM == lane_c // ATT_HEAD_DIM).astype(BF16)
    row_lane = lax.broadcasted_iota(jnp.int32, (x.shape[0], LANES), 1)
    first_half = (row_lane % AXIS_DIM) < (AXIS_DIM // 2)
    upper = row_lane >= ATT_HEAD_DIM

    r0 = lb_ref[0]
    r1 = lb_ref[1]
    rm = jnp.maximum(r0, r1)
    e0 = jnp.exp(r0 - rm)
    e1 = jnp.exp(r1 - rm)
    lb_all = e0 / (e0 + e1)

    if latent:
        (qa_ref, ka_ref, vat_ref, hq_ref, hv_ref, kf_ref, bf_ref, kb_ref, bb_ref, sg_ref) = out_refs
        cos = cos_ref[...]
        sin = sin_ref[...]
    else:
        (ka_ref, vat_ref, hv_ref, kf_ref, bf_ref, kb_ref, bb_ref) = out_refs

    p_ff = proj(C_FF, HG_WIDTH)
    if latent:
        pq = proj(C_AQ, ATT_WIDTH)
    pk = proj(C_AK, ATT_KV_WIDTH)
    pv = proj(C_AV, ATT_KV_WIDTH)
    p_fb = proj(C_FB, HG_WIDTH)
    if latent:
        p_hq = proj(C_HQ, HG_WIDTH)
    p_hv = proj(C_HI, HG_WIDTH)
    ms_k = _head64_meansq(pk, seg)
    if latent:
        ms_q = [_head64_meansq(pq[:, c * LANES:(c + 1) * LANES], seg) for c in range(ATT_WIDTH // LANES)]

    kn = pk * lax.rsqrt(ms_k + EPS) * gk
    if latent:
        kn = _rope(kn, cos, sin, first_half)
    ka_ref[0] = kn.astype(BF16)
    vat_ref[0] = pv.T.astype(BF16)

    if latent:
        for c in range(ATT_WIDTH // LANES):
            t = pq[:, c * LANES:(c + 1) * LANES]
            t = t * lax.rsqrt(ms_q[c] + EPS) * gq
            t = _rope(t, cos, sin, first_half) * (ATT_HEAD_DIM ** -0.5 * LOG2E)
            t_swapped = pltpu.roll(t, ATT_HEAD_DIM, 1)
            for half in range(2):
                head = 2 * c + half
                kv = head // ATT_GROUP
                src = t if half == kv else t_swapped
                keep = upper if kv == 1 else jnp.logical_not(upper)
                qa_ref[0, :, head * LANES:(head + 1) * LANES] = jnp.where(keep, src, 0.0).astype(BF16)
        hq_ref[0] = (_silu(p_hq) * (HG_HEAD_DIM ** -0.5)).astype(BF16)
        sg_ref[0] = _silu(proj(C_HG, HG_WIDTH)).astype(BF16)

    hv_ref[0] = p_hv.astype(BF16)
    kf, bf = _gate_prep(p_ff, lb_all[0:1, :], False)
    kf_ref[0] = kf
    bf_ref[0] = bf
    kb, bb = _gate_prep(p_fb, lb_all[1:2, :], True)
    kb_ref[0] = kb
    bb_ref[0] = bb


def _inproj_latent_kernel(x_ref, mod_ref, g1_ref, w_in, gq_ref, gk_ref, lb_ref, cos_ref, sin_ref, *refs,
                          tiles_per_batch):
    *out_refs, wb_ref, w_ref = refs
    step = pl.program_id(0)

    @pl.when(step < IN_WEIGHT_STEPS)
    def _():
        rows = w_in.shape[0]
        chunk = w_in[...].astype(BF16)
        w_ref[pl.ds(pl.multiple_of(step * rows, rows), rows), :] = chunk
        wb_ref[...] = chunk

    @pl.when(step >= IN_WEIGHT_STEPS)
    def _():
        _inproj_kernel(x_ref, mod_ref, g1_ref, w_ref, gq_ref, gk_ref, lb_ref, cos_ref, sin_ref, *out_refs,
                       latent=True, mod_row=(step - IN_WEIGHT_STEPS) // tiles_per_batch)


def _input_projection_latent(rows, mods, g1, w_in, gq, gk, lb_raw, cos, sin):
    b, t, d = rows.shape
    tm = min(ROW_TILE, t)
    nt = t // tm
    assert w_in.shape[0] % (IN_WEIGHT_STEPS * SUB) == 0
    chunk_rows = w_in.shape[0] // IN_WEIGHT_STEPS
    tile = lambda s: jnp.maximum(s - IN_WEIGHT_STEPS, 0)
    chunk = lambda s: jnp.minimum(s, IN_WEIGHT_STEPS - 1)
    bspec = lambda width: pl.BlockSpec((1, tm, width), lambda s: (tile(s) // nt, tile(s) % nt, 0))
    full = lambda a: pl.BlockSpec(a.shape, lambda s: (0,) * a.ndim)
    sds = lambda width, dt: jax.ShapeDtypeStruct((b, t, width), dt)
    table_spec = pl.BlockSpec((tm, LANES), lambda s: (tile(s) % nt, 0))
    w_spec = pl.BlockSpec((chunk_rows, w_in.shape[1]), lambda s: (chunk(s), 0))
    out_shape = [sds(ATT_HEADS * LANES, BF16), sds(ATT_KV_WIDTH, BF16),
                 jax.ShapeDtypeStruct((b, ATT_KV_WIDTH, t), BF16),
                 sds(HG_WIDTH, BF16), sds(HG_WIDTH, BF16), sds(HG_WIDTH, BF16), sds(HG_WIDTH, F32),
                 sds(HG_WIDTH, BF16), sds(HG_WIDTH, F32), sds(HG_WIDTH, BF16),
                 jax.ShapeDtypeStruct(w_in.shape, BF16)]
    out_specs = [bspec(ATT_HEADS * LANES), bspec(ATT_KV_WIDTH),
                 pl.BlockSpec((1, ATT_KV_WIDTH, tm), lambda s: (tile(s) // nt, 0, tile(s) % nt)),
                 bspec(HG_WIDTH), bspec(HG_WIDTH), bspec(HG_WIDTH), bspec(HG_WIDTH), bspec(HG_WIDTH),
                 bspec(HG_WIDTH), bspec(HG_WIDTH), w_spec]
    return pl.pallas_call(
        functools.partial(_inproj_latent_kernel, tiles_per_batch=nt),
        out_shape=out_shape,
        grid=(IN_WEIGHT_STEPS + b * nt,),
        in_specs=[bspec(d), full(mods), full(g1), w_spec, full(gq), full(gk), full(lb_raw),
                  table_spec, table_spec],
        out_specs=out_specs,
        scratch_shapes=[pltpu.VMEM(w_in.shape, BF16)],
        compiler_params=pltpu.CompilerParams(dimension_semantics=("arbitrary",),
                                             vmem_limit_bytes=VMEM_LIMIT),
        name="inproj_latent",
    )(rows, mods, g1, w_in, gq, gk, lb_raw, cos, sin)


def _input_projection_context(rows, mods, mod_row, g1, w_in, gq, gk, lb_raw):
    b, t, d = rows.shape
    tm = min(ROW_TILE, t)
    nt = t // tm
    bspec = lambda width: pl.BlockSpec((1, tm, width), lambda i, j: (i, j, 0))
    full = lambda a: pl.BlockSpec(a.shape, lambda i, j: (0,) * a.ndim)
    sds = lambda width, dt: jax.ShapeDtypeStruct((b, t, width), dt)
    out_shape = [sds(ATT_KV_WIDTH, BF16), jax.ShapeDtypeStruct((b, ATT_KV_WIDTH, t), BF16),
                 sds(HG_WIDTH, BF16), sds(HG_WIDTH, BF16), sds(HG_WIDTH, F32), sds(HG_WIDTH, BF16),
                 sds(HG_WIDTH, F32)]
    out_specs = [bspec(ATT_KV_WIDTH), pl.BlockSpec((1, ATT_KV_WIDTH, tm), lambda i, j: (i, 0, j)),
                 bspec(HG_WIDTH), bspec(HG_WIDTH), bspec(HG_WIDTH), bspec(HG_WIDTH), bspec(HG_WIDTH)]
    return pl.pallas_call(
        functools.partial(_inproj_context_kernel, mod_row=mod_row),
        out_shape=out_shape,
        grid=(b, nt),
        in_specs=[bspec(d), full(mods), full(g1), full(w_in), full(gq), full(gk), full(lb_raw)],
        out_specs=out_specs,
        compiler_params=pltpu.CompilerParams(dimension_semantics=("parallel", "parallel"),
                                             vmem_limit_bytes=VMEM_LIMIT),
        name="inproj_context",
    )(rows, mods, g1, w_in, gq, gk, lb_raw)


def _inproj_context_kernel(x_ref, mod_ref, g1_ref, w_ref, gq_ref, gk_ref, lb_ref, *out_refs, mod_row):
    _inproj_kernel(x_ref, mod_ref, g1_ref, w_ref, gq_ref, gk_ref, lb_ref, None, None, *out_refs,
                   latent=False, mod_row=mod_row)


def _attention_schedule(q_ref, kc_ref, vtc_ref, kx_ref, vtx_ref, o_ref, sc_ref, s_ref, acc_ref):
    tq = q_ref.shape[1]
    n_kv = kx_ref.shape[1] // KV_TILE
    nt_dims = (((1,), (1,)), ((), ()))
    cols = ATT_GROUP * tq

    qgs = [jnp.concatenate(
        [q_ref[0, :, (kv * ATT_GROUP + j) * LANES:(kv * ATT_GROUP + j + 1) * LANES] for j in range(ATT_GROUP)],
        axis=0) for kv in range(ATT_KV_HEADS)]

    def scores(kt, dst):
        for kv in range(ATT_KV_HEADS):
            dst(kv, lax.dot_general(kt, qgs[kv], nt_dims, preferred_element_type=F32))

    def consume(src, vt_all, maxima):
        ones = jnp.ones((ONES_ROWS, vt_all.shape[1]), BF16)
        out = []
        for kv in range(ATT_KV_HEADS):
            m = maxima[kv]
            m_new = jnp.maximum(m, jnp.max(src(kv), axis=0, keepdims=True))
            alpha = jnp.exp2(m - m_new)
            p = jnp.exp2(src(kv) - m_new).astype(BF16)
            vt = jnp.concatenate([vt_all[kv * ATT_HEAD_DIM:(kv + 1) * ATT_HEAD_DIM], ones], axis=0)
            acc_ref[kv] = alpha * acc_ref[kv] + jnp.dot(vt, p, preferred_element_type=F32)
            out.append(m_new)
        return tuple(out)

    def put_ctx(kv, val):
        sc_ref[kv] = val

    def put(slot):
        def dst(kv, val):
            s_ref[slot, kv] = val
        return dst

    def k_tile(j):
        return kx_ref[0, j * KV_TILE:(j + 1) * KV_TILE, :]

    def vt_tile(j):
        return vtx_ref[0, :, j * KV_TILE:(j + 1) * KV_TILE]

    acc_ref[...] = jnp.zeros(acc_ref.shape, F32)
    scores(kc_ref[0], put_ctx)
    for j in range(min(KV_AHEAD, n_kv)):
        scores(k_tile(j), put(j % KV_SLOTS))
    maxima = consume(lambda kv: sc_ref[kv], vtc_ref[0], (jnp.full((1, cols), -jnp.inf, F32),) * ATT_KV_HEADS)
    for j in range(n_kv):
        if j + KV_AHEAD < n_kv:
            scores(k_tile(j + KV_AHEAD), put((j + KV_AHEAD) % KV_SLOTS))
        maxima = consume(lambda kv, slot=j % KV_SLOTS: s_ref[slot, kv], vt_tile(j), maxima)
        yield

    for kv in range(ATT_KV_HEADS):
        o = acc_ref[kv, 0:ATT_HEAD_DIM] / acc_ref[kv, ATT_HEAD_DIM:ATT_HEAD_DIM + 1]
        for jj in range(ATT_GROUP // 2):
            pair = jnp.concatenate([o[:, (2 * jj) * tq:(2 * jj + 1) * tq],
                                    o[:, (2 * jj + 1) * tq:(2 * jj + 2) * tq]], axis=0)
            col = kv * (ATT_GROUP // 2) + jj
            o_ref[0, :, col * LANES:(col + 1) * LANES] = pair.T.astype(BF16)


NT_DIMS = (((1,), (1,)), ((), ()))


def _scan_window(st, q, k, v, b, b_row, msk_ref, reverse, want_out):
    nsub = WIN // SUB
    last = 0 if reverse else SUB - 1
    sub_of = (lambda p: nsub - 1 - p) if reverse else (lambda p: p)
    blk = lambda a, i: a[i * SUB:(i + 1) * SUB]
    d = int(reverse)

    b_last = [b_row(i * SUB + last) for i in range(nsub)]
    cum = []
    for p in range(nsub):
        cum.append(b_last[sub_of(p)] if p == 0 else cum[-1] + b_last[sub_of(p)])
    ks = [blk(k, i) * jnp.exp2(b_last[i] - blk(b, i)) for i in range(nsub)]

    k_end = jnp.concatenate(
        [ks[i] if sub_of(i) == nsub - 1 else ks[i] * jnp.exp2(cum[nsub - 1] - cum[sub_of(i)]) for i in range(nsub)],
        axis=0)
    st_new = jnp.exp2(cum[nsub - 1]) * st + jnp.dot(v.T, k_end, preferred_element_type=F32)
    if not want_out:
        return st_new, None

    qs = [blk(q, i) * jnp.exp2(blk(b, i)) for i in range(nsub)]
    q_start = jnp.concatenate(
        [qs[i] if sub_of(i) == 0 else qs[i] * jnp.exp2(cum[sub_of(i) - 1]) for i in range(nsub)], axis=0)
    o_carried = lax.dot_general(q_start, st, NT_DIMS, preferred_element_type=F32)

    scores = None
    for level, h in enumerate((nsub // 2, nsub // 4, nsub // 8)):
        q_rows, k_rows = [], []
        for i in range(nsub):
            p = sub_of(i)
            ref = (p // (2 * h)) * 2 * h + h - 1
            q_rows.append(qs[i] * jnp.exp2(cum[p - 1] - cum[ref]) if p - 1 > ref else qs[i])
            k_rows.append(ks[i] * jnp.exp2(cum[ref] - cum[p]) if p < ref else ks[i])
        term = (lax.dot_general(jnp.concatenate(q_rows, axis=0), jnp.concatenate(k_rows, axis=0), NT_DIMS,
                                preferred_element_type=F32) * msk_ref[d, level])
        scores = term if scores is None else scores + term

    def level_scores(level, q_exponent, k_exponent):
        q_l = q * jnp.exp2(jnp.minimum(q_exponent, 0.0))
        k_l = k if k_exponent is None else k * jnp.exp2(jnp.minimum(k_exponent, 0.0))
        return lax.dot_general(q_l, k_l, NT_DIMS, preferred_element_type=F32) * msk_ref[d, level]

    scores = scores + (jnp.dot(q * k, jnp.ones((LANES, LANES), BF16), preferred_element_type=F32)
                       * msk_ref[d, N_LEVELS])
    for level, h in ((3, SUB // 2), (4, SUB // 4)):
        ref_rows = [b_row(s * 2 * h + (h if reverse else h - 1)) for s in range(WIN // (2 * h))]
        diff = b - jnp.concatenate([jnp.broadcast_to(r, (2 * h, LANES)) for r in ref_rows], axis=0)
        decay = jnp.exp2(jnp.minimum(diff, -diff))
        scores = scores + (lax.dot_general(q * decay, k * decay, NT_DIMS, preferred_element_type=F32)
                           * msk_ref[d, level])
    before = lambda n: pltpu.roll(b, n, 0)
    after = lambda n: pltpu.roll(b, WIN - n, 0)
    off4 = lax.broadcasted_iota(jnp.int32, (WIN, LANES), 0) % 4
    if reverse:
        q_exp = b - jnp.where(off4 == 0, after(2), after(1))
        k_exp = jnp.where(off4 == 2, 0.0, before(1) - b)
    else:
        q_exp = b - jnp.where(off4 == 3, before(2), before(1))
        k_exp = jnp.where(off4 == 1, 0.0, after(1) - b)
    scores = scores + level_scores(5, q_exp, k_exp)
    scores = scores + level_scores(6, b - (after(1) if reverse else before(1)), None)
    return st_new, (scores.astype(BF16), o_carried)


N_LEVELS = 7


def _pair_masks():
    msk = np.zeros((2, N_LEVELS + 1, WIN, WIN), np.float32)
    pos = np.arange(WIN)
    for d in range(2):
        for level in range(N_LEVELS):
            h = WIN >> (level + 1)
            span, off = pos // (2 * h), pos % (2 * h)
            later = (off < h) if d else (off >= h)
            msk[d, level] = (span[:, None] == span[None, :]) & later[:, None] & ~later[None, :]
        msk[d, N_LEVELS] = np.eye(WIN)
    return jnp.asarray(msk, F32)


def _hgrn_steps(q_ref, v_ref, kf_ref, bf_ref, kb_ref, bb_ref, vc_ref, kfc_ref, bfc_ref, kbc_ref, bbc_ref,
                msk_ref, o_ref, stf_ref, stb_ref, scf_ref, scb_ref, ocf_ref, ocb_ref, w0, n_steps):
    t = q_ref.shape[1]
    ctx = vc_ref.shape[1]
    nw = t // WIN

    def window(refs, start, st, reverse, want_out):
        qr, vr, kr, br = refs
        sl = pl.ds(start, WIN)
        vw = vr[0, sl, :].astype(F32)
        kw = kr[0, sl, :].astype(F32)
        bw = br[0, sl, :]
        qw = qr[0, sl, :].astype(F32) if want_out else None
        b_row = lambda r: br[0, pl.ds(start + r, 1), :]
        return _scan_window(st, qw, kw, vw, bw, b_row, msk_ref, reverse, want_out)

    def restart():
        zero = jnp.zeros((LANES, LANES), F32)
        stf, stb = zero, zero
        n_cw = ctx // WIN
        for w in range(n_cw):
            stf, _ = window((None, vc_ref, kfc_ref, bfc_ref), w * WIN, stf, False, False)
            stb, _ = window((None, vc_ref, kbc_ref, bbc_ref), (n_cw - 1 - w) * WIN, stb, True, False)
        stf_ref[...] = stf
        stb_ref[...] = stb
        o_ref[...] = jnp.zeros(o_ref.shape, F32)

    def starts(w):
        return pl.multiple_of(w * WIN, WIN), pl.multiple_of((nw - 1 - w) * WIN, WIN)

    def scores_stage(w):
        fs, bs = starts(w)
        stf, (sf, cf) = window((q_ref, v_ref, kf_ref, bf_ref), fs, stf_ref[...], False, True)
        stf_ref[...] = stf
        scf_ref[...] = sf
        ocf_ref[...] = cf
        stb, (sb, cb) = window((q_ref, v_ref, kb_ref, bb_ref), bs, stb_ref[...], True, True)
        stb_ref[...] = stb
        scb_ref[...] = sb
        ocb_ref[...] = cb

    def output_stage(w):
        fs, bs = starts(w)
        o_ref[0, pl.ds(fs, WIN), :] += (
            jnp.dot(scf_ref[...], v_ref[0, pl.ds(fs, WIN), :], preferred_element_type=F32) + ocf_ref[...])
        o_ref[0, pl.ds(bs, WIN), :] += (
            jnp.dot(scb_ref[...], v_ref[0, pl.ds(bs, WIN), :], preferred_element_type=F32) + ocb_ref[...])

    def schedule():
        for u in range(n_steps + 1):
            if u > 0:
                output_stage(w0 + u - 1)
            if u < n_steps:
                scores_stage(w0 + u)
            yield

    return restart, schedule()


def _mixer_kernel(qa_ref, kc_ref, vtc_ref, kx_ref, vtx_ref,
                  hq_ref, hv_ref, kf_ref, bf_ref, kb_ref, bb_ref, hvc_ref, kfc_ref, bfc_ref, kbc_ref, bbc_ref,
                  msk_ref, wo_in, wgu_in, wd_in, att_ref, hg_ref, wo_b, wgu_b, wd_b,
                  sc_ref, s_ref, acc_ref, *hg_scratch, steps_per_head):
    nw = hq_ref.shape[1] // WIN
    n_steps = nw // steps_per_head
    phase = pl.program_id(1) % steps_per_head

    @pl.when(phase == 0)
    def _():
        for src, dst in ((wo_in, wo_b), (wgu_in, wgu_b), (wd_in, wd_b)):
            dst[...] = src[...].astype(BF16)

    restart, scan = _hgrn_steps(hq_ref, hv_ref, kf_ref, bf_ref, kb_ref, bb_ref,
                                hvc_ref, kfc_ref, bfc_ref, kbc_ref, bbc_ref, msk_ref, hg_ref, *hg_scratch,
                                w0=phase * n_steps, n_steps=n_steps)
    pl.when(phase == 0)(restart)

    for _ in _attention_schedule(qa_ref, kc_ref, vtc_ref, kx_ref, vtx_ref, att_ref, sc_ref, s_ref, acc_ref):
        next(scan, None)
    for _ in scan:
        pass


def _mixers(qa, kc, vtc, kx, vtx, hq, hv, kf, bf, kb, bb, hvc, kfc, bfc, kbc, bbc, weights):
    b, t, _ = qa.shape
    ctx = hvc.shape[1]
    tq = min(Q_TILE, t)
    n_q = t // tq
    nw = t // WIN
    assert tq % LANES == 0 and t % tq == 0 and t % KV_TILE == 0 and KV_AHEAD < KV_SLOTS
    assert n_q % HG_HEADS == 0 and nw % (n_q // HG_HEADS) == 0
    steps_per_head = n_q // HG_HEADS
    cols = ATT_GROUP * tq
    whole = lambda a: pl.BlockSpec((1,) + a.shape[1:], lambda i, j: (i, 0, 0))
    lat = pl.BlockSpec((1, t, HG_HEAD_DIM), lambda i, j: (i, 0, j // steps_per_head))
    con = pl.BlockSpec((1, ctx, HG_HEAD_DIM), lambda i, j: (i, 0, j // steps_per_head))
    msk = _pair_masks()
    const = lambda a: pl.BlockSpec(a.shape, lambda i, j: (0,) * a.ndim)
    n_chunks = b * HG_HEADS
    assert all(w.shape[0] % (n_chunks * SUB) == 0 for w in weights)
    chunk = lambda w: pl.BlockSpec((w.shape[0] // n_chunks, w.shape[1]),
                                   lambda i, j: (i * HG_HEADS + j // steps_per_head, 0))
    return pl.pallas_call(
        functools.partial(_mixer_kernel, steps_per_head=steps_per_head),
        out_shape=[jax.ShapeDtypeStruct((b, t, ATT_WIDTH), BF16), jax.ShapeDtypeStruct((b, t, HG_WIDTH), F32)]
                  + [jax.ShapeDtypeStruct(w.shape, BF16) for w in weights],
        grid=(b, n_q),
        in_specs=[pl.BlockSpec((1, tq, qa.shape[-1]), lambda i, j: (i, j, 0)),
                  whole(kc), whole(vtc), whole(kx), whole(vtx)] + [lat] * 6 + [con] * 5
                 + [const(msk)] + [chunk(w) for w in weights],
        out_specs=[pl.BlockSpec((1, tq, ATT_WIDTH), lambda i, j: (i, j, 0)), lat]
                  + [chunk(w) for w in weights],
        scratch_shapes=[pltpu.VMEM((ATT_KV_HEADS, kc.shape[1], cols), F32),
                        pltpu.VMEM((KV_SLOTS, ATT_KV_HEADS, KV_TILE, cols), F32),
                        pltpu.VMEM((ATT_KV_HEADS, ATT_HEAD_DIM + ONES_ROWS, cols), F32),
                        pltpu.VMEM((LANES, LANES), F32), pltpu.VMEM((LANES, LANES), F32),
                        pltpu.VMEM((WIN, WIN), BF16), pltpu.VMEM((WIN, WIN), BF16),
                        pltpu.VMEM((WIN, LANES), F32), pltpu.VMEM((WIN, LANES), F32)],
        compiler_params=pltpu.CompilerParams(dimension_semantics=("arbitrary", "arbitrary"),
                                             vmem_limit_bytes=VMEM_LIMIT),
        name="mixers",
    )(qa, kc, vtc, kx, vtx, hq, hv, kf, bf, kb, bb, hvc, kfc, bfc, kbc, bbc, msk, *weights)


def _out_ffn_kernel(x_ref, att_ref, hg_ref, sg_ref, mod_ref, ghg_ref, wo_ref, g2_ref, wgu_ref, wd_ref, gf_ref,
                    o_ref):
    d = x_ref.shape[-1]
    dff = wd_ref.shape[0]
    row = pl.ds(pl.program_id(0), 1)
    gate1 = mod_ref[row, 2 * d:3 * d]
    shift2 = mod_ref[row, 3 * d:4 * d]
    scale2 = mod_ref[row, 4 * d:5 * d]
    gate2 = mod_ref[row, 5 * d:6 * d]

    parts = [slice(r0, r0 + OUT_PART_ROWS) for r0 in range(0, x_ref.shape[1], OUT_PART_ROWS)]

    def mixed(rows):
        heads = []
        for hd in range(HG_HEADS):
            cs = slice(hd * HG_HEAD_DIM, (hd + 1) * HG_HEAD_DIM)
            o = _rms_rows(hg_ref[0, rows, cs]) * ghg_ref[...]
            heads.append((o * sg_ref[0, rows, cs].astype(F32)).astype(BF16))
        mix = (jnp.dot(att_ref[0, rows, :], wo_ref[0:ATT_WIDTH, :], preferred_element_type=F32)
               + jnp.dot(jnp.concatenate(heads, axis=1), wo_ref[ATT_WIDTH:ATT_WIDTH + HG_WIDTH, :],
                         preferred_element_type=F32))
        x1 = x_ref[0, rows, :] + gate1 * mix
        return x1, ((_rms_rows(x1) * g2_ref[...]) * (1.0 + scale2) + shift2).astype(BF16)

    def gate_up(h):
        return (jnp.dot(h, wgu_ref[:, 0:dff], preferred_element_type=F32),
                jnp.dot(h, wgu_ref[:, dff:2 * dff], preferred_element_type=F32))

    def down(a, g):
        return jnp.dot((_silu(a) * g).astype(BF16), wd_ref[...], preferred_element_type=F32)

    firsts = [mixed(rows) for rows in parts]
    ups = [gate_up(h) for _, h in firsts]
    ys = [down(a, g) for a, g in ups]
    for rows, (x1, _), y in zip(parts, firsts, ys):
        o_ref[0, rows, :] = _rms_rows(x1 + gate2 * y) * gf_ref[...]


def _out_ffn(x, att, hg, sg, mods, ghg, w_out, g2, w_gu, w_down, gf):
    b, t, d = x.shape
    tm = min(OUT_ROW_TILE, t)
    assert tm % OUT_PART_ROWS == 0
    bspec = lambda width: pl.BlockSpec((1, tm, width), lambda i, j: (i, j, 0))
    full = lambda a: pl.BlockSpec(a.shape, lambda i, j: (0,) * a.ndim, pipeline_mode=pl.Buffered(1))
    return pl.pallas_call(
        _out_ffn_kernel,
        out_shape=jax.ShapeDtypeStruct((b, t, d), F32),
        grid=(b, t // tm),
        in_specs=[bspec(d), bspec(ATT_WIDTH), bspec(HG_WIDTH), bspec(HG_WIDTH), full(mods),
                  full(ghg), full(w_out), full(g2), full(w_gu), full(w_down), full(gf)],
        out_specs=bspec(d),
        compiler_params=pltpu.CompilerParams(dimension_semantics=("parallel", "parallel"),
                                             vmem_limit_bytes=VMEM_LIMIT),
        name="out_ffn",
    )(x, att, hg, sg, mods, ghg, w_out, g2, w_gu, w_down, gf)


def _rope_tables(t):
    n_rows = t // GRID_W
    row = np.repeat(np.arange(n_rows, dtype=np.float32), GRID_W)
    col = np.tile(np.arange(GRID_W, dtype=np.float32), n_rows)
    inv = (np.float32(ROPE_THETA) ** (-np.arange(0, AXIS_DIM, 2, dtype=np.float32) / np.float32(AXIS_DIM)))
    inv = inv.astype(np.float32)
    ar = row[:, None] * inv
    ac = col[:, None] * inv
    ang = np.concatenate([ar, ar, ac, ac], axis=-1).astype(np.float32)
    sign = np.where((np.arange(ATT_HEAD_DIM) % AXIS_DIM) < AXIS_DIM // 2, -1.0, 1.0)
    reps = LANES // ATT_HEAD_DIM
    cos = np.tile(np.cos(ang.astype(np.float64)), (1, reps)).astype(np.float32)
    sin = np.tile(np.sin(ang.astype(np.float64)) * sign, (1, reps)).astype(np.float32)
    return jnp.asarray(cos), jnp.asarray(sin)


def kernel(x, c, ctx, c_ctx, w_mod, b_mod, g_norm1, w_in, g_q, g_k, lb_raw, g_hg, w_out, g_norm2, w_gu, w_down,
           g_final):
    b, t, d = x.shape
    assert w_mod.shape[0] == 1 and lb_raw.shape[0] == 2, "single-layer block"
    assert t % WIN == 0 and (t // WIN) % 2 == 0 and ctx.shape[1] % WIN == 0 and t % KV_TILE == 0

    mods = _modulation(c, c_ctx, w_mod[0], b_mod)

    cos, sin = _rope_tables(t)

    qa, kx, vtx, hq, hv, kf, bf, kb, bb, sg, w_in_b = _input_projection_latent(
        x, mods, g_norm1, w_in[0], g_q, g_k, lb_raw, cos, sin)
    kc, vtc, hvc, kfc, bfc, kbc, bbc = _input_projection_context(
        ctx, mods, b, g_norm1, w_in_b, g_q, g_k, lb_raw)

    att, hg, w_out_b, w_gu_b, w_down_b = _mixers(qa, kc, vtc, kx, vtx, hq, hv, kf, bf, kb, bb, hvc, kfc, bfc, kbc,
                                                 bbc, (w_out[0], w_gu[0], w_down[0]))
    return _out_ffn(x, att, hg, sg, mods, g_hg, w_out_b, g_norm2, w_gu_b, w_down_b, g_final[None, :])
```

```python
import functools

import jax
import jax.numpy as jnp
import numpy as np
from jax import lax
from jax.experimental import pallas as pl
from jax.experimental.pallas import tpu as pltpu

F32 = jnp.float32
BF16 = jnp.bfloat16

LANES = 128
GRID_W = 64
ATT_HEADS = 8
ATT_KV_HEADS = 2
ATT_GROUP = ATT_HEADS // ATT_KV_HEADS
ATT_HEAD_DIM = 64
ATT_WIDTH = ATT_HEADS * ATT_HEAD_DIM
ATT_KV_WIDTH = ATT_KV_HEADS * ATT_HEAD_DIM
AXIS_DIM = ATT_HEAD_DIM // 2
ROPE_THETA = 10000.0
HG_HEADS = 4
HG_HEAD_DIM = 128
HG_WIDTH = HG_HEADS * HG_HEAD_DIM
EPS = 1e-6

MOD_COLUMN_BLOCKS = 8
SUB = 16
WIN = 128
ROW_TILE = 256
IN_WEIGHT_STEPS = 2
OUT_ROW_TILE = 512
OUT_PART_ROWS = 256
Q_TILE = 256
KV_TILE = 512
KV_SLOTS = 4
KV_AHEAD = 2
ONES_ROWS = 16
LOG2E = 1.4426950408889634
VMEM_LIMIT = 56 * 1024 * 1024

C_AQ = 0
C_AK = C_AQ + ATT_WIDTH
C_AV = C_AK + ATT_KV_WIDTH
C_HQ = C_AV + ATT_KV_WIDTH
C_HI = C_HQ + HG_WIDTH
C_FF = C_HI + HG_WIDTH
C_FB = C_FF + HG_WIDTH
C_HG = C_FB + HG_WIDTH


def _silu(v):
    return v * jax.nn.sigmoid(v)


def _rms_rows(v):
    return v * lax.rsqrt(jnp.mean(v * v, axis=-1, keepdims=True) + EPS)


def _mod_kernel(c_ref, cx_ref, w_ref, b_ref, o_ref, a_ref):
    nb = c_ref.shape[0]
    a_ref[...] = jnp.zeros(a_ref.shape, F32)
    a_ref[0:nb, :] = c_ref[...]
    a_ref[nb:nb + 1, :] = cx_ref[...]
    a = _silu(a_ref[...])
    rows = a.shape[0]
    w = w_ref[...]
    a_hi = a.astype(BF16)
    a_lo = (a - a_hi.astype(F32)).astype(BF16)
    w_hi = w.astype(BF16)
    w_lo = (w - w_hi.astype(F32)).astype(BF16)
    both = jnp.dot(jnp.concatenate([a_hi, a_lo], axis=0), w_hi, preferred_element_type=F32)
    o_ref[...] = (both[0:rows] + both[rows:2 * rows]
                  + jnp.dot(a_hi, w_lo, preferred_element_type=F32) + b_ref[...])


def _modulation(c, c_ctx, w_mod, b_mod):
    nb, d = c.shape
    rows = -(-(nb + 1) // 8) * 8
    n = w_mod.shape[1]
    bn = n // MOD_COLUMN_BLOCKS
    return pl.pallas_call(
        _mod_kernel,
        out_shape=jax.ShapeDtypeStruct((rows, n), F32),
        grid=(n // bn,),
        in_specs=[pl.BlockSpec((nb, d), lambda j: (0, 0)),
                  pl.BlockSpec((1, d), lambda j: (0, 0)),
                  pl.BlockSpec((d, bn), lambda j: (0, j)),
                  pl.BlockSpec((1, bn), lambda j: (0, j))],
        out_specs=pl.BlockSpec((rows, bn), lambda j: (0, j)),
        scratch_shapes=[pltpu.VMEM((rows, d), F32)],
        compiler_params=pltpu.CompilerParams(dimension_semantics=("arbitrary",),
                                             vmem_limit_bytes=VMEM_LIMIT),
        name="modulation",
    )(c, c_ctx[None, :], w_mod, b_mod)


def _head64_meansq(p, seg2):
    sq = p * p
    hi = sq.astype(BF16)
    lo = (sq - hi.astype(F32)).astype(BF16)
    s = jnp.dot(jnp.concatenate([hi, lo], axis=1), seg2, preferred_element_type=F32)
    return s * (1.0 / ATT_HEAD_DIM)


def _rope(v, cos, sin_signed, first_half):
    fwd = pltpu.roll(v, LANES - AXIS_DIM // 2, 1)
    bwd = pltpu.roll(v, AXIS_DIM // 2, 1)
    return v * cos + jnp.where(first_half, fwd, bwd) * sin_signed


def _gate_prep(f, lb, suffix):
    fg = lb + (1.0 - lb) * jax.nn.sigmoid(f)
    cum = jnp.log2(fg)
    rows = cum.shape[0]
    pos = lax.broadcasted_iota(jnp.int32, cum.shape, 0) % SUB
    shift = 1
    while shift < SUB:
        if suffix:
            moved = pltpu.roll(cum, rows - shift, 0)
            cum = cum + jnp.where(pos < SUB - shift, moved, 0.0)
        else:
            moved = pltpu.roll(cum, shift, 0)
            cum = cum + jnp.where(pos >= shift, moved, 0.0)
        shift *= 2
    return (1.0 - fg).astype(BF16), cum


def _inproj_kernel(x_ref, mod_ref, g1_ref, w_ref, gq_ref, gk_ref, lb_ref, cos_ref, sin_ref, *out_refs, latent,
                   mod_row):
    d = x_ref.shape[-1]
    x = x_ref[0]
    row = pl.ds(mod_row, 1)
    shift = mod_ref[row, 0:d]
    scale = mod_ref[row, d:2 * d]
    gq = jnp.concatenate([gq_ref[...]] * (LANES // ATT_HEAD_DIM), axis=1)
    gk = jnp.concatenate([gk_ref[...]] * (LANES // ATT_HEAD_DIM), axis=1)
    h = (_rms_rows(x) * g1_ref[...]) * (1.0 + scale) + shift
    hb = h.astype(BF16)

    def proj(c0, width):
        return jnp.dot(hb, w_ref[:, c0:c0 + width], preferred_element_type=F32)

    lane = lax.broadcasted_iota(jnp.int32, (2 * LANES, LANES), 0) % LANES
    lane_c = lax.broadcasted_iota(jnp.int32, (2 * LANES, LANES), 1)
    seg = (lane // ATT_HEAD_DIM == lane_c // ATT_HEAD_DIM).astype(BF16)
    row_lane = lax.broadcasted_iota(jnp.int32, (x.shape[0], LANES), 1)
    first_half = (row_lane % AXIS_DIM) < (AXIS_DIM // 2)
    upper = row_lane >= ATT_HEAD_DIM

    r0 = lb_ref[0]
    r1 = lb_ref[1]
    rm = jnp.maximum(r0, r1)
    e0 = jnp.exp(r0 - rm)
    e1 = jnp.exp(r1 - rm)
    lb_all = e0 / (e0 + e1)

    if latent:
        (qa_ref, ka_ref, vat_ref, hq_ref, hv_ref, kf_ref, bf_ref, kb_ref, bb_ref, sg_ref) = out_refs
        cos = cos_ref[...]
        sin = sin_ref[...]
    else:
        (ka_ref, vat_ref, hv_ref, kf_ref, bf_ref, kb_ref, bb_ref) = out_refs

    p_ff = proj(C_FF, HG_WIDTH)
    if latent:
        pq = proj(C_AQ, ATT_WIDTH)
    pk = proj(C_AK, ATT_KV_WIDTH)
    pv = proj(C_AV, ATT_KV_WIDTH)
    p_fb = proj(C_FB, HG_WIDTH)
    if latent:
        p_hq = proj(C_HQ, HG_WIDTH)
    p_hv = proj(C_HI, HG_WIDTH)
    ms_k = _head64_meansq(pk, seg)
    if latent:
        ms_q = [_head64_meansq(pq[:, c * LANES:(c + 1) * LANES], seg) for c in range(ATT_WIDTH // LANES)]

    kn = pk * lax.rsqrt(ms_k + EPS) * gk
    if latent:
        kn = _rope(kn, cos, sin, first_half)
    ka_ref[0] = kn.astype(BF16)
    vat_ref[0] = pv.T.astype(BF16)

    if latent:
        for c in range(ATT_WIDTH // LANES):
            t = pq[:, c * LANES:(c + 1) * LANES]
            t = t * lax.rsqrt(ms_q[c] + EPS) * gq
            t = _rope(t, cos, sin, first_half) * (ATT_HEAD_DIM ** -0.5 * LOG2E)
            t_swapped = pltpu.roll(t, ATT_HEAD_DIM, 1)
            for half in range(2):
                head = 2 * c + half
                kv = head // ATT_GROUP
                src = t if half == kv else t_swapped
                keep = upper if kv == 1 else jnp.logical_not(upper)
                qa_ref[0, :, head * LANES:(head + 1) * LANES] = jnp.where(keep, src, 0.0).astype(BF16)
        hq_ref[0] = (_silu(p_hq) * (HG_HEAD_DIM ** -0.5)).astype(BF16)
        sg_ref[0] = _silu(proj(C_HG, HG_WIDTH)).astype(BF16)

    hv_ref[0] = p_hv.astype(BF16)
    kf, bf = _gate_prep(p_ff, lb_all[0:1, :], False)
    kf_ref[0] = kf
    bf_ref[0] = bf
    kb, bb = _gate_prep(p_fb, lb_all[1:2, :], True)
    kb_ref[0] = kb
    bb_ref[0] = bb


def _inproj_latent_kernel(x_ref, mod_ref, g1_ref, w_in, gq_ref, gk_ref, lb_ref, cos_ref, sin_ref, *refs,
                          tiles_per_batch):
    *out_refs, wb_ref, w_ref = refs
    step = pl.program_id(0)

    @pl.when(step < IN_WEIGHT_STEPS)
    def _():
        rows = w_in.shape[0]
        chunk = w_in[...].astype(BF16)
        w_ref[pl.ds(pl.multiple_of(step * rows, rows), rows), :] = chunk
        wb_ref[...] = chunk

    @pl.when(step >= IN_WEIGHT_STEPS)
    def _():
        _inproj_kernel(x_ref, mod_ref, g1_ref, w_ref, gq_ref, gk_ref, lb_ref, cos_ref, sin_ref, *out_refs,
                       latent=True, mod_row=(step - IN_WEIGHT_STEPS) // tiles_per_batch)


def _input_projection_latent(rows, mods, g1, w_in, gq, gk, lb_raw, cos, sin):
    b, t, d = rows.shape
    tm = min(ROW_TILE, t)
    nt = t // tm
    assert w_in.shape[0] % (IN_WEIGHT_STEPS * SUB) == 0
    chunk_rows = w_in.shape[0] // IN_WEIGHT_STEPS
    tile = lambda s: jnp.maximum(s - IN_WEIGHT_STEPS, 0)
    chunk = lambda s: jnp.minimum(s, IN_WEIGHT_STEPS - 1)
    bspec = lambda width: pl.BlockSpec((1, tm, width), lambda s: (tile(s) // nt, tile(s) % nt, 0))
    full = lambda a: pl.BlockSpec(a.shape, lambda s: (0,) * a.ndim)
    sds = lambda width, dt: jax.ShapeDtypeStruct((b, t, width), dt)
    table_spec = pl.BlockSpec((tm, LANES), lambda s: (tile(s) % nt, 0))
    w_spec = pl.BlockSpec((chunk_rows, w_in.shape[1]), lambda s: (chunk(s), 0))
    out_shape = [sds(ATT_HEADS * LANES, BF16), sds(ATT_KV_WIDTH, BF16),
                 jax.ShapeDtypeStruct((b, ATT_KV_WIDTH, t), BF16),
                 sds(HG_WIDTH, BF16), sds(HG_WIDTH, BF16), sds(HG_WIDTH, BF16), sds(HG_WIDTH, F32),
                 sds(HG_WIDTH, BF16), sds(HG_WIDTH, F32), sds(HG_WIDTH, BF16),
                 jax.ShapeDtypeStruct(w_in.shape, BF16)]
    out_specs = [bspec(ATT_HEADS * LANES), bspec(ATT_KV_WIDTH),
                 pl.BlockSpec((1, ATT_KV_WIDTH, tm), lambda s: (tile(s) // nt, 0, tile(s) % nt)),
                 bspec(HG_WIDTH), bspec(HG_WIDTH), bspec(HG_WIDTH), bspec(HG_WIDTH), bspec(HG_WIDTH),
                 bspec(HG_WIDTH), bspec(HG_WIDTH), w_spec]
    return pl.pallas_call(
        functools.partial(_inproj_latent_kernel, tiles_per_batch=nt),
        out_shape=out_shape,
        grid=(IN_WEIGHT_STEPS + b * nt,),
        in_specs=[bspec(d), full(mods), full(g1), w_spec, full(gq), full(gk), full(lb_raw),
                  table_spec, table_spec],
        out_specs=out_specs,
        scratch_shapes=[pltpu.VMEM(w_in.shape, BF16)],
        compiler_params=pltpu.CompilerParams(dimension_semantics=("arbitrary",),
                                             vmem_limit_bytes=VMEM_LIMIT),
        name="inproj_latent",
    )(rows, mods, g1, w_in, gq, gk, lb_raw, cos, sin)


def _input_projection_context(rows, mods, mod_row, g1, w_in, gq, gk, lb_raw):
    b, t, d = rows.shape
    tm = min(ROW_TILE, t)
    nt = t // tm
    bspec = lambda width: pl.BlockSpec((1, tm, width), lambda i, j: (i, j, 0))
    full = lambda a: pl.BlockSpec(a.shape, lambda i, j: (0,) * a.ndim)
    sds = lambda width, dt: jax.ShapeDtypeStruct((b, t, width), dt)
    out_shape = [sds(ATT_KV_WIDTH, BF16), jax.ShapeDtypeStruct((b, ATT_KV_WIDTH, t), BF16),
                 sds(HG_WIDTH, BF16), sds(HG_WIDTH, BF16), sds(HG_WIDTH, F32), sds(HG_WIDTH, BF16),
                 sds(HG_WIDTH, F32)]
    out_specs = [bspec(ATT_KV_WIDTH), pl.BlockSpec((1, ATT_KV_WIDTH, tm), lambda i, j: (i, 0, j)),
                 bspec(HG_WIDTH), bspec(HG_WIDTH), bspec(HG_WIDTH), bspec(HG_WIDTH), bspec(HG_WIDTH)]
    return pl.pallas_call(
        functools.partial(_inproj_context_kernel, mod_row=mod_row),
        out_shape=out_shape,
        grid=(b, nt),
        in_specs=[bspec(d), full(mods), full(g1), full(w_in), full(gq), full(gk), full(lb_raw)],
        out_specs=out_specs,
        compiler_params=pltpu.CompilerParams(dimension_semantics=("parallel", "parallel"),
                                             vmem_limit_bytes=VMEM_LIMIT),
        name="inproj_context",
    )(rows, mods, g1, w_in, gq, gk, lb_raw)


def _inproj_context_kernel(x_ref, mod_ref, g1_ref, w_ref, gq_ref, gk_ref, lb_ref, *out_refs, mod_row):
    _inproj_kernel(x_ref, mod_ref, g1_ref, w_ref, gq_ref, gk_ref, lb_ref, None, None, *out_refs,
                   latent=False, mod_row=mod_row)


def _attention_schedule(q_ref, kc_ref, vtc_ref, kx_ref, vtx_ref, o_ref, sc_ref, s_ref, acc_ref):
    tq = q_ref.shape[1]
    n_kv = kx_ref.shape[1] // KV_TILE
    nt_dims = (((1,), (1,)), ((), ()))
    cols = ATT_GROUP * tq

    qgs = [jnp.concatenate(
        [q_ref[0, :, (kv * ATT_GROUP + j) * LANES:(kv * ATT_GROUP + j + 1) * LANES] for j in range(ATT_GROUP)],
        axis=0) for kv in range(ATT_KV_HEADS)]

    def scores(kt, dst):
        for kv in range(ATT_KV_HEADS):
            dst(kv, lax.dot_general(kt, qgs[kv], nt_dims, preferred_element_type=F32))

    def consume(src, vt_all, maxima):
        ones = jnp.ones((ONES_ROWS, vt_all.shape[1]), BF16)
        out = []
        for kv in range(ATT_KV_HEADS):
            m = maxima[kv]
            m_new = jnp.maximum(m, jnp.max(src(kv), axis=0, keepdims=True))
            alpha = jnp.exp2(m - m_new)
            p = jnp.exp2(src(kv) - m_new).astype(BF16)
            vt = jnp.concatenate([vt_all[kv * ATT_HEAD_DIM:(kv + 1) * ATT_HEAD_DIM], ones], axis=0)
            acc_ref[kv] = alpha * acc_ref[kv] + jnp.dot(vt, p, preferred_element_type=F32)
            out.append(m_new)
        return tuple(out)

    def put_ctx(kv, val):
        sc_ref[kv] = val

    def put(slot):
        def dst(kv, val):
            s_ref[slot, kv] = val
        return dst

    def k_tile(j):
        return kx_ref[0, j * KV_TILE:(j + 1) * KV_TILE, :]

    def vt_tile(j):
        return vtx_ref[0, :, j * KV_TILE:(j + 1) * KV_TILE]

    acc_ref[...] = jnp.zeros(acc_ref.shape, F32)
    scores(kc_ref[0], put_ctx)
    for j in range(min(KV_AHEAD, n_kv)):
        scores(k_tile(j), put(j % KV_SLOTS))
    maxima = consume(lambda kv: sc_ref[kv], vtc_ref[0], (jnp.full((1, cols), -jnp.inf, F32),) * ATT_KV_HEADS)
    for j in range(n_kv):
        if j + KV_AHEAD < n_kv:
            scores(k_tile(j + KV_AHEAD), put((j + KV_AHEAD) % KV_SLOTS))
        maxima = consume(lambda kv, slot=j % KV_SLOTS: s_ref[slot, kv], vt_tile(j), maxima)
        yield

    for kv in range(ATT_KV_HEADS):
        o = acc_ref[kv, 0:ATT_HEAD_DIM] / acc_ref[kv, ATT_HEAD_DIM:ATT_HEAD_DIM + 1]
        for jj in range(ATT_GROUP // 2):
            pair = jnp.concatenate([o[:, (2 * jj) * tq:(2 * jj + 1) * tq],
                                    o[:, (2 * jj + 1) * tq:(2 * jj + 2) * tq]], axis=0)
            col = kv * (ATT_GROUP // 2) + jj
            o_ref[0, :, col * LANES:(col + 1) * LANES] = pair.T.astype(BF16)


NT_DIMS = (((1,), (1,)), ((), ()))


def _scan_window(st, q, k, v, b, b_row, msk_ref, reverse, want_out):
    nsub = WIN // SUB
    last = 0 if reverse else SUB - 1
    sub_of = (lambda p: nsub - 1 - p) if reverse else (lambda p: p)
    blk = lambda a, i: a[i * SUB:(i + 1) * SUB]
    d = int(reverse)

    b_last = [b_row(i * SUB + last) for i in range(nsub)]
    cum = []
    for p in range(nsub):
        cum.append(b_last[sub_of(p)] if p == 0 else cum[-1] + b_last[sub_of(p)])
    ks = [blk(k, i) * jnp.exp2(b_last[i] - blk(b, i)) for i in range(nsub)]

    k_end = jnp.concatenate(
        [ks[i] if sub_of(i) == nsub - 1 else ks[i] * jnp.exp2(cum[nsub - 1] - cum[sub_of(i)]) for i in range(nsub)],
        axis=0)
    st_new = jnp.exp2(cum[nsub - 1]) * st + jnp.dot(v.T, k_end.astype(BF16), preferred_element_type=F32)
    if not want_out:
        return st_new, None

    qs = [blk(q, i) * jnp.exp2(blk(b, i)) for i in range(nsub)]
    q_start = jnp.concatenate(
        [qs[i] if sub_of(i) == 0 else qs[i] * jnp.exp2(cum[sub_of(i) - 1]) for i in range(nsub)], axis=0)
    o_carried = lax.dot_general(q_start.astype(BF16), st.astype(BF16), NT_DIMS, preferred_element_type=F32)

    scores = None
    for level, h in enumerate((nsub // 2, nsub // 4, nsub // 8)):
        q_rows, k_rows = [], []
        for i in range(nsub):
            p = sub_of(i)
            ref = (p // (2 * h)) * 2 * h + h - 1
            q_rows.append(qs[i] * jnp.exp2(cum[p - 1] - cum[ref]) if p - 1 > ref else qs[i])
            k_rows.append(ks[i] * jnp.exp2(cum[ref] - cum[p]) if p < ref else ks[i])
        term = (lax.dot_general(jnp.concatenate(q_rows, axis=0).astype(BF16),
                                jnp.concatenate(k_rows, axis=0).astype(BF16), NT_DIMS,
                                preferred_element_type=F32) * msk_ref[d, level])
        scores = term if scores is None else scores + term

    def level_scores(level, q_exponent, k_exponent):
        q_l = q * jnp.exp2(jnp.minimum(q_exponent, 0.0))
        k_l = k if k_exponent is None else k * jnp.exp2(jnp.minimum(k_exponent, 0.0))
        return (lax.dot_general(q_l.astype(BF16), k_l.astype(BF16), NT_DIMS, preferred_element_type=F32)
                * msk_ref[d, level])

    scores = scores + (jnp.dot(q * k, jnp.ones((LANES, LANES), BF16), preferred_element_type=F32)
                       * msk_ref[d, N_LEVELS])
    for level, h in ((3, SUB // 2), (4, SUB // 4)):
        ref_rows = [b_row(s * 2 * h + (h if reverse else h - 1)) for s in range(WIN // (2 * h))]
        diff = b - jnp.concatenate([jnp.broadcast_to(r, (2 * h, LANES)) for r in ref_rows], axis=0)
        decay = jnp.exp2(jnp.minimum(diff, -diff))
        scores = scores + (lax.dot_general((q * decay).astype(BF16), (k * decay).astype(BF16), NT_DIMS,
                                           preferred_element_type=F32)
                           * msk_ref[d, level])
    before = lambda n: pltpu.roll(b, n, 0)
    after = lambda n: pltpu.roll(b, WIN - n, 0)
    off4 = lax.broadcasted_iota(jnp.int32, (WIN, LANES), 0) % 4
    if reverse:
        q_exp = b - jnp.where(off4 == 0, after(2), after(1))
        k_exp = jnp.where(off4 == 2, 0.0, before(1) - b)
    else:
        q_exp = b - jnp.where(off4 == 3, before(2), before(1))
        k_exp = jnp.where(off4 == 1, 0.0, after(1) - b)
    scores = scores + level_scores(5, q_exp, k_exp)
    scores = scores + level_scores(6, b - (after(1) if reverse else before(1)), None)
    return st_new, (scores.astype(BF16), o_carried)


N_LEVELS = 7


def _pair_masks():
    msk = np.zeros((2, N_LEVELS + 1, WIN, WIN), np.float32)
    pos = np.arange(WIN)
    for d in range(2):
        for level in range(N_LEVELS):
            h = WIN >> (level + 1)
            span, off = pos // (2 * h), pos % (2 * h)
            later = (off < h) if d else (off >= h)
            msk[d, level] = (span[:, None] == span[None, :]) & later[:, None] & ~later[None, :]
        msk[d, N_LEVELS] = np.eye(WIN)
    return jnp.asarray(msk, F32)


def _hgrn_steps(q_ref, v_ref, kf_ref, bf_ref, kb_ref, bb_ref, vc_ref, kfc_ref, bfc_ref, kbc_ref, bbc_ref,
                msk_ref, o_ref, stf_ref, stb_ref, scf_ref, scb_ref, ocf_ref, ocb_ref, w0, n_steps):
    t = q_ref.shape[1]
    ctx = vc_ref.shape[1]
    nw = t // WIN

    def window(refs, start, st, reverse, want_out):
        qr, vr, kr, br = refs
        sl = pl.ds(start, WIN)
        vw = vr[0, sl, :].astype(F32)
        kw = kr[0, sl, :].astype(F32)
        bw = br[0, sl, :]
        qw = qr[0, sl, :].astype(F32) if want_out else None
        b_row = lambda r: br[0, pl.ds(start + r, 1), :]
        return _scan_window(st, qw, kw, vw, bw, b_row, msk_ref, reverse, want_out)

    def restart():
        zero = jnp.zeros((LANES, LANES), F32)
        stf, stb = zero, zero
        n_cw = ctx // WIN
        for w in range(n_cw):
            stf, _ = window((None, vc_ref, kfc_ref, bfc_ref), w * WIN, stf, False, False)
            stb, _ = window((None, vc_ref, kbc_ref, bbc_ref), (n_cw - 1 - w) * WIN, stb, True, False)
        stf_ref[...] = stf
        stb_ref[...] = stb
        o_ref[...] = jnp.zeros(o_ref.shape, F32)

    def starts(w):
        return pl.multiple_of(w * WIN, WIN), pl.multiple_of((nw - 1 - w) * WIN, WIN)

    def scores_stage(w):
        fs, bs = starts(w)
        stf, (sf, cf) = window((q_ref, v_ref, kf_ref, bf_ref), fs, stf_ref[...], False, True)
        stf_ref[...] = stf
        scf_ref[...] = sf
        ocf_ref[...] = cf
        stb, (sb, cb) = window((q_ref, v_ref, kb_ref, bb_ref), bs, stb_ref[...], True, True)
        stb_ref[...] = stb
        scb_ref[...] = sb
        ocb_ref[...] = cb

    def output_stage(w):
        fs, bs = starts(w)
        o_ref[0, pl.ds(fs, WIN), :] += (
            jnp.dot(scf_ref[...], v_ref[0, pl.ds(fs, WIN), :], preferred_element_type=F32) + ocf_ref[...])
        o_ref[0, pl.ds(bs, WIN), :] += (
            jnp.dot(scb_ref[...], v_ref[0, pl.ds(bs, WIN), :], preferred_element_type=F32) + ocb_ref[...])

    def schedule():
        for u in range(n_steps + 1):
            if u > 0:
                output_stage(w0 + u - 1)
            if u < n_steps:
                scores_stage(w0 + u)
            yield

    return restart, schedule()


def _mixer_kernel(qa_ref, kc_ref, vtc_ref, kx_ref, vtx_ref,
                  hq_ref, hv_ref, kf_ref, bf_ref, kb_ref, bb_ref, hvc_ref, kfc_ref, bfc_ref, kbc_ref, bbc_ref,
                  msk_ref, wo_in, wgu_in, wd_in, att_ref, hg_ref, wo_b, wgu_b, wd_b,
                  sc_ref, s_ref, acc_ref, *hg_scratch, steps_per_head):
    nw = hq_ref.shape[1] // WIN
    n_steps = nw // steps_per_head
    phase = pl.program_id(1) % steps_per_head

    @pl.when(phase == 0)
    def _():
        for src, dst in ((wo_in, wo_b), (wgu_in, wgu_b), (wd_in, wd_b)):
            dst[...] = src[...].astype(BF16)

    restart, scan = _hgrn_steps(hq_ref, hv_ref, kf_ref, bf_ref, kb_ref, bb_ref,
                                hvc_ref, kfc_ref, bfc_ref, kbc_ref, bbc_ref, msk_ref, hg_ref, *hg_scratch,
                                w0=phase * n_steps, n_steps=n_steps)
    pl.when(phase == 0)(restart)

    for _ in _attention_schedule(qa_ref, kc_ref, vtc_ref, kx_ref, vtx_ref, att_ref, sc_ref, s_ref, acc_ref):
        next(scan, None)
    for _ in scan:
        pass


def _mixers(qa, kc, vtc, kx, vtx, hq, hv, kf, bf, kb, bb, hvc, kfc, bfc, kbc, bbc, weights):
    b, t, _ = qa.shape
    ctx = hvc.shape[1]
    tq = min(Q_TILE, t)
    n_q = t // tq
    nw = t // WIN
    assert tq % LANES == 0 and t % tq == 0 and t % KV_TILE == 0 and KV_AHEAD < KV_SLOTS
    assert n_q % HG_HEADS == 0 and nw % (n_q // HG_HEADS) == 0
    steps_per_head = n_q // HG_HEADS
    cols = ATT_GROUP * tq
    whole = lambda a: pl.BlockSpec((1,) + a.shape[1:], lambda i, j: (i, 0, 0))
    lat = pl.BlockSpec((1, t, HG_HEAD_DIM), lambda i, j: (i, 0, j // steps_per_head))
    con = pl.BlockSpec((1, ctx, HG_HEAD_DIM), lambda i, j: (i, 0, j // steps_per_head))
    msk = _pair_masks()
    const = lambda a: pl.BlockSpec(a.shape, lambda i, j: (0,) * a.ndim)
    n_chunks = b * HG_HEADS
    assert all(w.shape[0] % (n_chunks * SUB) == 0 for w in weights)
    chunk = lambda w: pl.BlockSpec((w.shape[0] // n_chunks, w.shape[1]),
                                   lambda i, j: (i * HG_HEADS + j // steps_per_head, 0))
    return pl.pallas_call(
        functools.partial(_mixer_kernel, steps_per_head=steps_per_head),
        out_shape=[jax.ShapeDtypeStruct((b, t, ATT_WIDTH), BF16), jax.ShapeDtypeStruct((b, t, HG_WIDTH), F32)]
                  + [jax.ShapeDtypeStruct(w.shape, BF16) for w in weights],
        grid=(b, n_q),
        in_specs=[pl.BlockSpec((1, tq, qa.shape[-1]), lambda i, j: (i, j, 0)),
                  whole(kc), whole(vtc), whole(kx), whole(vtx)] + [lat] * 6 + [con] * 5
                 + [const(msk)] + [chunk(w) for w in weights],
        out_specs=[pl.BlockSpec((1, tq, ATT_WIDTH), lambda i, j: (i, j, 0)), lat]
                  + [chunk(w) for w in weights],
        scratch_shapes=[pltpu.VMEM((ATT_KV_HEADS, kc.shape[1], cols), F32),
                        pltpu.VMEM((KV_SLOTS, ATT_KV_HEADS, KV_TILE, cols), F32),
                        pltpu.VMEM((ATT_KV_HEADS, ATT_HEAD_DIM + ONES_ROWS, cols), F32),
                        pltpu.VMEM((LANES, LANES), F32), pltpu.VMEM((LANES, LANES), F32),
                        pltpu.VMEM((WIN, WIN), BF16), pltpu.VMEM((WIN, WIN), BF16),
                        pltpu.VMEM((WIN, LANES), F32), pltpu.VMEM((WIN, LANES), F32)],
        compiler_params=pltpu.CompilerParams(dimension_semantics=("arbitrary", "arbitrary"),
                                             vmem_limit_bytes=VMEM_LIMIT),
        name="mixers",
    )(qa, kc, vtc, kx, vtx, hq, hv, kf, bf, kb, bb, hvc, kfc, bfc, kbc, bbc, msk, *weights)


def _out_ffn_kernel(x_ref, att_ref, hg_ref, sg_ref, mod_ref, ghg_ref, wo_ref, g2_ref, wgu_ref, wd_ref, gf_ref,
                    o_ref):
    d = x_ref.shape[-1]
    dff = wd_ref.shape[0]
    row = pl.ds(pl.program_id(0), 1)
    gate1 = mod_ref[row, 2 * d:3 * d]
    shift2 = mod_ref[row, 3 * d:4 * d]
    scale2 = mod_ref[row, 4 * d:5 * d]
    gate2 = mod_ref[row, 5 * d:6 * d]

    parts = [slice(r0, r0 + OUT_PART_ROWS) for r0 in range(0, x_ref.shape[1], OUT_PART_ROWS)]

    def mixed(rows):
        heads = []
        for hd in range(HG_HEADS):
            cs = slice(hd * HG_HEAD_DIM, (hd + 1) * HG_HEAD_DIM)
            o = _rms_rows(hg_ref[0, rows, cs]) * ghg_ref[...]
            heads.append((o * sg_ref[0, rows, cs].astype(F32)).astype(BF16))
        mix = (jnp.dot(att_ref[0, rows, :], wo_ref[0:ATT_WIDTH, :], preferred_element_type=F32)
               + jnp.dot(jnp.concatenate(heads, axis=1), wo_ref[ATT_WIDTH:ATT_WIDTH + HG_WIDTH, :],
                         preferred_element_type=F32))
        x1 = x_ref[0, rows, :] + gate1 * mix
        return x1, ((_rms_rows(x1) * g2_ref[...]) * (1.0 + scale2) + shift2).astype(BF16)

    def gate_up(h):
        return (jnp.dot(h, wgu_ref[:, 0:dff], preferred_element_type=F32),
                jnp.dot(h, wgu_ref[:, dff:2 * dff], preferred_element_type=F32))

    def down(a, g):
        return jnp.dot((_silu(a) * g).astype(BF16), wd_ref[...], preferred_element_type=F32)

    firsts = [mixed(rows) for rows in parts]
    ups = [gate_up(h) for _, h in firsts]
    ys = [down(a, g) for a, g in ups]
    for rows, (x1, _), y in zip(parts, firsts, ys):
        o_ref[0, rows, :] = _rms_rows(x1 + gate2 * y) * gf_ref[...]


def _out_ffn(x, att, hg, sg, mods, ghg, w_out, g2, w_gu, w_down, gf):
    b, t, d = x.shape
    tm = min(OUT_ROW_TILE, t)
    assert tm % OUT_PART_ROWS == 0
    bspec = lambda width: pl.BlockSpec((1, tm, width), lambda i, j: (i, j, 0))
    full = lambda a: pl.BlockSpec(a.shape, lambda i, j: (0,) * a.ndim, pipeline_mode=pl.Buffered(1))
    return pl.pallas_call(
        _out_ffn_kernel,
        out_shape=jax.ShapeDtypeStruct((b, t, d), F32),
        grid=(b, t // tm),
        in_specs=[bspec(d), bspec(ATT_WIDTH), bspec(HG_WIDTH), bspec(HG_WIDTH), full(mods),
                  full(ghg), full(w_out), full(g2), full(w_gu), full(w_down), full(gf)],
        out_specs=bspec(d),
        compiler_params=pltpu.CompilerParams(dimension_semantics=("parallel", "parallel"),
                                             vmem_limit_bytes=VMEM_LIMIT),
        name="out_ffn",
    )(x, att, hg, sg, mods, ghg, w_out, g2, w_gu, w_down, gf)


def _rope_tables(t):
    n_rows = t // GRID_W
    row = np.repeat(np.arange(n_rows, dtype=np.float32), GRID_W)
    col = np.tile(np.arange(GRID_W, dtype=np.float32), n_rows)
    inv = (np.float32(ROPE_THETA) ** (-np.arange(0, AXIS_DIM, 2, dtype=np.float32) / np.float32(AXIS_DIM)))
    inv = inv.astype(np.float32)
    ar = row[:, None] * inv
    ac = col[:, None] * inv
    ang = np.concatenate([ar, ar, ac, ac], axis=-1).astype(np.float32)
    sign = np.where((np.arange(ATT_HEAD_DIM) % AXIS_DIM) < AXIS_DIM // 2, -1.0, 1.0)
    reps = LANES // ATT_HEAD_DIM
    cos = np.tile(np.cos(ang.astype(np.float64)), (1, reps)).astype(np.float32)
    sin = np.tile(np.sin(ang.astype(np.float64)) * sign, (1, reps)).astype(np.float32)
    return jnp.asarray(cos), jnp.asarray(sin)


def kernel(x, c, ctx, c_ctx, w_mod, b_mod, g_norm1, w_in, g_q, g_k, lb_raw, g_hg, w_out, g_norm2, w_gu, w_down,
           g_final):
    b, t, d = x.shape
    assert w_mod.shape[0] == 1 and lb_raw.shape[0] == 2, "single-layer block"
    assert t % WIN == 0 and (t // WIN) % 2 == 0 and ctx.shape[1] % WIN == 0 and t % KV_TILE == 0

    mods = _modulation(c, c_ctx, w_mod[0], b_mod)

    cos, sin = _rope_tables(t)

    qa, kx, vtx, hq, hv, kf, bf, kb, bb, sg, w_in_b = _input_projection_latent(
        x, mods, g_norm1, w_in[0], g_q, g_k, lb_raw, cos, sin)
    kc, vtc, hvc, kfc, bfc, kbc, bbc = _input_projection_context(
        ctx, mods, b, g_norm1, w_in_b, g_q, g_k, lb_raw)

    att, hg, w_out_b, w_gu_b, w_down_b = _mixers(qa, kc, vtc, kx, vtx, hq, hv, kf, bf, kb, bb, hvc, kfc, bfc, kbc,
                                                 bbc, (w_out[0], w_gu[0], w_down[0]))
    return _out_ffn(x, att, hg, sg, mods, g_hg, w_out_b, g_norm2, w_gu_b, w_down_b, g_final[None, :])
```
